```python
import numpy as np
import jax, jax.numpy as jnp
from jax import lax

D_MODEL = 1024
BATCH = 4
SEQ = 8192
DEPTH = 4

HEAD_DIM = 64
NSA_HEADS = 8
NSA_KV_GROUPS = 2
NSA_HPG = NSA_HEADS // NSA_KV_GROUPS
CMP_BLOCK = 32
CMP_STRIDE = 16
CMP_HIDDEN = 256
SEL_BLOCK = 64
SEL_TOPK = 16
NSA_WINDOW = 512
Q_BLOCK = 128
FORCE_SCORE = 1e6
DIL_GROUPS = ((128, 1), (512, 4), (2048, 16))
DIL_HPG = 4
DIL_HEADS = DIL_HPG * len(DIL_GROUPS)
DIL_BLOCK = 128
D_FF = -(-8 * D_MODEL // (3 * 256)) * 256
NSA_Q = NSA_HEADS * HEAD_DIM
NSA_KV = 3 * 2 * NSA_KV_GROUPS * HEAD_DIM
NSA_GATE = NSA_HEADS * 3
DIL_QKV = 3 * DIL_HEADS * HEAD_DIM
MERGE_GATE = 2 * D_MODEL
IN_SPLITS = tuple(int(c) for c in np.cumsum([NSA_Q, NSA_KV, NSA_GATE, DIL_QKV]))
D_IN = NSA_Q + NSA_KV + NSA_GATE + DIL_QKV + MERGE_GATE
NSA_OUT = NSA_HEADS * HEAD_DIM
DIL_OUT = DIL_HPG * HEAD_DIM
N_ALIBI = NSA_HEADS + DIL_HEADS
EPS = 1e-6
NEG_INF = -1e30
TINY = 1e-30

kernel_name = "hybrid_nsa_dilated_alibi_trunk"


def rmsnorm(x, g):
    xf = x.astype(jnp.float32)
    y = xf * lax.rsqrt(jnp.mean(xf * xf, axis=-1, keepdims=True) + EPS)
    return (y * g.astype(jnp.float32)).astype(x.dtype)


def alibi_slopes():
    k = jnp.arange(1, N_ALIBI + 1, dtype=jnp.float32)
    s = jnp.exp2(-8.0 * k / N_ALIBI)
    d0 = s[:DIL_HPG]
    d1 = s[DIL_HPG:2 * DIL_HPG]
    nsa = s[2 * DIL_HPG:2 * DIL_HPG + NSA_HEADS]
    d2 = s[2 * DIL_HPG + NSA_HEADS:]
    return nsa, (d0, d1, d2)


def masked_softmax(s, mask):
    s = jnp.where(mask, s, NEG_INF)
    m = jnp.max(s, axis=-1, keepdims=True)
    e = jnp.where(mask, jnp.exp(s - m), 0.0)
    l = jnp.sum(e, axis=-1, keepdims=True)
    lt = jnp.maximum(l, TINY)
    return e / lt, (m + jnp.log(lt))[..., 0]


def compress(kv, pe, w1, w2):
    B, S, G, hd = kv.shape
    n_c = (S - CMP_BLOCK) // CMP_STRIDE + 1
    idx = jnp.arange(n_c)[:, None] * CMP_STRIDE + jnp.arange(CMP_BLOCK)[None, :]
    blk = kv[:, idx] + pe[:, None, :]
    blk = blk.transpose(0, 1, 3, 2, 4).reshape(B, n_c, G, CMP_BLOCK * hd)
    return jax.nn.silu(blk @ w1) @ w2


def nsa_attention(q, kvs, gates, pe_k, pe_v, w_ck1, w_ck2, w_cv1, w_cv2, slopes):
    B, S, H, hd = q.shape
    G, Hg = NSA_KV_GROUPS, NSA_HPG
    k_cmp, v_cmp = kvs[:, :, 0, 0], kvs[:, :, 0, 1]
    k_slc, v_slc = kvs[:, :, 1, 0], kvs[:, :, 1, 1]
    k_win, v_win = kvs[:, :, 2, 0], kvs[:, :, 2, 1]
    kc = compress(k_cmp, pe_k, w_ck1, w_ck2)
    vc = compress(v_cmp, pe_v, w_cv1, w_cv2)
    n_c = kc.shape[1]
    n_s = S // SEL_BLOCK
    n_top = min(SEL_TOPK, n_s)
    c_start = jnp.arange(n_c) * CMP_STRIDE
    pos_c = (c_start + CMP_BLOCK - 1).astype(jnp.float32)
    s_start = jnp.arange(n_s) * SEL_BLOCK
    overlap = jnp.clip(jnp.minimum(c_start[:, None] + CMP_BLOCK, s_start[None, :] + SEL_BLOCK)
                       - jnp.maximum(c_start[:, None], s_start[None, :]), 0).astype(jnp.float32) / CMP_BLOCK
    kb = k_slc.reshape(B, n_s, SEL_BLOCK, G, hd).transpose(0, 3, 1, 2, 4)
    vb = v_slc.reshape(B, n_s, SEL_BLOCK, G, hd).transpose(0, 3, 1, 2, 4)
    pad_w = ((0, 0), (NSA_WINDOW, 0), (0, 0), (0, 0))
    kw = jnp.pad(k_win, pad_w)
    vw = jnp.pad(v_win, pad_w)
    qs = (q * (hd ** -0.5)).reshape(B, S, G, Hg, hd)
    gs = gates.reshape(B, S, G, Hg, 3)
    m = slopes.reshape(G, Hg)[:, :, None, None]
    bi = jnp.arange(B)[:, None, None, None]
    gi = jnp.arange(G)[None, :, None, None]
    j = jnp.arange(n_s)

    def block(i):
        s0 = i * Q_BLOCK
        qb = lax.dynamic_slice_in_dim(qs, s0, Q_BLOCK, axis=1)
        gb = lax.dynamic_slice_in_dim(gs, s0, Q_BLOCK, axis=1)
        t = s0 + jnp.arange(Q_BLOCK)
        tf = t.astype(jnp.float32)
        sc = jnp.einsum('bqghd,bngd->bghqn', qb, kc).astype(jnp.float32)
        dc = tf[:, None] - pos_c[None, :]
        pc, _ = masked_softmax(sc - m * dc, dc >= 0)
        o_c = jnp.einsum('bghqn,bngd->bqghd', pc, vc)
        imp = jnp.einsum('bghqn,nj->bgqj', pc, overlap)
        jt = t // SEL_BLOCK
        valid = j[None, :] <= jt[:, None]
        forced = (j[None, :] == 0) | (j[None, :] == jt[:, None]) | (j[None, :] == jt[:, None] - 1)
        score = jnp.where(forced, FORCE_SCORE, jnp.where(valid, imp, -1.0))
        top, idx = lax.top_k(score, n_top)
        ks = kb[bi, gi, idx]
        vs = vb[bi, gi, idx]
        pos_s = idx[..., None] * SEL_BLOCK + jnp.arange(SEL_BLOCK)
        ds = (t[:, None, None] - pos_s).astype(jnp.float32)
        mask_s = ((top >= 0)[..., None] & (ds >= 0)).reshape(B, G, 1, Q_BLOCK, n_top * SEL_BLOCK)
        ss = jnp.einsum('bqghd,bgqnld->bghqnl', qb, ks).astype(jnp.float32)
        ss = ss.reshape(B, G, Hg, Q_BLOCK, n_top * SEL_BLOCK)
        ps, _ = masked_softmax(ss - m[None] * ds.reshape(B, G, 1, Q_BLOCK, n_top * SEL_BLOCK), mask_s)
        o_s = jnp.einsum('bghqk,bgqkd->bqghd', ps, vs.reshape(B, G, Q_BLOCK, n_top * SEL_BLOCK, hd))
        kwb = lax.dynamic_slice_in_dim(kw, s0, Q_BLOCK + NSA_WINDOW, axis=1)
        vwb = lax.dynamic_slice_in_dim(vw, s0, Q_BLOCK + NSA_WINDOW, axis=1)
        pos_w = s0 - NSA_WINDOW + jnp.arange(Q_BLOCK + NSA_WINDOW)
        dw = t[:, None] - pos_w[None, :]
        mask_w = (dw >= 0) & (dw < NSA_WINDOW) & (pos_w[None, :] >= 0)
        sw = jnp.einsum('bqghd,bkgd->bghqk', qb, kwb).astype(jnp.float32)
        pw, _ = masked_softmax(sw - m * dw.astype(jnp.float32), mask_w)
        o_w = jnp.einsum('bghqk,bkgd->bqghd', pw, vwb)
        o = gb[..., 0:1] * o_c + gb[..., 1:2] * o_s + gb[..., 2:3] * o_w
        return o.reshape(B, Q_BLOCK, H * hd)

    out = lax.map(block, jnp.arange(S // Q_BLOCK))
    return out.transpose(1, 0, 2, 3).reshape(B, S, H * hd)


def dilated_group(q, k, v, window, dil, slopes):
    B, S, Hg, hd = q.shape
    L = S // dil
    steps = window // dil
    c = DIL_BLOCK
    nb = -(-L // c)
    Lp = nb * c

    def sub(a):
        a = a.reshape(B, L, dil, Hg, hd).transpose(0, 2, 3, 1, 4)
        return jnp.pad(a, ((0, 0), (0, 0), (0, 0), (0, Lp - L), (0, 0)))

    def band(a):
        ap = jnp.pad(sub(a), ((0, 0), (0, 0), (0, 0), (c, 0), (0, 0)))
        prev = ap[:, :, :, :Lp].reshape(B, dil, Hg, nb, c, hd)
        cur = ap[:, :, :, c:].reshape(B, dil, Hg, nb, c, hd)
        return jnp.concatenate([prev, cur], axis=4)

    qs = (sub(q) * (hd ** -0.5)).reshape(B, dil, Hg, nb, c, hd)
    kband, vband = band(k), band(v)
    s = jnp.einsum('brhnqd,brhnkd->brhnqk', qs, kband).astype(jnp.float32)
    qi = jnp.arange(c)
    ki = jnp.arange(2 * c)
    delta = qi[:, None] + c - ki[None, :]
    k_abs = jnp.arange(nb)[:, None] * c - c + ki[None, :]
    mask = (delta >= 0)[None] & (delta <= steps)[None] & (k_abs[:, None, :] >= 0)
    bias = slopes[:, None, None, None] * (delta * dil).astype(jnp.float32)[None, None]
    p, lse = masked_softmax(s - bias, mask)
    o = jnp.einsum('brhnqk,brhnkd->brhnqd', p, vband).reshape(B, dil, Hg, Lp, hd)[:, :, :, :L]
    o = o.transpose(0, 3, 1, 2, 4).reshape(B, S, Hg, hd)
    lse = lse.reshape(B, dil, Hg, Lp)[..., :L].transpose(0, 3, 1, 2).reshape(B, S, Hg)
    return o, lse


def setup_inputs(seed: int = 0) -> dict:
    key = jax.random.key(seed)
    ks = jax.random.split(key, 16)

    def nrm(k, shape, fan):
        return jax.random.normal(k, shape, jnp.float32) * (fan ** -0.5)

    def gain(k, shape):
        return 1.0 + 0.01 * jax.random.normal(k, shape, jnp.float32)

    return {
        "x": jax.random.normal(ks[0], (BATCH, SEQ, D_MODEL), jnp.float32),
        "norm_mix": gain(ks[1], (DEPTH, D_MODEL)),
        "w_in": nrm(ks[2], (DEPTH, D_MODEL, D_IN), D_MODEL),
        "pe_k": 0.1 * jax.random.normal(ks[3], (DEPTH, CMP_BLOCK, HEAD_DIM), jnp.float32),
        "pe_v": 0.1 * jax.random.normal(ks[4], (DEPTH, CMP_BLOCK, HEAD_DIM), jnp.float32),
        "w_ck1": nrm(ks[5], (DEPTH, CMP_BLOCK * HEAD_DIM, CMP_HIDDEN), CMP_BLOCK * HEAD_DIM),
        "w_ck2": nrm(ks[6], (DEPTH, CMP_HIDDEN, HEAD_DIM), CMP_HIDDEN),
        "w_cv1": nrm(ks[7], (DEPTH, CMP_BLOCK * HEAD_DIM, CMP_HIDDEN), CMP_BLOCK * HEAD_DIM),
        "w_cv2": nrm(ks[8], (DEPTH, CMP_HIDDEN, HEAD_DIM), CMP_HIDDEN),
        "w_up_nsa": nrm(ks[9], (DEPTH, NSA_OUT, D_MODEL), NSA_OUT),
        "w_up_dil": nrm(ks[10], (DEPTH, DIL_OUT, D_MODEL), DIL_OUT),
        "w_out": nrm(ks[11], (DEPTH, D_MODEL, D_MODEL), D_MODEL),
        "norm_ffn": gain(ks[12], (DEPTH, D_MODEL)),
        "w_ffn_in": nrm(ks[13], (DEPTH, D_MODEL, 2 * D_FF), D_MODEL),
        "w_ffn_out": nrm(ks[14], (DEPTH, D_FF, D_MODEL), D_FF),
        "norm_final": gain(ks[15], (D_MODEL,)),
    }


def reference(x, norm_mix, w_in, pe_k, pe_v, w_ck1, w_ck2, w_cv1, w_cv2, w_up_nsa, w_up_dil,
              w_out, norm_ffn, w_ffn_in, w_ffn_out, norm_final):
    B, S, D = x.shape
    slopes_nsa, slopes_dil = alibi_slopes()
    for l in range(DEPTH):
        h = rmsnorm(x, norm_mix[l])
        proj = h @ w_in[l]
        q_a, kv_a, g_a, qkv_b, g_m = jnp.split(proj, IN_SPLITS, axis=-1)
        q_a = q_a.reshape(B, S, NSA_HEADS, HEAD_DIM)
        kv_a = kv_a.reshape(B, S, 3, 2, NSA_KV_GROUPS, HEAD_DIM)
        g_a = jax.nn.sigmoid(g_a).reshape(B, S, NSA_HEADS, 3)
        y_a = nsa_attention(q_a, kv_a, g_a, pe_k[l], pe_v[l], w_ck1[l], w_ck2[l],
                            w_cv1[l], w_cv2[l], slopes_nsa)
        qkv_b = qkv_b.reshape(B, S, 3, len(DIL_GROUPS), DIL_HPG, HEAD_DIM)
        outs, lses = [], []
        for gidx, (window, dil) in enumerate(DIL_GROUPS):
            o_g, lse_g = dilated_group(qkv_b[:, :, 0, gidx], qkv_b[:, :, 1, gidx],
                                       qkv_b[:, :, 2, gidx], window, dil, slopes_dil[gidx])
            outs.append(o_g)
            lses.append(lse_g)
        wts = jax.nn.softmax(jnp.stack(lses, axis=-1), axis=-1)
        y_b = jnp.sum(jnp.stack(outs, axis=-1) * wts[..., None, :], axis=-1).reshape(B, S, DIL_OUT)
        gate_a, gate_b = jnp.split(jax.nn.sigmoid(g_m), 2, axis=-1)
        merged = gate_a * (y_a @ w_up_nsa[l]) + gate_b * (y_b @ w_up_dil[l])
        x = x + (merged @ w_out[l]).astype(x.dtype)
        h = rmsnorm(x, norm_ffn[l])
        gt, up = jnp.split(h @ w_ffn_in[l], 2, axis=-1)
        x = x + ((jax.nn.silu(gt) * up) @ w_ffn_out[l]).astype(x.dtype)
    return rmsnorm(x, norm_final)
```

```python
import functools

import numpy as np
import jax
import jax.numpy as jnp
from jax import lax
from jax.experimental import pallas as pl
from jax.experimental.pallas import tpu as pltpu

F32 = jnp.float32
BF16 = jnp.bfloat16

HEAD_DIM = 64
NSA_HEADS = 8
NSA_KV_GROUPS = 2
NSA_HPG = NSA_HEADS // NSA_KV_GROUPS
CMP_BLOCK = 32
CMP_STRIDE = 16
CMP_HIDDEN = 256
SEL_BLOCK = 64
SEL_TOPK = 16
NSA_WINDOW = 512
Q_BLOCK = 128
FORCE_SCORE = 1e6
DIL_GROUPS = ((128, 1), (512, 4), (2048, 16))
DIL_HPG = 4
DIL_HEADS = DIL_HPG * len(DIL_GROUPS)
DIL_BLOCK = 128
N_ALIBI = NSA_HEADS + DIL_HEADS
EPS = 1e-6
NEG_INF = -1e30
TINY = 1e-30

LANES = 128
GROUP_W = NSA_HPG * HEAD_DIM
DIL_SUPER = 2048
VMEM_LIMIT = 48 * 1024 * 1024


def _dot(a, b):
    return jnp.dot(a, b, preferred_element_type=F32)


def _nt_dot(a, b):
    return lax.dot_general(a, b, (((1,), (1,)), ((), ())), preferred_element_type=F32)


def _iota(shape, dim):
    return lax.broadcasted_iota(jnp.int32, shape, dim)


def _rms(x, g):
    ms = jnp.mean(x * x, axis=-1, keepdims=True)
    return (x * lax.rsqrt(ms + EPS)) * g


def _params(sem):
    return pltpu.CompilerParams(dimension_semantics=sem, vmem_limit_bytes=VMEM_LIMIT)


QA_W, KVA_W, QB_W, KVB_W, GA_W = 512, 768, 768, 1536, 128
PROJ_COLS = QA_W + KVA_W + QB_W + KVB_W + GA_W


def _proj_kernel(x_ref, g_ref, w_ref, qa_ref, kva_ref, qb_ref, kvb_ref, ga_ref):
    y = _rms(x_ref[...], g_ref[...]).astype(BF16)
    c = 0
    qa_ref[...] = _dot(y, w_ref[:, c:c + QA_W]).astype(BF16)
    c += QA_W
    kva_ref[...] = _dot(y, w_ref[:, c:c + KVA_W]).astype(BF16)
    c += KVA_W
    for j in range(QB_W // LANES):
        qb_ref[j] = _dot(y, w_ref[:, c:c + LANES])
        c += LANES
    for j in range(KVB_W // LANES):
        kvb_ref[j] = _dot(y, w_ref[:, c:c + LANES])
        c += LANES
    ga_ref[...] = jax.nn.sigmoid(_dot(y, w_ref[:, c:c + GA_W]))


def _proj(x2, g, w, tm):
    n, d = x2.shape
    row = lambda i: (i, 0)
    fixed = lambda i: (0, 0)
    return pl.pallas_call(
        _proj_kernel,
        grid=(n // tm,),
        in_specs=[pl.BlockSpec((tm, d), row), pl.BlockSpec((1, d), fixed),
                  pl.BlockSpec((d, PROJ_COLS), fixed)],
        out_specs=[pl.BlockSpec((tm, QA_W), row), pl.BlockSpec((tm, KVA_W), row),
                   pl.BlockSpec((QB_W // LANES, tm, LANES), lambda i: (0, i, 0)),
                   pl.BlockSpec((KVB_W // LANES, tm, LANES), lambda i: (0, i, 0)),
                   pl.BlockSpec((tm, GA_W), row)],
        out_shape=[jax.ShapeDtypeStruct((n, QA_W), BF16), jax.ShapeDtypeStruct((n, KVA_W), BF16),
                   jax.ShapeDtypeStruct((QB_W // LANES, n, LANES), F32),
                   jax.ShapeDtypeStruct((KVB_W // LANES, n, LANES), F32),
                   jax.ShapeDtypeStruct((n, GA_W), F32)],
        compiler_params=_params(("parallel",)),
        name="proj",
    )(x2, g, w)


def _compress_kernel(x_ref, pe_ref, wa_ref, wb_ref, w2_ref, o_ref):
    xf = x_ref[0, 0].astype(F32)
    nch = xf.shape[0]
    xa = (xf + pe_ref[0:1, :]).astype(BF16)
    xb = (xf + pe_ref[1:2, :]).astype(BF16)
    ya = _dot(xa, wa_ref[...])
    yb = _dot(xb, wb_ref[...])
    h = ya + pltpu.roll(yb, nch - 1, 0)
    a = (h * jax.nn.sigmoid(h)).astype(BF16)
    o_ref[0, 0] = _dot(a, w2_ref[...]).astype(BF16)


def _compress(xc, pe, wa, wb, w2):
    b, g, nch, f = xc.shape
    hid = wa.shape[1]
    return pl.pallas_call(
        _compress_kernel,
        grid=(b, g),
        in_specs=[pl.BlockSpec((1, 1, nch, f), lambda i, j: (i, j, 0, 0)),
                  pl.BlockSpec((8, f), lambda i, j: (0, 0)),
                  pl.BlockSpec((f, hid), lambda i, j: (0, 0)),
                  pl.BlockSpec((f, hid), lambda i, j: (0, 0)),
                  pl.BlockSpec((hid, LANES), lambda i, j: (0, 0))],
        out_specs=pl.BlockSpec((1, 1, nch, LANES), lambda i, j: (i, j, 0, 0)),
        out_shape=jax.ShapeDtypeStruct((b, g, nch, LANES), BF16),
        compiler_params=_params(("parallel", "parallel")),
        name="compress",
    )(xc, pe, wa, wb, w2)


def _stack_heads(q):
    r = _iota((GROUP_W, LANES), 0)
    c = _iota((GROUP_W, LANES), 1)
    parts = []
    for h in range(NSA_HPG):
        pick = ((r == c + HEAD_DIM * h) & (c < HEAD_DIM)).astype(BF16)
        parts.append(_dot(q, pick).astype(BF16))
    return jnp.concatenate(parts, axis=0)


def _unstack_heads(o4):
    lane = _iota((Q_BLOCK, LANES), 1)
    o = [o4[Q_BLOCK * h:Q_BLOCK * (h + 1)] for h in range(NSA_HPG)]
    c0 = jnp.where(lane < HEAD_DIM, pltpu.roll(o[0], HEAD_DIM, 1), o[1])
    c1 = jnp.where(lane < HEAD_DIM, pltpu.roll(o[2], HEAD_DIM, 1), o[3])
    return jnp.concatenate([c0, c1], axis=1)


def _expand_gate(gates, g, branch):
    r = _iota((LANES, GROUP_W), 0)
    c = _iota((LANES, GROUP_W), 1)
    e = (r == 3 * (NSA_HPG * g + (c >> 6)) + branch).astype(BF16)
    hi = gates.astype(BF16)
    lo = (gates - hi.astype(F32)).astype(BF16)
    return _dot(hi, e) + _dot(lo, e)


def _slope_col(slopes_ref, g):
    hrow = _iota((NSA_HPG * Q_BLOCK, 1), 0) >> 7
    col = jnp.full((NSA_HPG * Q_BLOCK, 1), slopes_ref[NSA_HPG * g + NSA_HPG - 1], F32)
    for h in range(NSA_HPG - 2, -1, -1):
        col = jnp.where(hrow == h, slopes_ref[NSA_HPG * g + h], col)
    return col


def _cmp_kernel(slopes_ref, q_ref, kvc_ref, ga_ref, ov_ref, yc_ref, sel_ref, fl_ref, *, n_top):
    g = pl.program_id(1)
    i = pl.program_id(2)
    rows = NSA_HPG * Q_BLOCK
    q4 = _stack_heads(q_ref[0])
    kv = kvc_ref[0, 0]
    ncp = kv.shape[0]
    s = _nt_dot(q4, kv)
    qi = _iota((rows, ncp), 0) & (Q_BLOCK - 1)
    n = _iota((rows, ncp), 1)
    dc = i * Q_BLOCK + qi - (n * CMP_STRIDE + CMP_BLOCK - 1)
    mask = dc >= 0
    slope = _slope_col(slopes_ref, g)
    sb = jnp.where(mask, s - slope * dc.astype(F32), NEG_INF)
    m = jnp.max(sb, axis=-1, keepdims=True)
    e = jnp.where(mask, jnp.exp(sb - m), 0.0)
    l = jnp.sum(e, axis=-1, keepdims=True)
    pc = (e * (1.0 / jnp.maximum(l, TINY))).astype(BF16)
    oc = _unstack_heads(_dot(pc, kv))
    yc_ref[0] = _expand_gate(ga_ref[0], g, 0) * oc

    ov = ov_ref[...]
    imp = _dot(pc[0:Q_BLOCK], ov)
    for h in range(1, NSA_HPG):
        imp = imp + _dot(pc[Q_BLOCK * h:Q_BLOCK * (h + 1)], ov)

    qrow = _iota((Q_BLOCK, LANES), 0)
    j = _iota((Q_BLOCK, LANES), 1)
    jt = 2 * i + (qrow >> 6)
    valid = j <= jt
    forced = (j == 0) | (j == jt) | (j == jt - 1)
    score = jnp.where(forced, FORCE_SCORE, jnp.where(valid, imp, -1.0))
    st = score.T
    jr = _iota((LANES, Q_BLOCK), 0)
    cur = st
    picked = jnp.zeros((LANES, Q_BLOCK), F32)
    for _ in range(n_top):
        mx = jnp.max(cur, axis=0, keepdims=True)
        first = jnp.min(jnp.where(cur == mx, jr, LANES), axis=0, keepdims=True)
        hit = jr == first
        picked = jnp.where(hit, 1.0, picked)
        cur = jnp.where(hit, -3e38, cur)
    keep = jnp.where(st >= 0.0, picked, 0.0).T
    sel_ref[0, 0] = keep.astype(BF16)

    anyq = jnp.max(keep, axis=0, keepdims=True)
    anyq = jnp.broadcast_to(anyq, (8, LANES)).astype(BF16)
    pair = ((_iota((LANES, LANES), 0) >> 1) == _iota((LANES, LANES), 1)).astype(BF16)
    fl_ref[0] = (_dot(anyq, pair)[0:1] > 0.0).astype(jnp.int32)


def _cmp(slopes, qa, kvc, ga, ov, n_top):
    b, s, _ = qa.shape
    g = NSA_KV_GROUPS
    nq = s // Q_BLOCK
    ncp = kvc.shape[2]
    return pl.pallas_call(
        functools.partial(_cmp_kernel, n_top=n_top),
        grid=(b, g, nq),
        in_specs=[pl.BlockSpec(memory_space=pltpu.SMEM),
                  pl.BlockSpec((1, Q_BLOCK, GROUP_W), lambda bi, gi, i: (bi, i, gi)),
                  pl.BlockSpec((1, 1, ncp, LANES), lambda bi, gi, i: (bi, gi, 0, 0)),
                  pl.BlockSpec((1, Q_BLOCK, LANES), lambda bi, gi, i: (bi, i, 0)),
                  pl.BlockSpec((ncp, LANES), lambda bi, gi, i: (0, 0))],
        out_specs=[pl.BlockSpec((1, Q_BLOCK, GROUP_W), lambda bi, gi, i: (bi, i, gi)),
                   pl.BlockSpec((1, 1, Q_BLOCK, LANES), lambda bi, gi, i: (bi, gi, i, 0)),
                   pl.BlockSpec((1, 1, LANES), lambda bi, gi, i: ((bi * g + gi) * nq + i, 0, 0))],
        out_shape=[jax.ShapeDtypeStruct((b, s, g * GROUP_W), F32),
                   jax.ShapeDtypeStruct((b, g, s, LANES), BF16),
                   jax.ShapeDtypeStruct((b * g * nq, 1, LANES), jnp.int32)],
        compiler_params=_params(("parallel", "parallel", "parallel")),
        name="cmp",
    )(slopes, qa, kvc, ga, ov)


WIN_UNITS = NSA_WINDOW // Q_BLOCK + 1


def _selwin_kernel(pf_ref, slopes_ref, q_ref, kvs_ref, kvw_ref, sel_ref, ga_ref, yc_ref, ya_ref,
                   sbuf, jlist, mrun, lsum, acc, *, nq, nu):
    b = pl.program_id(0)
    g = pl.program_id(1)
    i = pl.program_id(2)
    rows = NSA_HPG * Q_BLOCK
    q4 = _stack_heads(q_ref[0])
    slope = _slope_col(slopes_ref, g)
    qrow = _iota((rows, LANES), 0) & (Q_BLOCK - 1)
    lane = _iota((rows, LANES), 1)
    rel = (qrow - lane).astype(F32)
    sel = sel_ref[0, 0]
    erow = _iota((LANES, LANES), 0)
    ehalf = _iota((LANES, LANES), 1) >> 6

    def finish(count, kv_ref, unit_of):
        m = jnp.max(mrun[...], axis=-1, keepdims=True)
        lsum[...] = jnp.zeros_like(lsum)
        acc[...] = jnp.zeros_like(acc)

        def body(c, carry):
            p = jnp.exp(sbuf[c] - m)
            lsum[...] += p
            u = unit_of(c)
            kvu = kv_ref[0, pl.ds(pl.multiple_of(u * Q_BLOCK, Q_BLOCK), Q_BLOCK), :]
            acc[...] += _dot(p.astype(BF16), kvu)
            return carry

        lax.fori_loop(0, count, body, 0)
        l = jnp.sum(lsum[...], axis=-1, keepdims=True)
        return _unstack_heads(acc[...] * (1.0 / jnp.maximum(l, TINY)))

    base = ((b * NSA_KV_GROUPS + g) * nq + i) * nu
    mrun[...] = jnp.full_like(mrun, NEG_INF)

    def sel_body(jj, cnt):
        def do(cnt):
            kvu = kvs_ref[0, pl.ds(pl.multiple_of(jj * Q_BLOCK, Q_BLOCK), Q_BLOCK), :]
            s = _nt_dot(q4, kvu)
            spread = (erow == 2 * jj + ehalf).astype(BF16)
            chosen = _dot(sel, spread) > 0.5
            dist = rel + ((i - jj) * Q_BLOCK).astype(F32)
            ok = jnp.concatenate([chosen] * NSA_HPG, axis=0) & (dist >= 0.0)
            sb = jnp.where(ok, s - slope * dist, NEG_INF)
            sbuf[cnt] = sb
            mrun[...] = jnp.maximum(mrun[...], sb)
            jlist[cnt] = jj
            return cnt + 1
        return lax.cond(pf_ref[base + jj] != 0, do, lambda c: c, cnt)

    cnt = lax.fori_loop(0, i + 1, sel_body, jnp.int32(0))
    o_s = finish(cnt, kvs_ref, lambda c: jlist[c])

    mrun[...] = jnp.full_like(mrun, NEG_INF)
    for d in range(WIN_UNITS):
        u = i - (WIN_UNITS - 1) + d
        uc = jnp.maximum(u, 0)
        kvu = kvw_ref[0, pl.ds(pl.multiple_of(uc * Q_BLOCK, Q_BLOCK), Q_BLOCK), :]
        s = _nt_dot(q4, kvu)
        dist = rel + float((WIN_UNITS - 1 - d) * Q_BLOCK)
        ok = (dist >= 0.0) & (dist < float(NSA_WINDOW)) & (u >= 0)
        sb = jnp.where(ok, s - slope * dist, NEG_INF)
        sbuf[d] = sb
        mrun[...] = jnp.maximum(mrun[...], sb)
    o_w = finish(WIN_UNITS, kvw_ref, lambda c: jnp.maximum(i - (WIN_UNITS - 1) + c, 0))

    gates = ga_ref[0]
    ya = yc_ref[0] + _expand_gate(gates, g, 1) * o_s + _expand_gate(gates, g, 2) * o_w
    ya_ref[0] = ya.astype(BF16)


def _selwin(pflags, slopes, qa, kva, sel, ga, yc):
    b, s, _ = qa.shape
    g = NSA_KV_GROUPS
    nq = s // Q_BLOCK
    nu = nq
    nbuf = max(nq, WIN_UNITS)
    rows = NSA_HPG * Q_BLOCK
    grid_spec = pltpu.PrefetchScalarGridSpec(
        num_scalar_prefetch=1,
        grid=(b, g, nq),
        in_specs=[pl.BlockSpec(memory_space=pltpu.SMEM),
                  pl.BlockSpec((1, Q_BLOCK, GROUP_W), lambda bi, gi, i, pf: (bi, i, gi)),
                  pl.BlockSpec((1, s, LANES), lambda bi, gi, i, pf: (bi, 0, 2 + gi)),
                  pl.BlockSpec((1, s, LANES), lambda bi, gi, i, pf: (bi, 0, 4 + gi)),
                  pl.BlockSpec((1, 1, Q_BLOCK, LANES), lambda bi, gi, i, pf: (bi, gi, i, 0)),
                  pl.BlockSpec((1, Q_BLOCK, LANES), lambda bi, gi, i, pf: (bi, i, 0)),
                  pl.BlockSpec((1, Q_BLOCK, GROUP_W), lambda bi, gi, i, pf: (bi, i, gi))],
        out_specs=pl.BlockSpec((1, Q_BLOCK, GROUP_W), lambda bi, gi, i, pf: (bi, i, gi)),
        scratch_shapes=[pltpu.VMEM((nbuf, rows, LANES), F32),
                        pltpu.SMEM((nbuf,), jnp.int32),
                        pltpu.VMEM((rows, LANES), F32),
                        pltpu.VMEM((rows, LANES), F32),
                        pltpu.VMEM((rows, LANES), F32)],
    )
    return pl.pallas_call(
        functools.partial(_selwin_kernel, nq=nq, nu=nu),
        grid_spec=grid_spec,
        out_shape=jax.ShapeDtypeStruct((b, s, g * GROUP_W), BF16),
        compiler_params=_params(("parallel", "parallel", "arbitrary")),
        name="selwin",
    )(pflags, slopes, qa, kva, kva, sel, ga, yc)


def _dil_group(k, dil, sup, slopes_ref, q_ref, kvc_ref, kvp_ref, mx_ref, w_ref, y_ref):
    c = DIL_BLOCK
    nblk = sup // (c * dil)
    steps = DIL_GROUPS[k][0] // dil
    has_halo = pl.program_id(1) > 0
    nch = DIL_HPG * HEAD_DIM // LANES
    upper = _iota((c, LANES), 1) >= HEAD_DIM
    qi = _iota((c, 2 * c), 0)
    ki = _iota((c, 2 * c), 1)
    delta = qi + c - ki
    in_band = (delta >= 0) & (delta <= steps)
    dist = (delta * dil).astype(F32)

    def body(idx, carry):
        r = idx // nblk
        n = idx - r * nblk
        start = r + dil * c * n
        rows = pl.ds(start, c, stride=dil)
        rows_prev = pl.ds(jnp.maximum(start - dil * c, r), c, stride=dil)
        rows_halo = pl.ds(r + dil * c * (nblk - 1), c, stride=dil)
        valid = in_band & ((ki >= c) | (n >= 1) | has_halo)

        def band(ch):
            prev = jnp.where(n >= 1, kvc_ref[ch, 0, rows_prev, :], kvp_ref[ch, 0, rows_halo, :])
            return jnp.concatenate([prev, kvc_ref[ch, 0, rows, :]], axis=0).astype(BF16)

        for ch in range(nch):
            qb = q_ref[ch, 0, rows, :].astype(BF16)
            kband = band(ch)
            vband = band(nch + ch)
            o = None
            lse = None
            for half in range(2):
                mine = upper if half else jnp.logical_not(upper)
                s = _nt_dot(jnp.where(mine, qb, jnp.zeros_like(qb)), kband)
                slope = slopes_ref[DIL_HPG * k + 2 * ch + half]
                sb = jnp.where(valid, s - slope * dist, NEG_INF)
                m = jnp.max(sb, axis=-1, keepdims=True)
                e = jnp.where(valid, jnp.exp(sb - m), 0.0)
                lt = jnp.maximum(jnp.sum(e, axis=-1, keepdims=True), TINY)
                oh = _dot((e * (1.0 / lt)).astype(BF16), vband)
                lh = jnp.broadcast_to(m + jnp.log(lt), (c, LANES))
                o = oh if o is None else jnp.where(mine, oh, o)
                lse = lh if lse is None else jnp.where(mine, lh, lse)
            if k == 0:
                mx_ref[ch, rows, :] = lse
                w_ref[ch, rows, :] = jnp.ones_like(lse)
                y_ref[ch, rows, :] = o
            else:
                mx_old = mx_ref[ch, rows, :]
                mx_new = jnp.maximum(mx_old, lse)
                a = jnp.exp(mx_old - mx_new)
                bnew = jnp.exp(lse - mx_new)
                mx_ref[ch, rows, :] = mx_new
                w_ref[ch, rows, :] = w_ref[ch, rows, :] * a + bnew
                y_ref[ch, rows, :] = y_ref[ch, rows, :] * a + bnew * o
        return carry

    lax.fori_loop(0, sup // c, body, 0)


def _dil_kernel(slopes_ref, q_ref, kvc_ref, kvp_ref, yb_ref, mx_ref, w_ref, y_ref, *, sup):
    gi = pl.program_id(2)
    for k, (_, dil) in enumerate(DIL_GROUPS):
        @pl.when(gi == k)
        def _(k=k, dil=dil):
            _dil_group(k, dil, sup, slopes_ref, q_ref, kvc_ref, kvp_ref, mx_ref, w_ref, y_ref)

    @pl.when(gi == len(DIL_GROUPS) - 1)
    def _():
        for ch in range(DIL_HPG * HEAD_DIM // LANES):
            yb_ref[0, :, ch * LANES:(ch + 1) * LANES] = (y_ref[ch] * (1.0 / w_ref[ch])).astype(BF16)


def _dil(slopes, qb, kvb, sup):
    _, b, s, _ = qb.shape
    hw = DIL_HPG * HEAD_DIM
    nch = hw // LANES
    ng = len(DIL_GROUPS)
    return pl.pallas_call(
        functools.partial(_dil_kernel, sup=sup),
        grid=(b, s // sup, ng),
        in_specs=[pl.BlockSpec(memory_space=pltpu.SMEM),
                  pl.BlockSpec((nch, 1, sup, LANES), lambda bi, si, gi: (gi, bi, si, 0)),
                  pl.BlockSpec((2 * nch, 1, sup, LANES), lambda bi, si, gi: (gi, bi, si, 0)),
                  pl.BlockSpec((2 * nch, 1, sup, LANES),
                               lambda bi, si, gi: (gi, bi, jnp.maximum(si - 1, 0), 0))],
        out_specs=pl.BlockSpec((1, sup, hw), lambda bi, si, gi: (bi, si, 0)),
        out_shape=jax.ShapeDtypeStruct((b, s, hw), BF16),
        scratch_shapes=[pltpu.VMEM((nch, sup, LANES), F32), pltpu.VMEM((nch, sup, LANES), F32),
                        pltpu.VMEM((nch, sup, LANES), F32)],
        compiler_params=_params(("parallel", "parallel", "arbitrary")),
        name="dil",
    )(slopes, qb, kvb, kvb)


def _merge_kernel(x_ref, g_ref, ya_ref, yb_ref, wg_ref, ua_ref, ub_ref, wo_ref, o_ref):
    x = x_ref[...]
    d = x.shape[1]
    hn = _rms(x, g_ref[...]).astype(BF16)
    gate_a = jax.nn.sigmoid(_dot(hn, wg_ref[:, 0:d]))
    gate_b = jax.nn.sigmoid(_dot(hn, wg_ref[:, d:2 * d]))
    merged = gate_a * _dot(ya_ref[...], ua_ref[...]) + gate_b * _dot(yb_ref[...], ub_ref[...])
    o_ref[...] = x + _dot(merged.astype(BF16), wo_ref[...])


def _merge(x2, g, ya, yb, wg, ua, ub, wo, tm):
    n, d = x2.shape
    row = lambda i: (i, 0)
    fixed = lambda i: (0, 0)
    return pl.pallas_call(
        _merge_kernel,
        grid=(n // tm,),
        in_specs=[pl.BlockSpec((tm, d), row), pl.BlockSpec((1, d), fixed),
                  pl.BlockSpec((tm, ya.shape[1]), row), pl.BlockSpec((tm, yb.shape[1]), row),
                  pl.BlockSpec(wg.shape, fixed), pl.BlockSpec(ua.shape, fixed),
                  pl.BlockSpec(ub.shape, fixed), pl.BlockSpec(wo.shape, fixed)],
        out_specs=pl.BlockSpec((tm, d), row),
        out_shape=jax.ShapeDtypeStruct((n, d), F32),
        compiler_params=_params(("parallel",)),
        name="merge",
    )(x2, g, ya, yb, wg, ua, ub, wo)


def _ffn_kernel(x_ref, g_ref, wg_ref, wu_ref, wo_ref, gf_ref, o_ref, hn_ref, acc_ref, *, final_norm):
    c = pl.program_id(1)

    @pl.when(c == 0)
    def _():
        hn_ref[...] = _rms(x_ref[...], g_ref[...]).astype(BF16)
        acc_ref[...] = jnp.zeros_like(acc_ref)

    hn = hn_ref[...]
    gt = _dot(hn, wg_ref[...])
    up = _dot(hn, wu_ref[...])
    act = ((gt * jax.nn.sigmoid(gt)) * up).astype(BF16)
    acc_ref[...] += _dot(act, wo_ref[...])

    @pl.when(c == pl.num_programs(1) - 1)
    def _():
        y = x_ref[...] + acc_ref[...]
        if final_norm:
            y = _rms(y, gf_ref[...])
        o_ref[...] = y


def _ffn(x2, g, w_in, w_out, gf, tm, tf, final_norm):
    n, d = x2.shape
    dff = w_out.shape[0]
    nc = dff // tf
    return pl.pallas_call(
        functools.partial(_ffn_kernel, final_norm=final_norm),
        grid=(n // tm, nc),
        in_specs=[pl.BlockSpec((tm, d), lambda i, c: (i, 0)),
                  pl.BlockSpec((1, d), lambda i, c: (0, 0)),
                  pl.BlockSpec((d, tf), lambda i, c: (0, c)),
                  pl.BlockSpec((d, tf), lambda i, c: (0, nc + c)),
                  pl.BlockSpec((tf, d), lambda i, c: (c, 0)),
                  pl.BlockSpec((1, d), lambda i, c: (0, 0))],
        out_specs=pl.BlockSpec((tm, d), lambda i, c: (i, 0)),
        out_shape=jax.ShapeDtypeStruct((n, d), F32),
        scratch_shapes=[pltpu.VMEM((tm, d), BF16), pltpu.VMEM((tm, d), F32)],
        compiler_params=_params(("parallel", "arbitrary")),
        name="ffn",
    )(x2, g, w_in, w_in, w_out, gf)


def _alibi_slopes():
    k = jnp.arange(1, N_ALIBI + 1, dtype=F32)
    s = jnp.exp2(-8.0 * k / N_ALIBI)
    nsa = s[2 * DIL_HPG:2 * DIL_HPG + NSA_HEADS]
    dil = jnp.concatenate([s[:2 * DIL_HPG], s[2 * DIL_HPG + NSA_HEADS:]])
    return nsa, dil


def _overlap(seq, ncp):
    n_c = (seq - CMP_BLOCK) // CMP_STRIDE + 1
    n_s = seq // SEL_BLOCK
    c_start = np.arange(n_c) * CMP_STRIDE
    s_start = np.arange(n_s) * SEL_BLOCK
    ov = np.clip(np.minimum(c_start[:, None] + CMP_BLOCK, s_start[None, :] + SEL_BLOCK)
                 - np.maximum(c_start[:, None], s_start[None, :]), 0, None).astype(np.float32) / CMP_BLOCK
    out = np.zeros((ncp, LANES), np.float32)
    out[:n_c, :n_s] = ov
    return jnp.asarray(out, BF16)


def _proj_weights(w_in):
    dep, d, _ = w_in.shape
    scale = HEAD_DIM ** -0.5
    c0 = NSA_HEADS * HEAD_DIM
    c1 = c0 + 3 * 2 * NSA_KV_GROUPS * HEAD_DIM
    c2 = c1 + NSA_HEADS * 3
    c3 = c2 + 3 * DIL_HEADS * HEAD_DIM
    qa = w_in[..., :c0] * scale
    kva = w_in[..., c0:c1].reshape(dep, d, 3, 2, NSA_KV_GROUPS, HEAD_DIM)
    kva = kva.transpose(0, 1, 2, 4, 3, 5).reshape(dep, d, KVA_W)
    ga = jnp.pad(w_in[..., c1:c2], ((0, 0), (0, 0), (0, GA_W - (c2 - c1))))
    qkvb = w_in[..., c2:c3].reshape(dep, d, 3, len(DIL_GROUPS), DIL_HPG * HEAD_DIM)
    qb = (qkvb[:, :, 0] * scale).reshape(dep, d, QB_W)
    kvb = qkvb[:, :, 1:3].transpose(0, 1, 3, 2, 4).reshape(dep, d, KVB_W)
    w_a = jnp.concatenate([qa, kva, qb, kvb, ga], axis=-1).astype(BF16)
    return w_a, w_in[..., c3:].astype(BF16)


def _compress_weights(pe_k, pe_v, w_ck1, w_ck2, w_cv1, w_cv2):
    dep = pe_k.shape[0]
    half = CMP_BLOCK // 2
    ck = w_ck1.reshape(dep, CMP_BLOCK, HEAD_DIM, CMP_HIDDEN)
    cv = w_cv1.reshape(dep, CMP_BLOCK, HEAD_DIM, CMP_HIDDEN)
    z = jnp.zeros_like(ck)
    w = jnp.concatenate([jnp.concatenate([ck, z], axis=-1),
                         jnp.concatenate([z, cv], axis=-1)], axis=2)
    feat = half * 2 * HEAD_DIM
    wa = w[:, :half].reshape(dep, feat, 2 * CMP_HIDDEN).astype(BF16)
    wb = w[:, half:].reshape(dep, feat, 2 * CMP_HIDDEN).astype(BF16)
    pe = jnp.concatenate([pe_k, pe_v], axis=-1)
    pe2 = jnp.stack([pe[:, :half].reshape(dep, feat), pe[:, half:].reshape(dep, feat)], axis=1)
    pe2 = jnp.pad(pe2, ((0, 0), (0, 6), (0, 0)))
    z2 = jnp.zeros_like(w_ck2)
    w2 = jnp.concatenate([jnp.concatenate([w_ck2, z2], axis=-1),
                          jnp.concatenate([z2, w_cv2], axis=-1)], axis=1).astype(BF16)
    return pe2, wa, wb, w2


def kernel(x, norm_mix, w_in, pe_k, pe_v, w_ck1, w_ck2, w_cv1, w_cv2, w_up_nsa, w_up_dil,
           w_out, norm_ffn, w_ffn_in, w_ffn_out, norm_final):
    b, s, d = x.shape
    depth = w_in.shape[0]
    n = b * s
    nq = s // Q_BLOCK
    nch = s // CMP_STRIDE
    n_s = s // SEL_BLOCK
    assert s % DIL_SUPER == 0 and n_s <= LANES
    n_top = min(SEL_TOPK, n_s)
    tm = 512
    tm_ffn = 1024 if n % 1024 == 0 else 512
    tf = 256

    slopes_nsa, slopes_dil = _alibi_slopes()
    ov = _overlap(s, nch)
    w_a, w_gm = _proj_weights(w_in)
    pe2, wa, wb, w2 = _compress_weights(pe_k, pe_v, w_ck1, w_ck2, w_cv1, w_cv2)
    ua = w_up_nsa.astype(BF16)
    ub = w_up_dil.astype(BF16)
    wo = w_out.astype(BF16)
    wfi = w_ffn_in.astype(BF16)
    wfo = w_ffn_out.astype(BF16)
    gf = norm_final.reshape(1, d)

    x2 = x.reshape(n, d)
    for l in range(depth):
        g_mix = norm_mix[l].reshape(1, d)
        qa, kva, qb, kvb, ga = _proj(x2, g_mix, w_a[l], tm)
        qa = qa.reshape(b, s, QA_W)
        kva = kva.reshape(b, s, KVA_W)
        ga = ga.reshape(b, s, GA_W)
        xc = kva[:, :, :NSA_KV_GROUPS * LANES].reshape(b, nch, CMP_STRIDE, NSA_KV_GROUPS, LANES)
        xc = xc.transpose(0, 3, 1, 2, 4).reshape(b, NSA_KV_GROUPS, nch, CMP_STRIDE * LANES)
        kvc = _compress(xc, pe2[l], wa[l], wb[l], w2[l])
        yc, sel, flags = _cmp(slopes_nsa, qa, kvc, ga, ov, n_top)
        pflags = flags.reshape(b * NSA_KV_GROUPS * nq, LANES)[:, :nq].reshape(-1)
        ya = _selwin(pflags, slopes_nsa, qa, kva, sel, ga, yc)
        yb = _dil(slopes_dil, qb.reshape(-1, b, s, LANES), kvb.reshape(-1, b, s, LANES), DIL_SUPER)
        x2 = _merge(x2, g_mix, ya.reshape(n, QA_W), yb.reshape(n, DIL_HPG * HEAD_DIM),
                    w_gm[l], ua[l], ub[l], wo[l], tm)
        x2 = _ffn(x2, norm_ffn[l].reshape(1, d), wfi[l], wfo[l], gf, tm_ffn, tf,
                  final_norm=(l == depth - 1))
    return x2.reshape(b, s, d)
```

```python
import functools

import numpy as np
import jax
import jax.numpy as jnp
from jax import lax
from jax.experimental import pallas as pl
from jax.experimental.pallas import tpu as pltpu

F32 = jnp.float32
BF16 = jnp.bfloat16

HEAD_DIM = 64
NSA_HEADS = 8
NSA_KV_GROUPS = 2
NSA_HPG = NSA_HEADS // NSA_KV_GROUPS
CMP_BLOCK = 32
CMP_STRIDE = 16
CMP_HIDDEN = 256
SEL_BLOCK = 64
SEL_TOPK = 16
NSA_WINDOW = 512
Q_BLOCK = 128
FORCE_SCORE = 1e6
DIL_GROUPS = ((128, 1), (512, 4), (2048, 16))
DIL_HPG = 4
DIL_HEADS = DIL_HPG * len(DIL_GROUPS)
DIL_BLOCK = 128
N_ALIBI = NSA_HEADS + DIL_HEADS
EPS = 1e-6
NEG_INF = -1e30
TINY = 1e-30

LANES = 128
GROUP_W = NSA_HPG * HEAD_DIM
DIL_SUPER = 2048
VMEM_LIMIT = 48 * 1024 * 1024


def _dot(a, b):
    return jnp.dot(a, b, preferred_element_type=F32)


def _nt_dot(a, b):
    return lax.dot_general(a, b, (((1,), (1,)), ((), ())), preferred_element_type=F32)


def _iota(shape, dim):
    return lax.broadcasted_iota(jnp.int32, shape, dim)


def _rms(x, g):
    ms = jnp.mean(x * x, axis=-1, keepdims=True)
    return (x * lax.rsqrt(ms + EPS)) * g


def _params(sem):
    return pltpu.CompilerParams(dimension_semantics=sem, vmem_limit_bytes=VMEM_LIMIT)


QA_W, KVA_W, QB_W, KVB_W, GA_W = 512, 768, 768, 1536, 128
PROJ_COLS = QA_W + KVA_W + QB_W + KVB_W + GA_W


def _proj_kernel(x_ref, g_ref, w_ref, qa_ref, kva_ref, qb_ref, kvb_ref, ga_ref):
    y = _rms(x_ref[...], g_ref[...]).astype(BF16)
    c = 0
    qa_ref[...] = _dot(y, w_ref[:, c:c + QA_W]).astype(BF16)
    c += QA_W
    kva_ref[...] = _dot(y, w_ref[:, c:c + KVA_W]).astype(BF16)
    c += KVA_W
    for j in range(QB_W // LANES):
        qb_ref[j] = _dot(y, w_ref[:, c:c + LANES])
        c += LANES
    for j in range(KVB_W // LANES):
        kvb_ref[j] = _dot(y, w_ref[:, c:c + LANES])
        c += LANES
    ga_ref[...] = jax.nn.sigmoid(_dot(y, w_ref[:, c:c + GA_W]))


def _proj(x2, g, w, tm):
    n, d = x2.shape
    row = lambda i: (i, 0)
    fixed = lambda i: (0, 0)
    return pl.pallas_call(
        _proj_kernel,
        grid=(n // tm,),
        in_specs=[pl.BlockSpec((tm, d), row), pl.BlockSpec((1, d), fixed),
                  pl.BlockSpec((d, PROJ_COLS), fixed)],
        out_specs=[pl.BlockSpec((tm, QA_W), row), pl.BlockSpec((tm, KVA_W), row),
                   pl.BlockSpec((QB_W // LANES, tm, LANES), lambda i: (0, i, 0)),
                   pl.BlockSpec((KVB_W // LANES, tm, LANES), lambda i: (0, i, 0)),
                   pl.BlockSpec((tm, GA_W), row)],
        out_shape=[jax.ShapeDtypeStruct((n, QA_W), BF16), jax.ShapeDtypeStruct((n, KVA_W), BF16),
                   jax.ShapeDtypeStruct((QB_W // LANES, n, LANES), F32),
                   jax.ShapeDtypeStruct((KVB_W // LANES, n, LANES), F32),
                   jax.ShapeDtypeStruct((n, GA_W), F32)],
        compiler_params=_params(("parallel",)),
        name="proj",
    )(x2, g, w)


def _compress_kernel(x_ref, pe_ref, wa_ref, wb_ref, w2_ref, o_ref):
    xf = x_ref[0, 0].astype(F32)
    nch = xf.shape[0]
    xa = (xf + pe_ref[0:1, :]).astype(BF16)
    xb = (xf + pe_ref[1:2, :]).astype(BF16)
    ya = _dot(xa, wa_ref[...])
    yb = _dot(xb, wb_ref[...])
    h = ya + pltpu.roll(yb, nch - 1, 0)
    a = (h * jax.nn.sigmoid(h)).astype(BF16)
    o_ref[0, 0] = _dot(a, w2_ref[...]).astype(BF16)


def _compress(xc, pe, wa, wb, w2):
    b, g, nch, f = xc.shape
    hid = wa.shape[1]
    return pl.pallas_call(
        _compress_kernel,
        grid=(b, g),
        in_specs=[pl.BlockSpec((1, 1, nch, f), lambda i, j: (i, j, 0, 0)),
                  pl.BlockSpec((8, f), lambda i, j: (0, 0)),
                  pl.BlockSpec((f, hid), lambda i, j: (0, 0)),
                  pl.BlockSpec((f, hid), lambda i, j: (0, 0)),
                  pl.BlockSpec((hid, LANES), lambda i, j: (0, 0))],
        out_specs=pl.BlockSpec((1, 1, nch, LANES), lambda i, j: (i, j, 0, 0)),
        out_shape=jax.ShapeDtypeStruct((b, g, nch, LANES), BF16),
        compiler_params=_params(("parallel", "parallel")),
        name="compress",
    )(xc, pe, wa, wb, w2)


def _head_pick():
    r = np.arange(GROUP_W)[None, :, None]
    c = np.arange(LANES)[None, None, :]
    h = np.arange(NSA_HPG)[:, None, None]
    return jnp.asarray((r == c + HEAD_DIM * h) & (c < HEAD_DIM), BF16)


def _gate_spread():
    g = np.arange(NSA_KV_GROUPS)[:, None, None, None]
    br = np.arange(3)[None, :, None, None]
    r = np.arange(LANES)[None, None, :, None]
    c = np.arange(GROUP_W)[None, None, None, :]
    return jnp.asarray(r == 3 * (NSA_HPG * g + c // HEAD_DIM) + br, BF16)


def _pair_matrix():
    r = np.arange(LANES)[:, None]
    c = np.arange(LANES)[None, :]
    return jnp.asarray(r // 2 == c, BF16)


def _stack_heads(q, pick_ref):
    return jnp.concatenate([_dot(q, pick_ref[h]).astype(BF16) for h in range(NSA_HPG)], axis=0)


def _unstack_heads(o4):
    lane = _iota((Q_BLOCK, LANES), 1)
    o = [o4[Q_BLOCK * h:Q_BLOCK * (h + 1)] for h in range(NSA_HPG)]
    c0 = jnp.where(lane < HEAD_DIM, pltpu.roll(o[0], HEAD_DIM, 1), o[1])
    c1 = jnp.where(lane < HEAD_DIM, pltpu.roll(o[2], HEAD_DIM, 1), o[3])
    return jnp.concatenate([c0, c1], axis=1)


def _expand_gate(gates, spread):
    hi = gates.astype(BF16)
    lo = (gates - hi.astype(F32)).astype(BF16)
    return _dot(hi, spread) + _dot(lo, spread)


def _ones_in_k_lanes(kv):
    lane = _iota(kv.shape, 1)
    return jnp.where(lane < HEAD_DIM, jnp.ones_like(kv), kv)


def _cmp_kernel(slopes_ref, q_ref, kvc_ref, ga_ref, ov_ref, pick_ref, gexp_ref, pair_ref,
                yc_ref, sel_ref, fl_ref, bias0, cq, rhs, *, n_top):
    g = pl.program_id(1)
    i = pl.program_id(2)
    hq = NSA_HPG
    ncp = cq.shape[1]

    @pl.when(i == 0)
    def _():
        cqv = (CMP_STRIDE * _iota((Q_BLOCK, ncp), 1) + (CMP_BLOCK - 1)) - _iota((Q_BLOCK, ncp), 0)
        cq[...] = cqv
        cf = cqv.astype(F32)
        for h in range(hq):
            bias0[h] = slopes_ref[hq * g + h] * cf
        rhs[:, 0:LANES] = _ones_in_k_lanes(kvc_ref[0, 0])
        rhs[:, LANES:2 * LANES] = ov_ref[...]

    q4 = _stack_heads(q_ref[0], pick_ref)
    s = _nt_dot(q4, kvc_ref[0, 0]).reshape(hq, Q_BLOCK, ncp)
    t0 = i * Q_BLOCK
    neg = jnp.where(cq[...] <= t0, 0.0, NEG_INF)
    tf = t0.astype(F32)
    ps = []
    for h in range(hq):
        sb = (s[h] + bias0[h]) + (neg - slopes_ref[hq * g + h] * tf)
        m = jnp.max(sb, axis=-1, keepdims=True)
        m = jnp.where(m < 0.1 * NEG_INF, -NEG_INF, m)
        ps.append(jnp.exp(sb - m).astype(BF16))
    big = _dot(jnp.concatenate(ps, axis=0), rhs[...])
    big = big * (1.0 / jnp.maximum(big[:, 0:1], TINY))
    oc = _unstack_heads(big[:, 0:LANES])
    yc_ref[0] = _expand_gate(ga_ref[0], gexp_ref[0, 0]) * oc
    imp = big[0:Q_BLOCK, LANES:2 * LANES]
    for h in range(1, hq):
        imp = imp + big[Q_BLOCK * h:Q_BLOCK * (h + 1), LANES:2 * LANES]

    qrow = _iota((Q_BLOCK, LANES), 0)
    j = _iota((Q_BLOCK, LANES), 1)
    jt = 2 * i + (qrow >> 6)
    valid = j <= jt
    forced = (j == 0) | (j == jt) | (j == jt - 1)
    score = jnp.where(forced, FORCE_SCORE, jnp.where(valid, imp, -1.0))
    st = score.T
    jr = _iota((LANES, Q_BLOCK), 0)
    cur = st
    taken = -3e38
    for _ in range(n_top):
        mx = jnp.max(cur, axis=0, keepdims=True)
        first = jnp.min(jnp.where(cur == mx, jr, LANES), axis=0, keepdims=True)
        cur = jnp.where(jr == first, taken, cur)
    keep = jnp.where((cur == taken) & (st >= 0.0), 1.0, 0.0).T
    sel_ref[0, 0] = keep.astype(BF16)

    anyq = jnp.max(keep, axis=0, keepdims=True)
    anyq = jnp.broadcast_to(anyq, (8, LANES)).astype(BF16)
    fl_ref[0] = (_dot(anyq, pair_ref[...])[0:1] > 0.0).astype(jnp.int32)


def _cmp(slopes, qa, kvc, ga, ov, pick, gexp, pair, n_top):
    b, s, _ = qa.shape
    g = NSA_KV_GROUPS
    nq = s // Q_BLOCK
    ncp = kvc.shape[2]
    return pl.pallas_call(
        functools.partial(_cmp_kernel, n_top=n_top),
        grid=(b, g, nq),
        in_specs=[pl.BlockSpec(memory_space=pltpu.SMEM),
                  pl.BlockSpec((1, Q_BLOCK, GROUP_W), lambda bi, gi, i: (bi, i, gi)),
                  pl.BlockSpec((1, 1, ncp, LANES), lambda bi, gi, i: (bi, gi, 0, 0)),
                  pl.BlockSpec((1, Q_BLOCK, LANES), lambda bi, gi, i: (bi, i, 0)),
                  pl.BlockSpec((ncp, LANES), lambda bi, gi, i: (0, 0)),
                  pl.BlockSpec(pick.shape, lambda bi, gi, i: (0, 0, 0)),
                  pl.BlockSpec((1, 3, LANES, GROUP_W), lambda bi, gi, i: (gi, 0, 0, 0)),
                  pl.BlockSpec(pair.shape, lambda bi, gi, i: (0, 0))],
        out_specs=[pl.BlockSpec((1, Q_BLOCK, GROUP_W), lambda bi, gi, i: (bi, i, gi)),
                   pl.BlockSpec((1, 1, Q_BLOCK, LANES), lambda bi, gi, i: (bi, gi, i, 0)),
                   pl.BlockSpec((1, 1, LANES), lambda bi, gi, i: ((bi * g + gi) * nq + i, 0, 0))],
        out_shape=[jax.ShapeDtypeStruct((b, s, g * GROUP_W), F32),
                   jax.ShapeDtypeStruct((b, g, s, LANES), BF16),
                   jax.ShapeDtypeStruct((b * g * nq, 1, LANES), jnp.int32)],
        scratch_shapes=[pltpu.VMEM((NSA_HPG, Q_BLOCK, ncp), F32),
                        pltpu.VMEM((Q_BLOCK, ncp), jnp.int32),
                        pltpu.VMEM((ncp, 2 * LANES), BF16)],
        compiler_params=_params(("parallel", "parallel", "arbitrary")),
        name="cmp",
    )(slopes, qa, kvc, ga, ov, pick, gexp, pair)


WIN_UNITS = NSA_WINDOW // Q_BLOCK + 1
SEL_CHUNK = 4


def _selwin_kernel(pf_ref, slopes_ref, q_ref, kvs_ref, kvw_ref, sel_ref, ga_ref, yc_ref, pick_ref,
                   gexp_ref, ya_ref, sbuf, jlist, mrun, acc, rneg, *, nq):
    b = pl.program_id(0)
    g = pl.program_id(1)
    i = pl.program_id(2)
    hq = NSA_HPG
    qmk = _iota((Q_BLOCK, LANES), 0) - _iota((Q_BLOCK, LANES), 1)

    @pl.when(i == 0)
    def _():
        rel = qmk.astype(F32)
        for h in range(hq):
            rneg[h] = -slopes_ref[hq * g + h] * rel

    q4 = _stack_heads(q_ref[0], pick_ref)
    sel = sel_ref[0, 0]
    dmat = _iota((LANES, LANES), 0) - (_iota((LANES, LANES), 1) >> 6)

    def unit(kv_ref, u):
        return kv_ref[0, pl.ds(pl.multiple_of(u * Q_BLOCK, Q_BLOCK), Q_BLOCK), :]

    def score_unit(kv_ref, u, slot, allow, off):
        s = _nt_dot(q4, unit(kv_ref, u)).reshape(hq, Q_BLOCK, LANES)
        neg = jnp.where(allow, 0.0, NEG_INF)
        for h in range(hq):
            sb = (s[h] + rneg[h]) + (neg - slopes_ref[hq * g + h] * off)
            sbuf[slot, h] = sb
            mrun[h] = jnp.maximum(mrun[h], sb)

    def spread_max():
        m = jnp.max(mrun[...], axis=-1, keepdims=True)
        mrun[...] = jnp.broadcast_to(m, mrun.shape)

    def weighted_values(kv_ref, slots, units):
        tot = None
        for slot, u in zip(slots, units):
            p = jnp.exp(sbuf[slot] - mrun[...]).astype(BF16).reshape(hq * Q_BLOCK, LANES)
            d = _dot(p, _ones_in_k_lanes(unit(kv_ref, u)))
            tot = d if tot is None else tot + d
        return tot

    def normalise(a):
        return _unstack_heads(a * (1.0 / jnp.maximum(a[:, 0:1], TINY)))

    base = ((b * NSA_KV_GROUPS + g) * nq + i) * nq

    def list_body(jj, cnt):
        used = pf_ref[base + jj] != 0

        @pl.when(used)
        def _():
            jlist[cnt] = jj
        return cnt + used.astype(jnp.int32)

    cnt = lax.fori_loop(0, i + 1, list_body, jnp.int32(0))
    for t in range(SEL_CHUNK - 1):
        jlist[cnt + t] = 0
    nchunk = (cnt + (SEL_CHUNK - 1)) // SEL_CHUNK
    mrun[...] = jnp.full_like(mrun, NEG_INF)

    def score_chunk(ci, carry):
        for t in range(SEL_CHUNK):
            slot = ci * SEL_CHUNK + t
            jj = jlist[slot]
            spread = jnp.where(dmat == 2 * jj, 1.0, 0.0).astype(BF16)
            hit = _dot(sel, spread) > 0.5
            bound = jnp.where(slot < cnt, jnp.where(jj < i, -LANES, 0), 2 * LANES)
            score_unit(kvs_ref, jj, slot, hit & (qmk >= bound), ((i - jj) * Q_BLOCK).astype(F32))
        return carry

    lax.fori_loop(0, nchunk, score_chunk, 0)
    spread_max()
    acc[...] = jnp.zeros_like(acc)

    def value_chunk(ci, carry):
        slots = [ci * SEL_CHUNK + t for t in range(SEL_CHUNK)]
        acc[...] += weighted_values(kvs_ref, slots, [jlist[s_] for s_ in slots])
        return carry

    lax.fori_loop(0, nchunk, value_chunk, 0)
    o_s = normalise(acc[...])

    mrun[...] = jnp.full_like(mrun, NEG_INF)
    win_units = []
    for d in range(WIN_UNITS):
        back = WIN_UNITS - 1 - d
        u = i - back
        dist = qmk + back * Q_BLOCK
        low = jnp.where(u >= 0, 0, NSA_WINDOW)
        win_units.append(jnp.maximum(u, 0))
        score_unit(kvw_ref, win_units[-1], d, (dist >= low) & (dist < NSA_WINDOW), float(back * Q_BLOCK))
    spread_max()
    o_w = normalise(weighted_values(kvw_ref, list(range(WIN_UNITS)), win_units))

    gates = ga_ref[0]
    ya = (yc_ref[0] + _expand_gate(gates, gexp_ref[0, 1]) * o_s
          + _expand_gate(gates, gexp_ref[0, 2]) * o_w)
    ya_ref[0] = ya.astype(BF16)


def _selwin(pflags, slopes, qa, kva, sel, ga, yc, pick, gexp):
    b, s, _ = qa.shape
    g = NSA_KV_GROUPS
    nq = s // Q_BLOCK
    nbuf = max(nq + SEL_CHUNK - 1, WIN_UNITS)
    grid_spec = pltpu.PrefetchScalarGridSpec(
        num_scalar_prefetch=1,
        grid=(b, g, nq),
        in_specs=[pl.BlockSpec(memory_space=pltpu.SMEM),
                  pl.BlockSpec((1, Q_BLOCK, GROUP_W), lambda bi, gi, i, pf: (bi, i, gi)),
                  pl.BlockSpec((1, s, LANES), lambda bi, gi, i, pf: (bi, 0, 2 + gi)),
                  pl.BlockSpec((1, s, LANES), lambda bi, gi, i, pf: (bi, 0, 4 + gi)),
                  pl.BlockSpec((1, 1, Q_BLOCK, LANES), lambda bi, gi, i, pf: (bi, gi, i, 0)),
                  pl.BlockSpec((1, Q_BLOCK, LANES), lambda bi, gi, i, pf: (bi, i, 0)),
                  pl.BlockSpec((1, Q_BLOCK, GROUP_W), lambda bi, gi, i, pf: (bi, i, gi)),
                  pl.BlockSpec(pick.shape, lambda bi, gi, i, pf: (0, 0, 0)),
                  pl.BlockSpec((1, 3, LANES, GROUP_W), lambda bi, gi, i, pf: (gi, 0, 0, 0))],
        out_specs=pl.BlockSpec((1, Q_BLOCK, GROUP_W), lambda bi, gi, i, pf: (bi, i, gi)),
        scratch_shapes=[pltpu.VMEM((nbuf, NSA_HPG, Q_BLOCK, LANES), F32),
                        pltpu.SMEM((nbuf,), jnp.int32),
                        pltpu.VMEM((NSA_HPG, Q_BLOCK, LANES), F32),
                        pltpu.VMEM((NSA_HPG * Q_BLOCK, LANES), F32),
                        pltpu.VMEM((NSA_HPG, Q_BLOCK, LANES), F32)],
    )
    return pl.pallas_call(
        functools.partial(_selwin_kernel, nq=nq),
        grid_spec=grid_spec,
        out_shape=jax.ShapeDtypeStruct((b, s, g * GROUP_W), BF16),
        compiler_params=_params(("parallel", "parallel", "arbitrary")),
        name="selwin",
    )(pflags, slopes, qa, kva, kva, sel, ga, yc, pick, gexp)


def _dil_bias(slopes):
    c = DIL_BLOCK
    qi = np.arange(c)[:, None]
    ki = np.arange(2 * c)[None, :]
    delta = qi + c - ki
    out = []
    for k, (window, dil) in enumerate(DIL_GROUPS):
        in_band = jnp.asarray((delta >= 0) & (delta <= window // dil))
        dist = jnp.asarray((delta * dil).astype(np.float32))
        for h in range(DIL_HPG):
            out.append(jnp.where(in_band, -(slopes[DIL_HPG * k + h] * dist), NEG_INF))
    return jnp.stack(out)


def _dil_group(k, dil, sup, q_ref, kvc_ref, kvp_ref, bias_ref, mx_ref, w_ref, y_ref):
    c = DIL_BLOCK
    nblk = sup // (c * dil)
    has_halo = pl.program_id(1) > 0
    nch = DIL_HPG * HEAD_DIM // LANES
    upper = _iota((c, LANES), 1) >= HEAD_DIM

    def body(idx, carry):
        r = idx // nblk
        n = idx - r * nblk
        start = r + dil * c * n
        rows = pl.ds(start, c, stride=dil)
        rows_prev = pl.ds(jnp.maximum(start - dil * c, r), c, stride=dil)
        rows_halo = pl.ds(r + dil * c * (nblk - 1), c, stride=dil)
        pen = jnp.where(jnp.logical_or(n >= 1, has_halo), 0.0, NEG_INF)

        def prev(ch):
            return jnp.where(n >= 1, kvc_ref[ch, 0, rows_prev, :], kvp_ref[ch, 0, rows_halo, :]).astype(BF16)

        for ch in range(nch):
            qb = q_ref[ch, 0, rows, :].astype(BF16)
            k_prev, k_cur = prev(ch), kvc_ref[ch, 0, rows, :].astype(BF16)
            v_prev, v_cur = prev(nch + ch), kvc_ref[nch + ch, 0, rows, :].astype(BF16)
            o = None
            lse = None
            for half in range(2):
                mine = upper if half else jnp.logical_not(upper)
                qh = jnp.where(mine, qb, jnp.zeros_like(qb))
                bias = bias_ref[2 * ch + half]
                sp = (_nt_dot(qh, k_prev) + bias[:, :c]) + pen
                sc = _nt_dot(qh, k_cur) + bias[:, c:]
                m = jnp.maximum(jnp.max(sp, axis=-1, keepdims=True), jnp.max(sc, axis=-1, keepdims=True))
                ep = jnp.exp(sp - m)
                ec = jnp.exp(sc - m)
                lt = jnp.maximum(jnp.sum(ep, axis=-1, keepdims=True) + jnp.sum(ec, axis=-1, keepdims=True), TINY)
                oh = (_dot(ep.astype(BF16), v_prev) + _dot(ec.astype(BF16), v_cur)) * (1.0 / lt)
                lh = jnp.broadcast_to(m + jnp.log(lt), (c, LANES))
                o = oh if o is None else jnp.where(mine, oh, o)
                lse = lh if lse is None else jnp.where(mine, lh, lse)
            if k == 0:
                mx_ref[ch, rows, :] = lse
                w_ref[ch, rows, :] = jnp.ones_like(lse)
                y_ref[ch, rows, :] = o
            else:
                mx_old = mx_ref[ch, rows, :]
                mx_new = jnp.maximum(mx_old, lse)
                a = jnp.exp(mx_old - mx_new)
                bnew = jnp.exp(lse - mx_new)
                mx_ref[ch, rows, :] = mx_new
                w_ref[ch, rows, :] = w_ref[ch, rows, :] * a + bnew
                y_ref[ch, rows, :] = y_ref[ch, rows, :] * a + bnew * o
        return carry

    lax.fori_loop(0, sup // c, body, 0)


def _dil_kernel(q_ref, kvc_ref, kvp_ref, bias_ref, yb_ref, mx_ref, w_ref, y_ref, *, sup):
    gi = pl.program_id(2)
    for k, (_, dil) in enumerate(DIL_GROUPS):
        @pl.when(gi == k)
        def _(k=k, dil=dil):
            _dil_group(k, dil, sup, q_ref, kvc_ref, kvp_ref, bias_ref, mx_ref, w_ref, y_ref)

    @pl.when(gi == len(DIL_GROUPS) - 1)
    def _():
        for ch in range(DIL_HPG * HEAD_DIM // LANES):
            yb_ref[0, :, ch * LANES:(ch + 1) * LANES] = (y_ref[ch] * (1.0 / w_ref[ch])).astype(BF16)


def _dil(qb, kvb, bias, sup):
    _, b, s, _ = qb.shape
    hw = DIL_HPG * HEAD_DIM
    nch = hw // LANES
    ng = len(DIL_GROUPS)
    return pl.pallas_call(
        functools.partial(_dil_kernel, sup=sup),
        grid=(b, s // sup, ng),
        in_specs=[pl.BlockSpec((nch, 1, sup, LANES), lambda bi, si, gi: (gi, bi, si, 0)),
                  pl.BlockSpec((2 * nch, 1, sup, LANES), lambda bi, si, gi: (gi, bi, si, 0)),
                  pl.BlockSpec((2 * nch, 1, sup, LANES),
                               lambda bi, si, gi: (gi, bi, jnp.maximum(si - 1, 0), 0)),
                  pl.BlockSpec((DIL_HPG, DIL_BLOCK, 2 * DIL_BLOCK), lambda bi, si, gi: (gi, 0, 0))],
        out_specs=pl.BlockSpec((1, sup, hw), lambda bi, si, gi: (bi, si, 0)),
        out_shape=jax.ShapeDtypeStruct((b, s, hw), BF16),
        scratch_shapes=[pltpu.VMEM((nch, sup, LANES), F32), pltpu.VMEM((nch, sup, LANES), F32),
                        pltpu.VMEM((nch, sup, LANES), F32)],
        compiler_params=_params(("parallel", "parallel", "arbitrary")),
        name="dil",
    )(qb, kvb, kvb, bias)


def _merge_kernel(x_ref, g_ref, ya_ref, yb_ref, wg_ref, ua_ref, ub_ref, wo_ref, o_ref):
    x = x_ref[...]
    d = x.shape[1]
    hn = _rms(x, g_ref[...]).astype(BF16)
    gate_a = jax.nn.sigmoid(_dot(hn, wg_ref[:, 0:d]))
    gate_b = jax.nn.sigmoid(_dot(hn, wg_ref[:, d:2 * d]))
    merged = gate_a * _dot(ya_ref[...], ua_ref[...]) + gate_b * _dot(yb_ref[...], ub_ref[...])
    o_ref[...] = x + _dot(merged.astype(BF16), wo_ref[...])


def _merge(x2, g, ya, yb, wg, ua, ub, wo, tm):
    n, d = x2.shape
    row = lambda i: (i, 0)
    fixed = lambda i: (0, 0)
    return pl.pallas_call(
        _merge_kernel,
        grid=(n // tm,),
        in_specs=[pl.BlockSpec((tm, d), row), pl.BlockSpec((1, d), fixed),
                  pl.BlockSpec((tm, ya.shape[1]), row), pl.BlockSpec((tm, yb.shape[1]), row),
                  pl.BlockSpec(wg.shape, fixed), pl.BlockSpec(ua.shape, fixed),
                  pl.BlockSpec(ub.shape, fixed), pl.BlockSpec(wo.shape, fixed)],
        out_specs=pl.BlockSpec((tm, d), row),
        out_shape=jax.ShapeDtypeStruct((n, d), F32),
        compiler_params=_params(("parallel",)),
        name="merge",
    )(x2, g, ya, yb, wg, ua, ub, wo)


def _ffn_kernel(x_ref, g_ref, wg_ref, wu_ref, wo_ref, gf_ref, o_ref, hn_ref, acc_ref, *, final_norm):
    c = pl.program_id(1)

    @pl.when(c == 0)
    def _():
        hn_ref[...] = _rms(x_ref[...], g_ref[...]).astype(BF16)
        acc_ref[...] = jnp.zeros_like(acc_ref)

    hn = hn_ref[...]
    gt = _dot(hn, wg_ref[...])
    up = _dot(hn, wu_ref[...])
    act = ((gt * jax.nn.sigmoid(gt)) * up).astype(BF16)
    acc_ref[...] += _dot(act, wo_ref[...])

    @pl.when(c == pl.num_programs(1) - 1)
    def _():
        y = x_ref[...] + acc_ref[...]
        if final_norm:
            y = _rms(y, gf_ref[...])
        o_ref[...] = y


def _ffn(x2, g, w_in, w_out, gf, tm, tf, final_norm):
    n, d = x2.shape
    dff = w_out.shape[0]
    nc = dff // tf
    return pl.pallas_call(
        functools.partial(_ffn_kernel, final_norm=final_norm),
        grid=(n // tm, nc),
        in_specs=[pl.BlockSpec((tm, d), lambda i, c: (i, 0)),
                  pl.BlockSpec((1, d), lambda i, c: (0, 0)),
                  pl.BlockSpec((d, tf), lambda i, c: (0, c)),
                  pl.BlockSpec((d, tf), lambda i, c: (0, nc + c)),
                  pl.BlockSpec((tf, d), lambda i, c: (c, 0)),
                  pl.BlockSpec((1, d), lambda i, c: (0, 0))],
        out_specs=pl.BlockSpec((tm, d), lambda i, c: (i, 0)),
        out_shape=jax.ShapeDtypeStruct((n, d), F32),
        scratch_shapes=[pltpu.VMEM((tm, d), BF16), pltpu.VMEM((tm, d), F32)],
        compiler_params=_params(("parallel", "arbitrary")),
        name="ffn",
    )(x2, g, w_in, w_in, w_out, gf)


def _alibi_slopes():
    k = jnp.arange(1, N_ALIBI + 1, dtype=F32)
    s = jnp.exp2(-8.0 * k / N_ALIBI)
    nsa = s[2 * DIL_HPG:2 * DIL_HPG + NSA_HEADS]
    dil = jnp.concatenate([s[:2 * DIL_HPG], s[2 * DIL_HPG + NSA_HEADS:]])
    return nsa, dil


def _overlap(seq, ncp):
    n_c = (seq - CMP_BLOCK) // CMP_STRIDE + 1
    n_s = seq // SEL_BLOCK
    c_start = np.arange(n_c) * CMP_STRIDE
    s_start = np.arange(n_s) * SEL_BLOCK
    ov = np.clip(np.minimum(c_start[:, None] + CMP_BLOCK, s_start[None, :] + SEL_BLOCK)
                 - np.maximum(c_start[:, None], s_start[None, :]), 0, None).astype(np.float32) / CMP_BLOCK
    out = np.zeros((ncp, LANES), np.float32)
    out[:n_c, :n_s] = ov
    return jnp.asarray(out, BF16)


def _proj_weights(w_in):
    dep, d, _ = w_in.shape
    scale = HEAD_DIM ** -0.5
    c0 = NSA_HEADS * HEAD_DIM
    c1 = c0 + 3 * 2 * NSA_KV_GROUPS * HEAD_DIM
    c2 = c1 + NSA_HEADS * 3
    c3 = c2 + 3 * DIL_HEADS * HEAD_DIM
    qa = w_in[..., :c0] * scale
    kva = w_in[..., c0:c1].reshape(dep, d, 3, 2, NSA_KV_GROUPS, HEAD_DIM)
    kva = kva.transpose(0, 1, 2, 4, 3, 5).reshape(dep, d, KVA_W)
    ga = jnp.pad(w_in[..., c1:c2], ((0, 0), (0, 0), (0, GA_W - (c2 - c1))))
    qkvb = w_in[..., c2:c3].reshape(dep, d, 3, len(DIL_GROUPS), DIL_HPG * HEAD_DIM)
    qb = (qkvb[:, :, 0] * scale).reshape(dep, d, QB_W)
    kvb = qkvb[:, :, 1:3].transpose(0, 1, 3, 2, 4).reshape(dep, d, KVB_W)
    w_a = jnp.concatenate([qa, kva, qb, kvb, ga], axis=-1).astype(BF16)
    return w_a, w_in[..., c3:].astype(BF16)


def _compress_weights(pe_k, pe_v, w_ck1, w_ck2, w_cv1, w_cv2):
    dep = pe_k.shape[0]
    half = CMP_BLOCK // 2
    ck = w_ck1.reshape(dep, CMP_BLOCK, HEAD_DIM, CMP_HIDDEN)
    cv = w_cv1.reshape(dep, CMP_BLOCK, HEAD_DIM, CMP_HIDDEN)
    z = jnp.zeros_like(ck)
    w = jnp.concatenate([jnp.concatenate([ck, z], axis=-1),
                         jnp.concatenate([z, cv], axis=-1)], axis=2)
    feat = half * 2 * HEAD_DIM
    wa = w[:, :half].reshape(dep, feat, 2 * CMP_HIDDEN).astype(BF16)
    wb = w[:, half:].reshape(dep, feat, 2 * CMP_HIDDEN).astype(BF16)
    pe = jnp.concatenate([pe_k, pe_v], axis=-1)
    pe2 = jnp.stack([pe[:, :half].reshape(dep, feat), pe[:, half:].reshape(dep, feat)], axis=1)
    pe2 = jnp.pad(pe2, ((0, 0), (0, 6), (0, 0)))
    z2 = jnp.zeros_like(w_ck2)
    w2 = jnp.concatenate([jnp.concatenate([w_ck2, z2], axis=-1),
                          jnp.concatenate([z2, w_cv2], axis=-1)], axis=1).astype(BF16)
    return pe2, wa, wb, w2


def kernel(x, norm_mix, w_in, pe_k, pe_v, w_ck1, w_ck2, w_cv1, w_cv2, w_up_nsa, w_up_dil,
           w_out, norm_ffn, w_ffn_in, w_ffn_out, norm_final):
    b, s, d = x.shape
    depth = w_in.shape[0]
    n = b * s
    nq = s // Q_BLOCK
    nch = s // CMP_STRIDE
    n_s = s // SEL_BLOCK
    assert s % DIL_SUPER == 0 and n_s <= LANES
    n_top = min(SEL_TOPK, n_s)
    tm = 512
    tm_ffn = 1024 if n % 1024 == 0 else 512
    tf = 256

    slopes_nsa, slopes_dil = _alibi_slopes()
    ov = _overlap(s, nch)
    pick, gexp, pair = _head_pick(), _gate_spread(), _pair_matrix()
    dbias = _dil_bias(slopes_dil)
    w_a, w_gm = _proj_weights(w_in)
    pe2, wa, wb, w2 = _compress_weights(pe_k, pe_v, w_ck1, w_ck2, w_cv1, w_cv2)
    ua = w_up_nsa.astype(BF16)
    ub = w_up_dil.astype(BF16)
    wo = w_out.astype(BF16)
    wfi = w_ffn_in.astype(BF16)
    wfo = w_ffn_out.astype(BF16)
    gf = norm_final.reshape(1, d)

    x2 = x.reshape(n, d)
    for l in range(depth):
        g_mix = norm_mix[l].reshape(1, d)
        qa, kva, qb, kvb, ga = _proj(x2, g_mix, w_a[l], tm)
        qa = qa.reshape(b, s, QA_W)
        kva = kva.reshape(b, s, KVA_W)
        ga = ga.reshape(b, s, GA_W)
        xc = kva[:, :, :NSA_KV_GROUPS * LANES].reshape(b, nch, CMP_STRIDE, NSA_KV_GROUPS, LANES)
        xc = xc.transpose(0, 3, 1, 2, 4).reshape(b, NSA_KV_GROUPS, nch, CMP_STRIDE * LANES)
        kvc = _compress(xc, pe2[l], wa[l], wb[l], w2[l])
        yc, sel, flags = _cmp(slopes_nsa, qa, kvc, ga, ov, pick, gexp, pair, n_top)
        pflags = flags.reshape(b * NSA_KV_GROUPS * nq, LANES)[:, :nq].reshape(-1)
        ya = _selwin(pflags, slopes_nsa, qa, kva, sel, ga, yc, pick, gexp)
        yb = _dil(qb.reshape(-1, b, s, LANES), kvb.reshape(-1, b, s, LANES), dbias, DIL_SUPER)
        x2 = _merge(x2, g_mix, ya.reshape(n, QA_W), yb.reshape(n, DIL_HPG * HEAD_DIM),
                    w_gm[l], ua[l], ub[l], wo[l], tm)
        x2 = _ffn(x2, norm_ffn[l].reshape(1, d), wfi[l], wfo[l], gf, tm_ffn, tf,
                  final_norm=(l == depth - 1))
    return x2.reshape(b, s, d)
```

```python
import functools

import numpy as np
import jax
import jax.numpy as jnp
from jax import lax
from jax.experimental import pallas as pl
from jax.experimental.pallas import tpu as pltpu

F32 = jnp.float32
BF16 = jnp.bfloat16

HEAD_DIM = 64
NSA_HEADS = 8
NSA_KV_GROUPS = 2
NSA_HPG = NSA_HEADS // NSA_KV_GROUPS
CMP_BLOCK = 32
CMP_STRIDE = 16
CMP_HIDDEN = 256
SEL_BLOCK = 64
SEL_TOPK = 16
NSA_WINDOW = 512
Q_BLOCK = 128
FORCE_SCORE = 1e6
DIL_GROUPS = ((128, 1), (512, 4), (2048, 16))
DIL_HPG = 4
DIL_HEADS = DIL_HPG * len(DIL_GROUPS)
DIL_BLOCK = 128
N_ALIBI = NSA_HEADS + DIL_HEADS
EPS = 1e-6
NEG_INF = -1e30
TINY = 1e-30

LANES = 128
GROUP_W = NSA_HPG * HEAD_DIM
DIL_SUPER = 2048
DIL_UNROLL = 4
VMEM_LIMIT = 48 * 1024 * 1024


def _dot(a, b):
    return jnp.dot(a, b, preferred_element_type=F32)


def _nt_dot(a, b):
    return lax.dot_general(a, b, (((1,), (1,)), ((), ())), preferred_element_type=F32)


def _iota(shape, dim):
    return lax.broadcasted_iota(jnp.int32, shape, dim)


def _rms(x, g):
    ms = jnp.mean(x * x, axis=-1, keepdims=True)
    return (x * lax.rsqrt(ms + EPS)) * g


def _params(sem):
    return pltpu.CompilerParams(dimension_semantics=sem, vmem_limit_bytes=VMEM_LIMIT)


QA_W, KVA_W, QB_W, KVB_W, GA_W = 512, 768, 768, 1536, 128
PROJ_COLS = QA_W + KVA_W + QB_W + KVB_W + GA_W


def _proj_kernel(x_ref, g_ref, w_ref, qa_ref, kva_ref, qb_ref, kvb_ref, ga_ref):
    y = _rms(x_ref[...], g_ref[...]).astype(BF16)
    c = 0
    qa_ref[...] = _dot(y, w_ref[:, c:c + QA_W]).astype(BF16)
    c += QA_W
    kva_ref[...] = _dot(y, w_ref[:, c:c + KVA_W]).astype(BF16)
    c += KVA_W
    for j in range(QB_W // LANES):
        qb_ref[j] = _dot(y, w_ref[:, c:c + LANES])
        c += LANES
    for j in range(KVB_W // LANES):
        kvb_ref[j] = _dot(y, w_ref[:, c:c + LANES])
        c += LANES
    ga_ref[...] = jax.nn.sigmoid(_dot(y, w_ref[:, c:c + GA_W]))


def _proj(x2, g, w, tm):
    n, d = x2.shape
    row = lambda i: (i, 0)
    fixed = lambda i: (0, 0)
    return pl.pallas_call(
        _proj_kernel,
        grid=(n // tm,),
        in_specs=[pl.BlockSpec((tm, d), row), pl.BlockSpec((1, d), fixed),
                  pl.BlockSpec((d, PROJ_COLS), fixed)],
        out_specs=[pl.BlockSpec((tm, QA_W), row), pl.BlockSpec((tm, KVA_W), row),
                   pl.BlockSpec((QB_W // LANES, tm, LANES), lambda i: (0, i, 0)),
                   pl.BlockSpec((KVB_W // LANES, tm, LANES), lambda i: (0, i, 0)),
                   pl.BlockSpec((tm, GA_W), row)],
        out_shape=[jax.ShapeDtypeStruct((n, QA_W), BF16), jax.ShapeDtypeStruct((n, KVA_W), BF16),
                   jax.ShapeDtypeStruct((QB_W // LANES, n, LANES), F32),
                   jax.ShapeDtypeStruct((KVB_W // LANES, n, LANES), F32),
                   jax.ShapeDtypeStruct((n, GA_W), F32)],
        compiler_params=_params(("parallel",)),
        name="proj",
    )(x2, g, w)


def _compress_kernel(x_ref, pe_ref, wa_ref, wb_ref, w2_ref, o_ref):
    xf = x_ref[0, 0].astype(F32)
    nch = xf.shape[0]
    xa = (xf + pe_ref[0:1, :]).astype(BF16)
    xb = (xf + pe_ref[1:2, :]).astype(BF16)
    ya = _dot(xa, wa_ref[...])
    yb = _dot(xb, wb_ref[...])
    h = ya + pltpu.roll(yb, nch - 1, 0)
    a = (h * jax.nn.sigmoid(h)).astype(BF16)
    o_ref[0, 0] = _dot(a, w2_ref[...]).astype(BF16)


def _compress(xc, pe, wa, wb, w2):
    b, g, nch, f = xc.shape
    hid = wa.shape[1]
    return pl.pallas_call(
        _compress_kernel,
        grid=(b, g),
        in_specs=[pl.BlockSpec((1, 1, nch, f), lambda i, j: (i, j, 0, 0)),
                  pl.BlockSpec((8, f), lambda i, j: (0, 0)),
                  pl.BlockSpec((f, hid), lambda i, j: (0, 0)),
                  pl.BlockSpec((f, hid), lambda i, j: (0, 0)),
                  pl.BlockSpec((hid, LANES), lambda i, j: (0, 0))],
        out_specs=pl.BlockSpec((1, 1, nch, LANES), lambda i, j: (i, j, 0, 0)),
        out_shape=jax.ShapeDtypeStruct((b, g, nch, LANES), BF16),
        compiler_params=_params(("parallel", "parallel")),
        name="compress",
    )(xc, pe, wa, wb, w2)


def _head_pick():
    r = np.arange(GROUP_W)[None, :, None]
    c = np.arange(LANES)[None, None, :]
    h = np.arange(NSA_HPG)[:, None, None]
    return jnp.asarray((r == c + HEAD_DIM * h) & (c < HEAD_DIM), BF16)


def _gate_spread():
    g = np.arange(NSA_KV_GROUPS)[:, None, None, None]
    br = np.arange(3)[None, :, None, None]
    r = np.arange(LANES)[None, None, :, None]
    c = np.arange(GROUP_W)[None, None, None, :]
    return jnp.asarray(r == 3 * (NSA_HPG * g + c // HEAD_DIM) + br, BF16)


def _stack_heads(q, pick_ref):
    return jnp.concatenate([_dot(q, pick_ref[h]).astype(BF16) for h in range(NSA_HPG)], axis=0)


def _unstack_heads(o4):
    lane = _iota((Q_BLOCK, LANES), 1)
    o = [o4[Q_BLOCK * h:Q_BLOCK * (h + 1)] for h in range(NSA_HPG)]
    c0 = jnp.where(lane < HEAD_DIM, pltpu.roll(o[0], HEAD_DIM, 1), o[1])
    c1 = jnp.where(lane < HEAD_DIM, pltpu.roll(o[2], HEAD_DIM, 1), o[3])
    return jnp.concatenate([c0, c1], axis=1)


def _expand_gate(gates, spread):
    hi = gates.astype(BF16)
    lo = (gates - hi.astype(F32)).astype(BF16)
    return _dot(hi, spread) + _dot(lo, spread)


def _ones_in_k_lanes(kv):
    lane = _iota(kv.shape, 1)
    return jnp.where(lane < HEAD_DIM, jnp.ones_like(kv), kv)


def _cmp_kernel(slopes_ref, q_ref, kvc_ref, ga_ref, ov_ref, pick_ref, gexp_ref,
                yc_ref, sel_ref, fl_ref, bias0, cq, rhs, *, n_top):
    g = pl.program_id(1)
    i = pl.program_id(2)
    hq = NSA_HPG
    ncp = cq.shape[1]

    @pl.when(i == 0)
    def _():
        cqv = (CMP_STRIDE * _iota((Q_BLOCK, ncp), 1) + (CMP_BLOCK - 1)) - _iota((Q_BLOCK, ncp), 0)
        cq[...] = cqv
        cf = cqv.astype(F32)
        for h in range(hq):
            bias0[h] = slopes_ref[hq * g + h] * cf
        rhs[:, 0:LANES] = _ones_in_k_lanes(kvc_ref[0, 0])
        rhs[:, LANES:2 * LANES] = ov_ref[...]

    q4 = _stack_heads(q_ref[0], pick_ref)
    s = _nt_dot(q4, kvc_ref[0, 0]).reshape(hq, Q_BLOCK, ncp)
    t0 = i * Q_BLOCK
    neg = jnp.where(cq[...] <= t0, 0.0, NEG_INF)
    tf = t0.astype(F32)
    ps = []
    for h in range(hq):
        sb = (s[h] + bias0[h]) + (neg - slopes_ref[hq * g + h] * tf)
        m = jnp.max(sb, axis=-1, keepdims=True)
        m = jnp.where(m < 0.1 * NEG_INF, -NEG_INF, m)
        ps.append(jnp.exp(sb - m).astype(BF16))
    big = _dot(jnp.concatenate(ps, axis=0), rhs[...])
    big = big * (1.0 / jnp.maximum(big[:, 0:1], TINY))
    oc = _unstack_heads(big[:, 0:LANES])
    yc_ref[0] = _expand_gate(ga_ref[0], gexp_ref[0, 0]) * oc
    imp = big[0:Q_BLOCK, LANES:2 * LANES]
    for h in range(1, hq):
        imp = imp + big[Q_BLOCK * h:Q_BLOCK * (h + 1), LANES:2 * LANES]

    qrow = _iota((Q_BLOCK, LANES), 0)
    j = _iota((Q_BLOCK, LANES), 1)
    jt = 2 * i + (qrow >> 6)
    valid = j <= jt
    forced = (j == 0) | (j == jt) | (j == jt - 1)
    score = jnp.where(forced, FORCE_SCORE, jnp.where(valid, imp, -1.0))
    st = score.T
    jr = _iota((LANES, Q_BLOCK), 0)
    cur = st
    taken = -3e38
    for _ in range(n_top):
        mx = jnp.max(cur, axis=0, keepdims=True)
        first = jnp.min(jnp.where(cur == mx, jr, LANES), axis=0, keepdims=True)
        cur = jnp.where(jr == first, taken, cur)
    keep = jnp.where((cur == taken) & (st >= 0.0), 1.0, 0.0).T
    sel_ref[0, 0] = keep.astype(BF16)

    fl_ref[0] = (jnp.max(keep, axis=0, keepdims=True) > 0.0).astype(jnp.int32)


def _cmp(slopes, qa, kvc, ga, ov, pick, gexp, n_top):
    b, s, _ = qa.shape
    g = NSA_KV_GROUPS
    nq = s // Q_BLOCK
    ncp = kvc.shape[2]
    return pl.pallas_call(
        functools.partial(_cmp_kernel, n_top=n_top),
        grid=(b, g, nq),
        in_specs=[pl.BlockSpec(memory_space=pltpu.SMEM),
                  pl.BlockSpec((1, Q_BLOCK, GROUP_W), lambda bi, gi, i: (bi, i, gi)),
                  pl.BlockSpec((1, 1, ncp, LANES), lambda bi, gi, i: (bi, gi, 0, 0)),
                  pl.BlockSpec((1, Q_BLOCK, LANES), lambda bi, gi, i: (bi, i, 0)),
                  pl.BlockSpec((ncp, LANES), lambda bi, gi, i: (0, 0)),
                  pl.BlockSpec(pick.shape, lambda bi, gi, i: (0, 0, 0)),
                  pl.BlockSpec((1, 3, LANES, GROUP_W), lambda bi, gi, i: (gi, 0, 0, 0))],
        out_specs=[pl.BlockSpec((1, Q_BLOCK, GROUP_W), lambda bi, gi, i: (bi, i, gi)),
                   pl.BlockSpec((1, 1, Q_BLOCK, LANES), lambda bi, gi, i: (bi, gi, i, 0)),
                   pl.BlockSpec((1, 1, LANES), lambda bi, gi, i: ((bi * g + gi) * nq + i, 0, 0))],
        out_shape=[jax.ShapeDtypeStruct((b, s, g * GROUP_W), F32),
                   jax.ShapeDtypeStruct((b, g, s, LANES), BF16),
                   jax.ShapeDtypeStruct((b * g * nq, 1, LANES), jnp.int32)],
        scratch_shapes=[pltpu.VMEM((NSA_HPG, Q_BLOCK, ncp), F32),
                        pltpu.VMEM((Q_BLOCK, ncp), jnp.int32),
                        pltpu.VMEM((ncp, 2 * LANES), BF16)],
        compiler_params=_params(("parallel", "parallel", "arbitrary")),
        name="cmp",
    )(slopes, qa, kvc, ga, ov, pick, gexp)


WIN_UNITS = NSA_WINDOW // Q_BLOCK + 1
SLOT_CHUNK = 4


def _selwin_kernel(pf_ref, slopes_ref, q_ref, kvs_ref, kvw_ref, sel_ref, ga_ref, yc_ref, pick_ref,
                   gexp_ref, ya_ref, sbuf, sj1, sj2, mrun, acc, rneg, *, nq):
    b = pl.program_id(0)
    g = pl.program_id(1)
    i = pl.program_id(2)
    hq = NSA_HPG
    qmk = _iota((Q_BLOCK, LANES), 0) - _iota((Q_BLOCK, LANES), 1)

    @pl.when(i == 0)
    def _():
        rel = qmk.astype(F32)
        for h in range(hq):
            rneg[h] = -slopes_ref[hq * g + h] * rel

    q4 = _stack_heads(q_ref[0], pick_ref)
    sel = sel_ref[0, 0]
    erow = _iota((LANES, LANES), 0)
    upper = _iota((Q_BLOCK, LANES), 1) >= SEL_BLOCK
    dead = -(1 << 20)

    for d in range(WIN_UNITS):
        u = i - (WIN_UNITS - 1) + d
        sj1[d] = jnp.where(u >= 0, 2 * u, -1)
        sj2[d] = jnp.where(u >= 0, 2 * u + 1, -1)

    base = ((b * NSA_KV_GROUPS + g) * nq + i) * LANES

    def list_body(j, k):
        used = pf_ref[base + j] != 0
        slot = WIN_UNITS + (k >> 1)

        @pl.when(used & ((k & 1) == 0))
        def _():
            sj1[slot] = j
            sj2[slot] = -1

        @pl.when(used & ((k & 1) == 1))
        def _():
            sj2[slot] = j
        return k + used.astype(jnp.int32)

    nblk = lax.fori_loop(0, 2 * i + 2, list_body, jnp.int32(0))
    nsel = (nblk + 1) >> 1
    for t in range(SLOT_CHUNK - 1):
        sj1[WIN_UNITS + nsel + t] = -1
        sj2[WIN_UNITS + nsel + t] = -1
    nchunk = (nsel + (SLOT_CHUNK - 1)) // SLOT_CHUNK
    t0 = i * Q_BLOCK

    def kv_tile(kv_ref, slot):
        def half(j):
            return kv_ref[0, pl.ds(pl.multiple_of(jnp.maximum(j, 0) * SEL_BLOCK, SEL_BLOCK), SEL_BLOCK), :]
        return jnp.concatenate([half(sj1[slot]), half(sj2[slot])], axis=0)

    def score_slot(kv_ref, slot, window):
        j1 = sj1[slot]
        j2 = sj2[slot]
        s = _nt_dot(q4, kv_tile(kv_ref, slot)).reshape(hq, Q_BLOCK, LANES)
        off = jnp.where(upper, jnp.where(j2 >= 0, t0 - SEL_BLOCK * j2 + SEL_BLOCK, dead),
                        jnp.where(j1 >= 0, t0 - SEL_BLOCK * j1, dead))
        dist = qmk + off
        if window:
            allow = (dist >= 0) & (dist < NSA_WINDOW)
        else:
            spread = jnp.where(erow == jnp.where(upper, j2, j1), 1.0, 0.0).astype(BF16)
            allow = (dist >= 0) & (_dot(sel, spread) > 0.5)
        neg = jnp.where(allow, 0.0, NEG_INF)
        offf = off.astype(F32)
        for h in range(hq):
            sb = (s[h] + rneg[h]) + (neg - slopes_ref[hq * g + h] * offf)
            sbuf[slot, h] = sb
            mrun[h] = jnp.maximum(mrun[h], sb)

    def spread_max():
        m = jnp.max(mrun[...], axis=-1, keepdims=True)
        mrun[...] = jnp.broadcast_to(m, mrun.shape)

    def weighted_values(kv_ref, slots):
        tot = None
        for slot in slots:
            p = jnp.exp(sbuf[slot] - mrun[...]).astype(BF16).reshape(hq * Q_BLOCK, LANES)
            d = _dot(p, _ones_in_k_lanes(kv_tile(kv_ref, slot)))
            tot = d if tot is None else tot + d
        return tot

    def normalise(a):
        return _unstack_heads(a * (1.0 / jnp.maximum(a[:, 0:1], TINY)))

    mrun[...] = jnp.full_like(mrun, NEG_INF)
    for d in range(WIN_UNITS):
        score_slot(kvw_ref, d, True)
    spread_max()
    o_w = normalise(weighted_values(kvw_ref, list(range(WIN_UNITS))))

    mrun[...] = jnp.full_like(mrun, NEG_INF)

    def score_chunk(ci, carry):
        for t in range(SLOT_CHUNK):
            score_slot(kvs_ref, WIN_UNITS + ci * SLOT_CHUNK + t, False)
        return carry

    lax.fori_loop(0, nchunk, score_chunk, 0)
    spread_max()
    acc[...] = jnp.zeros_like(acc)

    def value_chunk(ci, carry):
        acc[...] += weighted_values(kvs_ref, [WIN_UNITS + ci * SLOT_CHUNK + t for t in range(SLOT_CHUNK)])
        return carry

    lax.fori_loop(0, nchunk, value_chunk, 0)
    o_s = normalise(acc[...])

    gates = ga_ref[0]
    ya = (yc_ref[0] + _expand_gate(gates, gexp_ref[0, 1]) * o_s
          + _expand_gate(gates, gexp_ref[0, 2]) * o_w)
    ya_ref[0] = ya.astype(BF16)


def _selwin(pflags, slopes, qa, kva, sel, ga, yc, pick, gexp):
    b, s, _ = qa.shape
    g = NSA_KV_GROUPS
    nq = s // Q_BLOCK
    nbuf = WIN_UNITS + nq + SLOT_CHUNK - 1
    grid_spec = pltpu.PrefetchScalarGridSpec(
        num_scalar_prefetch=1,
        grid=(b, g, nq),
        in_specs=[pl.BlockSpec(memory_space=pltpu.SMEM),
                  pl.BlockSpec((1, Q_BLOCK, GROUP_W), lambda bi, gi, i, pf: (bi, i, gi)),
                  pl.BlockSpec((1, s, LANES), lambda bi, gi, i, pf: (bi, 0, 2 + gi)),
                  pl.BlockSpec((1, s, LANES), lambda bi, gi, i, pf: (bi, 0, 4 + gi)),
                  pl.BlockSpec((1, 1, Q_BLOCK, LANES), lambda bi, gi, i, pf: (bi, gi, i, 0)),
                  pl.BlockSpec((1, Q_BLOCK, LANES), lambda bi, gi, i, pf: (bi, i, 0)),
                  pl.BlockSpec((1, Q_BLOCK, GROUP_W), lambda bi, gi, i, pf: (bi, i, gi)),
                  pl.BlockSpec(pick.shape, lambda bi, gi, i, pf: (0, 0, 0)),
                  pl.BlockSpec((1, 3, LANES, GROUP_W), lambda bi, gi, i, pf: (gi, 0, 0, 0))],
        out_specs=pl.BlockSpec((1, Q_BLOCK, GROUP_W), lambda bi, gi, i, pf: (bi, i, gi)),
        scratch_shapes=[pltpu.VMEM((nbuf, NSA_HPG, Q_BLOCK, LANES), F32),
                        pltpu.SMEM((nbuf,), jnp.int32),
                        pltpu.SMEM((nbuf,), jnp.int32),
                        pltpu.VMEM((NSA_HPG, Q_BLOCK, LANES), F32),
                        pltpu.VMEM((NSA_HPG * Q_BLOCK, LANES), F32),
                        pltpu.VMEM((NSA_HPG, Q_BLOCK, LANES), F32)],
    )
    return pl.pallas_call(
        functools.partial(_selwin_kernel, nq=nq),
        grid_spec=grid_spec,
        out_shape=jax.ShapeDtypeStruct((b, s, g * GROUP_W), BF16),
        compiler_params=_params(("parallel", "parallel", "arbitrary")),
        name="selwin",
    )(pflags, slopes, qa, kva, kva, sel, ga, yc, pick, gexp)


def _dil_bias(slopes):
    c = DIL_BLOCK
    qi = np.arange(c)[:, None]
    ki = np.arange(2 * c)[None, :]
    delta = qi + c - ki
    out = []
    for k, (window, dil) in enumerate(DIL_GROUPS):
        in_band = jnp.asarray((delta >= 0) & (delta <= window // dil))
        dist = jnp.asarray((delta * dil).astype(np.float32))
        for h in range(DIL_HPG):
            out.append(jnp.where(in_band, -(slopes[DIL_HPG * k + h] * dist), NEG_INF))
    return jnp.stack(out).reshape(len(out) // 2, 2 * c, 2 * c)


def _dil_group(k, dil, sup, q_ref, kvc_ref, kvp_ref, bias_ref, mx_ref, w_ref, y_ref):
    c = DIL_BLOCK
    nblk = sup // (c * dil)
    has_halo = pl.program_id(1) > 0
    nch = DIL_HPG * HEAD_DIM // LANES
    upper = _iota((c, LANES), 1) >= HEAD_DIM

    def block(idx):
        r = idx // nblk
        n = idx - r * nblk
        start = r + dil * c * n
        rows = pl.ds(start, c, stride=dil)
        rows_prev = pl.ds(jnp.maximum(start - dil * c, r), c, stride=dil)
        rows_halo = pl.ds(r + dil * c * (nblk - 1), c, stride=dil)
        pen = jnp.where(jnp.logical_or(n >= 1, has_halo), 0.0, NEG_INF)

        def prev(ch):
            return jnp.where(n >= 1, kvc_ref[ch, 0, rows_prev, :], kvp_ref[ch, 0, rows_halo, :]).astype(BF16)

        for ch in range(nch):
            qb = q_ref[ch, 0, rows, :].astype(BF16)
            zero = jnp.zeros_like(qb)
            qs = jnp.concatenate([jnp.where(upper, zero, qb), jnp.where(upper, qb, zero)], axis=0)
            kband = jnp.concatenate([prev(ch), kvc_ref[ch, 0, rows, :].astype(BF16)], axis=0)
            vband = jnp.concatenate([prev(nch + ch), kvc_ref[nch + ch, 0, rows, :].astype(BF16)], axis=0)
            s = _nt_dot(qs, kband) + bias_ref[ch]
            s = jnp.concatenate([s[:, :c] + pen, s[:, c:]], axis=1)
            m = jnp.max(s, axis=-1, keepdims=True)
            e = jnp.exp(s - m).astype(BF16)
            ones = jnp.ones((2 * c, LANES), BF16)
            ol = _dot(e, jnp.concatenate([vband, ones], axis=1))
            lt = jnp.maximum(ol[:, LANES:], TINY)
            o2 = ol[:, :LANES] * (1.0 / lt)
            l2 = m + jnp.log(lt)
            o = jnp.where(upper, o2[c:], o2[:c])
            lse = jnp.where(upper, l2[c:], l2[:c])
            if k == 0:
                mx_ref[ch, rows, :] = lse
                w_ref[ch, rows, :] = jnp.ones_like(lse)
                y_ref[ch, rows, :] = o
            else:
                mx_old = mx_ref[ch, rows, :]
                mx_new = jnp.maximum(mx_old, lse)
                a = jnp.exp(mx_old - mx_new)
                bnew = jnp.exp(lse - mx_new)
                mx_ref[ch, rows, :] = mx_new
                w_ref[ch, rows, :] = w_ref[ch, rows, :] * a + bnew
                y_ref[ch, rows, :] = y_ref[ch, rows, :] * a + bnew * o

    def body(t, carry):
        for j in range(DIL_UNROLL):
            block(t * DIL_UNROLL + j)
        return carry

    lax.fori_loop(0, sup // (c * DIL_UNROLL), body, 0)


def _dil_kernel(q_ref, kvc_ref, kvp_ref, bias_ref, yb_ref, mx_ref, w_ref, y_ref, *, sup):
    gi = pl.program_id(2)
    for k, (_, dil) in enumerate(DIL_GROUPS):
        @pl.when(gi == k)
        def _(k=k, dil=dil):
            _dil_group(k, dil, sup, q_ref, kvc_ref, kvp_ref, bias_ref, mx_ref, w_ref, y_ref)

    @pl.when(gi == len(DIL_GROUPS) - 1)
    def _():
        for ch in range(DIL_HPG * HEAD_DIM // LANES):
            yb_ref[0, :, ch * LANES:(ch + 1) * LANES] = (y_ref[ch] * (1.0 / w_ref[ch])).astype(BF16)


def _dil(qb, kvb, bias, sup):
    _, b, s, _ = qb.shape
    hw = DIL_HPG * HEAD_DIM
    nch = hw // LANES
    ng = len(DIL_GROUPS)
    return pl.pallas_call(
        functools.partial(_dil_kernel, sup=sup),
        grid=(b, s // sup, ng),
        in_specs=[pl.BlockSpec((nch, 1, sup, LANES), lambda bi, si, gi: (gi, bi, si, 0)),
                  pl.BlockSpec((2 * nch, 1, sup, LANES), lambda bi, si, gi: (gi, bi, si, 0)),
                  pl.BlockSpec((2 * nch, 1, sup, LANES),
                               lambda bi, si, gi: (gi, bi, jnp.maximum(si - 1, 0), 0)),
                  pl.BlockSpec((nch, 2 * DIL_BLOCK, 2 * DIL_BLOCK), lambda bi, si, gi: (gi, 0, 0))],
        out_specs=pl.BlockSpec((1, sup, hw), lambda bi, si, gi: (bi, si, 0)),
        out_shape=jax.ShapeDtypeStruct((b, s, hw), BF16),
        scratch_shapes=[pltpu.VMEM((nch, sup, LANES), F32), pltpu.VMEM((nch, sup, LANES), F32),
                        pltpu.VMEM((nch, sup, LANES), F32)],
        compiler_params=_params(("parallel", "parallel", "arbitrary")),
        name="dil",
    )(qb, kvb, kvb, bias)


def _merge_kernel(x_ref, g_ref, ya_ref, yb_ref, wg_ref, ua_ref, ub_ref, wo_ref, o_ref):
    x = x_ref[...]
    d = x.shape[1]
    hn = _rms(x, g_ref[...]).astype(BF16)
    gate_a = jax.nn.sigmoid(_dot(hn, wg_ref[:, 0:d]))
    gate_b = jax.nn.sigmoid(_dot(hn, wg_ref[:, d:2 * d]))
    merged = gate_a * _dot(ya_ref[...], ua_ref[...]) + gate_b * _dot(yb_ref[...], ub_ref[...])
    o_ref[...] = x + _dot(merged.astype(BF16), wo_ref[...])


def _merge(x2, g, ya, yb, wg, ua, ub, wo, tm):
    n, d = x2.shape
    row = lambda i: (i, 0)
    fixed = lambda i: (0, 0)
    return pl.pallas_call(
        _merge_kernel,
        grid=(n // tm,),
        in_specs=[pl.BlockSpec((tm, d), row), pl.BlockSpec((1, d), fixed),
                  pl.BlockSpec((tm, ya.shape[1]), row), pl.BlockSpec((tm, yb.shape[1]), row),
                  pl.BlockSpec(wg.shape, fixed), pl.BlockSpec(ua.shape, fixed),
                  pl.BlockSpec(ub.shape, fixed), pl.BlockSpec(wo.shape, fixed)],
        out_specs=pl.BlockSpec((tm, d), row),
        out_shape=jax.ShapeDtypeStruct((n, d), F32),
        compiler_params=_params(("parallel",)),
        name="merge",
    )(x2, g, ya, yb, wg, ua, ub, wo)


def _ffn_kernel(x_ref, g_ref, wg_ref, wu_ref, wo_ref, gf_ref, o_ref, hn_ref, acc_ref, *, final_norm):
    c = pl.program_id(1)

    @pl.when(c == 0)
    def _():
        hn_ref[...] = _rms(x_ref[...], g_ref[...]).astype(BF16)
        acc_ref[...] = jnp.zeros_like(acc_ref)

    hn = hn_ref[...]
    gt = _dot(hn, wg_ref[...])
    up = _dot(hn, wu_ref[...])
    act = ((gt * jax.nn.sigmoid(gt)) * up).astype(BF16)
    acc_ref[...] += _dot(act, wo_ref[...])

    @pl.when(c == pl.num_programs(1) - 1)
    def _():
        y = x_ref[...] + acc_ref[...]
        if final_norm:
            y = _rms(y, gf_ref[...])
        o_ref[...] = y


def _ffn(x2, g, w_in, w_out, gf, tm, tf, final_norm):
    n, d = x2.shape
    dff = w_out.shape[0]
    nc = dff // tf
    return pl.pallas_call(
        functools.partial(_ffn_kernel, final_norm=final_norm),
        grid=(n // tm, nc),
        in_specs=[pl.BlockSpec((tm, d), lambda i, c: (i, 0)),
                  pl.BlockSpec((1, d), lambda i, c: (0, 0)),
                  pl.BlockSpec((d, tf), lambda i, c: (0, c)),
                  pl.BlockSpec((d, tf), lambda i, c: (0, nc + c)),
                  pl.BlockSpec((tf, d), lambda i, c: (c, 0)),
                  pl.BlockSpec((1, d), lambda i, c: (0, 0))],
        out_specs=pl.BlockSpec((tm, d), lambda i, c: (i, 0)),
        out_shape=jax.ShapeDtypeStruct((n, d), F32),
        scratch_shapes=[pltpu.VMEM((tm, d), BF16), pltpu.VMEM((tm, d), F32)],
        compiler_params=_params(("parallel", "arbitrary")),
        name="ffn",
    )(x2, g, w_in, w_in, w_out, gf)


def _alibi_slopes():
    k = jnp.arange(1, N_ALIBI + 1, dtype=F32)
    s = jnp.exp2(-8.0 * k / N_ALIBI)
    nsa = s[2 * DIL_HPG:2 * DIL_HPG + NSA_HEADS]
    dil = jnp.concatenate([s[:2 * DIL_HPG], s[2 * DIL_HPG + NSA_HEADS:]])
    return nsa, dil


def _overlap(seq, ncp):
    n_c = (seq - CMP_BLOCK) // CMP_STRIDE + 1
    n_s = seq // SEL_BLOCK
    c_start = np.arange(n_c) * CMP_STRIDE
    s_start = np.arange(n_s) * SEL_BLOCK
    ov = np.clip(np.minimum(c_start[:, None] + CMP_BLOCK, s_start[None, :] + SEL_BLOCK)
                 - np.maximum(c_start[:, None], s_start[None, :]), 0, None).astype(np.float32) / CMP_BLOCK
    out = np.zeros((ncp, LANES), np.float32)
    out[:n_c, :n_s] = ov
    return jnp.asarray(out, BF16)


def _proj_weights(w_in):
    dep, d, _ = w_in.shape
    scale = HEAD_DIM ** -0.5
    c0 = NSA_HEADS * HEAD_DIM
    c1 = c0 + 3 * 2 * NSA_KV_GROUPS * HEAD_DIM
    c2 = c1 + NSA_HEADS * 3
    c3 = c2 + 3 * DIL_HEADS * HEAD_DIM
    qa = w_in[..., :c0] * scale
    kva = w_in[..., c0:c1].reshape(dep, d, 3, 2, NSA_KV_GROUPS, HEAD_DIM)
    kva = kva.transpose(0, 1, 2, 4, 3, 5).reshape(dep, d, KVA_W)
    ga = jnp.pad(w_in[..., c1:c2], ((0, 0), (0, 0), (0, GA_W - (c2 - c1))))
    qkvb = w_in[..., c2:c3].reshape(dep, d, 3, len(DIL_GROUPS), DIL_HPG * HEAD_DIM)
    qb = (qkvb[:, :, 0] * scale).reshape(dep, d, QB_W)
    kvb = qkvb[:, :, 1:3].transpose(0, 1, 3, 2, 4).reshape(dep, d, KVB_W)
    w_a = jnp.concatenate([qa, kva, qb, kvb, ga], axis=-1).astype(BF16)
    return w_a, w_in[..., c3:].astype(BF16)


def _compress_weights(pe_k, pe_v, w_ck1, w_ck2, w_cv1, w_cv2):
    dep = pe_k.shape[0]
    half = CMP_BLOCK // 2
    ck = w_ck1.reshape(dep, CMP_BLOCK, HEAD_DIM, CMP_HIDDEN)
    cv = w_cv1.reshape(dep, CMP_BLOCK, HEAD_DIM, CMP_HIDDEN)
    z = jnp.zeros_like(ck)
    w = jnp.concatenate([jnp.concatenate([ck, z], axis=-1),
                         jnp.concatenate([z, cv], axis=-1)], axis=2)
    feat = half * 2 * HEAD_DIM
    wa = w[:, :half].reshape(dep, feat, 2 * CMP_HIDDEN).astype(BF16)
    wb = w[:, half:].reshape(dep, feat, 2 * CMP_HIDDEN).astype(BF16)
    pe = jnp.concatenate([pe_k, pe_v], axis=-1)
    pe2 = jnp.stack([pe[:, :half].reshape(dep, feat), pe[:, half:].reshape(dep, feat)], axis=1)
    pe2 = jnp.pad(pe2, ((0, 0), (0, 6), (0, 0)))
    z2 = jnp.zeros_like(w_ck2)
    w2 = jnp.concatenate([jnp.concatenate([w_ck2, z2], axis=-1),
                          jnp.concatenate([z2, w_cv2], axis=-1)], axis=1).astype(BF16)
    return pe2, wa, wb, w2


def kernel(x, norm_mix, w_in, pe_k, pe_v, w_ck1, w_ck2, w_cv1, w_cv2, w_up_nsa, w_up_dil,
           w_out, norm_ffn, w_ffn_in, w_ffn_out, norm_final):
    b, s, d = x.shape
    depth = w_in.shape[0]
    n = b * s
    nq = s // Q_BLOCK
    nch = s // CMP_STRIDE
    n_s = s // SEL_BLOCK
    assert s % DIL_SUPER == 0 and n_s <= LANES
    n_top = min(SEL_TOPK, n_s)
    tm = 512
    tm_ffn = 1024 if n % 1024 == 0 else 512
    tf = 256

    slopes_nsa, slopes_dil = _alibi_slopes()
    ov = _overlap(s, nch)
    pick, gexp = _head_pick(), _gate_spread()
    dbias = _dil_bias(slopes_dil)
    w_a, w_gm = _proj_weights(w_in)
    pe2, wa, wb, w2 = _compress_weights(pe_k, pe_v, w_ck1, w_ck2, w_cv1, w_cv2)
    ua = w_up_nsa.astype(BF16)
    ub = w_up_dil.astype(BF16)
    wo = w_out.astype(BF16)
    wfi = w_ffn_in.astype(BF16)
    wfo = w_ffn_out.astype(BF16)
    gf = norm_final.reshape(1, d)

    x2 = x.reshape(n, d)
    for l in range(depth):
        g_mix = norm_mix[l].reshape(1, d)
        qa, kva, qb, kvb, ga = _proj(x2, g_mix, w_a[l], tm)
        qa = qa.reshape(b, s, QA_W)
        kva = kva.reshape(b, s, KVA_W)
        ga = ga.reshape(b, s, GA_W)
        xc = kva[:, :, :NSA_KV_GROUPS * LANES].reshape(b, nch, CMP_STRIDE, NSA_KV_GROUPS, LANES)
        xc = xc.transpose(0, 3, 1, 2, 4).reshape(b, NSA_KV_GROUPS, nch, CMP_STRIDE * LANES)
        kvc = _compress(xc, pe2[l], wa[l], wb[l], w2[l])
        yc, sel, flags = _cmp(slopes_nsa, qa, kvc, ga, ov, pick, gexp, n_top)
        pflags = flags.reshape(-1)
        ya = _selwin(pflags, slopes_nsa, qa, kva, sel, ga, yc, pick, gexp)
        yb = _dil(qb.reshape(-1, b, s, LANES), kvb.reshape(-1, b, s, LANES), dbias, DIL_SUPER)
        x2 = _merge(x2, g_mix, ya.reshape(n, QA_W), yb.reshape(n, DIL_HPG * HEAD_DIM),
                    w_gm[l], ua[l], ub[l], wo[l], tm)
        x2 = _ffn(x2, norm_ffn[l].reshape(1, d), wfi[l], wfo[l], gf, tm_ffn, tf,
                  final_norm=(l == depth - 1))
    return x2.reshape(b, s, d)
```

```python
import functools

import numpy as np
import jax
import jax.numpy as jnp
from jax import lax
from jax.experimental import pallas as pl
from jax.experimental.pallas import tpu as pltpu

F32 = jnp.float32
BF16 = jnp.bfloat16

HEAD_DIM = 64
NSA_HEADS = 8
NSA_KV_GROUPS = 2
NSA_HPG = NSA_HEADS // NSA_KV_GROUPS
CMP_BLOCK = 32
CMP_STRIDE = 16
CMP_HIDDEN = 256
SEL_BLOCK = 64
SEL_TOPK = 16
NSA_WINDOW = 512
Q_BLOCK = 128
FORCE_SCORE = 1e6
DIL_GROUPS = ((128, 1), (512, 4), (2048, 16))
DIL_HPG = 4
DIL_HEADS = DIL_HPG * len(DIL_GROUPS)
DIL_BLOCK = 128
N_ALIBI = NSA_HEADS + DIL_HEADS
EPS = 1e-6
NEG_INF = -1e30
TINY = 1e-30

LANES = 128
GROUP_W = NSA_HPG * HEAD_DIM
DIL_SUPER = 2048
DIL_UNROLL = 4
CMP_TILES = 2
VMEM_LIMIT = 48 * 1024 * 1024


def _dot(a, b):
    return jnp.dot(a, b, preferred_element_type=F32)


def _nt_dot(a, b):
    return lax.dot_general(a, b, (((1,), (1,)), ((), ())), preferred_element_type=F32)


def _iota(shape, dim):
    return lax.broadcasted_iota(jnp.int32, shape, dim)


def _rms(x, g):
    ms = jnp.mean(x * x, axis=-1, keepdims=True)
    return (x * lax.rsqrt(ms + EPS)) * g


def _params(sem):
    return pltpu.CompilerParams(dimension_semantics=sem, vmem_limit_bytes=VMEM_LIMIT)


QA_W, KVA_W, QB_W, KVB_W, GA_W = 512, 768, 768, 1536, 128
PROJ_COLS = QA_W + KVA_W + QB_W + KVB_W + GA_W


def _proj_kernel(x_ref, g_ref, w_ref, qa_ref, kva_ref, qb_ref, kvb_ref, ga_ref):
    y = _rms(x_ref[...], g_ref[...]).astype(BF16)
    c = 0
    qa_ref[...] = _dot(y, w_ref[:, c:c + QA_W]).astype(BF16)
    c += QA_W
    kva_ref[...] = _dot(y, w_ref[:, c:c + KVA_W]).astype(BF16)
    c += KVA_W
    for j in range(0, QB_W // LANES, 2):
        r = _dot(y, w_ref[:, c:c + 2 * LANES])
        qb_ref[j] = r[:, :LANES]
        qb_ref[j + 1] = r[:, LANES:]
        c += 2 * LANES
    for j in range(0, KVB_W // LANES, 2):
        r = _dot(y, w_ref[:, c:c + 2 * LANES])
        kvb_ref[j] = r[:, :LANES]
        kvb_ref[j + 1] = r[:, LANES:]
        c += 2 * LANES
    ga_ref[...] = jax.nn.sigmoid(_dot(y, w_ref[:, c:c + GA_W]))


def _proj(x2, g, w, tm):
    n, d = x2.shape
    row = lambda i: (i, 0)
    fixed = lambda i: (0, 0)
    return pl.pallas_call(
        _proj_kernel,
        grid=(n // tm,),
        in_specs=[pl.BlockSpec((tm, d), row), pl.BlockSpec((1, d), fixed),
                  pl.BlockSpec((d, PROJ_COLS), fixed)],
        out_specs=[pl.BlockSpec((tm, QA_W), row), pl.BlockSpec((tm, KVA_W), row),
                   pl.BlockSpec((QB_W // LANES, tm, LANES), lambda i: (0, i, 0)),
                   pl.BlockSpec((KVB_W // LANES, tm, LANES), lambda i: (0, i, 0)),
                   pl.BlockSpec((tm, GA_W), row)],
        out_shape=[jax.ShapeDtypeStruct((n, QA_W), BF16), jax.ShapeDtypeStruct((n, KVA_W), BF16),
                   jax.ShapeDtypeStruct((QB_W // LANES, n, LANES), F32),
                   jax.ShapeDtypeStruct((KVB_W // LANES, n, LANES), F32),
                   jax.ShapeDtypeStruct((n, GA_W), F32)],
        compiler_params=_params(("parallel",)),
        name="proj",
    )(x2, g, w)


def _compress_kernel(x_ref, pe_ref, wa_ref, wb_ref, w2_ref, o_ref):
    xf = x_ref[0, 0].astype(F32)
    nch = xf.shape[0]
    xa = (xf + pe_ref[0:1, :]).astype(BF16)
    xb = (xf + pe_ref[1:2, :]).astype(BF16)
    ya = _dot(xa, wa_ref[...])
    yb = _dot(xb, wb_ref[...])
    h = ya + pltpu.roll(yb, nch - 1, 0)
    a = (h * jax.nn.sigmoid(h)).astype(BF16)
    o_ref[0, 0] = _dot(a, w2_ref[...]).astype(BF16)


def _compress(xc, pe, wa, wb, w2):
    b, g, nch, f = xc.shape
    hid = wa.shape[1]
    return pl.pallas_call(
        _compress_kernel,
        grid=(b, g),
        in_specs=[pl.BlockSpec((1, 1, nch, f), lambda i, j: (i, j, 0, 0)),
                  pl.BlockSpec((8, f), lambda i, j: (0, 0)),
                  pl.BlockSpec((f, hid), lambda i, j: (0, 0)),
                  pl.BlockSpec((f, hid), lambda i, j: (0, 0)),
                  pl.BlockSpec((hid, LANES), lambda i, j: (0, 0))],
        out_specs=pl.BlockSpec((1, 1, nch, LANES), lambda i, j: (i, j, 0, 0)),
        out_shape=jax.ShapeDtypeStruct((b, g, nch, LANES), BF16),
        compiler_params=_params(("parallel", "parallel")),
        name="compress",
    )(xc, pe, wa, wb, w2)


def _head_pick():
    r = np.arange(GROUP_W)[None, :, None]
    c = np.arange(LANES)[None, None, :]
    h = np.arange(NSA_HPG)[:, None, None]
    return jnp.asarray((r == c + HEAD_DIM * h) & (c < HEAD_DIM), BF16)


def _gate_spread():
    g = np.arange(NSA_KV_GROUPS)[:, None, None, None]
    br = np.arange(3)[None, :, None, None]
    r = np.arange(LANES)[None, None, :, None]
    c = np.arange(GROUP_W)[None, None, None, :]
    return jnp.asarray(r == 3 * (NSA_HPG * g + c // HEAD_DIM) + br, BF16)


def _stack_heads(q, pick_ref):
    return jnp.concatenate([_dot(q, pick_ref[h]).astype(BF16) for h in range(NSA_HPG)], axis=0)


def _unstack_heads(o4):
    lane = _iota((Q_BLOCK, LANES), 1)
    o = [o4[Q_BLOCK * h:Q_BLOCK * (h + 1)] for h in range(NSA_HPG)]
    c0 = jnp.where(lane < HEAD_DIM, pltpu.roll(o[0], HEAD_DIM, 1), o[1])
    c1 = jnp.where(lane < HEAD_DIM, pltpu.roll(o[2], HEAD_DIM, 1), o[3])
    return jnp.concatenate([c0, c1], axis=1)


def _row_sums_all_lanes(a):
    lane = _iota(a.shape, 1)
    return jnp.where(lane < HEAD_DIM, a, pltpu.roll(a, HEAD_DIM, 1))


def _normalise_unstack(a):
    return _unstack_heads(a * (1.0 / jnp.maximum(_row_sums_all_lanes(a), TINY)))


def _expand_gate(gates, spread):
    hi = gates.astype(BF16)
    lo = (gates - hi.astype(F32)).astype(BF16)
    return _dot(hi, spread) + _dot(lo, spread)


def _ones_in_k_lanes(kv):
    lane = _iota(kv.shape, 1)
    return jnp.where(lane < HEAD_DIM, jnp.ones_like(kv), kv)


def _cmp_kernel(slopes_ref, q_ref, kvc_ref, ga_ref, ov_ref, pick_ref, gexp_ref,
                yc_ref, sel_ref, fl_ref, bias0, cq, rhs, *, n_top):
    g = pl.program_id(1)
    i = pl.program_id(2)
    hq = NSA_HPG
    ncp = cq.shape[1]

    @pl.when(i == 0)
    def _():
        cqv = (CMP_STRIDE * _iota((Q_BLOCK, ncp), 1) + (CMP_BLOCK - 1)) - _iota((Q_BLOCK, ncp), 0)
        cq[...] = cqv
        cf = cqv.astype(F32)
        for h in range(hq):
            bias0[h] = slopes_ref[hq * g + h] * cf
        rhs[:, 0:LANES] = _ones_in_k_lanes(kvc_ref[0, 0])
        rhs[:, LANES:2 * LANES] = ov_ref[...]

    for k in range(CMP_TILES):
        it = CMP_TILES * i + k
        rows = slice(k * Q_BLOCK, (k + 1) * Q_BLOCK)
        q4 = _stack_heads(q_ref[0, rows, :], pick_ref)
        s = _nt_dot(q4, kvc_ref[0, 0]).reshape(hq, Q_BLOCK, ncp)
        t0 = it * Q_BLOCK
        neg = jnp.where(cq[...] <= t0, 0.0, NEG_INF)
        tf = t0.astype(F32)
        ps = []
        for h in range(hq):
            sb = (s[h] + bias0[h]) + (neg - slopes_ref[hq * g + h] * tf)
            m = jnp.max(sb, axis=-1, keepdims=True)
            m = jnp.where(m < 0.1 * NEG_INF, -NEG_INF, m)
            ps.append(jnp.exp(sb - m).astype(BF16))
        big = _dot(jnp.concatenate(ps, axis=0), rhs[...])
        inv = 1.0 / jnp.maximum(_row_sums_all_lanes(big[:, 0:LANES]), TINY)
        oc = _unstack_heads(big[:, 0:LANES] * inv)
        yc_ref[0, rows, :] = _expand_gate(ga_ref[0, rows, :], gexp_ref[0, 0]) * oc
        pov = big[:, LANES:2 * LANES] * inv
        imp = pov[0:Q_BLOCK]
        for h in range(1, hq):
            imp = imp + pov[Q_BLOCK * h:Q_BLOCK * (h + 1)]

        qrow = _iota((Q_BLOCK, LANES), 0)
        j = _iota((Q_BLOCK, LANES), 1)
        jt = 2 * it + (qrow >> 6)
        valid = j <= jt
        forced = (j == 0) | (j == jt) | (j == jt - 1)
        score = jnp.where(forced, FORCE_SCORE, jnp.where(valid, imp, -1.0))
        st = score.T
        jr = _iota((LANES, Q_BLOCK), 0)
        cur = st
        taken = -3e38
        for _ in range(n_top):
            mx = jnp.max(cur, axis=0, keepdims=True)
            first = jnp.min(jnp.where(cur == mx, jr, LANES), axis=0, keepdims=True)
            cur = jnp.where(jr == first, taken, cur)
        keep = jnp.where((cur == taken) & (st >= 0.0), 1.0, 0.0).T
        sel_ref[0, 0, rows, :] = keep.astype(BF16)

        fl_ref[k] = (jnp.max(keep, axis=0, keepdims=True) > 0.0).astype(jnp.int32)


def _cmp(slopes, qa, kvc, ga, ov, pick, gexp, n_top):
    b, s, _ = qa.shape
    g = NSA_KV_GROUPS
    nq = s // Q_BLOCK
    ncp = kvc.shape[2]
    tq = CMP_TILES * Q_BLOCK
    nt = s // tq
    return pl.pallas_call(
        functools.partial(_cmp_kernel, n_top=n_top),
        grid=(b, g, nt),
        in_specs=[pl.BlockSpec(memory_space=pltpu.SMEM),
                  pl.BlockSpec((1, tq, GROUP_W), lambda bi, gi, i: (bi, i, gi)),
                  pl.BlockSpec((1, 1, ncp, LANES), lambda bi, gi, i: (bi, gi, 0, 0)),
                  pl.BlockSpec((1, tq, LANES), lambda bi, gi, i: (bi, i, 0)),
                  pl.BlockSpec((ncp, LANES), lambda bi, gi, i: (0, 0)),
                  pl.BlockSpec(pick.shape, lambda bi, gi, i: (0, 0, 0)),
                  pl.BlockSpec((1, 3, LANES, GROUP_W), lambda bi, gi, i: (gi, 0, 0, 0))],
        out_specs=[pl.BlockSpec((1, tq, GROUP_W), lambda bi, gi, i: (bi, i, gi)),
                   pl.BlockSpec((1, 1, tq, LANES), lambda bi, gi, i: (bi, gi, i, 0)),
                   pl.BlockSpec((CMP_TILES, 1, LANES), lambda bi, gi, i: ((bi * g + gi) * nt + i, 0, 0))],
        out_shape=[jax.ShapeDtypeStruct((b, s, g * GROUP_W), F32),
                   jax.ShapeDtypeStruct((b, g, s, LANES), BF16),
                   jax.ShapeDtypeStruct((b * g * nq, 1, LANES), jnp.int32)],
        scratch_shapes=[pltpu.VMEM((NSA_HPG, Q_BLOCK, ncp), F32),
                        pltpu.VMEM((Q_BLOCK, ncp), jnp.int32),
                        pltpu.VMEM((ncp, 2 * LANES), BF16)],
        compiler_params=_params(("parallel", "parallel", "arbitrary")),
        name="cmp",
    )(slopes, qa, kvc, ga, ov, pick, gexp)


WIN_UNITS = NSA_WINDOW // Q_BLOCK + 1
SLOT_CHUNK = 8


def _selwin_kernel(pf_ref, slopes_ref, q_ref, kvs_ref, kvw_ref, sel_ref, ga_ref, yc_ref, pick_ref,
                   gexp_ref, ya_ref, sbuf, slot_unit, mrun, acc, rneg, *, nq):
    b = pl.program_id(0)
    g = pl.program_id(1)
    i = pl.program_id(2)
    hq = NSA_HPG
    qmk = _iota((Q_BLOCK, LANES), 0) - _iota((Q_BLOCK, LANES), 1)

    @pl.when(i == 0)
    def _():
        rel = qmk.astype(F32)
        for h in range(hq):
            rneg[h] = -slopes_ref[hq * g + h] * rel

    q4 = _stack_heads(q_ref[0], pick_ref)
    sel = sel_ref[0, 0]
    dmat = _iota((LANES, LANES), 0) - (_iota((LANES, LANES), 1) >> 6)
    dead = -(1 << 20)

    for d in range(WIN_UNITS):
        u = i - (WIN_UNITS - 1) + d
        slot_unit[d] = jnp.where(u >= 0, u, -1)

    base = ((b * NSA_KV_GROUPS + g) * nq + i) * LANES

    def list_body(jj, k):
        used = (pf_ref[base + 2 * jj] | pf_ref[base + 2 * jj + 1]) != 0

        @pl.when(used)
        def _():
            slot_unit[WIN_UNITS + k] = jj
        return k + used.astype(jnp.int32)

    nsel = lax.fori_loop(0, i + 1, list_body, jnp.int32(0))
    for t in range(SLOT_CHUNK - 1):
        slot_unit[WIN_UNITS + nsel + t] = -1
    nchunk = (nsel + (SLOT_CHUNK - 1)) // SLOT_CHUNK

    def kv_tile(kv_ref, slot):
        u = jnp.maximum(slot_unit[slot], 0)
        return kv_ref[0, pl.ds(pl.multiple_of(u * Q_BLOCK, Q_BLOCK), Q_BLOCK), :]

    def score_slot(kv_ref, slot, window):
        u = slot_unit[slot]
        s = _nt_dot(q4, kv_tile(kv_ref, slot)).reshape(hq, Q_BLOCK, LANES)
        off = jnp.where(u >= 0, (i - u) * Q_BLOCK, dead)
        dist = qmk + off
        if window:
            allow = (dist >= 0) & (dist < NSA_WINDOW)
        else:
            spread = jnp.where(dmat == 2 * u, 1.0, 0.0).astype(BF16)
            allow = (dist >= 0) & (_dot(sel, spread) > 0.5)
        neg = jnp.where(allow, 0.0, NEG_INF)
        offf = off.astype(F32)
        for h in range(hq):
            sb = (s[h] + rneg[h]) + (neg - slopes_ref[hq * g + h] * offf)
            sbuf[slot, h] = sb
            mrun[h] = jnp.maximum(mrun[h], sb)

    def spread_max():
        m = jnp.max(mrun[...], axis=-1, keepdims=True)
        mrun[...] = jnp.broadcast_to(m, mrun.shape)

    def weighted_values(kv_ref, slots):
        tot = None
        for slot in slots:
            p = jnp.exp(sbuf[slot] - mrun[...]).astype(BF16).reshape(hq * Q_BLOCK, LANES)
            d = _dot(p, _ones_in_k_lanes(kv_tile(kv_ref, slot)))
            tot = d if tot is None else tot + d
        return tot

    mrun[...] = jnp.full_like(mrun, NEG_INF)
    for d in range(WIN_UNITS):
        score_slot(kvw_ref, d, True)
    spread_max()
    o_w = _normalise_unstack(weighted_values(kvw_ref, list(range(WIN_UNITS))))

    mrun[...] = jnp.full_like(mrun, NEG_INF)

    def score_chunk(ci, carry):
        for t in range(SLOT_CHUNK):
            score_slot(kvs_ref, WIN_UNITS + ci * SLOT_CHUNK + t, False)
        return carry

    lax.fori_loop(0, nchunk, score_chunk, 0)
    spread_max()
    acc[...] = jnp.zeros_like(acc)

    def value_chunk(ci, carry):
        acc[...] += weighted_values(kvs_ref, [WIN_UNITS + ci * SLOT_CHUNK + t for t in range(SLOT_CHUNK)])
        return carry

    lax.fori_loop(0, nchunk, value_chunk, 0)
    o_s = _normalise_unstack(acc[...])

    gates = ga_ref[0]
    ya = (yc_ref[0] + _expand_gate(gates, gexp_ref[0, 1]) * o_s
          + _expand_gate(gates, gexp_ref[0, 2]) * o_w)
    ya_ref[0] = ya.astype(BF16)


def _selwin(pflags, slopes, qa, kva, sel, ga, yc, pick, gexp):
    b, s, _ = qa.shape
    g = NSA_KV_GROUPS
    nq = s // Q_BLOCK
    nbuf = WIN_UNITS + nq + SLOT_CHUNK - 1
    grid_spec = pltpu.PrefetchScalarGridSpec(
        num_scalar_prefetch=1,
        grid=(b, g, nq),
        in_specs=[pl.BlockSpec(memory_space=pltpu.SMEM),
                  pl.BlockSpec((1, Q_BLOCK, GROUP_W), lambda bi, gi, i, pf: (bi, i, gi)),
                  pl.BlockSpec((1, s, LANES), lambda bi, gi, i, pf: (bi, 0, 2 + gi)),
                  pl.BlockSpec((1, s, LANES), lambda bi, gi, i, pf: (bi, 0, 4 + gi)),
                  pl.BlockSpec((1, 1, Q_BLOCK, LANES), lambda bi, gi, i, pf: (bi, gi, i, 0)),
                  pl.BlockSpec((1, Q_BLOCK, LANES), lambda bi, gi, i, pf: (bi, i, 0)),
                  pl.BlockSpec((1, Q_BLOCK, GROUP_W), lambda bi, gi, i, pf: (bi, i, gi)),
                  pl.BlockSpec(pick.shape, lambda bi, gi, i, pf: (0, 0, 0)),
                  pl.BlockSpec((1, 3, LANES, GROUP_W), lambda bi, gi, i, pf: (gi, 0, 0, 0))],
        out_specs=pl.BlockSpec((1, Q_BLOCK, GROUP_W), lambda bi, gi, i, pf: (bi, i, gi)),
        scratch_shapes=[pltpu.VMEM((nbuf, NSA_HPG, Q_BLOCK, LANES), F32),
                        pltpu.SMEM((nbuf,), jnp.int32),
                        pltpu.VMEM((NSA_HPG, Q_BLOCK, LANES), F32),
                        pltpu.VMEM((NSA_HPG * Q_BLOCK, LANES), F32),
                        pltpu.VMEM((NSA_HPG, Q_BLOCK, LANES), F32)],
    )
    return pl.pallas_call(
        functools.partial(_selwin_kernel, nq=nq),
        grid_spec=grid_spec,
        out_shape=jax.ShapeDtypeStruct((b, s, g * GROUP_W), BF16),
        compiler_params=_params(("parallel", "parallel", "arbitrary")),
        name="selwin",
    )(pflags, slopes, qa, kva, kva, sel, ga, yc, pick, gexp)


def _dil_bias(slopes):
    c = DIL_BLOCK
    qi = np.arange(c)[:, None]
    ki = np.arange(2 * c)[None, :]
    delta = qi + c - ki
    out = []
    for k, (window, dil) in enumerate(DIL_GROUPS):
        in_band = jnp.asarray((delta >= 0) & (delta <= window // dil))
        dist = jnp.asarray((delta * dil).astype(np.float32))
        for h in range(DIL_HPG):
            out.append(jnp.where(in_band, -(slopes[DIL_HPG * k + h] * dist), NEG_INF))
    return jnp.stack(out).reshape(len(out) // 2, 2 * c, 2 * c)


def _dil_group(k, dil, sup, q_ref, kvc_ref, kvp_ref, bias_ref, mx_ref, w_ref, y_ref):
    c = DIL_BLOCK
    nblk = sup // (c * dil)
    has_halo = pl.program_id(1) > 0
    nch = DIL_HPG * HEAD_DIM // LANES
    upper = _iota((c, LANES), 1) >= HEAD_DIM

    def block(idx):
        r = idx // nblk
        n = idx - r * nblk
        start = r + dil * c * n
        rows = pl.ds(start, c, stride=dil)
        rows_prev = pl.ds(jnp.maximum(start - dil * c, r), c, stride=dil)
        rows_halo = pl.ds(r + dil * c * (nblk - 1), c, stride=dil)
        pen = jnp.where(jnp.logical_or(n >= 1, has_halo), 0.0, NEG_INF)

        def prev(ch):
            return jnp.where(n >= 1, kvc_ref[ch, 0, rows_prev, :], kvp_ref[ch, 0, rows_halo, :]).astype(BF16)

        for ch in range(nch):
            qb = q_ref[ch, 0, rows, :].astype(BF16)
            zero = jnp.zeros_like(qb)
            qs = jnp.concatenate([jnp.where(upper, zero, qb), jnp.where(upper, qb, zero)], axis=0)
            kband = jnp.concatenate([prev(ch), kvc_ref[ch, 0, rows, :].astype(BF16)], axis=0)
            vband = jnp.concatenate([prev(nch + ch), kvc_ref[nch + ch, 0, rows, :].astype(BF16)], axis=0)
            s = _nt_dot(qs, kband) + bias_ref[ch]
            s = jnp.concatenate([s[:, :c] + pen, s[:, c:]], axis=1)
            m = jnp.max(s, axis=-1, keepdims=True)
            e = jnp.exp(s - m).astype(BF16)
            ones = jnp.ones((2 * c, LANES), BF16)
            ol = _dot(e, jnp.concatenate([vband, ones], axis=1))
            lt = jnp.maximum(ol[:, LANES:], TINY)
            o2 = ol[:, :LANES] * (1.0 / lt)
            l2 = m + jnp.log(lt)
            o = jnp.where(upper, o2[c:], o2[:c])
            lse = jnp.where(upper, l2[c:], l2[:c])
            if k == 0:
                mx_ref[ch, rows, :] = lse
                w_ref[ch, rows, :] = jnp.ones_like(lse)
                y_ref[ch, rows, :] = o
            else:
                mx_old = mx_ref[ch, rows, :]
                mx_new = jnp.maximum(mx_old, lse)
                a = jnp.exp(mx_old - mx_new)
                bnew = jnp.exp(lse - mx_new)
                mx_ref[ch, rows, :] = mx_new
                w_ref[ch, rows, :] = w_ref[ch, rows, :] * a + bnew
                y_ref[ch, rows, :] = y_ref[ch, rows, :] * a + bnew * o

    def body(t, carry):
        for j in range(DIL_UNROLL):
            block(t * DIL_UNROLL + j)
        return carry

    lax.fori_loop(0, sup // (c * DIL_UNROLL), body, 0)


def _dil_kernel(q_ref, kvc_ref, kvp_ref, bias_ref, yb_ref, mx_ref, w_ref, y_ref, *, sup):
    gi = pl.program_id(2)
    for k, (_, dil) in enumerate(DIL_GROUPS):
        @pl.when(gi == k)
        def _(k=k, dil=dil):
            _dil_group(k, dil, sup, q_ref, kvc_ref, kvp_ref, bias_ref, mx_ref, w_ref, y_ref)

    @pl.when(gi == len(DIL_GROUPS) - 1)
    def _():
        for ch in range(DIL_HPG * HEAD_DIM // LANES):
            yb_ref[0, :, ch * LANES:(ch + 1) * LANES] = (y_ref[ch] * (1.0 / w_ref[ch])).astype(BF16)


def _dil(qb, kvb, bias, sup):
    _, b, s, _ = qb.shape
    hw = DIL_HPG * HEAD_DIM
    nch = hw // LANES
    ng = len(DIL_GROUPS)
    return pl.pallas_call(
        functools.partial(_dil_kernel, sup=sup),
        grid=(b, s // sup, ng),
        in_specs=[pl.BlockSpec((nch, 1, sup, LANES), lambda bi, si, gi: (gi, bi, si, 0)),
                  pl.BlockSpec((2 * nch, 1, sup, LANES), lambda bi, si, gi: (gi, bi, si, 0)),
                  pl.BlockSpec((2 * nch, 1, sup, LANES),
                               lambda bi, si, gi: (gi, bi, jnp.maximum(si - 1, 0), 0)),
                  pl.BlockSpec((nch, 2 * DIL_BLOCK, 2 * DIL_BLOCK), lambda bi, si, gi: (gi, 0, 0))],
        out_specs=pl.BlockSpec((1, sup, hw), lambda bi, si, gi: (bi, si, 0)),
        out_shape=jax.ShapeDtypeStruct((b, s, hw), BF16),
        scratch_shapes=[pltpu.VMEM((nch, sup, LANES), F32), pltpu.VMEM((nch, sup, LANES), F32),
                        pltpu.VMEM((nch, sup, LANES), F32)],
        compiler_params=_params(("parallel", "parallel", "arbitrary")),
        name="dil",
    )(qb, kvb, kvb, bias)


def _merge_kernel(x_ref, g_ref, ya_ref, yb_ref, wg_ref, ua_ref, ub_ref, wo_ref, o_ref):
    x = x_ref[...]
    d = x.shape[1]
    hn = _rms(x, g_ref[...]).astype(BF16)
    gate_a = jax.nn.sigmoid(_dot(hn, wg_ref[:, 0:d]))
    gate_b = jax.nn.sigmoid(_dot(hn, wg_ref[:, d:2 * d]))
    merged = gate_a * _dot(ya_ref[...], ua_ref[...]) + gate_b * _dot(yb_ref[...], ub_ref[...])
    o_ref[...] = x + _dot(merged.astype(BF16), wo_ref[...])


def _merge(x2, g, ya, yb, wg, ua, ub, wo, tm):
    n, d = x2.shape
    row = lambda i: (i, 0)
    fixed = lambda i: (0, 0)
    return pl.pallas_call(
        _merge_kernel,
        grid=(n // tm,),
        in_specs=[pl.BlockSpec((tm, d), row), pl.BlockSpec((1, d), fixed),
                  pl.BlockSpec((tm, ya.shape[1]), row), pl.BlockSpec((tm, yb.shape[1]), row),
                  pl.BlockSpec(wg.shape, fixed), pl.BlockSpec(ua.shape, fixed),
                  pl.BlockSpec(ub.shape, fixed), pl.BlockSpec(wo.shape, fixed)],
        out_specs=pl.BlockSpec((tm, d), row),
        out_shape=jax.ShapeDtypeStruct((n, d), F32),
        compiler_params=_params(("parallel",)),
        name="merge",
    )(x2, g, ya, yb, wg, ua, ub, wo)


def _ffn_kernel(x_ref, g_ref, wg_ref, wu_ref, wo_ref, gf_ref, o_ref, hn_ref, acc_ref, *, final_norm):
    c = pl.program_id(1)

    @pl.when(c == 0)
    def _():
        hn_ref[...] = _rms(x_ref[...], g_ref[...]).astype(BF16)
        acc_ref[...] = jnp.zeros_like(acc_ref)

    hn = hn_ref[...]
    gt = _dot(hn, wg_ref[...])
    up = _dot(hn, wu_ref[...])
    act = ((gt * jax.nn.sigmoid(gt)) * up).astype(BF16)
    acc_ref[...] += _dot(act, wo_ref[...])

    @pl.when(c == pl.num_programs(1) - 1)
    def _():
        y = x_ref[...] + acc_ref[...]
        if final_norm:
            y = _rms(y, gf_ref[...])
        o_ref[...] = y


def _ffn(x2, g, w_in, w_out, gf, tm, tf, final_norm):
    n, d = x2.shape
    dff = w_out.shape[0]
    nc = dff // tf
    return pl.pallas_call(
        functools.partial(_ffn_kernel, final_norm=final_norm),
        grid=(n // tm, nc),
        in_specs=[pl.BlockSpec((tm, d), lambda i, c: (i, 0)),
                  pl.BlockSpec((1, d), lambda i, c: (0, 0)),
                  pl.BlockSpec((d, tf), lambda i, c: (0, c)),
                  pl.BlockSpec((d, tf), lambda i, c: (0, nc + c)),
                  pl.BlockSpec((tf, d), lambda i, c: (c, 0)),
                  pl.BlockSpec((1, d), lambda i, c: (0, 0))],
        out_specs=pl.BlockSpec((tm, d), lambda i, c: (i, 0)),
        out_shape=jax.ShapeDtypeStruct((n, d), F32),
        scratch_shapes=[pltpu.VMEM((tm, d), BF16), pltpu.VMEM((tm, d), F32)],
        compiler_params=_params(("parallel", "arbitrary")),
        name="ffn",
    )(x2, g, w_in, w_in, w_out, gf)


def _alibi_slopes():
    k = jnp.arange(1, N_ALIBI + 1, dtype=F32)
    s = jnp.exp2(-8.0 * k / N_ALIBI)
    nsa = s[2 * DIL_HPG:2 * DIL_HPG + NSA_HEADS]
    dil = jnp.concatenate([s[:2 * DIL_HPG], s[2 * DIL_HPG + NSA_HEADS:]])
    return nsa, dil


def _overlap(seq, ncp):
    n_c = (seq - CMP_BLOCK) // CMP_STRIDE + 1
    n_s = seq // SEL_BLOCK
    c_start = np.arange(n_c) * CMP_STRIDE
    s_start = np.arange(n_s) * SEL_BLOCK
    ov = np.clip(np.minimum(c_start[:, None] + CMP_BLOCK, s_start[None, :] + SEL_BLOCK)
                 - np.maximum(c_start[:, None], s_start[None, :]), 0, None).astype(np.float32) / CMP_BLOCK
    out = np.zeros((ncp, LANES), np.float32)
    out[:n_c, :n_s] = ov
    return jnp.asarray(out, BF16)


def _proj_weights(w_in):
    dep, d, _ = w_in.shape
    scale = HEAD_DIM ** -0.5
    c0 = NSA_HEADS * HEAD_DIM
    c1 = c0 + 3 * 2 * NSA_KV_GROUPS * HEAD_DIM
    c2 = c1 + NSA_HEADS * 3
    c3 = c2 + 3 * DIL_HEADS * HEAD_DIM
    qa = w_in[..., :c0] * scale
    kva = w_in[..., c0:c1].reshape(dep, d, 3, 2, NSA_KV_GROUPS, HEAD_DIM)
    kva = kva.transpose(0, 1, 2, 4, 3, 5).reshape(dep, d, KVA_W)
    ga = jnp.pad(w_in[..., c1:c2], ((0, 0), (0, 0), (0, GA_W - (c2 - c1))))
    qkvb = w_in[..., c2:c3].reshape(dep, d, 3, len(DIL_GROUPS), DIL_HPG * HEAD_DIM)
    qb = (qkvb[:, :, 0] * scale).reshape(dep, d, QB_W)
    kvb = qkvb[:, :, 1:3].transpose(0, 1, 3, 2, 4).reshape(dep, d, KVB_W)
    w_a = jnp.concatenate([qa, kva, qb, kvb, ga], axis=-1).astype(BF16)
    return w_a, w_in[..., c3:].astype(BF16)


def _compress_weights(pe_k, pe_v, w_ck1, w_ck2, w_cv1, w_cv2):
    dep = pe_k.shape[0]
    half = CMP_BLOCK // 2
    ck = w_ck1.reshape(dep, CMP_BLOCK, HEAD_DIM, CMP_HIDDEN)
    cv = w_cv1.reshape(dep, CMP_BLOCK, HEAD_DIM, CMP_HIDDEN)
    z = jnp.zeros_like(ck)
    w = jnp.concatenate([jnp.concatenate([ck, z], axis=-1),
                         jnp.concatenate([z, cv], axis=-1)], axis=2)
    feat = half * 2 * HEAD_DIM
    wa = w[:, :half].reshape(dep, feat, 2 * CMP_HIDDEN).astype(BF16)
    wb = w[:, half:].reshape(dep, feat, 2 * CMP_HIDDEN).astype(BF16)
    pe = jnp.concatenate([pe_k, pe_v], axis=-1)
    pe2 = jnp.stack([pe[:, :half].reshape(dep, feat), pe[:, half:].reshape(dep, feat)], axis=1)
    pe2 = jnp.pad(pe2, ((0, 0), (0, 6), (0, 0)))
    z2 = jnp.zeros_like(w_ck2)
    w2 = jnp.concatenate([jnp.concatenate([w_ck2, z2], axis=-1),
                          jnp.concatenate([z2, w_cv2], axis=-1)], axis=1).astype(BF16)
    return pe2, wa, wb, w2


def kernel(x, norm_mix, w_in, pe_k, pe_v, w_ck1, w_ck2, w_cv1, w_cv2, w_up_nsa, w_up_dil,
           w_out, norm_ffn, w_ffn_in, w_ffn_out, norm_final):
    b, s, d = x.shape
    depth = w_in.shape[0]
    n = b * s
    nq = s // Q_BLOCK
    nch = s // CMP_STRIDE
    n_s = s // SEL_BLOCK
    assert s % DIL_SUPER == 0 and n_s <= LANES
    n_top = min(SEL_TOPK, n_s)
    tm = 512
    tm_ffn = 512
    tf = w_ffn_out.shape[1] // 2

    slopes_nsa, slopes_dil = _alibi_slopes()
    ov = _overlap(s, nch)
    pick, gexp = _head_pick(), _gate_spread()
    dbias = _dil_bias(slopes_dil)
    w_a, w_gm = _proj_weights(w_in)
    pe2, wa, wb, w2 = _compress_weights(pe_k, pe_v, w_ck1, w_ck2, w_cv1, w_cv2)
    ua = w_up_nsa.astype(BF16)
    ub = w_up_dil.astype(BF16)
    wo = w_out.astype(BF16)
    wfi = w_ffn_in.astype(BF16)
    wfo = w_ffn_out.astype(BF16)
    gf = norm_final.reshape(1, d)

    x2 = x.reshape(n, d)
    for l in range(depth):
        g_mix = norm_mix[l].reshape(1, d)
        qa, kva, qb, kvb, ga = _proj(x2, g_mix, w_a[l], tm)
        qa = qa.reshape(b, s, QA_W)
        kva = kva.reshape(b, s, KVA_W)
        ga = ga.reshape(b, s, GA_W)
        xc = kva[:, :, :NSA_KV_GROUPS * LANES].reshape(b, nch, CMP_STRIDE, NSA_KV_GROUPS, LANES)
        xc = xc.transpose(0, 3, 1, 2, 4).reshape(b, NSA_KV_GROUPS, nch, CMP_STRIDE * LANES)
        kvc = _compress(xc, pe2[l], wa[l], wb[l], w2[l])
        yc, sel, flags = _cmp(slopes_nsa, qa, kvc, ga, ov, pick, gexp, n_top)
        pflags = flags.reshape(-1)
        ya = _selwin(pflags, slopes_nsa, qa, kva, sel, ga, yc, pick, gexp)
        yb = _dil(qb.reshape(-1, b, s, LANES), kvb.reshape(-1, b, s, LANES), dbias, DIL_SUPER)
        x2 = _merge(x2, g_mix, ya.reshape(n, QA_W), yb.reshape(n, DIL_HPG * HEAD_DIM),
                    w_gm[l], ua[l], ub[l], wo[l], tm)
        x2 = _ffn(x2, norm_ffn[l].reshape(1, d), wfi[l], wfo[l], gf, tm_ffn, tf,
                  final_norm=(l == depth - 1))
    return x2.reshape(b, s, d)
```

```python
import functools

import numpy as np
import jax
import jax.numpy as jnp
from jax import lax
from jax.experimental import pallas as pl
from jax.experimental.pallas import tpu as pltpu

F32 = jnp.float32
BF16 = jnp.bfloat16

HEAD_DIM = 64
NSA_HEADS = 8
NSA_KV_GROUPS = 2
NSA_HPG = NSA_HEADS // NSA_KV_GROUPS
CMP_BLOCK = 32
CMP_STRIDE = 16
CMP_HIDDEN = 256
SEL_BLOCK = 64
SEL_TOPK = 16
NSA_WINDOW = 512
Q_BLOCK = 128
FORCE_SCORE = 1e6
DIL_GROUPS = ((128, 1), (512, 4), (2048, 16))
DIL_HPG = 4
DIL_HEADS = DIL_HPG * len(DIL_GROUPS)
DIL_BLOCK = 128
N_ALIBI = NSA_HEADS + DIL_HEADS
EPS = 1e-6
NEG_INF = -1e30
TINY = 1e-30
LOG2E = 1.4426950408889634
LN2 = 0.6931471805599453

LANES = 128
GROUP_W = NSA_HPG * HEAD_DIM
DIL_SUPER = 2048
DIL_UNROLL = 4
CMP_TILES = 2
VMEM_LIMIT = 48 * 1024 * 1024


def _dot(a, b):
    return jnp.dot(a, b, preferred_element_type=F32)


def _nt_dot(a, b):
    return lax.dot_general(a, b, (((1,), (1,)), ((), ())), preferred_element_type=F32)


def _iota(shape, dim):
    return lax.broadcasted_iota(jnp.int32, shape, dim)


def _rms(x, g):
    ms = jnp.mean(x * x, axis=-1, keepdims=True)
    return (x * lax.rsqrt(ms + EPS)) * g


def _params(sem):
    return pltpu.CompilerParams(dimension_semantics=sem, vmem_limit_bytes=VMEM_LIMIT)


QA_W, KVA_W, QB_W, KVB_W, GA_W = 512, 768, 768, 1536, 128
PROJ_COLS = QA_W + KVA_W + QB_W + KVB_W + GA_W


def _proj_kernel(x_ref, g_ref, w_ref, qa_ref, kva_ref, qb_ref, kvb_ref, ga_ref):
    y = _rms(x_ref[...], g_ref[...]).astype(BF16)
    c = 0
    qa_ref[...] = _dot(y, w_ref[:, c:c + QA_W]).astype(BF16)
    c += QA_W
    kva_ref[...] = _dot(y, w_ref[:, c:c + KVA_W]).astype(BF16)
    c += KVA_W
    for j in range(0, QB_W // LANES, 2):
        r = _dot(y, w_ref[:, c:c + 2 * LANES])
        qb_ref[j] = r[:, :LANES]
        qb_ref[j + 1] = r[:, LANES:]
        c += 2 * LANES
    for j in range(0, KVB_W // LANES, 2):
        r = _dot(y, w_ref[:, c:c + 2 * LANES])
        kvb_ref[j] = r[:, :LANES]
        kvb_ref[j + 1] = r[:, LANES:]
        c += 2 * LANES
    ga_ref[...] = jax.nn.sigmoid(_dot(y, w_ref[:, c:c + GA_W]))


def _proj(x2, g, w, tm):
    n, d = x2.shape
    row = lambda i: (i, 0)
    fixed = lambda i: (0, 0)
    return pl.pallas_call(
        _proj_kernel,
        grid=(n // tm,),
        in_specs=[pl.BlockSpec((tm, d), row), pl.BlockSpec((1, d), fixed),
                  pl.BlockSpec((d, PROJ_COLS), fixed)],
        out_specs=[pl.BlockSpec((tm, QA_W), row), pl.BlockSpec((tm, KVA_W), row),
                   pl.BlockSpec((QB_W // LANES, tm, LANES), lambda i: (0, i, 0)),
                   pl.BlockSpec((KVB_W // LANES, tm, LANES), lambda i: (0, i, 0)),
                   pl.BlockSpec((tm, GA_W), row)],
        out_shape=[jax.ShapeDtypeStruct((n, QA_W), BF16), jax.ShapeDtypeStruct((n, KVA_W), BF16),
                   jax.ShapeDtypeStruct((QB_W // LANES, n, LANES), F32),
                   jax.ShapeDtypeStruct((KVB_W // LANES, n, LANES), F32),
                   jax.ShapeDtypeStruct((n, GA_W), F32)],
        compiler_params=_params(("parallel",)),
        name="proj",
    )(x2, g, w)


def _compress_kernel(x_ref, pe_ref, wa_ref, wb_ref, w2_ref, o_ref):
    xf = x_ref[0, 0].astype(F32)
    nch = xf.shape[0]
    xa = (xf + pe_ref[0:1, :]).astype(BF16)
    xb = (xf + pe_ref[1:2, :]).astype(BF16)
    ya = _dot(xa, wa_ref[...])
    yb = _dot(xb, wb_ref[...])
    h = ya + pltpu.roll(yb, nch - 1, 0)
    a = (h * jax.nn.sigmoid(h)).astype(BF16)
    o_ref[0, 0] = _dot(a, w2_ref[...]).astype(BF16)


def _compress(xc, pe, wa, wb, w2):
    b, g, nch, f = xc.shape
    hid = wa.shape[1]
    return pl.pallas_call(
        _compress_kernel,
        grid=(b, g),
        in_specs=[pl.BlockSpec((1, 1, nch, f), lambda i, j: (i, j, 0, 0)),
                  pl.BlockSpec((8, f), lambda i, j: (0, 0)),
                  pl.BlockSpec((f, hid), lambda i, j: (0, 0)),
                  pl.BlockSpec((f, hid), lambda i, j: (0, 0)),
                  pl.BlockSpec((hid, LANES), lambda i, j: (0, 0))],
        out_specs=pl.BlockSpec((1, 1, nch, LANES), lambda i, j: (i, j, 0, 0)),
        out_shape=jax.ShapeDtypeStruct((b, g, nch, LANES), BF16),
        compiler_params=_params(("parallel", "parallel")),
        name="compress",
    )(xc, pe, wa, wb, w2)


def _head_pick():
    r = np.arange(GROUP_W)[None, :, None]
    c = np.arange(LANES)[None, None, :]
    h = np.arange(NSA_HPG)[:, None, None]
    return jnp.asarray((r == c + HEAD_DIM * h) & (c < HEAD_DIM), BF16)


def _gate_spread():
    g = np.arange(NSA_KV_GROUPS)[:, None, None, None]
    br = np.arange(3)[None, :, None, None]
    r = np.arange(LANES)[None, None, :, None]
    c = np.arange(GROUP_W)[None, None, None, :]
    return jnp.asarray(r == 3 * (NSA_HPG * g + c // HEAD_DIM) + br, BF16)


def _stack_heads(q, pick_ref):
    return jnp.concatenate([_dot(q, pick_ref[h]).astype(BF16) for h in range(NSA_HPG)], axis=0)


def _unstack_heads(o4):
    lane = _iota((Q_BLOCK, LANES), 1)
    o = [o4[Q_BLOCK * h:Q_BLOCK * (h + 1)] for h in range(NSA_HPG)]
    c0 = jnp.where(lane < HEAD_DIM, pltpu.roll(o[0], HEAD_DIM, 1), o[1])
    c1 = jnp.where(lane < HEAD_DIM, pltpu.roll(o[2], HEAD_DIM, 1), o[3])
    return jnp.concatenate([c0, c1], axis=1)


def _row_sums_all_lanes(a):
    lane = _iota(a.shape, 1)
    return jnp.where(lane < HEAD_DIM, a, pltpu.roll(a, HEAD_DIM, 1))


def _normalise_unstack(a):
    return _unstack_heads(a * (1.0 / jnp.maximum(_row_sums_all_lanes(a), TINY)))


def _expand_gate(gates, spread):
    hi = gates.astype(BF16)
    lo = (gates - hi.astype(F32)).astype(BF16)
    return _dot(hi, spread) + _dot(lo, spread)


def _ones_in_k_lanes(kv):
    lane = _iota(kv.shape, 1)
    return jnp.where(lane < HEAD_DIM, jnp.ones_like(kv), kv)


def _cmp_kernel(slopes_ref, q_ref, kvc_ref, ga_ref, ov_ref, pick_ref, gexp_ref,
                yc_ref, sel_ref, fl_ref, bias0, cq, rhs, *, n_top):
    g = pl.program_id(1)
    i = pl.program_id(2)
    hq = NSA_HPG
    ncp = cq.shape[1]

    @pl.when(i == 0)
    def _():
        kpos = CMP_STRIDE * _iota((Q_BLOCK, ncp), 1) + (CMP_BLOCK - 1)
        cq[...] = kpos - _iota((Q_BLOCK, ncp), 0)
        kf = kpos[0:8].astype(F32)
        for h in range(hq):
            bias0[h] = slopes_ref[hq * g + h] * kf
        rhs[:, 0:LANES] = _ones_in_k_lanes(kvc_ref[0, 0])
        rhs[:, LANES:2 * LANES] = ov_ref[...]

    for k in range(CMP_TILES):
        it = CMP_TILES * i + k
        rows = slice(k * Q_BLOCK, (k + 1) * Q_BLOCK)
        q4 = _stack_heads(q_ref[0, rows, :], pick_ref)
        s = _nt_dot(q4, kvc_ref[0, 0]).reshape(hq, Q_BLOCK, ncp)
        t0 = it * Q_BLOCK
        neg = jnp.where(cq[...] <= t0, 0.0, NEG_INF)
        tf = t0.astype(F32)
        ps = []
        for h in range(hq):
            sb = s[h] + (neg + (bias0[h, 0:1] - slopes_ref[hq * g + h] * tf))
            m = jnp.max(sb, axis=-1, keepdims=True)
            m = jnp.where(m < 0.1 * NEG_INF, -NEG_INF, m)
            ps.append(jnp.exp2(sb - m).astype(BF16))
        big = _dot(jnp.concatenate(ps, axis=0), rhs[...])
        inv = 1.0 / jnp.maximum(_row_sums_all_lanes(big[:, 0:LANES]), TINY)
        oc = _unstack_heads(big[:, 0:LANES] * inv)
        yc_ref[0, rows, :] = _expand_gate(ga_ref[0, rows, :], gexp_ref[0, 0]) * oc
        pov = big[:, LANES:2 * LANES] * inv
        imp = pov[0:Q_BLOCK]
        for h in range(1, hq):
            imp = imp + pov[Q_BLOCK * h:Q_BLOCK * (h + 1)]

        qrow = _iota((Q_BLOCK, LANES), 0)
        j = _iota((Q_BLOCK, LANES), 1)
        jt = 2 * it + (qrow >> 6)
        valid = j <= jt
        forced = (j == 0) | (j == jt) | (j == jt - 1)
        score = jnp.where(forced, FORCE_SCORE, jnp.where(valid, imp, -1.0))
        st = score.T
        jr = _iota((LANES, Q_BLOCK), 0)
        cur = st
        taken = -3e38
        for _ in range(n_top):
            mx = jnp.max(cur, axis=0, keepdims=True)
            first = jnp.min(jnp.where(cur == mx, jr, LANES), axis=0, keepdims=True)
            cur = jnp.where(jr == first, taken, cur)
        keep = jnp.where((cur == taken) & (st >= 0.0), 1.0, 0.0).T
        sel_ref[0, 0, rows, :] = keep.astype(BF16)

        fl_ref[k] = (jnp.max(keep, axis=0, keepdims=True) > 0.0).astype(jnp.int32)


def _cmp(slopes, qa, kvc, ga, ov, pick, gexp, n_top):
    b, s, _ = qa.shape
    g = NSA_KV_GROUPS
    nq = s // Q_BLOCK
    ncp = kvc.shape[2]
    tq = CMP_TILES * Q_BLOCK
    nt = s // tq
    return pl.pallas_call(
        functools.partial(_cmp_kernel, n_top=n_top),
        grid=(b, g, nt),
        in_specs=[pl.BlockSpec(memory_space=pltpu.SMEM),
                  pl.BlockSpec((1, tq, GROUP_W), lambda bi, gi, i: (bi, i, gi)),
                  pl.BlockSpec((1, 1, ncp, LANES), lambda bi, gi, i: (bi, gi, 0, 0)),
                  pl.BlockSpec((1, tq, LANES), lambda bi, gi, i: (bi, i, 0)),
                  pl.BlockSpec((ncp, LANES), lambda bi, gi, i: (0, 0)),
                  pl.BlockSpec(pick.shape, lambda bi, gi, i: (0, 0, 0)),
                  pl.BlockSpec((1, 3, LANES, GROUP_W), lambda bi, gi, i: (gi, 0, 0, 0))],
        out_specs=[pl.BlockSpec((1, tq, GROUP_W), lambda bi, gi, i: (bi, i, gi)),
                   pl.BlockSpec((1, 1, tq, LANES), lambda bi, gi, i: (bi, gi, i, 0)),
                   pl.BlockSpec((CMP_TILES, 1, LANES), lambda bi, gi, i: ((bi * g + gi) * nt + i, 0, 0))],
        out_shape=[jax.ShapeDtypeStruct((b, s, g * GROUP_W), F32),
                   jax.ShapeDtypeStruct((b, g, s, LANES), BF16),
                   jax.ShapeDtypeStruct((b * g * nq, 1, LANES), jnp.int32)],
        scratch_shapes=[pltpu.VMEM((NSA_HPG, 8, ncp), F32),
                        pltpu.VMEM((Q_BLOCK, ncp), jnp.int32),
                        pltpu.VMEM((ncp, 2 * LANES), BF16)],
        compiler_params=_params(("parallel", "parallel", "arbitrary")),
        name="cmp",
    )(slopes, qa, kvc, ga, ov, pick, gexp)


WIN_UNITS = NSA_WINDOW // Q_BLOCK + 1
SLOT_CHUNK = 8


def _selwin_kernel(pf_ref, slopes_ref, q_ref, kvs_ref, kvw_ref, sel_ref, ga_ref, yc_ref, pick_ref,
                   gexp_ref, ya_ref, sbuf, slot_unit, mrun, acc, *, nq):
    b = pl.program_id(0)
    g = pl.program_id(1)
    i = pl.program_id(2)
    hq = NSA_HPG
    qmk = _iota((Q_BLOCK, LANES), 0) - _iota((Q_BLOCK, LANES), 1)
    koff = _iota((1, LANES), 1).astype(F32)
    q4 = _stack_heads(q_ref[0], pick_ref)
    sel = sel_ref[0, 0]
    dmat = _iota((LANES, LANES), 0) - (_iota((LANES, LANES), 1) >> 6)
    dead = -(1 << 20)

    for d in range(WIN_UNITS):
        u = i - (WIN_UNITS - 1) + d
        slot_unit[d] = jnp.where(u >= 0, u, -1)

    base = ((b * NSA_KV_GROUPS + g) * nq + i) * LANES

    def list_body(jj, k):
        used = (pf_ref[base + 2 * jj] | pf_ref[base + 2 * jj + 1]) != 0

        @pl.when(used)
        def _():
            slot_unit[WIN_UNITS + k] = jj
        return k + used.astype(jnp.int32)

    nsel = lax.fori_loop(0, i + 1, list_body, jnp.int32(0))
    for t in range(SLOT_CHUNK - 1):
        slot_unit[WIN_UNITS + nsel + t] = -1
    nchunk = (nsel + (SLOT_CHUNK - 1)) // SLOT_CHUNK

    def kv_tile(kv_ref, slot):
        u = jnp.maximum(slot_unit[slot], 0)
        return kv_ref[0, pl.ds(pl.multiple_of(u * Q_BLOCK, Q_BLOCK), Q_BLOCK), :]

    def score_slot(kv_ref, slot, window):
        u = slot_unit[slot]
        s = _nt_dot(q4, kv_tile(kv_ref, slot)).reshape(hq, Q_BLOCK, LANES)
        off = jnp.where(u >= 0, (i - u) * Q_BLOCK, dead)
        dist = qmk + off
        if window:
            allow = (dist >= 0) & (dist < NSA_WINDOW)
        else:
            spread = jnp.where(dmat == 2 * u, 1.0, 0.0).astype(BF16)
            allow = (dist >= 0) & (_dot(sel, spread) > 0.5)
        neg = jnp.where(allow, 0.0, NEG_INF)
        kd = koff - off.astype(F32)
        for h in range(hq):
            sb = s[h] + (neg + slopes_ref[hq * g + h] * kd)
            sbuf[slot, h] = sb
            mrun[h] = jnp.maximum(mrun[h], sb)

    def spread_max():
        m = jnp.max(mrun[...], axis=-1, keepdims=True)
        mrun[...] = jnp.broadcast_to(m, mrun.shape)

    def weighted_values(kv_ref, slots):
        tot = None
        for slot in slots:
            p = jnp.exp2(sbuf[slot] - mrun[...]).astype(BF16).reshape(hq * Q_BLOCK, LANES)
            d = _dot(p, _ones_in_k_lanes(kv_tile(kv_ref, slot)))
            tot = d if tot is None else tot + d
        return tot

    mrun[...] = jnp.full_like(mrun, NEG_INF)
    for d in range(WIN_UNITS):
        score_slot(kvw_ref, d, True)
    spread_max()
    o_w = _normalise_unstack(weighted_values(kvw_ref, list(range(WIN_UNITS))))

    mrun[...] = jnp.full_like(mrun, NEG_INF)

    def score_chunk(ci, carry):
        for t in range(SLOT_CHUNK):
            score_slot(kvs_ref, WIN_UNITS + ci * SLOT_CHUNK + t, False)
        return carry

    lax.fori_loop(0, nchunk, score_chunk, 0)
    spread_max()
    acc[...] = jnp.zeros_like(acc)

    def value_chunk(ci, carry):
        acc[...] += weighted_values(kvs_ref, [WIN_UNITS + ci * SLOT_CHUNK + t for t in range(SLOT_CHUNK)])
        return carry

    lax.fori_loop(0, nchunk, value_chunk, 0)
    o_s = _normalise_unstack(acc[...])

    gates = ga_ref[0]
    ya = (yc_ref[0] + _expand_gate(gates, gexp_ref[0, 1]) * o_s
          + _expand_gate(gates, gexp_ref[0, 2]) * o_w)
    ya_ref[0] = ya.astype(BF16)


def _selwin(pflags, slopes, qa, kva, sel, ga, yc, pick, gexp):
    b, s, _ = qa.shape
    g = NSA_KV_GROUPS
    nq = s // Q_BLOCK
    nbuf = WIN_UNITS + nq + SLOT_CHUNK - 1
    grid_spec = pltpu.PrefetchScalarGridSpec(
        num_scalar_prefetch=1,
        grid=(b, g, nq),
        in_specs=[pl.BlockSpec(memory_space=pltpu.SMEM),
                  pl.BlockSpec((1, Q_BLOCK, GROUP_W), lambda bi, gi, i, pf: (bi, i, gi)),
                  pl.BlockSpec((1, s, LANES), lambda bi, gi, i, pf: (bi, 0, 2 + gi)),
                  pl.BlockSpec((1, s, LANES), lambda bi, gi, i, pf: (bi, 0, 4 + gi)),
                  pl.BlockSpec((1, 1, Q_BLOCK, LANES), lambda bi, gi, i, pf: (bi, gi, i, 0)),
                  pl.BlockSpec((1, Q_BLOCK, LANES), lambda bi, gi, i, pf: (bi, i, 0)),
                  pl.BlockSpec((1, Q_BLOCK, GROUP_W), lambda bi, gi, i, pf: (bi, i, gi)),
                  pl.BlockSpec(pick.shape, lambda bi, gi, i, pf: (0, 0, 0)),
                  pl.BlockSpec((1, 3, LANES, GROUP_W), lambda bi, gi, i, pf: (gi, 0, 0, 0))],
        out_specs=pl.BlockSpec((1, Q_BLOCK, GROUP_W), lambda bi, gi, i, pf: (bi, i, gi)),
        scratch_shapes=[pltpu.VMEM((nbuf, NSA_HPG, Q_BLOCK, LANES), F32),
                        pltpu.SMEM((nbuf,), jnp.int32),
                        pltpu.VMEM((NSA_HPG, Q_BLOCK, LANES), F32),
                        pltpu.VMEM((NSA_HPG * Q_BLOCK, LANES), F32)],
    )
    return pl.pallas_call(
        functools.partial(_selwin_kernel, nq=nq),
        grid_spec=grid_spec,
        out_shape=jax.ShapeDtypeStruct((b, s, g * GROUP_W), BF16),
        compiler_params=_params(("parallel", "parallel", "arbitrary")),
        name="selwin",
    )(pflags, slopes, qa, kva, kva, sel, ga, yc, pick, gexp)


def _dil_bias(slopes):
    c = DIL_BLOCK
    qi = np.arange(c)[:, None]
    ki = np.arange(2 * c)[None, :]
    delta = qi + c - ki
    out = []
    for k, (window, dil) in enumerate(DIL_GROUPS):
        in_band = jnp.asarray((delta >= 0) & (delta <= window // dil))
        dist = jnp.asarray((delta * dil).astype(np.float32))
        for h in range(DIL_HPG):
            out.append(jnp.where(in_band, -(slopes[DIL_HPG * k + h] * dist) * LOG2E, NEG_INF))
    return jnp.stack(out).reshape(len(out) // 2, 2 * c, 2 * c)


def _dil_group(k, dil, sup, q_ref, kvc_ref, kvp_ref, bias_ref, mx_ref, w_ref, y_ref):
    c = DIL_BLOCK
    nblk = sup // (c * dil)
    has_halo = pl.program_id(1) > 0
    nch = DIL_HPG * HEAD_DIM // LANES
    upper = _iota((c, LANES), 1) >= HEAD_DIM

    def block(idx):
        r = idx // nblk
        n = idx - r * nblk
        start = r + dil * c * n
        rows = pl.ds(start, c, stride=dil)
        rows_prev = pl.ds(jnp.maximum(start - dil * c, r), c, stride=dil)
        rows_halo = pl.ds(r + dil * c * (nblk - 1), c, stride=dil)
        pen = jnp.where(jnp.logical_or(n >= 1, has_halo), 0.0, NEG_INF)

        def prev(ch):
            return jnp.where(n >= 1, kvc_ref[ch, 0, rows_prev, :], kvp_ref[ch, 0, rows_halo, :]).astype(BF16)

        for ch in range(nch):
            qb = q_ref[ch, 0, rows, :].astype(BF16)
            zero = jnp.zeros_like(qb)
            qs = jnp.concatenate([jnp.where(upper, zero, qb), jnp.where(upper, qb, zero)], axis=0)
            kband = jnp.concatenate([prev(ch), kvc_ref[ch, 0, rows, :].astype(BF16)], axis=0)
            vband = jnp.concatenate([prev(nch + ch), kvc_ref[nch + ch, 0, rows, :].astype(BF16)], axis=0)
            s = _nt_dot(qs, kband) + bias_ref[ch]
            s = jnp.concatenate([s[:, :c] + pen, s[:, c:]], axis=1)
            m = jnp.max(s, axis=-1, keepdims=True)
            e = jnp.exp2(s - m).astype(BF16)
            ones = jnp.ones((2 * c, LANES), BF16)
            ol = _dot(e, jnp.concatenate([vband, ones], axis=1))
            lt = jnp.maximum(ol[:, LANES:], TINY)
            o2 = ol[:, :LANES] * (1.0 / lt)
            l2 = m * LN2 + jnp.log(lt)
            o = jnp.where(upper, o2[c:], o2[:c])
            lse = jnp.where(upper, l2[c:], l2[:c])
            if k == 0:
                mx_ref[ch, rows, :] = lse
                w_ref[ch, rows, :] = jnp.ones_like(lse)
                y_ref[ch, rows, :] = o
            else:
                mx_old = mx_ref[ch, rows, :]
                mx_new = jnp.maximum(mx_old, lse)
                a = jnp.exp(mx_old - mx_new)
                bnew = jnp.exp(lse - mx_new)
                mx_ref[ch, rows, :] = mx_new
                w_ref[ch, rows, :] = w_ref[ch, rows, :] * a + bnew
                y_ref[ch, rows, :] = y_ref[ch, rows, :] * a + bnew * o

    def body(t, carry):
        for j in range(DIL_UNROLL):
            block(t * DIL_UNROLL + j)
        return carry

    lax.fori_loop(0, sup // (c * DIL_UNROLL), body, 0)


def _dil_kernel(q_ref, kvc_ref, kvp_ref, bias_ref, yb_ref, mx_ref, w_ref, y_ref, *, sup):
    gi = pl.program_id(2)
    for k, (_, dil) in enumerate(DIL_GROUPS):
        @pl.when(gi == k)
        def _(k=k, dil=dil):
            _dil_group(k, dil, sup, q_ref, kvc_ref, kvp_ref, bias_ref, mx_ref, w_ref, y_ref)

    @pl.when(gi == len(DIL_GROUPS) - 1)
    def _():
        for ch in range(DIL_HPG * HEAD_DIM // LANES):
            yb_ref[0, :, ch * LANES:(ch + 1) * LANES] = (y_ref[ch] * (1.0 / w_ref[ch])).astype(BF16)


def _dil(qb, kvb, bias, sup):
    _, b, s, _ = qb.shape
    hw = DIL_HPG * HEAD_DIM
    nch = hw // LANES
    ng = len(DIL_GROUPS)
    return pl.pallas_call(
        functools.partial(_dil_kernel, sup=sup),
        grid=(b, s // sup, ng),
        in_specs=[pl.BlockSpec((nch, 1, sup, LANES), lambda bi, si, gi: (gi, bi, si, 0)),
                  pl.BlockSpec((2 * nch, 1, sup, LANES), lambda bi, si, gi: (gi, bi, si, 0)),
                  pl.BlockSpec((2 * nch, 1, sup, LANES),
                               lambda bi, si, gi: (gi, bi, jnp.maximum(si - 1, 0), 0)),
                  pl.BlockSpec((nch, 2 * DIL_BLOCK, 2 * DIL_BLOCK), lambda bi, si, gi: (gi, 0, 0))],
        out_specs=pl.BlockSpec((1, sup, hw), lambda bi, si, gi: (bi, si, 0)),
        out_shape=jax.ShapeDtypeStruct((b, s, hw), BF16),
        scratch_shapes=[pltpu.VMEM((nch, sup, LANES), F32), pltpu.VMEM((nch, sup, LANES), F32),
                        pltpu.VMEM((nch, sup, LANES), F32)],
        compiler_params=_params(("parallel", "parallel", "arbitrary")),
        name="dil",
    )(qb, kvb, kvb, bias)


def _merge_kernel(x_ref, g_ref, ya_ref, yb_ref, wg_ref, ua_ref, ub_ref, wo_ref, o_ref):
    x = x_ref[...]
    d = x.shape[1]
    hn = _rms(x, g_ref[...]).astype(BF16)
    gate_a = jax.nn.sigmoid(_dot(hn, wg_ref[:, 0:d]))
    gate_b = jax.nn.sigmoid(_dot(hn, wg_ref[:, d:2 * d]))
    merged = gate_a * _dot(ya_ref[...], ua_ref[...]) + gate_b * _dot(yb_ref[...], ub_ref[...])
    o_ref[...] = x + _dot(merged.astype(BF16), wo_ref[...])


def _merge(x2, g, ya, yb, wg, ua, ub, wo, tm):
    n, d = x2.shape
    row = lambda i: (i, 0)
    fixed = lambda i: (0, 0)
    return pl.pallas_call(
        _merge_kernel,
        grid=(n // tm,),
        in_specs=[pl.BlockSpec((tm, d), row), pl.BlockSpec((1, d), fixed),
                  pl.BlockSpec((tm, ya.shape[1]), row), pl.BlockSpec((tm, yb.shape[1]), row),
                  pl.BlockSpec(wg.shape, fixed), pl.BlockSpec(ua.shape, fixed),
                  pl.BlockSpec(ub.shape, fixed), pl.BlockSpec(wo.shape, fixed)],
        out_specs=pl.BlockSpec((tm, d), row),
        out_shape=jax.ShapeDtypeStruct((n, d), F32),
        compiler_params=_params(("parallel",)),
        name="merge",
    )(x2, g, ya, yb, wg, ua, ub, wo)


def _ffn_kernel(x_ref, g_ref, wi_ref, wo_ref, gf_ref, o_ref, *, tf, final_norm):
    x = x_ref[...]
    dff = wo_ref.shape[0]
    hn = _rms(x, g_ref[...]).astype(BF16)
    y = x
    for c in range(0, dff, tf):
        gt = _dot(hn, wi_ref[:, c:c + tf])
        up = _dot(hn, wi_ref[:, dff + c:dff + c + tf])
        act = ((gt * jax.nn.sigmoid(gt)) * up).astype(BF16)
        y = y + _dot(act, wo_ref[c:c + tf, :])
    if final_norm:
        y = _rms(y, gf_ref[...])
    o_ref[...] = y


def _ffn(x2, g, w_in, w_out, gf, tm, tf, final_norm):
    n, d = x2.shape
    fixed = lambda i: (0, 0)
    once = pl.Buffered(1)
    return pl.pallas_call(
        functools.partial(_ffn_kernel, tf=tf, final_norm=final_norm),
        grid=(n // tm,),
        in_specs=[pl.BlockSpec((tm, d), lambda i: (i, 0)),
                  pl.BlockSpec((1, d), fixed),
                  pl.BlockSpec(w_in.shape, fixed, pipeline_mode=once),
                  pl.BlockSpec(w_out.shape, fixed, pipeline_mode=once),
                  pl.BlockSpec((1, d), fixed)],
        out_specs=pl.BlockSpec((tm, d), lambda i: (i, 0)),
        out_shape=jax.ShapeDtypeStruct((n, d), F32),
        compiler_params=_params(("parallel",)),
        name="ffn",
    )(x2, g, w_in, w_out, gf)


def _alibi_slopes():
    k = jnp.arange(1, N_ALIBI + 1, dtype=F32)
    s = jnp.exp2(-8.0 * k / N_ALIBI)
    nsa = s[2 * DIL_HPG:2 * DIL_HPG + NSA_HEADS]
    dil = jnp.concatenate([s[:2 * DIL_HPG], s[2 * DIL_HPG + NSA_HEADS:]])
    return nsa, dil


def _overlap(seq, ncp):
    n_c = (seq - CMP_BLOCK) // CMP_STRIDE + 1
    n_s = seq // SEL_BLOCK
    c_start = np.arange(n_c) * CMP_STRIDE
    s_start = np.arange(n_s) * SEL_BLOCK
    ov = np.clip(np.minimum(c_start[:, None] + CMP_BLOCK, s_start[None, :] + SEL_BLOCK)
                 - np.maximum(c_start[:, None], s_start[None, :]), 0, None).astype(np.float32) / CMP_BLOCK
    out = np.zeros((ncp, LANES), np.float32)
    out[:n_c, :n_s] = ov
    return jnp.asarray(out, BF16)


def _proj_weights(w_in):
    dep, d, _ = w_in.shape
    scale = HEAD_DIM ** -0.5 * LOG2E
    c0 = NSA_HEADS * HEAD_DIM
    c1 = c0 + 3 * 2 * NSA_KV_GROUPS * HEAD_DIM
    c2 = c1 + NSA_HEADS * 3
    c3 = c2 + 3 * DIL_HEADS * HEAD_DIM
    qa = w_in[..., :c0] * scale
    kva = w_in[..., c0:c1].reshape(dep, d, 3, 2, NSA_KV_GROUPS, HEAD_DIM)
    kva = kva.transpose(0, 1, 2, 4, 3, 5).reshape(dep, d, KVA_W)
    ga = jnp.pad(w_in[..., c1:c2], ((0, 0), (0, 0), (0, GA_W - (c2 - c1))))
    qkvb = w_in[..., c2:c3].reshape(dep, d, 3, len(DIL_GROUPS), DIL_HPG * HEAD_DIM)
    qb = (qkvb[:, :, 0] * scale).reshape(dep, d, QB_W)
    kvb = qkvb[:, :, 1:3].transpose(0, 1, 3, 2, 4).reshape(dep, d, KVB_W)
    w_a = jnp.concatenate([qa, kva, qb, kvb, ga], axis=-1).astype(BF16)
    return w_a, w_in[..., c3:].astype(BF16)


def _compress_weights(pe_k, pe_v, w_ck1, w_ck2, w_cv1, w_cv2):
    dep = pe_k.shape[0]
    half = CMP_BLOCK // 2
    ck = w_ck1.reshape(dep, CMP_BLOCK, HEAD_DIM, CMP_HIDDEN)
    cv = w_cv1.reshape(dep, CMP_BLOCK, HEAD_DIM, CMP_HIDDEN)
    z = jnp.zeros_like(ck)
    w = jnp.concatenate([jnp.concatenate([ck, z], axis=-1),
                         jnp.concatenate([z, cv], axis=-1)], axis=2)
    feat = half * 2 * HEAD_DIM
    wa = w[:, :half].reshape(dep, feat, 2 * CMP_HIDDEN).astype(BF16)
    wb = w[:, half:].reshape(dep, feat, 2 * CMP_HIDDEN).astype(BF16)
    pe = jnp.concatenate([pe_k, pe_v], axis=-1)
    pe2 = jnp.stack([pe[:, :half].reshape(dep, feat), pe[:, half:].reshape(dep, feat)], axis=1)
    pe2 = jnp.pad(pe2, ((0, 0), (0, 6), (0, 0)))
    z2 = jnp.zeros_like(w_ck2)
    w2 = jnp.concatenate([jnp.concatenate([w_ck2, z2], axis=-1),
                          jnp.concatenate([z2, w_cv2], axis=-1)], axis=1).astype(BF16)
    return pe2, wa, wb, w2


def kernel(x, norm_mix, w_in, pe_k, pe_v, w_ck1, w_ck2, w_cv1, w_cv2, w_up_nsa, w_up_dil,
           w_out, norm_ffn, w_ffn_in, w_ffn_out, norm_final):
    b, s, d = x.shape
    depth = w_in.shape[0]
    n = b * s
    nq = s // Q_BLOCK
    nch = s // CMP_STRIDE
    n_s = s // SEL_BLOCK
    assert s % DIL_SUPER == 0 and n_s <= LANES
    n_top = min(SEL_TOPK, n_s)
    tm = 512
    tm_ffn = 512
    tf = w_ffn_out.shape[1] // 2

    slopes_nsa, slopes_dil = _alibi_slopes()
    slopes_nsa = slopes_nsa * LOG2E
    ov = _overlap(s, nch)
    pick, gexp = _head_pick(), _gate_spread()
    dbias = _dil_bias(slopes_dil)
    w_a, w_gm = _proj_weights(w_in)
    pe2, wa, wb, w2 = _compress_weights(pe_k, pe_v, w_ck1, w_ck2, w_cv1, w_cv2)
    ua = w_up_nsa.astype(BF16)
    ub = w_up_dil.astype(BF16)
    wo = w_out.astype(BF16)
    wfi = w_ffn_in.astype(BF16)
    wfo = w_ffn_out.astype(BF16)
    gf = norm_final.reshape(1, d)

    x2 = x.reshape(n, d)
    for l in range(depth):
        g_mix = norm_mix[l].reshape(1, d)
        qa, kva, qb, kvb, ga = _proj(x2, g_mix, w_a[l], tm)
        qa = qa.reshape(b, s, QA_W)
        kva = kva.reshape(b, s, KVA_W)
        ga = ga.reshape(b, s, GA_W)
        xc = kva[:, :, :NSA_KV_GROUPS * LANES].reshape(b, nch, CMP_STRIDE, NSA_KV_GROUPS, LANES)
        xc = xc.transpose(0, 3, 1, 2, 4).reshape(b, NSA_KV_GROUPS, nch, CMP_STRIDE * LANES)
        kvc = _compress(xc, pe2[l], wa[l], wb[l], w2[l])
        yc, sel, flags = _cmp(slopes_nsa, qa, kvc, ga, ov, pick, gexp, n_top)
        pflags = flags.reshape(-1)
        ya = _selwin(pflags, slopes_nsa, qa, kva, sel, ga, yc, pick, gexp)
        yb = _dil(qb.reshape(-1, b, s, LANES), kvb.reshape(-1, b, s, LANES), dbias, DIL_SUPER)
        x2 = _merge(x2, g_mix, ya.reshape(n, QA_W), yb.reshape(n, DIL_HPG * HEAD_DIM),
                    w_gm[l], ua[l], ub[l], wo[l], tm)
        x2 = _ffn(x2, norm_ffn[l].reshape(1, d), wfi[l], wfo[l], gf, tm_ffn, tf,
                  final_norm=(l == depth - 1))
    return x2.reshape(b, s, d)
```

```python
import functools

import numpy as np
import jax
import jax.numpy as jnp
from jax import lax
from jax.experimental import pallas as pl
from jax.experimental.pallas import tpu as pltpu

F32 = jnp.float32
BF16 = jnp.bfloat16

HEAD_DIM = 64
NSA_HEADS = 8
NSA_KV_GROUPS = 2
NSA_HPG = NSA_HEADS // NSA_KV_GROUPS
CMP_BLOCK = 32
CMP_STRIDE = 16
CMP_HIDDEN = 256
SEL_BLOCK = 64
SEL_TOPK = 16
NSA_WINDOW = 512
Q_BLOCK = 128
FORCE_SCORE = 1e6
DIL_GROUPS = ((128, 1), (512, 4), (2048, 16))
DIL_HPG = 4
DIL_HEADS = DIL_HPG * len(DIL_GROUPS)
DIL_BLOCK = 128
N_ALIBI = NSA_HEADS + DIL_HEADS
EPS = 1e-6
NEG_INF = -1e30
TINY = 1e-30
LOG2E = 1.4426950408889634
LN2 = 0.6931471805599453

LANES = 128
GROUP_W = NSA_HPG * HEAD_DIM
DIL_SUPER = 2048
DIL_UNROLL = 4
CMP_TILES = 4
VMEM_LIMIT = 48 * 1024 * 1024


def _dot(a, b):
    return jnp.dot(a, b, preferred_element_type=F32)


def _nt_dot(a, b):
    return lax.dot_general(a, b, (((1,), (1,)), ((), ())), preferred_element_type=F32)


def _iota(shape, dim):
    return lax.broadcasted_iota(jnp.int32, shape, dim)


def _rms(x, g):
    ms = jnp.mean(x * x, axis=-1, keepdims=True)
    return (x * lax.rsqrt(ms + EPS)) * g


def _params(sem):
    return pltpu.CompilerParams(dimension_semantics=sem, vmem_limit_bytes=VMEM_LIMIT)


QA_W = NSA_HEADS * LANES
QP_W = NSA_HPG * LANES
KVA_W, QB_W, KVB_W, GA_W = 768, 768, 1536, 128
PROJ_COLS = QA_W + KVA_W + QB_W + KVB_W + GA_W


def _proj_kernel(x_ref, g_ref, w_ref, qa_ref, kva_ref, qb_ref, kvb_ref, ga_ref):
    y = _rms(x_ref[...], g_ref[...]).astype(BF16)
    c = 0
    qa_ref[...] = _dot(y, w_ref[:, c:c + QA_W]).astype(BF16)
    c += QA_W
    kva_ref[...] = _dot(y, w_ref[:, c:c + KVA_W]).astype(BF16)
    c += KVA_W
    for j in range(0, QB_W // LANES, 2):
        r = _dot(y, w_ref[:, c:c + 2 * LANES])
        qb_ref[j] = r[:, :LANES]
        qb_ref[j + 1] = r[:, LANES:]
        c += 2 * LANES
    for j in range(0, KVB_W // LANES, 2):
        r = _dot(y, w_ref[:, c:c + 2 * LANES])
        kvb_ref[j] = r[:, :LANES]
        kvb_ref[j + 1] = r[:, LANES:]
        c += 2 * LANES
    ga_ref[...] = jax.nn.sigmoid(_dot(y, w_ref[:, c:c + GA_W]))


def _proj(x2, g, w, tm):
    n, d = x2.shape
    row = lambda i: (i, 0)
    fixed = lambda i: (0, 0)
    return pl.pallas_call(
        _proj_kernel,
        grid=(n // tm,),
        in_specs=[pl.BlockSpec((tm, d), row), pl.BlockSpec((1, d), fixed),
                  pl.BlockSpec((d, PROJ_COLS), fixed, pipeline_mode=pl.Buffered(1))],
        out_specs=[pl.BlockSpec((tm, QA_W), row), pl.BlockSpec((tm, KVA_W), row),
                   pl.BlockSpec((QB_W // LANES, tm, LANES), lambda i: (0, i, 0)),
                   pl.BlockSpec((KVB_W // LANES, tm, LANES), lambda i: (0, i, 0)),
                   pl.BlockSpec((tm, GA_W), row)],
        out_shape=[jax.ShapeDtypeStruct((n, QA_W), BF16), jax.ShapeDtypeStruct((n, KVA_W), BF16),
                   jax.ShapeDtypeStruct((QB_W // LANES, n, LANES), F32),
                   jax.ShapeDtypeStruct((KVB_W // LANES, n, LANES), F32),
                   jax.ShapeDtypeStruct((n, GA_W), F32)],
        compiler_params=_params(("parallel",)),
        name="proj",
    )(x2, g, w)


def _compress_kernel(x_ref, pe_ref, wa_ref, wb_ref, w2_ref, o_ref):
    xf = x_ref[0, 0].astype(F32)
    nch = xf.shape[0]
    xa = (xf + pe_ref[0:1, :]).astype(BF16)
    xb = (xf + pe_ref[1:2, :]).astype(BF16)
    ya = _dot(xa, wa_ref[...])
    yb = _dot(xb, wb_ref[...])
    h = ya + pltpu.roll(yb, nch - 1, 0)
    a = (h * jax.nn.sigmoid(h)).astype(BF16)
    o_ref[0, 0] = _dot(a, w2_ref[...]).astype(BF16)


def _compress(xc, pe, wa, wb, w2):
    b, g, nch, f = xc.shape
    hid = wa.shape[1]
    return pl.pallas_call(
        _compress_kernel,
        grid=(b, g),
        in_specs=[pl.BlockSpec((1, 1, nch, f), lambda i, j: (i, j, 0, 0)),
                  pl.BlockSpec((8, f), lambda i, j: (0, 0)),
                  pl.BlockSpec((f, hid), lambda i, j: (0, 0)),
                  pl.BlockSpec((f, hid), lambda i, j: (0, 0)),
                  pl.BlockSpec((hid, LANES), lambda i, j: (0, 0))],
        out_specs=pl.BlockSpec((1, 1, nch, LANES), lambda i, j: (i, j, 0, 0)),
        out_shape=jax.ShapeDtypeStruct((b, g, nch, LANES), BF16),
        compiler_params=_params(("parallel", "parallel")),
        name="compress",
    )(xc, pe, wa, wb, w2)


def _gate_spread():
    g = np.arange(NSA_KV_GROUPS)[:, None, None, None]
    br = np.arange(3)[None, :, None, None]
    r = np.arange(LANES)[None, None, :, None]
    c = np.arange(GROUP_W)[None, None, None, :]
    return jnp.asarray(r == 3 * (NSA_HPG * g + c // HEAD_DIM) + br, BF16)


def _stack_heads(q):
    return jnp.concatenate([q[:, LANES * h:LANES * (h + 1)] for h in range(NSA_HPG)], axis=0)


def _unstack_heads(o4):
    lane = _iota((Q_BLOCK, LANES), 1)
    o = [o4[Q_BLOCK * h:Q_BLOCK * (h + 1)] for h in range(NSA_HPG)]
    c0 = jnp.where(lane < HEAD_DIM, pltpu.roll(o[0], HEAD_DIM, 1), o[1])
    c1 = jnp.where(lane < HEAD_DIM, pltpu.roll(o[2], HEAD_DIM, 1), o[3])
    return jnp.concatenate([c0, c1], axis=1)


def _row_sums_all_lanes(a):
    lane = _iota(a.shape, 1)
    return jnp.where(lane < HEAD_DIM, a, pltpu.roll(a, HEAD_DIM, 1))


def _normalise_unstack(a):
    return _unstack_heads(a * (1.0 / jnp.maximum(_row_sums_all_lanes(a), TINY)))


def _expand_gate(gates, spread):
    hi = gates.astype(BF16)
    lo = (gates - hi.astype(F32)).astype(BF16)
    return _dot(hi, spread) + _dot(lo, spread)


def _ones_in_k_lanes(kv):
    lane = _iota(kv.shape, 1)
    return jnp.where(lane < HEAD_DIM, jnp.ones_like(kv), kv)


def _cmp_kernel(slopes_ref, q_ref, kvc_ref, ga_ref, ov_ref, gexp_ref,
                yc_ref, sel_ref, fl_ref, bias0, cq, rhs, *, n_top):
    g = pl.program_id(1)
    i = pl.program_id(2)
    hq = NSA_HPG
    ncp = cq.shape[1]

    @pl.when(i == 0)
    def _():
        kpos = CMP_STRIDE * _iota((Q_BLOCK, ncp), 1) + (CMP_BLOCK - 1)
        cq[...] = kpos - _iota((Q_BLOCK, ncp), 0)
        kf = kpos[0:8].astype(F32)
        for h in range(hq):
            bias0[h] = slopes_ref[hq * g + h] * kf
        rhs[:, 0:LANES] = _ones_in_k_lanes(kvc_ref[0, 0])
        rhs[:, LANES:2 * LANES] = ov_ref[...]

    for k in range(CMP_TILES):
        it = CMP_TILES * i + k
        rows = slice(k * Q_BLOCK, (k + 1) * Q_BLOCK)
        q4 = _stack_heads(q_ref[0, rows, :])
        s = _nt_dot(q4, kvc_ref[0, 0]).reshape(hq, Q_BLOCK, ncp)
        t0 = it * Q_BLOCK
        neg = jnp.where(cq[...] <= t0, 0.0, NEG_INF)
        tf = t0.astype(F32)
        ps = []
        for h in range(hq):
            sb = s[h] + (neg + (bias0[h, 0:1] - slopes_ref[hq * g + h] * tf))
            m = jnp.max(sb, axis=-1, keepdims=True)
            m = jnp.where(m < 0.1 * NEG_INF, -NEG_INF, m)
            ps.append(jnp.exp2(sb - m).astype(BF16))
        big = _dot(jnp.concatenate(ps, axis=0), rhs[...])
        inv = 1.0 / jnp.maximum(_row_sums_all_lanes(big[:, 0:LANES]), TINY)
        oc = _unstack_heads(big[:, 0:LANES] * inv)
        yc_ref[0, rows, :] = _expand_gate(ga_ref[0, rows, :], gexp_ref[0, 0]) * oc
        pov = big[:, LANES:2 * LANES] * inv
        imp = pov[0:Q_BLOCK]
        for h in range(1, hq):
            imp = imp + pov[Q_BLOCK * h:Q_BLOCK * (h + 1)]

        qrow = _iota((Q_BLOCK, LANES), 0)
        j = _iota((Q_BLOCK, LANES), 1)
        jt = 2 * it + (qrow >> 6)
        valid = j <= jt
        forced = (j == 0) | (j == jt) | (j == jt - 1)
        score = jnp.where(forced, FORCE_SCORE, jnp.where(valid, imp, -1.0))
        st = score.T
        jr = _iota((LANES, Q_BLOCK), 0)
        cur = st
        taken = -3e38
        for _ in range(n_top):
            mx = jnp.max(cur, axis=0, keepdims=True)
            first = jnp.min(jnp.where(cur == mx, jr, LANES), axis=0, keepdims=True)
            cur = jnp.where(jr == first, taken, cur)
        keep = jnp.where((cur == taken) & (st >= 0.0), 1.0, 0.0).T
        sel_ref[0, 0, rows, :] = keep.astype(BF16)

        fl_ref[k] = (jnp.max(keep, axis=0, keepdims=True) > 0.0).astype(jnp.int32)


def _cmp(slopes, qa, kvc, ga, ov, gexp, n_top):
    b, s, _ = qa.shape
    g = NSA_KV_GROUPS
    nq = s // Q_BLOCK
    ncp = kvc.shape[2]
    tq = CMP_TILES * Q_BLOCK
    nt = s // tq
    return pl.pallas_call(
        functools.partial(_cmp_kernel, n_top=n_top),
        grid=(b, g, nt),
        in_specs=[pl.BlockSpec(memory_space=pltpu.SMEM),
                  pl.BlockSpec((1, tq, QP_W), lambda bi, gi, i: (bi, i, gi)),
                  pl.BlockSpec((1, 1, ncp, LANES), lambda bi, gi, i: (bi, gi, 0, 0)),
                  pl.BlockSpec((1, tq, LANES), lambda bi, gi, i: (bi, i, 0)),
                  pl.BlockSpec((ncp, LANES), lambda bi, gi, i: (0, 0)),
                  pl.BlockSpec((1, 3, LANES, GROUP_W), lambda bi, gi, i: (gi, 0, 0, 0))],
        out_specs=[pl.BlockSpec((1, tq, GROUP_W), lambda bi, gi, i: (bi, i, gi)),
                   pl.BlockSpec((1, 1, tq, LANES), lambda bi, gi, i: (bi, gi, i, 0)),
                   pl.BlockSpec((CMP_TILES, 1, LANES), lambda bi, gi, i: ((bi * g + gi) * nt + i, 0, 0))],
        out_shape=[jax.ShapeDtypeStruct((b, s, g * GROUP_W), F32),
                   jax.ShapeDtypeStruct((b, g, s, LANES), BF16),
                   jax.ShapeDtypeStruct((b * g * nq, 1, LANES), jnp.int32)],
        scratch_shapes=[pltpu.VMEM((NSA_HPG, 8, ncp), F32),
                        pltpu.VMEM((Q_BLOCK, ncp), jnp.int32),
                        pltpu.VMEM((ncp, 2 * LANES), BF16)],
        compiler_params=_params(("parallel", "parallel", "arbitrary")),
        name="cmp",
    )(slopes, qa, kvc, ga, ov, gexp)


WIN_UNITS = NSA_WINDOW // Q_BLOCK + 1
SLOT_CHUNK = 8


def _selwin_kernel(pf_ref, slopes_ref, q_ref, kvs_ref, kvw_ref, sel_ref, ga_ref, yc_ref,
                   gexp_ref, ya_ref, sbuf, slot_unit, mrun, acc, accw, *, nq):
    b = pl.program_id(0)
    g = pl.program_id(1)
    i = pl.program_id(2)
    hq = NSA_HPG
    qmk = _iota((Q_BLOCK, LANES), 0) - _iota((Q_BLOCK, LANES), 1)
    koff = _iota((1, LANES), 1).astype(F32)
    q4 = _stack_heads(q_ref[0])
    sel = sel_ref[0, 0]
    dmat = _iota((LANES, LANES), 0) - (_iota((LANES, LANES), 1) >> 6)
    dead = -(1 << 20)

    for d in range(WIN_UNITS):
        u = i - (WIN_UNITS - 1) + d
        slot_unit[d] = jnp.where(u >= 0, u, -1)

    base = ((b * NSA_KV_GROUPS + g) * nq + i) * LANES

    def list_body(jj, k):
        used = (pf_ref[base + 2 * jj] | pf_ref[base + 2 * jj + 1]) != 0

        @pl.when(used)
        def _():
            slot_unit[WIN_UNITS + k] = jj
        return k + used.astype(jnp.int32)

    nsel = lax.fori_loop(0, i + 1, list_body, jnp.int32(0))
    for t in range(SLOT_CHUNK - 1):
        slot_unit[WIN_UNITS + nsel + t] = -1
    nchunk = (nsel + (SLOT_CHUNK - 1)) // SLOT_CHUNK

    def kv_tile(kv_ref, slot):
        u = jnp.maximum(slot_unit[slot], 0)
        return kv_ref[0, pl.ds(pl.multiple_of(u * Q_BLOCK, Q_BLOCK), Q_BLOCK), :]

    def score_slot(kv_ref, slot, window):
        u = slot_unit[slot]
        s = _nt_dot(q4, kv_tile(kv_ref, slot)).reshape(hq, Q_BLOCK, LANES)
        off = jnp.where(u >= 0, (i - u) * Q_BLOCK, dead)
        dist = qmk + off
        if window:
            allow = (dist >= 0) & (dist < NSA_WINDOW)
        else:
            spread = jnp.where(dmat == 2 * u, 1.0, 0.0).astype(BF16)
            allow = (dist >= 0) & (_dot(sel, spread) > 0.5)
        neg = jnp.where(allow, 0.0, NEG_INF)
        kd = koff - off.astype(F32)
        for h in range(hq):
            sb = s[h] + (neg + slopes_ref[hq * g + h] * kd)
            sbuf[slot, h] = sb
            mrun[h] = jnp.maximum(mrun[h], sb)

    def spread_max():
        m = jnp.max(mrun[...], axis=-1, keepdims=True)
        mrun[...] = jnp.broadcast_to(m, mrun.shape)

    def weighted_values(kv_ref, slots):
        tot = None
        for slot in slots:
            p = jnp.exp2(sbuf[slot] - mrun[...]).astype(BF16).reshape(hq * Q_BLOCK, LANES)
            d = _dot(p, _ones_in_k_lanes(kv_tile(kv_ref, slot)))
            tot = d if tot is None else tot + d
        return tot

    mrun[...] = jnp.full_like(mrun, NEG_INF)
    for d in range(WIN_UNITS):
        score_slot(kvw_ref, d, True)
    spread_max()
    accw[...] = weighted_values(kvw_ref, list(range(WIN_UNITS)))

    mrun[...] = jnp.full_like(mrun, NEG_INF)

    def score_chunk(ci, carry):
        for t in range(SLOT_CHUNK):
            score_slot(kvs_ref, WIN_UNITS + ci * SLOT_CHUNK + t, False)
        return carry

    lax.fori_loop(0, nchunk, score_chunk, 0)
    spread_max()
    acc[...] = jnp.zeros_like(acc)

    def value_chunk(ci, carry):
        acc[...] += weighted_values(kvs_ref, [WIN_UNITS + ci * SLOT_CHUNK + t for t in range(SLOT_CHUNK)])
        return carry

    lax.fori_loop(0, nchunk, value_chunk, 0)
    o_s = _normalise_unstack(acc[...])
    o_w = _normalise_unstack(accw[...])

    gates = ga_ref[0]
    ya = (yc_ref[0] + _expand_gate(gates, gexp_ref[0, 1]) * o_s
          + _expand_gate(gates, gexp_ref[0, 2]) * o_w)
    ya_ref[0] = ya.astype(BF16)


def _selwin(pflags, slopes, qa, kva, sel, ga, yc, gexp):
    b, s, _ = qa.shape
    g = NSA_KV_GROUPS
    nq = s // Q_BLOCK
    nbuf = WIN_UNITS + nq + SLOT_CHUNK - 1
    grid_spec = pltpu.PrefetchScalarGridSpec(
        num_scalar_prefetch=1,
        grid=(b, g, nq),
        in_specs=[pl.BlockSpec(memory_space=pltpu.SMEM),
                  pl.BlockSpec((1, Q_BLOCK, QP_W), lambda bi, gi, i, pf: (bi, i, gi)),
                  pl.BlockSpec((1, s, LANES), lambda bi, gi, i, pf: (bi, 0, 2 + gi)),
                  pl.BlockSpec((1, s, LANES), lambda bi, gi, i, pf: (bi, 0, 4 + gi)),
                  pl.BlockSpec((1, 1, Q_BLOCK, LANES), lambda bi, gi, i, pf: (bi, gi, i, 0)),
                  pl.BlockSpec((1, Q_BLOCK, LANES), lambda bi, gi, i, pf: (bi, i, 0)),
                  pl.BlockSpec((1, Q_BLOCK, GROUP_W), lambda bi, gi, i, pf: (bi, i, gi)),
                  pl.BlockSpec((1, 3, LANES, GROUP_W), lambda bi, gi, i, pf: (gi, 0, 0, 0))],
        out_specs=pl.BlockSpec((1, Q_BLOCK, GROUP_W), lambda bi, gi, i, pf: (bi, i, gi)),
        scratch_shapes=[pltpu.VMEM((nbuf, NSA_HPG, Q_BLOCK, LANES), F32),
                        pltpu.SMEM((nbuf,), jnp.int32),
                        pltpu.VMEM((NSA_HPG, Q_BLOCK, LANES), F32),
                        pltpu.VMEM((NSA_HPG * Q_BLOCK, LANES), F32),
                        pltpu.VMEM((NSA_HPG * Q_BLOCK, LANES), F32)],
    )
    return pl.pallas_call(
        functools.partial(_selwin_kernel, nq=nq),
        grid_spec=grid_spec,
        out_shape=jax.ShapeDtypeStruct((b, s, g * GROUP_W), BF16),
        compiler_params=_params(("parallel", "parallel", "arbitrary")),
        name="selwin",
    )(pflags, slopes, qa, kva, kva, sel, ga, yc, gexp)


def _dil_bias(slopes):
    c = DIL_BLOCK
    qi = np.arange(c)[:, None]
    ki = np.arange(2 * c)[None, :]
    delta = qi + c - ki
    out = []
    for k, (window, dil) in enumerate(DIL_GROUPS):
        in_band = jnp.asarray((delta >= 0) & (delta <= window // dil))
        dist = jnp.asarray((delta * dil).astype(np.float32))
        for h in range(DIL_HPG):
            out.append(jnp.where(in_band, -(slopes[DIL_HPG * k + h] * dist) * LOG2E, NEG_INF))
    return jnp.stack(out).reshape(len(out) // 2, 2 * c, 2 * c)


def _dil_group(k, dil, sup, q_ref, kvc_ref, kvp_ref, bias_ref, mx_ref, w_ref, y_ref):
    c = DIL_BLOCK
    nblk = sup // (c * dil)
    has_halo = pl.program_id(1) > 0
    nch = DIL_HPG * HEAD_DIM // LANES
    upper = _iota((c, LANES), 1) >= HEAD_DIM

    def block(idx):
        r = idx // nblk
        n = idx - r * nblk
        start = r + dil * c * n
        rows = pl.ds(start, c, stride=dil)
        rows_prev = pl.ds(jnp.maximum(start - dil * c, r), c, stride=dil)
        rows_halo = pl.ds(r + dil * c * (nblk - 1), c, stride=dil)
        pen = jnp.where(jnp.logical_or(n >= 1, has_halo), 0.0, NEG_INF)

        def prev(ch):
            return jnp.where(n >= 1, kvc_ref[ch, 0, rows_prev, :], kvp_ref[ch, 0, rows_halo, :]).astype(BF16)

        for ch in range(nch):
            qb = q_ref[ch, 0, rows, :].astype(BF16)
            zero = jnp.zeros_like(qb)
            qs = jnp.concatenate([jnp.where(upper, zero, qb), jnp.where(upper, qb, zero)], axis=0)
            kband = jnp.concatenate([prev(ch), kvc_ref[ch, 0, rows, :].astype(BF16)], axis=0)
            vband = jnp.concatenate([prev(nch + ch), kvc_ref[nch + ch, 0, rows, :].astype(BF16)], axis=0)
            s = _nt_dot(qs, kband) + bias_ref[ch]
            s = jnp.concatenate([s[:, :c] + pen, s[:, c:]], axis=1)
            m = jnp.max(s, axis=-1, keepdims=True)
            e = jnp.exp2(s - m).astype(BF16)
            ones = jnp.ones((2 * c, LANES), BF16)
            ol = _dot(e, jnp.concatenate([vband, ones], axis=1))
            lt = jnp.maximum(ol[:, LANES:], TINY)
            o2 = ol[:, :LANES] * (1.0 / lt)
            l2 = m * LN2 + jnp.log(lt)
            o = jnp.where(upper, o2[c:], o2[:c])
            lse = jnp.where(upper, l2[c:], l2[:c])
            if k == 0:
                mx_ref[ch, rows, :] = lse
                w_ref[ch, rows, :] = jnp.ones_like(lse)
                y_ref[ch, rows, :] = o
            else:
                mx_old = mx_ref[ch, rows, :]
                mx_new = jnp.maximum(mx_old, lse)
                a = jnp.exp(mx_old - mx_new)
                bnew = jnp.exp(lse - mx_new)
                mx_ref[ch, rows, :] = mx_new
                w_ref[ch, rows, :] = w_ref[ch, rows, :] * a + bnew
                y_ref[ch, rows, :] = y_ref[ch, rows, :] * a + bnew * o

    def body(t, carry):
        for j in range(DIL_UNROLL):
            block(t * DIL_UNROLL + j)
        return carry

    lax.fori_loop(0, sup // (c * DIL_UNROLL), body, 0)


def _dil_kernel(q_ref, kvc_ref, kvp_ref, bias_ref, yb_ref, mx_ref, w_ref, y_ref, *, sup):
    gi = pl.program_id(2)
    for k, (_, dil) in enumerate(DIL_GROUPS):
        @pl.when(gi == k)
        def _(k=k, dil=dil):
            _dil_group(k, dil, sup, q_ref, kvc_ref, kvp_ref, bias_ref, mx_ref, w_ref, y_ref)

    @pl.when(gi == len(DIL_GROUPS) - 1)
    def _():
        for ch in range(DIL_HPG * HEAD_DIM // LANES):
            yb_ref[0, :, ch * LANES:(ch + 1) * LANES] = (y_ref[ch] * (1.0 / w_ref[ch])).astype(BF16)


def _dil(qb, kvb, bias, sup):
    _, b, s, _ = qb.shape
    hw = DIL_HPG * HEAD_DIM
    nch = hw // LANES
    ng = len(DIL_GROUPS)
    return pl.pallas_call(
        functools.partial(_dil_kernel, sup=sup),
        grid=(b, s // sup, ng),
        in_specs=[pl.BlockSpec((nch, 1, sup, LANES), lambda bi, si, gi: (gi, bi, si, 0)),
                  pl.BlockSpec((2 * nch, 1, sup, LANES), lambda bi, si, gi: (gi, bi, si, 0)),
                  pl.BlockSpec((2 * nch, 1, sup, LANES),
                               lambda bi, si, gi: (gi, bi, jnp.maximum(si - 1, 0), 0)),
                  pl.BlockSpec((nch, 2 * DIL_BLOCK, 2 * DIL_BLOCK), lambda bi, si, gi: (gi, 0, 0))],
        out_specs=pl.BlockSpec((1, sup, hw), lambda bi, si, gi: (bi, si, 0)),
        out_shape=jax.ShapeDtypeStruct((b, s, hw), BF16),
        scratch_shapes=[pltpu.VMEM((nch, sup, LANES), F32), pltpu.VMEM((nch, sup, LANES), F32),
                        pltpu.VMEM((nch, sup, LANES), F32)],
        compiler_params=_params(("parallel", "parallel", "arbitrary")),
        name="dil",
    )(qb, kvb, kvb, bias)


def _merge_kernel(x_ref, g_ref, ya_ref, yb_ref, wg_ref, ua_ref, ub_ref, wo_ref, o_ref):
    x = x_ref[...]
    d = x.shape[1]
    hn = _rms(x, g_ref[...]).astype(BF16)
    gate_a = jax.nn.sigmoid(_dot(hn, wg_ref[:, 0:d]))
    gate_b = jax.nn.sigmoid(_dot(hn, wg_ref[:, d:2 * d]))
    merged = gate_a * _dot(ya_ref[...], ua_ref[...]) + gate_b * _dot(yb_ref[...], ub_ref[...])
    o_ref[...] = x + _dot(merged.astype(BF16), wo_ref[...])


def _merge(x2, g, ya, yb, wg, ua, ub, wo, tm):
    n, d = x2.shape
    row = lambda i: (i, 0)
    fixed = lambda i: (0, 0)
    return pl.pallas_call(
        _merge_kernel,
        grid=(n // tm,),
        in_specs=[pl.BlockSpec((tm, d), row), pl.BlockSpec((1, d), fixed),
                  pl.BlockSpec((tm, ya.shape[1]), row), pl.BlockSpec((tm, yb.shape[1]), row),
                  pl.BlockSpec(wg.shape, fixed), pl.BlockSpec(ua.shape, fixed),
                  pl.BlockSpec(ub.shape, fixed), pl.BlockSpec(wo.shape, fixed)],
        out_specs=pl.BlockSpec((tm, d), row),
        out_shape=jax.ShapeDtypeStruct((n, d), F32),
        compiler_params=_params(("parallel",)),
        name="merge",
    )(x2, g, ya, yb, wg, ua, ub, wo)


def _ffn_kernel(x_ref, g_ref, wi_ref, wo_ref, gf_ref, o_ref, *, tf, final_norm):
    x = x_ref[...]
    dff = wo_ref.shape[0]
    hn = _rms(x, g_ref[...]).astype(BF16)
    y = x
    for c in range(0, dff, tf):
        gt = _dot(hn, wi_ref[:, c:c + tf])
        up = _dot(hn, wi_ref[:, dff + c:dff + c + tf])
        act = ((gt * jax.nn.sigmoid(gt)) * up).astype(BF16)
        y = y + _dot(act, wo_ref[c:c + tf, :])
    if final_norm:
        y = _rms(y, gf_ref[...])
    o_ref[...] = y


def _ffn(x2, g, w_in, w_out, gf, tm, tf, final_norm):
    n, d = x2.shape
    fixed = lambda i: (0, 0)
    once = pl.Buffered(1)
    return pl.pallas_call(
        functools.partial(_ffn_kernel, tf=tf, final_norm=final_norm),
        grid=(n // tm,),
        in_specs=[pl.BlockSpec((tm, d), lambda i: (i, 0)),
                  pl.BlockSpec((1, d), fixed),
                  pl.BlockSpec(w_in.shape, fixed, pipeline_mode=once),
                  pl.BlockSpec(w_out.shape, fixed, pipeline_mode=once),
                  pl.BlockSpec((1, d), fixed)],
        out_specs=pl.BlockSpec((tm, d), lambda i: (i, 0)),
        out_shape=jax.ShapeDtypeStruct((n, d), F32),
        compiler_params=_params(("parallel",)),
        name="ffn",
    )(x2, g, w_in, w_out, gf)


def _alibi_slopes():
    k = jnp.arange(1, N_ALIBI + 1, dtype=F32)
    s = jnp.exp2(-8.0 * k / N_ALIBI)
    nsa = s[2 * DIL_HPG:2 * DIL_HPG + NSA_HEADS]
    dil = jnp.concatenate([s[:2 * DIL_HPG], s[2 * DIL_HPG + NSA_HEADS:]])
    return nsa, dil


def _overlap(seq, ncp):
    n_c = (seq - CMP_BLOCK) // CMP_STRIDE + 1
    n_s = seq // SEL_BLOCK
    c_start = np.arange(n_c) * CMP_STRIDE
    s_start = np.arange(n_s) * SEL_BLOCK
    ov = np.clip(np.minimum(c_start[:, None] + CMP_BLOCK, s_start[None, :] + SEL_BLOCK)
                 - np.maximum(c_start[:, None], s_start[None, :]), 0, None).astype(np.float32) / CMP_BLOCK
    out = np.zeros((ncp, LANES), np.float32)
    out[:n_c, :n_s] = ov
    return jnp.asarray(out, BF16)


def _proj_weights(w_in):
    dep, d, _ = w_in.shape
    scale = HEAD_DIM ** -0.5 * LOG2E
    c0 = NSA_HEADS * HEAD_DIM
    c1 = c0 + 3 * 2 * NSA_KV_GROUPS * HEAD_DIM
    c2 = c1 + NSA_HEADS * 3
    c3 = c2 + 3 * DIL_HEADS * HEAD_DIM
    qa = (w_in[..., :c0] * scale).reshape(dep, d, NSA_HEADS, HEAD_DIM)
    qa = jnp.pad(qa, ((0, 0), (0, 0), (0, 0), (0, LANES - HEAD_DIM))).reshape(dep, d, QA_W)
    kva = w_in[..., c0:c1].reshape(dep, d, 3, 2, NSA_KV_GROUPS, HEAD_DIM)
    kva = kva.transpose(0, 1, 2, 4, 3, 5).reshape(dep, d, KVA_W)
    ga = jnp.pad(w_in[..., c1:c2], ((0, 0), (0, 0), (0, GA_W - (c2 - c1))))
    qkvb = w_in[..., c2:c3].reshape(dep, d, 3, len(DIL_GROUPS), DIL_HPG * HEAD_DIM)
    qb = (qkvb[:, :, 0] * scale).reshape(dep, d, QB_W)
    kvb = qkvb[:, :, 1:3].transpose(0, 1, 3, 2, 4).reshape(dep, d, KVB_W)
    w_a = jnp.concatenate([qa, kva, qb, kvb, ga], axis=-1).astype(BF16)
    return w_a, w_in[..., c3:].astype(BF16)


def _compress_weights(pe_k, pe_v, w_ck1, w_ck2, w_cv1, w_cv2):
    dep = pe_k.shape[0]
    half = CMP_BLOCK // 2
    ck = w_ck1.reshape(dep, CMP_BLOCK, HEAD_DIM, CMP_HIDDEN)
    cv = w_cv1.reshape(dep, CMP_BLOCK, HEAD_DIM, CMP_HIDDEN)
    z = jnp.zeros_like(ck)
    w = jnp.concatenate([jnp.concatenate([ck, z], axis=-1),
                         jnp.concatenate([z, cv], axis=-1)], axis=2)
    feat = half * 2 * HEAD_DIM
    wa = w[:, :half].reshape(dep, feat, 2 * CMP_HIDDEN).astype(BF16)
    wb = w[:, half:].reshape(dep, feat, 2 * CMP_HIDDEN).astype(BF16)
    pe = jnp.concatenate([pe_k, pe_v], axis=-1)
    pe2 = jnp.stack([pe[:, :half].reshape(dep, feat), pe[:, half:].reshape(dep, feat)], axis=1)
    pe2 = jnp.pad(pe2, ((0, 0), (0, 6), (0, 0)))
    z2 = jnp.zeros_like(w_ck2)
    w2 = jnp.concatenate([jnp.concatenate([w_ck2, z2], axis=-1),
                          jnp.concatenate([z2, w_cv2], axis=-1)], axis=1).astype(BF16)
    return pe2, wa, wb, w2


def kernel(x, norm_mix, w_in, pe_k, pe_v, w_ck1, w_ck2, w_cv1, w_cv2, w_up_nsa, w_up_dil,
           w_out, norm_ffn, w_ffn_in, w_ffn_out, norm_final):
    b, s, d = x.shape
    depth = w_in.shape[0]
    n = b * s
    nq = s // Q_BLOCK
    nch = s // CMP_STRIDE
    n_s = s // SEL_BLOCK
    assert s % DIL_SUPER == 0 and n_s <= LANES
    n_top = min(SEL_TOPK, n_s)
    tm = 512
    tm_ffn = 512
    tf = w_ffn_out.shape[1] // 2

    slopes_nsa, slopes_dil = _alibi_slopes()
    slopes_nsa = slopes_nsa * LOG2E
    ov = _overlap(s, nch)
    gexp = _gate_spread()
    dbias = _dil_bias(slopes_dil)
    w_a, w_gm = _proj_weights(w_in)
    pe2, wa, wb, w2 = _compress_weights(pe_k, pe_v, w_ck1, w_ck2, w_cv1, w_cv2)
    ua = w_up_nsa.astype(BF16)
    ub = w_up_dil.astype(BF16)
    wo = w_out.astype(BF16)
    wfi = w_ffn_in.astype(BF16)
    wfo = w_ffn_out.astype(BF16)
    gf = norm_final.reshape(1, d)

    x2 = x.reshape(n, d)
    for l in range(depth):
        g_mix = norm_mix[l].reshape(1, d)
        qa, kva, qb, kvb, ga = _proj(x2, g_mix, w_a[l], tm)
        qa = qa.reshape(b, s, QA_W)
        kva = kva.reshape(b, s, KVA_W)
        ga = ga.reshape(b, s, GA_W)
        xc = kva[:, :, :NSA_KV_GROUPS * LANES].reshape(b, nch, CMP_STRIDE, NSA_KV_GROUPS, LANES)
        xc = xc.transpose(0, 3, 1, 2, 4).reshape(b, NSA_KV_GROUPS, nch, CMP_STRIDE * LANES)
        kvc = _compress(xc, pe2[l], wa[l], wb[l], w2[l])
        yc, sel, flags = _cmp(slopes_nsa, qa, kvc, ga, ov, gexp, n_top)
        pflags = flags.reshape(-1)
        ya = _selwin(pflags, slopes_nsa, qa, kva, sel, ga, yc, gexp)
        yb = _dil(qb.reshape(-1, b, s, LANES), kvb.reshape(-1, b, s, LANES), dbias, DIL_SUPER)
        x2 = _merge(x2, g_mix, ya.reshape(n, NSA_HEADS * HEAD_DIM), yb.reshape(n, DIL_HPG * HEAD_DIM),
                    w_gm[l], ua[l], ub[l], wo[l], tm)
        x2 = _ffn(x2, norm_ffn[l].reshape(1, d), wfi[l], wfo[l], gf, tm_ffn, tf,
                  final_norm=(l == depth - 1))
    return x2.reshape(b, s, d)
```

```python
import functools

import numpy as np
import jax
import jax.numpy as jnp
from jax import lax
from jax.experimental import pallas as pl
from jax.experimental.pallas import tpu as pltpu

F32 = jnp.float32
BF16 = jnp.bfloat16

HEAD_DIM = 64
NSA_HEADS = 8
NSA_KV_GROUPS = 2
NSA_HPG = NSA_HEADS // NSA_KV_GROUPS
CMP_BLOCK = 32
CMP_STRIDE = 16
CMP_HIDDEN = 256
SEL_BLOCK = 64
SEL_TOPK = 16
NSA_WINDOW = 512
Q_BLOCK = 128
FORCE_SCORE = 1e6
DIL_GROUPS = ((128, 1), (512, 4), (2048, 16))
DIL_HPG = 4
DIL_HEADS = DIL_HPG * len(DIL_GROUPS)
DIL_BLOCK = 128
N_ALIBI = NSA_HEADS + DIL_HEADS
EPS = 1e-6
NEG_INF = -1e30
TINY = 1e-30
LOG2E = 1.4426950408889634
LN2 = 0.6931471805599453

LANES = 128
GROUP_W = NSA_HPG * HEAD_DIM
DIL_SUPER = 2048
DIL_UNROLL = 4
CMP_TILES = 4
VMEM_LIMIT = 48 * 1024 * 1024


def _dot(a, b):
    return jnp.dot(a, b, preferred_element_type=F32)


def _nt_dot(a, b):
    return lax.dot_general(a, b, (((1,), (1,)), ((), ())), preferred_element_type=F32)


def _iota(shape, dim):
    return lax.broadcasted_iota(jnp.int32, shape, dim)


def _rms(x, g):
    ms = jnp.mean(x * x, axis=-1, keepdims=True)
    return (x * lax.rsqrt(ms + EPS)) * g


def _params(sem):
    return pltpu.CompilerParams(dimension_semantics=sem, vmem_limit_bytes=VMEM_LIMIT)


QA_W = NSA_HEADS * LANES
QP_W = NSA_HPG * LANES
KVA_W, QB_W, KVB_W, GA_W = 768, 768, 1536, 128
PROJ_COLS = QA_W + KVA_W + QB_W + KVB_W + GA_W


def _proj_kernel(x_ref, g_ref, w_ref, qa_ref, kva_ref, qb_ref, kvb_ref, ga_ref):
    y = _rms(x_ref[...], g_ref[...]).astype(BF16)
    c = 0
    qa_ref[...] = _dot(y, w_ref[:, c:c + QA_W]).astype(BF16)
    c += QA_W
    kva_ref[...] = _dot(y, w_ref[:, c:c + KVA_W]).astype(BF16)
    c += KVA_W
    for j in range(0, QB_W // LANES, 2):
        r = _dot(y, w_ref[:, c:c + 2 * LANES])
        qb_ref[j] = r[:, :LANES]
        qb_ref[j + 1] = r[:, LANES:]
        c += 2 * LANES
    for j in range(0, KVB_W // LANES, 2):
        r = _dot(y, w_ref[:, c:c + 2 * LANES])
        kvb_ref[j] = r[:, :LANES]
        kvb_ref[j + 1] = r[:, LANES:]
        c += 2 * LANES
    ga_ref[...] = jax.nn.sigmoid(_dot(y, w_ref[:, c:c + GA_W]))


def _proj(x2, g, w, tm):
    n, d = x2.shape
    row = lambda i: (i, 0)
    fixed = lambda i: (0, 0)
    return pl.pallas_call(
        _proj_kernel,
        grid=(n // tm,),
        in_specs=[pl.BlockSpec((tm, d), row), pl.BlockSpec((1, d), fixed),
                  pl.BlockSpec((d, PROJ_COLS), fixed, pipeline_mode=pl.Buffered(1))],
        out_specs=[pl.BlockSpec((tm, QA_W), row), pl.BlockSpec((tm, KVA_W), row),
                   pl.BlockSpec((QB_W // LANES, tm, LANES), lambda i: (0, i, 0)),
                   pl.BlockSpec((KVB_W // LANES, tm, LANES), lambda i: (0, i, 0)),
                   pl.BlockSpec((tm, GA_W), row)],
        out_shape=[jax.ShapeDtypeStruct((n, QA_W), BF16), jax.ShapeDtypeStruct((n, KVA_W), BF16),
                   jax.ShapeDtypeStruct((QB_W // LANES, n, LANES), F32),
                   jax.ShapeDtypeStruct((KVB_W // LANES, n, LANES), F32),
                   jax.ShapeDtypeStruct((n, GA_W), F32)],
        compiler_params=_params(("parallel",)),
        name="proj",
    )(x2, g, w)


def _compress_kernel(x_ref, pe_ref, wa_ref, wb_ref, w2_ref, o_ref):
    xf = x_ref[0, 0].astype(F32)
    nch = xf.shape[0]
    xa = (xf + pe_ref[0:1, :]).astype(BF16)
    xb = (xf + pe_ref[1:2, :]).astype(BF16)
    ya = _dot(xa, wa_ref[...])
    yb = _dot(xb, wb_ref[...])
    h = ya + pltpu.roll(yb, nch - 1, 0)
    a = (h * jax.nn.sigmoid(h)).astype(BF16)
    o_ref[0, 0] = _dot(a, w2_ref[...]).astype(BF16)


def _compress(xc, pe, wa, wb, w2):
    b, g, nch, f = xc.shape
    hid = wa.shape[1]
    return pl.pallas_call(
        _compress_kernel,
        grid=(b, g),
        in_specs=[pl.BlockSpec((1, 1, nch, f), lambda i, j: (i, j, 0, 0)),
                  pl.BlockSpec((8, f), lambda i, j: (0, 0)),
                  pl.BlockSpec((f, hid), lambda i, j: (0, 0)),
                  pl.BlockSpec((f, hid), lambda i, j: (0, 0)),
                  pl.BlockSpec((hid, LANES), lambda i, j: (0, 0))],
        out_specs=pl.BlockSpec((1, 1, nch, LANES), lambda i, j: (i, j, 0, 0)),
        out_shape=jax.ShapeDtypeStruct((b, g, nch, LANES), BF16),
        compiler_params=_params(("parallel", "parallel")),
        name="compress",
    )(xc, pe, wa, wb, w2)


def _gate_spread():
    g = np.arange(NSA_KV_GROUPS)[:, None, None, None]
    br = np.arange(3)[None, :, None, None]
    r = np.arange(LANES)[None, None, :, None]
    c = np.arange(GROUP_W)[None, None, None, :]
    return jnp.asarray(r == 3 * (NSA_HPG * g + c // HEAD_DIM) + br, BF16)


def _stack_heads(q):
    return jnp.concatenate([q[:, LANES * h:LANES * (h + 1)] for h in range(NSA_HPG)], axis=0)


def _unstack_heads(o4):
    lane = _iota((Q_BLOCK, LANES), 1)
    o = [o4[Q_BLOCK * h:Q_BLOCK * (h + 1)] for h in range(NSA_HPG)]
    c0 = jnp.where(lane < HEAD_DIM, pltpu.roll(o[0], HEAD_DIM, 1), o[1])
    c1 = jnp.where(lane < HEAD_DIM, pltpu.roll(o[2], HEAD_DIM, 1), o[3])
    return jnp.concatenate([c0, c1], axis=1)


def _row_sums_all_lanes(a):
    lane = _iota(a.shape, 1)
    return jnp.where(lane < HEAD_DIM, a, pltpu.roll(a, HEAD_DIM, 1))


def _normalise_unstack(a):
    return _unstack_heads(a * (1.0 / jnp.maximum(_row_sums_all_lanes(a), TINY)))


def _expand_gate(gates, spread):
    hi = gates.astype(BF16)
    lo = (gates - hi.astype(F32)).astype(BF16)
    return _dot(hi, spread) + _dot(lo, spread)


def _ones_in_k_lanes(kv):
    lane = _iota(kv.shape, 1)
    return jnp.where(lane < HEAD_DIM, jnp.ones_like(kv), kv)


def _cmp_kernel(slopes_ref, q_ref, kvc_ref, ga_ref, ov_ref, gexp_ref,
                yc_ref, sel_ref, fl_ref, bias0, cq, rhs, *, n_top):
    g = pl.program_id(1)
    i = pl.program_id(2)
    hq = NSA_HPG
    ncp = cq.shape[1]

    @pl.when(i == 0)
    def _():
        kpos = CMP_STRIDE * _iota((Q_BLOCK, ncp), 1) + (CMP_BLOCK - 1)
        cq[...] = kpos - _iota((Q_BLOCK, ncp), 0)
        kf = kpos[0:8].astype(F32)
        for h in range(hq):
            bias0[h] = slopes_ref[hq * g + h] * kf
        rhs[:, 0:LANES] = _ones_in_k_lanes(kvc_ref[0, 0])
        rhs[:, LANES:2 * LANES] = ov_ref[...]

    for k in range(CMP_TILES):
        it = CMP_TILES * i + k
        rows = slice(k * Q_BLOCK, (k + 1) * Q_BLOCK)
        q4 = _stack_heads(q_ref[0, rows, :])
        s = _nt_dot(q4, kvc_ref[0, 0]).reshape(hq, Q_BLOCK, ncp)
        t0 = it * Q_BLOCK
        neg = jnp.where(cq[...] <= t0, 0.0, NEG_INF)
        tf = t0.astype(F32)
        ps = []
        for h in range(hq):
            sb = s[h] + (neg + (bias0[h, 0:1] - slopes_ref[hq * g + h] * tf))
            m = jnp.max(sb, axis=-1, keepdims=True)
            m = jnp.where(m < 0.1 * NEG_INF, -NEG_INF, m)
            ps.append(jnp.exp2(sb - m).astype(BF16))
        big = _dot(jnp.concatenate(ps, axis=0), rhs[...])
        inv = 1.0 / jnp.maximum(_row_sums_all_lanes(big[:, 0:LANES]), TINY)
        oc = _unstack_heads(big[:, 0:LANES] * inv)
        yc_ref[0, rows, :] = _expand_gate(ga_ref[0, rows, :], gexp_ref[0, 0]) * oc
        pov = big[:, LANES:2 * LANES] * inv
        imp = pov[0:Q_BLOCK]
        for h in range(1, hq):
            imp = imp + pov[Q_BLOCK * h:Q_BLOCK * (h + 1)]

        qrow = _iota((Q_BLOCK, LANES), 0)
        j = _iota((Q_BLOCK, LANES), 1)
        jt = 2 * it + (qrow >> 6)
        valid = j <= jt
        forced = (j == 0) | (j == jt) | (j == jt - 1)
        score = jnp.where(forced, FORCE_SCORE, jnp.where(valid, imp, -1.0))
        st = score.T
        jr = _iota((LANES, Q_BLOCK), 0)
        cur = st
        taken = -3e38
        for _ in range(n_top):
            mx = jnp.max(cur, axis=0, keepdims=True)
            first = jnp.min(jnp.where(cur == mx, jr, LANES), axis=0, keepdims=True)
            cur = jnp.where(jr == first, taken, cur)
        keep = jnp.where((cur == taken) & (st >= 0.0), 1.0, 0.0).T
        sel_ref[0, 0, rows, :] = keep.astype(BF16)

        fl_ref[k] = (jnp.max(keep, axis=0, keepdims=True) > 0.0).astype(jnp.int32)


def _cmp(slopes, qa, kvc, ga, ov, gexp, n_top):
    b, s, _ = qa.shape
    g = NSA_KV_GROUPS
    nq = s // Q_BLOCK
    ncp = kvc.shape[2]
    tq = CMP_TILES * Q_BLOCK
    nt = s // tq
    return pl.pallas_call(
        functools.partial(_cmp_kernel, n_top=n_top),
        grid=(b, g, nt),
        in_specs=[pl.BlockSpec(memory_space=pltpu.SMEM),
                  pl.BlockSpec((1, tq, QP_W), lambda bi, gi, i: (bi, i, gi)),
                  pl.BlockSpec((1, 1, ncp, LANES), lambda bi, gi, i: (bi, gi, 0, 0)),
                  pl.BlockSpec((1, tq, LANES), lambda bi, gi, i: (bi, i, 0)),
                  pl.BlockSpec((ncp, LANES), lambda bi, gi, i: (0, 0)),
                  pl.BlockSpec((1, 3, LANES, GROUP_W), lambda bi, gi, i: (gi, 0, 0, 0))],
        out_specs=[pl.BlockSpec((1, tq, GROUP_W), lambda bi, gi, i: (bi, i, gi)),
                   pl.BlockSpec((1, 1, tq, LANES), lambda bi, gi, i: (bi, gi, i, 0)),
                   pl.BlockSpec((CMP_TILES, 1, LANES), lambda bi, gi, i: ((bi * g + gi) * nt + i, 0, 0))],
        out_shape=[jax.ShapeDtypeStruct((b, s, g * GROUP_W), F32),
                   jax.ShapeDtypeStruct((b, g, s, LANES), BF16),
                   jax.ShapeDtypeStruct((b * g * nq, 1, LANES), jnp.int32)],
        scratch_shapes=[pltpu.VMEM((NSA_HPG, 8, ncp), F32),
                        pltpu.VMEM((Q_BLOCK, ncp), jnp.int32),
                        pltpu.VMEM((ncp, 2 * LANES), BF16)],
        compiler_params=_params(("parallel", "parallel", "arbitrary")),
        name="cmp",
    )(slopes, qa, kvc, ga, ov, gexp)


WIN_UNITS = NSA_WINDOW // Q_BLOCK + 1
LIST_UNROLL = 4
SLOT_CHUNK = 8


def _selwin_kernel(pf_ref, slopes_ref, q_ref, kvs_ref, kvw_ref, sel_ref, ga_ref, yc_ref,
                   gexp_ref, ya_ref, sbuf, slot_unit, mrun, acc, accw, *, nq):
    b = pl.program_id(0)
    g = pl.program_id(1)
    i = pl.program_id(2)
    hq = NSA_HPG
    qmk = _iota((Q_BLOCK, LANES), 0) - _iota((Q_BLOCK, LANES), 1)
    koff = _iota((1, LANES), 1).astype(F32)
    q4 = _stack_heads(q_ref[0])
    sel = sel_ref[0, 0]
    dmat = _iota((LANES, LANES), 0) - (_iota((LANES, LANES), 1) >> 6)
    dead = -(1 << 20)

    for d in range(WIN_UNITS):
        u = i - (WIN_UNITS - 1) + d
        slot_unit[d] = jnp.where(u >= 0, u, -1)

    base = ((b * NSA_KV_GROUPS + g) * nq + i) * LANES

    def list_body(j4, k):
        for t in range(LIST_UNROLL):
            jj = j4 * LIST_UNROLL + t
            used = (pf_ref[base + 2 * jj] | pf_ref[base + 2 * jj + 1]) != 0

            @pl.when(used)
            def _(jj=jj, k=k):
                slot_unit[WIN_UNITS + k] = jj
            k = k + used.astype(jnp.int32)
        return k

    nsel = lax.fori_loop(0, i // LIST_UNROLL + 1, list_body, jnp.int32(0))
    for t in range(SLOT_CHUNK - 1):
        slot_unit[WIN_UNITS + nsel + t] = -1
    nchunk = (nsel + (SLOT_CHUNK - 1)) // SLOT_CHUNK

    def kv_tile(kv_ref, slot):
        u = jnp.maximum(slot_unit[slot], 0)
        return kv_ref[0, pl.ds(pl.multiple_of(u * Q_BLOCK, Q_BLOCK), Q_BLOCK), :]

    def score_slots(kv_ref, slots, window):
        units = [slot_unit[slot] for slot in slots]
        s_all = _nt_dot(q4, jnp.concatenate([kv_tile(kv_ref, slot) for slot in slots], axis=0))
        if not window:
            spread = jnp.concatenate(
                [jnp.where(dmat == 2 * u, 1.0, 0.0).astype(BF16) for u in units], axis=1)
            chosen = _dot(sel, spread)
        for n, (slot, u) in enumerate(zip(slots, units)):
            s = s_all[:, n * LANES:(n + 1) * LANES].reshape(hq, Q_BLOCK, LANES)
            off = jnp.where(u >= 0, (i - u) * Q_BLOCK, dead)
            dist = qmk + off
            if window:
                allow = (dist >= 0) & (dist < NSA_WINDOW)
            else:
                allow = (dist >= 0) & (chosen[:, n * LANES:(n + 1) * LANES] > 0.5)
            neg = jnp.where(allow, 0.0, NEG_INF)
            kd = koff - off.astype(F32)
            for h in range(hq):
                sb = s[h] + (neg + slopes_ref[hq * g + h] * kd)
                sbuf[slot, h] = sb
                mrun[h] = jnp.maximum(mrun[h], sb)

    def spread_max():
        m = jnp.max(mrun[...], axis=-1, keepdims=True)
        mrun[...] = jnp.broadcast_to(m, mrun.shape)

    def weighted_values(kv_ref, slots):
        tot = None
        for n in range(0, len(slots), 2):
            pair = slots[n:n + 2]
            p = jnp.concatenate(
                [jnp.exp2(sbuf[slot] - mrun[...]).astype(BF16).reshape(hq * Q_BLOCK, LANES)
                 for slot in pair], axis=1)
            v = jnp.concatenate([_ones_in_k_lanes(kv_tile(kv_ref, slot)) for slot in pair], axis=0)
            d = _dot(p, v)
            tot = d if tot is None else tot + d
        return tot

    mrun[...] = jnp.full_like(mrun, NEG_INF)
    for d in range(0, WIN_UNITS, 2):
        score_slots(kvw_ref, list(range(d, min(d + 2, WIN_UNITS))), True)
    spread_max()
    accw[...] = weighted_values(kvw_ref, list(range(WIN_UNITS)))

    mrun[...] = jnp.full_like(mrun, NEG_INF)

    def score_chunk(ci, carry):
        for t in range(0, SLOT_CHUNK, 2):
            first = WIN_UNITS + ci * SLOT_CHUNK + t
            score_slots(kvs_ref, [first, first + 1], False)
        return carry

    lax.fori_loop(0, nchunk, score_chunk, 0)
    spread_max()
    acc[...] = jnp.zeros_like(acc)

    def value_chunk(ci, carry):
        acc[...] += weighted_values(kvs_ref, [WIN_UNITS + ci * SLOT_CHUNK + t for t in range(SLOT_CHUNK)])
        return carry

    lax.fori_loop(0, nchunk, value_chunk, 0)
    o_s = _normalise_unstack(acc[...])
    o_w = _normalise_unstack(accw[...])

    gates = ga_ref[0]
    ya = (yc_ref[0] + _expand_gate(gates, gexp_ref[0, 1]) * o_s
          + _expand_gate(gates, gexp_ref[0, 2]) * o_w)
    ya_ref[0] = ya.astype(BF16)


def _selwin(pflags, slopes, qa, kva, sel, ga, yc, gexp):
    b, s, _ = qa.shape
    g = NSA_KV_GROUPS
    nq = s // Q_BLOCK
    nbuf = WIN_UNITS + nq + SLOT_CHUNK - 1
    grid_spec = pltpu.PrefetchScalarGridSpec(
        num_scalar_prefetch=1,
        grid=(b, g, nq),
        in_specs=[pl.BlockSpec(memory_space=pltpu.SMEM),
                  pl.BlockSpec((1, Q_BLOCK, QP_W), lambda bi, gi, i, pf: (bi, i, gi)),
                  pl.BlockSpec((1, s, LANES), lambda bi, gi, i, pf: (bi, 0, 2 + gi)),
                  pl.BlockSpec((1, s, LANES), lambda bi, gi, i, pf: (bi, 0, 4 + gi)),
                  pl.BlockSpec((1, 1, Q_BLOCK, LANES), lambda bi, gi, i, pf: (bi, gi, i, 0)),
                  pl.BlockSpec((1, Q_BLOCK, LANES), lambda bi, gi, i, pf: (bi, i, 0)),
                  pl.BlockSpec((1, Q_BLOCK, GROUP_W), lambda bi, gi, i, pf: (bi, i, gi)),
                  pl.BlockSpec((1, 3, LANES, GROUP_W), lambda bi, gi, i, pf: (gi, 0, 0, 0))],
        out_specs=pl.BlockSpec((1, Q_BLOCK, GROUP_W), lambda bi, gi, i, pf: (bi, i, gi)),
        scratch_shapes=[pltpu.VMEM((nbuf, NSA_HPG, Q_BLOCK, LANES), F32),
                        pltpu.SMEM((nbuf,), jnp.int32),
                        pltpu.VMEM((NSA_HPG, Q_BLOCK, LANES), F32),
                        pltpu.VMEM((NSA_HPG * Q_BLOCK, LANES), F32),
                        pltpu.VMEM((NSA_HPG * Q_BLOCK, LANES), F32)],
    )
    return pl.pallas_call(
        functools.partial(_selwin_kernel, nq=nq),
        grid_spec=grid_spec,
        out_shape=jax.ShapeDtypeStruct((b, s, g * GROUP_W), BF16),
        compiler_params=_params(("parallel", "parallel", "arbitrary")),
        name="selwin",
    )(pflags, slopes, qa, kva, kva, sel, ga, yc, gexp)


def _dil_bias(slopes):
    c = DIL_BLOCK
    qi = np.arange(c)[:, None]
    ki = np.arange(2 * c)[None, :]
    delta = qi + c - ki
    out = []
    for k, (window, dil) in enumerate(DIL_GROUPS):
        in_band = jnp.asarray((delta >= 0) & (delta <= window // dil))
        dist = jnp.asarray((delta * dil).astype(np.float32))
        for h in range(DIL_HPG):
            out.append(jnp.where(in_band, -(slopes[DIL_HPG * k + h] * dist) * LOG2E, NEG_INF))
    return jnp.stack(out).reshape(len(out) // 2, 2 * c, 2 * c)


def _dil_group(k, dil, sup, q_ref, kvc_ref, kvp_ref, bias_ref, mx_ref, w_ref, y_ref):
    c = DIL_BLOCK
    nblk = sup // (c * dil)
    has_halo = pl.program_id(1) > 0
    nch = DIL_HPG * HEAD_DIM // LANES
    upper = _iota((c, LANES), 1) >= HEAD_DIM

    def block(idx):
        r = idx // nblk
        n = idx - r * nblk
        start = r + dil * c * n
        rows = pl.ds(start, c, stride=dil)
        rows_prev = pl.ds(jnp.maximum(start - dil * c, r), c, stride=dil)
        rows_halo = pl.ds(r + dil * c * (nblk - 1), c, stride=dil)
        pen = jnp.where(jnp.logical_or(n >= 1, has_halo), 0.0, NEG_INF)

        def prev(ch):
            return jnp.where(n >= 1, kvc_ref[ch, 0, rows_prev, :], kvp_ref[ch, 0, rows_halo, :]).astype(BF16)

        for ch in range(nch):
            qb = q_ref[ch, 0, rows, :].astype(BF16)
            zero = jnp.zeros_like(qb)
            qs = jnp.concatenate([jnp.where(upper, zero, qb), jnp.where(upper, qb, zero)], axis=0)
            kband = jnp.concatenate([prev(ch), kvc_ref[ch, 0, rows, :].astype(BF16)], axis=0)
            vband = jnp.concatenate([prev(nch + ch), kvc_ref[nch + ch, 0, rows, :].astype(BF16)], axis=0)
            s = _nt_dot(qs, kband) + bias_ref[ch]
            s = jnp.concatenate([s[:, :c] + pen, s[:, c:]], axis=1)
            m = jnp.max(s, axis=-1, keepdims=True)
            e = jnp.exp2(s - m).astype(BF16)
            ones = jnp.ones((2 * c, LANES), BF16)
            ol = _dot(e, jnp.concatenate([vband, ones], axis=1))
            lt = jnp.maximum(ol[:, LANES:], TINY)
            o2 = ol[:, :LANES] * (1.0 / lt)
            l2 = m * LN2 + jnp.log(lt)
            o = jnp.where(upper, o2[c:], o2[:c])
            lse = jnp.where(upper, l2[c:], l2[:c])
            if k == 0:
                mx_ref[ch, rows, :] = lse
                w_ref[ch, rows, :] = jnp.ones_like(lse)
                y_ref[ch, rows, :] = o
            else:
                mx_old = mx_ref[ch, rows, :]
                mx_new = jnp.maximum(mx_old, lse)
                a = jnp.exp(mx_old - mx_new)
                bnew = jnp.exp(lse - mx_new)
                mx_ref[ch, rows, :] = mx_new
                w_ref[ch, rows, :] = w_ref[ch, rows, :] * a + bnew
                y_ref[ch, rows, :] = y_ref[ch, rows, :] * a + bnew * o

    def body(t, carry):
        for j in range(DIL_UNROLL):
            block(t * DIL_UNROLL + j)
        return carry

    lax.fori_loop(0, sup // (c * DIL_UNROLL), body, 0)


def _dil_kernel(q_ref, kvc_ref, kvp_ref, bias_ref, yb_ref, mx_ref, w_ref, y_ref, *, sup):
    gi = pl.program_id(2)
    for k, (_, dil) in enumerate(DIL_GROUPS):
        @pl.when(gi == k)
        def _(k=k, dil=dil):
            _dil_group(k, dil, sup, q_ref, kvc_ref, kvp_ref, bias_ref, mx_ref, w_ref, y_ref)

    @pl.when(gi == len(DIL_GROUPS) - 1)
    def _():
        for ch in range(DIL_HPG * HEAD_DIM // LANES):
            yb_ref[0, :, ch * LANES:(ch + 1) * LANES] = (y_ref[ch] * (1.0 / w_ref[ch])).astype(BF16)


def _dil(qb, kvb, bias, sup):
    _, b, s, _ = qb.shape
    hw = DIL_HPG * HEAD_DIM
    nch = hw // LANES
    ng = len(DIL_GROUPS)
    return pl.pallas_call(
        functools.partial(_dil_kernel, sup=sup),
        grid=(b, s // sup, ng),
        in_specs=[pl.BlockSpec((nch, 1, sup, LANES), lambda bi, si, gi: (gi, bi, si, 0)),
                  pl.BlockSpec((2 * nch, 1, sup, LANES), lambda bi, si, gi: (gi, bi, si, 0)),
                  pl.BlockSpec((2 * nch, 1, sup, LANES),
                               lambda bi, si, gi: (gi, bi, jnp.maximum(si - 1, 0), 0)),
                  pl.BlockSpec((nch, 2 * DIL_BLOCK, 2 * DIL_BLOCK), lambda bi, si, gi: (gi, 0, 0))],
        out_specs=pl.BlockSpec((1, sup, hw), lambda bi, si, gi: (bi, si, 0)),
        out_shape=jax.ShapeDtypeStruct((b, s, hw), BF16),
        scratch_shapes=[pltpu.VMEM((nch, sup, LANES), F32), pltpu.VMEM((nch, sup, LANES), F32),
                        pltpu.VMEM((nch, sup, LANES), F32)],
        compiler_params=_params(("parallel", "parallel", "arbitrary")),
        name="dil",
    )(qb, kvb, kvb, bias)


def _merge_kernel(x_ref, g_ref, ya_ref, yb_ref, wg_ref, ua_ref, ub_ref, wo_ref, o_ref):
    x = x_ref[...]
    d = x.shape[1]
    hn = _rms(x, g_ref[...]).astype(BF16)
    gate_a = jax.nn.sigmoid(_dot(hn, wg_ref[:, 0:d]))
    gate_b = jax.nn.sigmoid(_dot(hn, wg_ref[:, d:2 * d]))
    merged = gate_a * _dot(ya_ref[...], ua_ref[...]) + gate_b * _dot(yb_ref[...], ub_ref[...])
    o_ref[...] = x + _dot(merged.astype(BF16), wo_ref[...])


def _merge(x2, g, ya, yb, wg, ua, ub, wo, tm):
    n, d = x2.shape
    row = lambda i: (i, 0)
    fixed = lambda i: (0, 0)
    return pl.pallas_call(
        _merge_kernel,
        grid=(n // tm,),
        in_specs=[pl.BlockSpec((tm, d), row), pl.BlockSpec((1, d), fixed),
                  pl.BlockSpec((tm, ya.shape[1]), row), pl.BlockSpec((tm, yb.shape[1]), row),
                  pl.BlockSpec(wg.shape, fixed), pl.BlockSpec(ua.shape, fixed),
                  pl.BlockSpec(ub.shape, fixed), pl.BlockSpec(wo.shape, fixed)],
        out_specs=pl.BlockSpec((tm, d), row),
        out_shape=jax.ShapeDtypeStruct((n, d), F32),
        compiler_params=_params(("parallel",)),
        name="merge",
    )(x2, g, ya, yb, wg, ua, ub, wo)


def _ffn_kernel(x_ref, g_ref, wi_ref, wo_ref, gf_ref, o_ref, *, tf, final_norm):
    x = x_ref[...]
    dff = wo_ref.shape[0]
    hn = _rms(x, g_ref[...]).astype(BF16)
    y = x
    for c in range(0, dff, tf):
        gt = _dot(hn, wi_ref[:, c:c + tf])
        up = _dot(hn, wi_ref[:, dff + c:dff + c + tf])
        act = ((gt * jax.nn.sigmoid(gt)) * up).astype(BF16)
        y = y + _dot(act, wo_ref[c:c + tf, :])
    if final_norm:
        y = _rms(y, gf_ref[...])
    o_ref[...] = y


def _ffn(x2, g, w_in, w_out, gf, tm, tf, final_norm):
    n, d = x2.shape
    fixed = lambda i: (0, 0)
    once = pl.Buffered(1)
    return pl.pallas_call(
        functools.partial(_ffn_kernel, tf=tf, final_norm=final_norm),
        grid=(n // tm,),
        in_specs=[pl.BlockSpec((tm, d), lambda i: (i, 0)),
                  pl.BlockSpec((1, d), fixed),
                  pl.BlockSpec(w_in.shape, fixed, pipeline_mode=once),
                  pl.BlockSpec(w_out.shape, fixed, pipeline_mode=once),
                  pl.BlockSpec((1, d), fixed)],
        out_specs=pl.BlockSpec((tm, d), lambda i: (i, 0)),
        out_shape=jax.ShapeDtypeStruct((n, d), F32),
        compiler_params=_params(("parallel",)),
        name="ffn",
    )(x2, g, w_in, w_out, gf)


def _alibi_slopes():
    k = jnp.arange(1, N_ALIBI + 1, dtype=F32)
    s = jnp.exp2(-8.0 * k / N_ALIBI)
    nsa = s[2 * DIL_HPG:2 * DIL_HPG + NSA_HEADS]
    dil = jnp.concatenate([s[:2 * DIL_HPG], s[2 * DIL_HPG + NSA_HEADS:]])
    return nsa, dil


def _overlap(seq, ncp):
    n_c = (seq - CMP_BLOCK) // CMP_STRIDE + 1
    n_s = seq // SEL_BLOCK
    c_start = np.arange(n_c) * CMP_STRIDE
    s_start = np.arange(n_s) * SEL_BLOCK
    ov = np.clip(np.minimum(c_start[:, None] + CMP_BLOCK, s_start[None, :] + SEL_BLOCK)
                 - np.maximum(c_start[:, None], s_start[None, :]), 0, None).astype(np.float32) / CMP_BLOCK
    out = np.zeros((ncp, LANES), np.float32)
    out[:n_c, :n_s] = ov
    return jnp.asarray(out, BF16)


def _proj_weights(w_in):
    dep, d, _ = w_in.shape
    scale = HEAD_DIM ** -0.5 * LOG2E
    c0 = NSA_HEADS * HEAD_DIM
    c1 = c0 + 3 * 2 * NSA_KV_GROUPS * HEAD_DIM
    c2 = c1 + NSA_HEADS * 3
    c3 = c2 + 3 * DIL_HEADS * HEAD_DIM
    qa = (w_in[..., :c0] * scale).reshape(dep, d, NSA_HEADS, HEAD_DIM)
    qa = jnp.pad(qa, ((0, 0), (0, 0), (0, 0), (0, LANES - HEAD_DIM))).reshape(dep, d, QA_W)
    kva = w_in[..., c0:c1].reshape(dep, d, 3, 2, NSA_KV_GROUPS, HEAD_DIM)
    kva = kva.transpose(0, 1, 2, 4, 3, 5).reshape(dep, d, KVA_W)
    ga = jnp.pad(w_in[..., c1:c2], ((0, 0), (0, 0), (0, GA_W - (c2 - c1))))
    qkvb = w_in[..., c2:c3].reshape(dep, d, 3, len(DIL_GROUPS), DIL_HPG * HEAD_DIM)
    qb = (qkvb[:, :, 0] * scale).reshape(dep, d, QB_W)
    kvb = qkvb[:, :, 1:3].transpose(0, 1, 3, 2, 4).reshape(dep, d, KVB_W)
    w_a = jnp.concatenate([qa, kva, qb, kvb, ga], axis=-1).astype(BF16)
    return w_a, w_in[..., c3:].astype(BF16)


def _compress_weights(pe_k, pe_v, w_ck1, w_ck2, w_cv1, w_cv2):
    dep = pe_k.shape[0]
    half = CMP_BLOCK // 2
    ck = w_ck1.reshape(dep, CMP_BLOCK, HEAD_DIM, CMP_HIDDEN)
    cv = w_cv1.reshape(dep, CMP_BLOCK, HEAD_DIM, CMP_HIDDEN)
    z = jnp.zeros_like(ck)
    w = jnp.concatenate([jnp.concatenate([ck, z], axis=-1),
                         jnp.concatenate([z, cv], axis=-1)], axis=2)
    feat = half * 2 * HEAD_DIM
    wa = w[:, :half].reshape(dep, feat, 2 * CMP_HIDDEN).astype(BF16)
    wb = w[:, half:].reshape(dep, feat, 2 * CMP_HIDDEN).astype(BF16)
    pe = jnp.concatenate([pe_k, pe_v], axis=-1)
    pe2 = jnp.stack([pe[:, :half].reshape(dep, feat), pe[:, half:].reshape(dep, feat)], axis=1)
    pe2 = jnp.pad(pe2, ((0, 0), (0, 6), (0, 0)))
    z2 = jnp.zeros_like(w_ck2)
    w2 = jnp.concatenate([jnp.concatenate([w_ck2, z2], axis=-1),
                          jnp.concatenate([z2, w_cv2], axis=-1)], axis=1).astype(BF16)
    return pe2, wa, wb, w2


def kernel(x, norm_mix, w_in, pe_k, pe_v, w_ck1, w_ck2, w_cv1, w_cv2, w_up_nsa, w_up_dil,
           w_out, norm_ffn, w_ffn_in, w_ffn_out, norm_final):
    b, s, d = x.shape
    depth = w_in.shape[0]
    n = b * s
    nq = s // Q_BLOCK
    nch = s // CMP_STRIDE
    n_s = s // SEL_BLOCK
    assert s % DIL_SUPER == 0 and n_s <= LANES
    n_top = min(SEL_TOPK, n_s)
    tm = 512
    tm_ffn = 512
    tf = w_ffn_out.shape[1] // 2

    slopes_nsa, slopes_dil = _alibi_slopes()
    slopes_nsa = slopes_nsa * LOG2E
    ov = _overlap(s, nch)
    gexp = _gate_spread()
    dbias = _dil_bias(slopes_dil)
    w_a, w_gm = _proj_weights(w_in)
    pe2, wa, wb, w2 = _compress_weights(pe_k, pe_v, w_ck1, w_ck2, w_cv1, w_cv2)
    ua = w_up_nsa.astype(BF16)
    ub = w_up_dil.astype(BF16)
    wo = w_out.astype(BF16)
    wfi = w_ffn_in.astype(BF16)
    wfo = w_ffn_out.astype(BF16)
    gf = norm_final.reshape(1, d)

    x2 = x.reshape(n, d)
    for l in range(depth):
        g_mix = norm_mix[l].reshape(1, d)
        qa, kva, qb, kvb, ga = _proj(x2, g_mix, w_a[l], tm)
        qa = qa.reshape(b, s, QA_W)
        kva = kva.reshape(b, s, KVA_W)
        ga = ga.reshape(b, s, GA_W)
        xc = kva[:, :, :NSA_KV_GROUPS * LANES].reshape(b, nch, CMP_STRIDE, NSA_KV_GROUPS, LANES)
        xc = xc.transpose(0, 3, 1, 2, 4).reshape(b, NSA_KV_GROUPS, nch, CMP_STRIDE * LANES)
        kvc = _compress(xc, pe2[l], wa[l], wb[l], w2[l])
        yc, sel, flags = _cmp(slopes_nsa, qa, kvc, ga, ov, gexp, n_top)
        pflags = flags.reshape(-1)
        ya = _selwin(pflags, slopes_nsa, qa, kva, sel, ga, yc, gexp)
        yb = _dil(qb.reshape(-1, b, s, LANES), kvb.reshape(-1, b, s, LANES), dbias, DIL_SUPER)
        x2 = _merge(x2, g_mix, ya.reshape(n, NSA_HEADS * HEAD_DIM), yb.reshape(n, DIL_HPG * HEAD_DIM),
                    w_gm[l], ua[l], ub[l], wo[l], tm)
        x2 = _ffn(x2, norm_ffn[l].reshape(1, d), wfi[l], wfo[l], gf, tm_ffn, tf,
                  final_norm=(l == depth - 1))
    return x2.reshape(b, s, d)
```

```python
import functools

import numpy as np
import jax
import jax.numpy as jnp
from jax import lax
from jax.experimental import pallas as pl
from jax.experimental.pallas import tpu as pltpu

F32 = jnp.float32
BF16 = jnp.bfloat16

HEAD_DIM = 64
NSA_HEADS = 8
NSA_KV_GROUPS = 2
NSA_HPG = NSA_HEADS // NSA_KV_GROUPS
CMP_BLOCK = 32
CMP_STRIDE = 16
CMP_HIDDEN = 256
SEL_BLOCK = 64
SEL_TOPK = 16
NSA_WINDOW = 512
Q_BLOCK = 128
FORCE_SCORE = 1e6
DIL_GROUPS = ((128, 1), (512, 4), (2048, 16))
DIL_HPG = 4
DIL_HEADS = DIL_HPG * len(DIL_GROUPS)
DIL_BLOCK = 128
N_ALIBI = NSA_HEADS + DIL_HEADS
EPS = 1e-6
NEG_INF = -1e30
TINY = 1e-30
LOG2E = 1.4426950408889634
LN2 = 0.6931471805599453

LANES = 128
GROUP_W = NSA_HPG * HEAD_DIM
DIL_SUPER = 2048
DIL_UNROLL = 4
CMP_TILES = 4
VMEM_LIMIT = 48 * 1024 * 1024
MIXFFN_VMEM_LIMIT = 56 * 1024 * 1024


def _dot(a, b):
    return jnp.dot(a, b, preferred_element_type=F32)


def _nt_dot(a, b):
    return lax.dot_general(a, b, (((1,), (1,)), ((), ())), preferred_element_type=F32)


def _iota(shape, dim):
    return lax.broadcasted_iota(jnp.int32, shape, dim)


def _rms(x, g):
    ms = jnp.mean(x * x, axis=-1, keepdims=True)
    return (x * lax.rsqrt(ms + EPS)) * g


def _params(sem):
    return pltpu.CompilerParams(dimension_semantics=sem, vmem_limit_bytes=VMEM_LIMIT)


QA_W = NSA_HEADS * LANES
QP_W = NSA_HPG * LANES
CMP_W = NSA_KV_GROUPS * LANES
KVA_W, QB_W, KVB_W, GA_W = 512, 768, 1536, 128
PROJ_COLS = QA_W + CMP_W + KVA_W + QB_W + KVB_W + GA_W


def _proj_kernel(x_ref, g_ref, w_ref, qa_ref, xc_ref, kva_ref, qb_ref, kvb_ref, ga_ref, cmp_scr):
    y = _rms(x_ref[...], g_ref[...]).astype(BF16)
    c = 0
    qa_ref[...] = _dot(y, w_ref[:, c:c + QA_W]).astype(BF16)
    c += QA_W
    rc = _dot(y, w_ref[:, c:c + CMP_W])
    c += CMP_W
    nrow = xc_ref.shape[1]
    for gi in range(NSA_KV_GROUPS):
        cmp_scr[gi] = rc[:, gi * LANES:(gi + 1) * LANES]
        for l in range(CMP_STRIDE):
            xc_ref[gi, :, l * LANES:(l + 1) * LANES] = (
                cmp_scr[gi, pl.ds(l, nrow, stride=CMP_STRIDE), :].astype(BF16))
    kva_ref[...] = _dot(y, w_ref[:, c:c + KVA_W]).astype(BF16)
    c += KVA_W
    for j in range(0, QB_W // LANES, 2):
        r = _dot(y, w_ref[:, c:c + 2 * LANES])
        qb_ref[j] = r[:, :LANES]
        qb_ref[j + 1] = r[:, LANES:]
        c += 2 * LANES
    for j in range(0, KVB_W // LANES, 2):
        r = _dot(y, w_ref[:, c:c + 2 * LANES])
        kvb_ref[j] = r[:, :LANES]
        kvb_ref[j + 1] = r[:, LANES:]
        c += 2 * LANES
    ga_ref[...] = jax.nn.sigmoid(_dot(y, w_ref[:, c:c + GA_W]))


def _proj(x2, g, w, tm):
    n, d = x2.shape
    row = lambda i: (i, 0)
    fixed = lambda i: (0, 0)
    return pl.pallas_call(
        _proj_kernel,
        grid=(n // tm,),
        in_specs=[pl.BlockSpec((tm, d), row), pl.BlockSpec((1, d), fixed),
                  pl.BlockSpec((d, PROJ_COLS), fixed, pipeline_mode=pl.Buffered(1))],
        out_specs=[pl.BlockSpec((tm, QA_W), row),
                   pl.BlockSpec((NSA_KV_GROUPS, tm // CMP_STRIDE, CMP_STRIDE * LANES), lambda i: (0, i, 0)),
                   pl.BlockSpec((tm, KVA_W), row),
                   pl.BlockSpec((QB_W // LANES, tm, LANES), lambda i: (0, i, 0)),
                   pl.BlockSpec((KVB_W // LANES, tm, LANES), lambda i: (0, i, 0)),
                   pl.BlockSpec((tm, GA_W), row)],
        out_shape=[jax.ShapeDtypeStruct((n, QA_W), BF16),
                   jax.ShapeDtypeStruct((NSA_KV_GROUPS, n // CMP_STRIDE, CMP_STRIDE * LANES), BF16),
                   jax.ShapeDtypeStruct((n, KVA_W), BF16),
                   jax.ShapeDtypeStruct((QB_W // LANES, n, LANES), F32),
                   jax.ShapeDtypeStruct((KVB_W // LANES, n, LANES), F32),
                   jax.ShapeDtypeStruct((n, GA_W), F32)],
        scratch_shapes=[pltpu.VMEM((NSA_KV_GROUPS, tm, LANES), F32)],
        compiler_params=_params(("parallel",)),
        name="proj",
    )(x2, g, w)


def _compress_kernel(x_ref, pe_ref, wa_ref, wb_ref, w2_ref, o_ref):
    xf = x_ref[0].astype(F32)
    nch = xf.shape[0]
    xa = (xf + pe_ref[0:1, :]).astype(BF16)
    xb = (xf + pe_ref[1:2, :]).astype(BF16)
    ya = _dot(xa, wa_ref[...])
    yb = _dot(xb, wb_ref[...])
    h = ya + pltpu.roll(yb, nch - 1, 0)
    a = (h * jax.nn.sigmoid(h)).astype(BF16)
    o_ref[0, 0] = _dot(a, w2_ref[...]).astype(BF16)


def _compress(xc, pe, wa, wb, w2, b):
    g, rows, f = xc.shape
    nch = rows // b
    hid = wa.shape[1]
    return pl.pallas_call(
        _compress_kernel,
        grid=(b, g),
        in_specs=[pl.BlockSpec((1, nch, f), lambda i, j: (j, i, 0)),
                  pl.BlockSpec((8, f), lambda i, j: (0, 0)),
                  pl.BlockSpec((f, hid), lambda i, j: (0, 0)),
                  pl.BlockSpec((f, hid), lambda i, j: (0, 0)),
                  pl.BlockSpec((hid, LANES), lambda i, j: (0, 0))],
        out_specs=pl.BlockSpec((1, 1, nch, LANES), lambda i, j: (i, j, 0, 0)),
        out_shape=jax.ShapeDtypeStruct((b, g, nch, LANES), BF16),
        compiler_params=_params(("parallel", "parallel")),
        name="compress",
    )(xc, pe, wa, wb, w2)


def _gate_spread():
    g = np.arange(NSA_KV_GROUPS)[:, None, None, None]
    br = np.arange(3)[None, :, None, None]
    r = np.arange(LANES)[None, None, :, None]
    c = np.arange(GROUP_W)[None, None, None, :]
    return jnp.asarray(r == 3 * (NSA_HPG * g + c // HEAD_DIM) + br, BF16)


def _stack_heads(q):
    return jnp.concatenate([q[:, LANES * h:LANES * (h + 1)] for h in range(NSA_HPG)], axis=0)


def _unstack_heads(o4):
    lane = _iota((Q_BLOCK, LANES), 1)
    o = [o4[Q_BLOCK * h:Q_BLOCK * (h + 1)] for h in range(NSA_HPG)]
    c0 = jnp.where(lane < HEAD_DIM, pltpu.roll(o[0], HEAD_DIM, 1), o[1])
    c1 = jnp.where(lane < HEAD_DIM, pltpu.roll(o[2], HEAD_DIM, 1), o[3])
    return jnp.concatenate([c0, c1], axis=1)


def _row_sums_all_lanes(a):
    lane = _iota(a.shape, 1)
    return jnp.where(lane < HEAD_DIM, a, pltpu.roll(a, HEAD_DIM, 1))


def _normalise_unstack(a):
    return _unstack_heads(a * (1.0 / jnp.maximum(_row_sums_all_lanes(a), TINY)))


def _expand_gate(gates, spread):
    hi = gates.astype(BF16)
    lo = (gates - hi.astype(F32)).astype(BF16)
    return _dot(hi, spread) + _dot(lo, spread)


def _ones_in_k_lanes(kv):
    lane = _iota(kv.shape, 1)
    return jnp.where(lane < HEAD_DIM, jnp.ones_like(kv), kv)


def _cmp_kernel(slopes_ref, q_ref, kvc_ref, ga_ref, ov_ref, gexp_ref,
                yc_ref, sel_ref, fl_ref, bias0, cq, rhs, *, n_top):
    g = pl.program_id(1)
    i = pl.program_id(2)
    hq = NSA_HPG
    ncp = cq.shape[1]

    @pl.when(i == 0)
    def _():
        kpos = CMP_STRIDE * _iota((Q_BLOCK, ncp), 1) + (CMP_BLOCK - 1)
        cq[...] = kpos - _iota((Q_BLOCK, ncp), 0)
        kf = kpos[0:8].astype(F32)
        for h in range(hq):
            bias0[h] = slopes_ref[hq * g + h] * kf
        rhs[:, 0:LANES] = _ones_in_k_lanes(kvc_ref[0, 0])
        rhs[:, LANES:2 * LANES] = ov_ref[...]

    for k in range(CMP_TILES):
        it = CMP_TILES * i + k
        rows = slice(k * Q_BLOCK, (k + 1) * Q_BLOCK)
        q4 = _stack_heads(q_ref[0, rows, :])
        s = _nt_dot(q4, kvc_ref[0, 0]).reshape(hq, Q_BLOCK, ncp)
        t0 = it * Q_BLOCK
        neg = jnp.where(cq[...] <= t0, 0.0, NEG_INF)
        tf = t0.astype(F32)
        ps = []
        for h in range(hq):
            sb = s[h] + (neg + (bias0[h, 0:1] - slopes_ref[hq * g + h] * tf))
            m = jnp.max(sb, axis=-1, keepdims=True)
            m = jnp.where(m < 0.1 * NEG_INF, -NEG_INF, m)
            ps.append(jnp.exp2(sb - m).astype(BF16))
        big = _dot(jnp.concatenate(ps, axis=0), rhs[...])
        inv = 1.0 / jnp.maximum(_row_sums_all_lanes(big[:, 0:LANES]), TINY)
        oc = _unstack_heads(big[:, 0:LANES] * inv)
        yc_ref[0, rows, :] = _expand_gate(ga_ref[0, rows, :], gexp_ref[0, 0]) * oc
        pov = big[:, LANES:2 * LANES] * inv
        imp = pov[0:Q_BLOCK]
        for h in range(1, hq):
            imp = imp + pov[Q_BLOCK * h:Q_BLOCK * (h + 1)]

        qrow = _iota((Q_BLOCK, LANES), 0)
        j = _iota((Q_BLOCK, LANES), 1)
        jt = 2 * it + (qrow >> 6)
        valid = j <= jt
        forced = (j == 0) | (j == jt) | (j == jt - 1)
        score = jnp.where(forced, FORCE_SCORE, jnp.where(valid, imp, -1.0))
        st = score.T
        jr = _iota((LANES, Q_BLOCK), 0)
        cur = st
        taken = -3e38
        for _ in range(n_top):
            mx = jnp.max(cur, axis=0, keepdims=True)
            first = jnp.min(jnp.where(cur == mx, jr, LANES), axis=0, keepdims=True)
            cur = jnp.where(jr == first, taken, cur)
        keep = jnp.where((cur == taken) & (st >= 0.0), 1.0, 0.0).T
        sel_ref[0, 0, rows, :] = keep.astype(BF16)

        fl_ref[k] = (jnp.max(keep, axis=0, keepdims=True) > 0.0).astype(jnp.int32)


def _cmp(slopes, qa, kvc, ga, ov, gexp, n_top):
    b, s, _ = qa.shape
    g = NSA_KV_GROUPS
    nq = s // Q_BLOCK
    ncp = kvc.shape[2]
    tq = CMP_TILES * Q_BLOCK
    nt = s // tq
    return pl.pallas_call(
        functools.partial(_cmp_kernel, n_top=n_top),
        grid=(b, g, nt),
        in_specs=[pl.BlockSpec(memory_space=pltpu.SMEM),
                  pl.BlockSpec((1, tq, QP_W), lambda bi, gi, i: (bi, i, gi)),
                  pl.BlockSpec((1, 1, ncp, LANES), lambda bi, gi, i: (bi, gi, 0, 0)),
                  pl.BlockSpec((1, tq, LANES), lambda bi, gi, i: (bi, i, 0)),
                  pl.BlockSpec((ncp, LANES), lambda bi, gi, i: (0, 0)),
                  pl.BlockSpec((1, 3, LANES, GROUP_W), lambda bi, gi, i: (gi, 0, 0, 0))],
        out_specs=[pl.BlockSpec((1, tq, GROUP_W), lambda bi, gi, i: (bi, i, gi)),
                   pl.BlockSpec((1, 1, tq, LANES), lambda bi, gi, i: (bi, gi, i, 0)),
                   pl.BlockSpec((CMP_TILES, 1, LANES), lambda bi, gi, i: ((bi * g + gi) * nt + i, 0, 0))],
        out_shape=[jax.ShapeDtypeStruct((b, s, g * GROUP_W), F32),
                   jax.ShapeDtypeStruct((b, g, s, LANES), BF16),
                   jax.ShapeDtypeStruct((b * g * nq, 1, LANES), jnp.int32)],
        scratch_shapes=[pltpu.VMEM((NSA_HPG, 8, ncp), F32),
                        pltpu.VMEM((Q_BLOCK, ncp), jnp.int32),
                        pltpu.VMEM((ncp, 2 * LANES), BF16)],
        compiler_params=_params(("parallel", "parallel", "arbitrary")),
        name="cmp",
    )(slopes, qa, kvc, ga, ov, gexp)


WIN_UNITS = NSA_WINDOW // Q_BLOCK + 1
LIST_UNROLL = 4
SLOT_CHUNK = 8


def _selwin_kernel(pf_ref, slopes_ref, q_ref, kvs_ref, kvw_ref, sel_ref, ga_ref, yc_ref,
                   gexp_ref, ya_ref, sbuf, slot_unit, mrun, acc, accw, *, nq):
    b = pl.program_id(0)
    g = pl.program_id(1)
    i = pl.program_id(2)
    hq = NSA_HPG
    qmk = _iota((Q_BLOCK, LANES), 0) - _iota((Q_BLOCK, LANES), 1)
    koff = _iota((1, LANES), 1).astype(F32)
    q4 = _stack_heads(q_ref[0])
    sel = sel_ref[0, 0]
    dmat = _iota((LANES, LANES), 0) - (_iota((LANES, LANES), 1) >> 6)
    dead = -(1 << 20)

    for d in range(WIN_UNITS):
        u = i - (WIN_UNITS - 1) + d
        slot_unit[d] = jnp.where(u >= 0, u, -1)

    base = ((b * NSA_KV_GROUPS + g) * nq + i) * LANES

    def list_body(j4, k):
        for t in range(LIST_UNROLL):
            jj = j4 * LIST_UNROLL + t
            used = (pf_ref[base + 2 * jj] | pf_ref[base + 2 * jj + 1]) != 0

            @pl.when(used)
            def _(jj=jj, k=k):
                slot_unit[WIN_UNITS + k] = jj
            k = k + used.astype(jnp.int32)
        return k

    nsel = lax.fori_loop(0, i // LIST_UNROLL + 1, list_body, jnp.int32(0))
    for t in range(SLOT_CHUNK - 1):
        slot_unit[WIN_UNITS + nsel + t] = -1
    nchunk = (nsel + (SLOT_CHUNK - 1)) // SLOT_CHUNK

    def kv_tile(kv_ref, slot):
        u = jnp.maximum(slot_unit[slot], 0)
        return kv_ref[0, pl.ds(pl.multiple_of(u * Q_BLOCK, Q_BLOCK), Q_BLOCK), :]

    def score_slots(kv_ref, slots, window):
        units = [slot_unit[slot] for slot in slots]
        s_all = _nt_dot(q4, jnp.concatenate([kv_tile(kv_ref, slot) for slot in slots], axis=0))
        if not window:
            spread = jnp.concatenate(
                [jnp.where(dmat == 2 * u, 1.0, 0.0).astype(BF16) for u in units], axis=1)
            chosen = _dot(sel, spread)
        for n, (slot, u) in enumerate(zip(slots, units)):
            s = s_all[:, n * LANES:(n + 1) * LANES].reshape(hq, Q_BLOCK, LANES)
            off = jnp.where(u >= 0, (i - u) * Q_BLOCK, dead)
            dist = qmk + off
            if window:
                allow = (dist >= 0) & (dist < NSA_WINDOW)
            else:
                allow = (dist >= 0) & (chosen[:, n * LANES:(n + 1) * LANES] > 0.5)
            neg = jnp.where(allow, 0.0, NEG_INF)
            kd = koff - off.astype(F32)
            for h in range(hq):
                sb = s[h] + (neg + slopes_ref[hq * g + h] * kd)
                sbuf[slot, h] = sb
                mrun[h] = jnp.maximum(mrun[h], sb)

    def spread_max():
        m = jnp.max(mrun[...], axis=-1, keepdims=True)
        mrun[...] = jnp.broadcast_to(m, mrun.shape)

    def weighted_values(kv_ref, slots):
        tot = None
        for n in range(0, len(slots), 2):
            pair = slots[n:n + 2]
            p = jnp.concatenate(
                [jnp.exp2(sbuf[slot] - mrun[...]).astype(BF16).reshape(hq * Q_BLOCK, LANES)
                 for slot in pair], axis=1)
            v = jnp.concatenate([_ones_in_k_lanes(kv_tile(kv_ref, slot)) for slot in pair], axis=0)
            d = _dot(p, v)
            tot = d if tot is None else tot + d
        return tot

    mrun[...] = jnp.full_like(mrun, NEG_INF)
    for d in range(0, WIN_UNITS, 2):
        score_slots(kvw_ref, list(range(d, min(d + 2, WIN_UNITS))), True)
    spread_max()
    accw[...] = weighted_values(kvw_ref, list(range(WIN_UNITS)))

    mrun[...] = jnp.full_like(mrun, NEG_INF)

    def score_chunk(ci, carry):
        for t in range(0, SLOT_CHUNK, 2):
            first = WIN_UNITS + ci * SLOT_CHUNK + t
            score_slots(kvs_ref, [first, first + 1], False)
        return carry

    lax.fori_loop(0, nchunk, score_chunk, 0)
    spread_max()
    acc[...] = jnp.zeros_like(acc)

    def value_chunk(ci, carry):
        acc[...] += weighted_values(kvs_ref, [WIN_UNITS + ci * SLOT_CHUNK + t for t in range(SLOT_CHUNK)])
        return carry

    lax.fori_loop(0, nchunk, value_chunk, 0)
    o_s = _normalise_unstack(acc[...])
    o_w = _normalise_unstack(accw[...])

    gates = ga_ref[0]
    ya = (yc_ref[0] + _expand_gate(gates, gexp_ref[0, 1]) * o_s
          + _expand_gate(gates, gexp_ref[0, 2]) * o_w)
    ya_ref[0] = ya.astype(BF16)


def _selwin(pflags, slopes, qa, kva, sel, ga, yc, gexp):
    b, s, _ = qa.shape
    g = NSA_KV_GROUPS
    nq = s // Q_BLOCK
    nbuf = WIN_UNITS + nq + SLOT_CHUNK - 1
    grid_spec = pltpu.PrefetchScalarGridSpec(
        num_scalar_prefetch=1,
        grid=(b, g, nq),
        in_specs=[pl.BlockSpec(memory_space=pltpu.SMEM),
                  pl.BlockSpec((1, Q_BLOCK, QP_W), lambda bi, gi, i, pf: (bi, i, gi)),
                  pl.BlockSpec((1, s, LANES), lambda bi, gi, i, pf: (bi, 0, gi)),
                  pl.BlockSpec((1, s, LANES), lambda bi, gi, i, pf: (bi, 0, NSA_KV_GROUPS + gi)),
                  pl.BlockSpec((1, 1, Q_BLOCK, LANES), lambda bi, gi, i, pf: (bi, gi, i, 0)),
                  pl.BlockSpec((1, Q_BLOCK, LANES), lambda bi, gi, i, pf: (bi, i, 0)),
                  pl.BlockSpec((1, Q_BLOCK, GROUP_W), lambda bi, gi, i, pf: (bi, i, gi)),
                  pl.BlockSpec((1, 3, LANES, GROUP_W), lambda bi, gi, i, pf: (gi, 0, 0, 0))],
        out_specs=pl.BlockSpec((1, Q_BLOCK, GROUP_W), lambda bi, gi, i, pf: (bi, i, gi)),
        scratch_shapes=[pltpu.VMEM((nbuf, NSA_HPG, Q_BLOCK, LANES), F32),
                        pltpu.SMEM((nbuf,), jnp.int32),
                        pltpu.VMEM((NSA_HPG, Q_BLOCK, LANES), F32),
                        pltpu.VMEM((NSA_HPG * Q_BLOCK, LANES), F32),
                        pltpu.VMEM((NSA_HPG * Q_BLOCK, LANES), F32)],
    )
    return pl.pallas_call(
        functools.partial(_selwin_kernel, nq=nq),
        grid_spec=grid_spec,
        out_shape=jax.ShapeDtypeStruct((b, s, g * GROUP_W), BF16),
        compiler_params=_params(("parallel", "parallel", "arbitrary")),
        name="selwin",
    )(pflags, slopes, qa, kva, kva, sel, ga, yc, gexp)


def _dil_bias(slopes):
    c = DIL_BLOCK
    qi = np.arange(c)[:, None]
    ki = np.arange(2 * c)[None, :]
    delta = qi + c - ki
    out = []
    for k, (window, dil) in enumerate(DIL_GROUPS):
        in_band = jnp.asarray((delta >= 0) & (delta <= window // dil))
        dist = jnp.asarray((delta * dil).astype(np.float32))
        for h in range(DIL_HPG):
            out.append(jnp.where(in_band, -(slopes[DIL_HPG * k + h] * dist) * LOG2E, NEG_INF))
    return jnp.stack(out).reshape(len(out) // 2, 2 * c, 2 * c)


def _dil_group(k, dil, sup, q_ref, kvc_ref, kvp_ref, bias_ref, mx_ref, w_ref, y_ref):
    c = DIL_BLOCK
    nblk = sup // (c * dil)
    has_halo = pl.program_id(1) > 0
    nch = DIL_HPG * HEAD_DIM // LANES
    upper = _iota((c, LANES), 1) >= HEAD_DIM

    def block(idx):
        r = idx // nblk
        n = idx - r * nblk
        start = r + dil * c * n
        rows = pl.ds(start, c, stride=dil)
        rows_prev = pl.ds(jnp.maximum(start - dil * c, r), c, stride=dil)
        rows_halo = pl.ds(r + dil * c * (nblk - 1), c, stride=dil)
        pen = jnp.where(jnp.logical_or(n >= 1, has_halo), 0.0, NEG_INF)

        def prev(ch):
            return jnp.where(n >= 1, kvc_ref[ch, 0, rows_prev, :], kvp_ref[ch, 0, rows_halo, :]).astype(BF16)

        for ch in range(nch):
            qb = q_ref[ch, 0, rows, :].astype(BF16)
            zero = jnp.zeros_like(qb)
            qs = jnp.concatenate([jnp.where(upper, zero, qb), jnp.where(upper, qb, zero)], axis=0)
            kband = jnp.concatenate([prev(ch), kvc_ref[ch, 0, rows, :].astype(BF16)], axis=0)
            vband = jnp.concatenate([prev(nch + ch), kvc_ref[nch + ch, 0, rows, :].astype(BF16)], axis=0)
            s = _nt_dot(qs, kband) + bias_ref[ch]
            s = jnp.concatenate([s[:, :c] + pen, s[:, c:]], axis=1)
            m = jnp.max(s, axis=-1, keepdims=True)
            e = jnp.exp2(s - m).astype(BF16)
            ones = jnp.ones((2 * c, LANES), BF16)
            ol = _dot(e, jnp.concatenate([vband, ones], axis=1))
            lt = jnp.maximum(ol[:, LANES:], TINY)
            o2 = ol[:, :LANES] * (1.0 / lt)
            l2 = m * LN2 + jnp.log(lt)
            o = jnp.where(upper, o2[c:], o2[:c])
            lse = jnp.where(upper, l2[c:], l2[:c])
            if k == 0:
                mx_ref[ch, rows, :] = lse
                w_ref[ch, rows, :] = jnp.ones_like(lse)
                y_ref[ch, rows, :] = o
            else:
                mx_old = mx_ref[ch, rows, :]
                mx_new = jnp.maximum(mx_old, lse)
                a = jnp.exp(mx_old - mx_new)
                bnew = jnp.exp(lse - mx_new)
                mx_ref[ch, rows, :] = mx_new
                w_ref[ch, rows, :] = w_ref[ch, rows, :] * a + bnew
                y_ref[ch, rows, :] = y_ref[ch, rows, :] * a + bnew * o

    def body(t, carry):
        for j in range(DIL_UNROLL):
            block(t * DIL_UNROLL + j)
        return carry

    lax.fori_loop(0, sup // (c * DIL_UNROLL), body, 0)


def _dil_kernel(q_ref, kvc_ref, kvp_ref, bias_ref, yb_ref, mx_ref, w_ref, y_ref, *, sup):
    gi = pl.program_id(2)
    for k, (_, dil) in enumerate(DIL_GROUPS):
        @pl.when(gi == k)
        def _(k=k, dil=dil):
            _dil_group(k, dil, sup, q_ref, kvc_ref, kvp_ref, bias_ref, mx_ref, w_ref, y_ref)

    @pl.when(gi == len(DIL_GROUPS) - 1)
    def _():
        for ch in range(DIL_HPG * HEAD_DIM // LANES):
            yb_ref[0, :, ch * LANES:(ch + 1) * LANES] = (y_ref[ch] * (1.0 / w_ref[ch])).astype(BF16)


def _dil(qb, kvb, bias, sup):
    _, b, s, _ = qb.shape
    hw = DIL_HPG * HEAD_DIM
    nch = hw // LANES
    ng = len(DIL_GROUPS)
    return pl.pallas_call(
        functools.partial(_dil_kernel, sup=sup),
        grid=(b, s // sup, ng),
        in_specs=[pl.BlockSpec((nch, 1, sup, LANES), lambda bi, si, gi: (gi, bi, si, 0)),
                  pl.BlockSpec((2 * nch, 1, sup, LANES), lambda bi, si, gi: (gi, bi, si, 0)),
                  pl.BlockSpec((2 * nch, 1, sup, LANES),
                               lambda bi, si, gi: (gi, bi, jnp.maximum(si - 1, 0), 0)),
                  pl.BlockSpec((nch, 2 * DIL_BLOCK, 2 * DIL_BLOCK), lambda bi, si, gi: (gi, 0, 0))],
        out_specs=pl.BlockSpec((1, sup, hw), lambda bi, si, gi: (bi, si, 0)),
        out_shape=jax.ShapeDtypeStruct((b, s, hw), BF16),
        scratch_shapes=[pltpu.VMEM((nch, sup, LANES), F32), pltpu.VMEM((nch, sup, LANES), F32),
                        pltpu.VMEM((nch, sup, LANES), F32)],
        compiler_params=_params(("parallel", "parallel", "arbitrary")),
        name="dil",
    )(qb, kvb, kvb, bias)


def _merge_kernel(x_ref, g_ref, ya_ref, yb_ref, wg_ref, ua_ref, ub_ref, wo_ref, o_ref):
    x = x_ref[...]
    d = x.shape[1]
    hn = _rms(x, g_ref[...]).astype(BF16)
    gate_a = jax.nn.sigmoid(_dot(hn, wg_ref[:, 0:d]))
    gate_b = jax.nn.sigmoid(_dot(hn, wg_ref[:, d:2 * d]))
    merged = gate_a * _dot(ya_ref[...], ua_ref[...]) + gate_b * _dot(yb_ref[...], ub_ref[...])
    o_ref[...] = x + _dot(merged.astype(BF16), wo_ref[...])


def _merge(x2, g, ya, yb, wg, ua, ub, wo, tm):
    n, d = x2.shape
    row = lambda i: (i, 0)
    fixed = lambda i: (0, 0)
    return pl.pallas_call(
        _merge_kernel,
        grid=(n // tm,),
        in_specs=[pl.BlockSpec((tm, d), row), pl.BlockSpec((1, d), fixed),
                  pl.BlockSpec((tm, ya.shape[1]), row), pl.BlockSpec((tm, yb.shape[1]), row),
                  pl.BlockSpec(wg.shape, fixed), pl.BlockSpec(ua.shape, fixed),
                  pl.BlockSpec(ub.shape, fixed), pl.BlockSpec(wo.shape, fixed)],
        out_specs=pl.BlockSpec((tm, d), row),
        out_shape=jax.ShapeDtypeStruct((n, d), F32),
        compiler_params=_params(("parallel",)),
        name="merge",
    )(x2, g, ya, yb, wg, ua, ub, wo)


def _mixffn_kernel(x_ref, g_ref, ya_ref, yb_ref, wg_ref, ua_ref, ub_ref, wo_ref,
                   g2_ref, wi_ref, wf_ref, gf_ref, o_ref, *, tf, final_norm):
    x = x_ref[...]
    d = x.shape[1]
    hn = _rms(x, g_ref[...]).astype(BF16)
    gate_a = jax.nn.sigmoid(_dot(hn, wg_ref[:, 0:d]))
    gate_b = jax.nn.sigmoid(_dot(hn, wg_ref[:, d:2 * d]))
    merged = gate_a * _dot(ya_ref[...], ua_ref[...]) + gate_b * _dot(yb_ref[...], ub_ref[...])
    x = x + _dot(merged.astype(BF16), wo_ref[...])
    dff = wf_ref.shape[0]
    hn = _rms(x, g2_ref[...]).astype(BF16)
    y = x
    for c in range(0, dff, tf):
        gt = _dot(hn, wi_ref[:, c:c + tf])
        up = _dot(hn, wi_ref[:, dff + c:dff + c + tf])
        act = ((gt * jax.nn.sigmoid(gt)) * up).astype(BF16)
        y = y + _dot(act, wf_ref[c:c + tf, :])
    if final_norm:
        y = _rms(y, gf_ref[...])
    o_ref[...] = y


def _mixffn(x2, g, ya, yb, wg, ua, ub, wo, g2, w_in, w_out, gf, tm, tf, final_norm):
    n, d = x2.shape
    row = lambda i: (i, 0)
    fixed = lambda i: (0, 0)
    once = pl.Buffered(1)
    weights = [wg, ua, ub, wo]
    return pl.pallas_call(
        functools.partial(_mixffn_kernel, tf=tf, final_norm=final_norm),
        grid=(n // tm,),
        in_specs=([pl.BlockSpec((tm, d), row), pl.BlockSpec((1, d), fixed),
                   pl.BlockSpec((tm, ya.shape[1]), row), pl.BlockSpec((tm, yb.shape[1]), row)]
                  + [pl.BlockSpec(w.shape, fixed, pipeline_mode=once) for w in weights]
                  + [pl.BlockSpec((1, d), fixed),
                     pl.BlockSpec(w_in.shape, fixed, pipeline_mode=once),
                     pl.BlockSpec(w_out.shape, fixed, pipeline_mode=once),
                     pl.BlockSpec((1, d), fixed)]),
        out_specs=pl.BlockSpec((tm, d), row),
        out_shape=jax.ShapeDtypeStruct((n, d), F32),
        compiler_params=pltpu.CompilerParams(dimension_semantics=("parallel",),
                                             vmem_limit_bytes=MIXFFN_VMEM_LIMIT),
        name="mixffn",
    )(x2, g, ya, yb, wg, ua, ub, wo, g2, w_in, w_out, gf)


def _ffn_kernel(x_ref, g_ref, wi_ref, wo_ref, gf_ref, o_ref, *, tf, final_norm):
    x = x_ref[...]
    dff = wo_ref.shape[0]
    hn = _rms(x, g_ref[...]).astype(BF16)
    y = x
    for c in range(0, dff, tf):
        gt = _dot(hn, wi_ref[:, c:c + tf])
        up = _dot(hn, wi_ref[:, dff + c:dff + c + tf])
        act = ((gt * jax.nn.sigmoid(gt)) * up).astype(BF16)
        y = y + _dot(act, wo_ref[c:c + tf, :])
    if final_norm:
        y = _rms(y, gf_ref[...])
    o_ref[...] = y


def _ffn(x2, g, w_in, w_out, gf, tm, tf, final_norm):
    n, d = x2.shape
    fixed = lambda i: (0, 0)
    once = pl.Buffered(1)
    return pl.pallas_call(
        functools.partial(_ffn_kernel, tf=tf, final_norm=final_norm),
        grid=(n // tm,),
        in_specs=[pl.BlockSpec((tm, d), lambda i: (i, 0)),
                  pl.BlockSpec((1, d), fixed),
                  pl.BlockSpec(w_in.shape, fixed, pipeline_mode=once),
                  pl.BlockSpec(w_out.shape, fixed, pipeline_mode=once),
                  pl.BlockSpec((1, d), fixed)],
        out_specs=pl.BlockSpec((tm, d), lambda i: (i, 0)),
        out_shape=jax.ShapeDtypeStruct((n, d), F32),
        compiler_params=_params(("parallel",)),
        name="ffn",
    )(x2, g, w_in, w_out, gf)


def _alibi_slopes():
    k = jnp.arange(1, N_ALIBI + 1, dtype=F32)
    s = jnp.exp2(-8.0 * k / N_ALIBI)
    nsa = s[2 * DIL_HPG:2 * DIL_HPG + NSA_HEADS]
    dil = jnp.concatenate([s[:2 * DIL_HPG], s[2 * DIL_HPG + NSA_HEADS:]])
    return nsa, dil


def _overlap(seq, ncp):
    n_c = (seq - CMP_BLOCK) // CMP_STRIDE + 1
    n_s = seq // SEL_BLOCK
    c_start = np.arange(n_c) * CMP_STRIDE
    s_start = np.arange(n_s) * SEL_BLOCK
    ov = np.clip(np.minimum(c_start[:, None] + CMP_BLOCK, s_start[None, :] + SEL_BLOCK)
                 - np.maximum(c_start[:, None], s_start[None, :]), 0, None).astype(np.float32) / CMP_BLOCK
    out = np.zeros((ncp, LANES), np.float32)
    out[:n_c, :n_s] = ov
    return jnp.asarray(out, BF16)


def _proj_weights(w_in):
    dep, d, _ = w_in.shape
    scale = HEAD_DIM ** -0.5 * LOG2E
    c0 = NSA_HEADS * HEAD_DIM
    c1 = c0 + 3 * 2 * NSA_KV_GROUPS * HEAD_DIM
    c2 = c1 + NSA_HEADS * 3
    c3 = c2 + 3 * DIL_HEADS * HEAD_DIM
    qa = (w_in[..., :c0] * scale).reshape(dep, d, NSA_HEADS, HEAD_DIM)
    qa = jnp.pad(qa, ((0, 0), (0, 0), (0, 0), (0, LANES - HEAD_DIM))).reshape(dep, d, QA_W)
    kva = w_in[..., c0:c1].reshape(dep, d, 3, 2, NSA_KV_GROUPS, HEAD_DIM)
    kva = kva.transpose(0, 1, 2, 4, 3, 5).reshape(dep, d, CMP_W + KVA_W)
    ga = jnp.pad(w_in[..., c1:c2], ((0, 0), (0, 0), (0, GA_W - (c2 - c1))))
    qkvb = w_in[..., c2:c3].reshape(dep, d, 3, len(DIL_GROUPS), DIL_HPG * HEAD_DIM)
    qb = (qkvb[:, :, 0] * scale).reshape(dep, d, QB_W)
    kvb = qkvb[:, :, 1:3].transpose(0, 1, 3, 2, 4).reshape(dep, d, KVB_W)
    w_a = jnp.concatenate([qa, kva, qb, kvb, ga], axis=-1).astype(BF16)
    return w_a, w_in[..., c3:].astype(BF16)


def _compress_weights(pe_k, pe_v, w_ck1, w_ck2, w_cv1, w_cv2):
    dep = pe_k.shape[0]
    half = CMP_BLOCK // 2
    ck = w_ck1.reshape(dep, CMP_BLOCK, HEAD_DIM, CMP_HIDDEN)
    cv = w_cv1.reshape(dep, CMP_BLOCK, HEAD_DIM, CMP_HIDDEN)
    z = jnp.zeros_like(ck)
    w = jnp.concatenate([jnp.concatenate([ck, z], axis=-1),
                         jnp.concatenate([z, cv], axis=-1)], axis=2)
    feat = half * 2 * HEAD_DIM
    wa = w[:, :half].reshape(dep, feat, 2 * CMP_HIDDEN).astype(BF16)
    wb = w[:, half:].reshape(dep, feat, 2 * CMP_HIDDEN).astype(BF16)
    pe = jnp.concatenate([pe_k, pe_v], axis=-1)
    pe2 = jnp.stack([pe[:, :half].reshape(dep, feat), pe[:, half:].reshape(dep, feat)], axis=1)
    pe2 = jnp.pad(pe2, ((0, 0), (0, 6), (0, 0)))
    z2 = jnp.zeros_like(w_ck2)
    w2 = jnp.concatenate([jnp.concatenate([w_ck2, z2], axis=-1),
                          jnp.concatenate([z2, w_cv2], axis=-1)], axis=1).astype(BF16)
    return pe2, wa, wb, w2


def kernel(x, norm_mix, w_in, pe_k, pe_v, w_ck1, w_ck2, w_cv1, w_cv2, w_up_nsa, w_up_dil,
           w_out, norm_ffn, w_ffn_in, w_ffn_out, norm_final):
    b, s, d = x.shape
    depth = w_in.shape[0]
    n = b * s
    nq = s // Q_BLOCK
    nch = s // CMP_STRIDE
    n_s = s // SEL_BLOCK
    assert s % DIL_SUPER == 0 and n_s <= LANES
    n_top = min(SEL_TOPK, n_s)
    tm = 512
    tm_ffn = 512
    tf = w_ffn_out.shape[1] // 2

    slopes_nsa, slopes_dil = _alibi_slopes()
    slopes_nsa = slopes_nsa * LOG2E
    ov = _overlap(s, nch)
    gexp = _gate_spread()
    dbias = _dil_bias(slopes_dil)
    w_a, w_gm = _proj_weights(w_in)
    pe2, wa, wb, w2 = _compress_weights(pe_k, pe_v, w_ck1, w_ck2, w_cv1, w_cv2)
    ua = w_up_nsa.astype(BF16)
    ub = w_up_dil.astype(BF16)
    wo = w_out.astype(BF16)
    wfi = w_ffn_in.astype(BF16)
    wfo = w_ffn_out.astype(BF16)
    gf = norm_final.reshape(1, d)

    x2 = x.reshape(n, d)
    for l in range(depth):
        g_mix = norm_mix[l].reshape(1, d)
        qa, xc, kva, qb, kvb, ga = _proj(x2, g_mix, w_a[l], tm)
        qa = qa.reshape(b, s, QA_W)
        kva = kva.reshape(b, s, KVA_W)
        ga = ga.reshape(b, s, GA_W)
        kvc = _compress(xc, pe2[l], wa[l], wb[l], w2[l], b)
        yc, sel, flags = _cmp(slopes_nsa, qa, kvc, ga, ov, gexp, n_top)
        pflags = flags.reshape(-1)
        ya = _selwin(pflags, slopes_nsa, qa, kva, sel, ga, yc, gexp)
        yb = _dil(qb.reshape(-1, b, s, LANES), kvb.reshape(-1, b, s, LANES), dbias, DIL_SUPER)
        x2 = _mixffn(x2, g_mix, ya.reshape(n, NSA_HEADS * HEAD_DIM), yb.reshape(n, DIL_HPG * HEAD_DIM),
                     w_gm[l], ua[l], ub[l], wo[l], norm_ffn[l].reshape(1, d), wfi[l], wfo[l], gf,
                     tm_ffn, tf, final_norm=(l == depth - 1))
    return x2.reshape(b, s, d)
```

```python
import functools

import numpy as np
import jax
import jax.numpy as jnp
from jax import lax
from jax.experimental import pallas as pl
from jax.experimental.pallas import tpu as pltpu

F32 = jnp.float32
BF16 = jnp.bfloat16

HEAD_DIM = 64
NSA_HEADS = 8
NSA_KV_GROUPS = 2
NSA_HPG = NSA_HEADS // NSA_KV_GROUPS
CMP_BLOCK = 32
CMP_STRIDE = 16
CMP_HIDDEN = 256
SEL_BLOCK = 64
SEL_TOPK = 16
NSA_WINDOW = 512
Q_BLOCK = 128
FORCE_SCORE = 1e6
DIL_GROUPS = ((128, 1), (512, 4), (2048, 16))
DIL_HPG = 4
DIL_HEADS = DIL_HPG * len(DIL_GROUPS)
DIL_BLOCK = 128
N_ALIBI = NSA_HEADS + DIL_HEADS
EPS = 1e-6
NEG_INF = -1e30
TINY = 1e-30
LOG2E = 1.4426950408889634
LN2 = 0.6931471805599453

LANES = 128
GROUP_W = NSA_HPG * HEAD_DIM
DIL_SUPER = 2048
DIL_UNROLL = 4
CMP_TILES = 8
VMEM_LIMIT = 48 * 1024 * 1024
MIXFFN_VMEM_LIMIT = 56 * 1024 * 1024


def _dot(a, b):
    return jnp.dot(a, b, preferred_element_type=F32)


def _nt_dot(a, b):
    return lax.dot_general(a, b, (((1,), (1,)), ((), ())), preferred_element_type=F32)


def _iota(shape, dim):
    return lax.broadcasted_iota(jnp.int32, shape, dim)


def _rms(x, g):
    ms = jnp.mean(x * x, axis=-1, keepdims=True)
    return (x * lax.rsqrt(ms + EPS)) * g


def _params(sem):
    return pltpu.CompilerParams(dimension_semantics=sem, vmem_limit_bytes=VMEM_LIMIT)


QA_W = NSA_HEADS * LANES
QP_W = NSA_HPG * LANES
CMP_W = NSA_KV_GROUPS * LANES
KVA_W, QB_W, KVB_W, GA_W = 512, 768, 1536, 128
PROJ_COLS = NSA_HEADS * HEAD_DIM + CMP_W + KVA_W + QB_W + KVB_W + GA_W


def _proj_kernel(x_ref, g_ref, w_ref, qa_ref, xc_ref, kva_ref, qb_ref, kvb_ref, ga_ref, cmp_scr):
    y = _rms(x_ref[...], g_ref[...]).astype(BF16)
    c = 0
    low = _iota((x_ref.shape[0], LANES), 1) < HEAD_DIM
    for p in range(0, NSA_HEADS // 2, 2):
        r2 = _dot(y, w_ref[:, c:c + 2 * LANES])
        c += 2 * LANES
        for k in range(2):
            r = r2[:, k * LANES:(k + 1) * LANES]
            h0 = 2 * (p + k)
            qa_ref[:, h0 * LANES:(h0 + 1) * LANES] = jnp.where(low, r, 0.0).astype(BF16)
            qa_ref[:, (h0 + 1) * LANES:(h0 + 2) * LANES] = (
                jnp.where(low, pltpu.roll(r, HEAD_DIM, 1), 0.0).astype(BF16))
    rc = _dot(y, w_ref[:, c:c + CMP_W])
    c += CMP_W
    nrow = xc_ref.shape[1]
    for gi in range(NSA_KV_GROUPS):
        cmp_scr[gi] = rc[:, gi * LANES:(gi + 1) * LANES]
        for l in range(CMP_STRIDE):
            xc_ref[gi, :, l * LANES:(l + 1) * LANES] = (
                cmp_scr[gi, pl.ds(l, nrow, stride=CMP_STRIDE), :].astype(BF16))
    kva_ref[...] = _dot(y, w_ref[:, c:c + KVA_W]).astype(BF16)
    c += KVA_W
    for j in range(0, QB_W // LANES, 2):
        r = _dot(y, w_ref[:, c:c + 2 * LANES])
        qb_ref[j] = r[:, :LANES]
        qb_ref[j + 1] = r[:, LANES:]
        c += 2 * LANES
    for j in range(0, KVB_W // LANES, 2):
        r = _dot(y, w_ref[:, c:c + 2 * LANES])
        kvb_ref[j] = r[:, :LANES]
        kvb_ref[j + 1] = r[:, LANES:]
        c += 2 * LANES
    ga_ref[...] = jax.nn.sigmoid(_dot(y, w_ref[:, c:c + GA_W]))


def _proj(x2, g, w, tm):
    n, d = x2.shape
    row = lambda i: (i, 0)
    fixed = lambda i: (0, 0)
    return pl.pallas_call(
        _proj_kernel,
        grid=(n // tm,),
        in_specs=[pl.BlockSpec((tm, d), row), pl.BlockSpec((1, d), fixed),
                  pl.BlockSpec((d, PROJ_COLS), fixed, pipeline_mode=pl.Buffered(1))],
        out_specs=[pl.BlockSpec((tm, QA_W), row),
                   pl.BlockSpec((NSA_KV_GROUPS, tm // CMP_STRIDE, CMP_STRIDE * LANES), lambda i: (0, i, 0)),
                   pl.BlockSpec((tm, KVA_W), row),
                   pl.BlockSpec((QB_W // LANES, tm, LANES), lambda i: (0, i, 0)),
                   pl.BlockSpec((KVB_W // LANES, tm, LANES), lambda i: (0, i, 0)),
                   pl.BlockSpec((tm, GA_W), row)],
        out_shape=[jax.ShapeDtypeStruct((n, QA_W), BF16),
                   jax.ShapeDtypeStruct((NSA_KV_GROUPS, n // CMP_STRIDE, CMP_STRIDE * LANES), BF16),
                   jax.ShapeDtypeStruct((n, KVA_W), BF16),
                   jax.ShapeDtypeStruct((QB_W // LANES, n, LANES), F32),
                   jax.ShapeDtypeStruct((KVB_W // LANES, n, LANES), F32),
                   jax.ShapeDtypeStruct((n, GA_W), F32)],
        scratch_shapes=[pltpu.VMEM((NSA_KV_GROUPS, tm, LANES), F32)],
        compiler_params=_params(("parallel",)),
        name="proj",
    )(x2, g, w)


def _compress_kernel(x_ref, pe_ref, wa_ref, wb_ref, w2_ref, o_ref):
    xf = x_ref[0].astype(F32)
    nch = xf.shape[0]
    xa = (xf + pe_ref[0:1, :]).astype(BF16)
    xb = (xf + pe_ref[1:2, :]).astype(BF16)
    ya = _dot(xa, wa_ref[...])
    yb = _dot(xb, wb_ref[...])
    h = ya + pltpu.roll(yb, nch - 1, 0)
    a = (h * jax.nn.sigmoid(h)).astype(BF16)
    o_ref[0, 0] = _dot(a, w2_ref[...]).astype(BF16)


def _compress(xc, pe, wa, wb, w2, b):
    g, rows, f = xc.shape
    nch = rows // b
    hid = wa.shape[1]
    return pl.pallas_call(
        _compress_kernel,
        grid=(b, g),
        in_specs=[pl.BlockSpec((1, nch, f), lambda i, j: (j, i, 0)),
                  pl.BlockSpec((8, f), lambda i, j: (0, 0)),
                  pl.BlockSpec((f, hid), lambda i, j: (0, 0)),
                  pl.BlockSpec((f, hid), lambda i, j: (0, 0)),
                  pl.BlockSpec((hid, LANES), lambda i, j: (0, 0))],
        out_specs=pl.BlockSpec((1, 1, nch, LANES), lambda i, j: (i, j, 0, 0)),
        out_shape=jax.ShapeDtypeStruct((b, g, nch, LANES), BF16),
        compiler_params=_params(("parallel", "parallel")),
        name="compress",
    )(xc, pe, wa, wb, w2)


def _gate_spread():
    g = np.arange(NSA_KV_GROUPS)[:, None, None, None]
    br = np.arange(3)[None, :, None, None]
    r = np.arange(LANES)[None, None, :, None]
    c = np.arange(GROUP_W)[None, None, None, :]
    return jnp.asarray(r == 3 * (NSA_HPG * g + c // HEAD_DIM) + br, BF16)


def _stack_heads(q):
    return jnp.concatenate([q[:, LANES * h:LANES * (h + 1)] for h in range(NSA_HPG)], axis=0)


def _unstack_heads(o4):
    lane = _iota((Q_BLOCK, LANES), 1)
    o = [o4[Q_BLOCK * h:Q_BLOCK * (h + 1)] for h in range(NSA_HPG)]
    c0 = jnp.where(lane < HEAD_DIM, pltpu.roll(o[0], HEAD_DIM, 1), o[1])
    c1 = jnp.where(lane < HEAD_DIM, pltpu.roll(o[2], HEAD_DIM, 1), o[3])
    return jnp.concatenate([c0, c1], axis=1)


def _row_sums_all_lanes(a):
    lane = _iota(a.shape, 1)
    return jnp.where(lane < HEAD_DIM, a, pltpu.roll(a, HEAD_DIM, 1))


def _normalise_unstack(a):
    return _unstack_heads(a * (1.0 / jnp.maximum(_row_sums_all_lanes(a), TINY)))


def _expand_gate(gates, spread):
    hi = gates.astype(BF16)
    lo = (gates - hi.astype(F32)).astype(BF16)
    return _dot(hi, spread) + _dot(lo, spread)


def _ones_in_k_lanes(kv):
    lane = _iota(kv.shape, 1)
    return jnp.where(lane < HEAD_DIM, jnp.ones_like(kv), kv)


def _cmp_kernel(slopes_ref, q_ref, kvc_ref, ga_ref, ov_ref, gexp_ref,
                yc_ref, sel_ref, fl_ref, bias0, cq, rhs, *, n_top):
    g = pl.program_id(1)
    i = pl.program_id(2)
    hq = NSA_HPG
    ncp = cq.shape[1]

    @pl.when(i == 0)
    def _():
        kpos = CMP_STRIDE * _iota((Q_BLOCK, ncp), 1) + (CMP_BLOCK - 1)
        cq[...] = kpos - _iota((Q_BLOCK, ncp), 0)
        kf = kpos[0:8].astype(F32)
        for h in range(hq):
            bias0[h] = slopes_ref[hq * g + h] * kf
        rhs[:, 0:LANES] = _ones_in_k_lanes(kvc_ref[0, 0])
        rhs[:, LANES:2 * LANES] = ov_ref[...]

    for k in range(CMP_TILES):
        it = CMP_TILES * i + k
        rows = slice(k * Q_BLOCK, (k + 1) * Q_BLOCK)
        q4 = _stack_heads(q_ref[0, rows, :])
        s = _nt_dot(q4, kvc_ref[0, 0]).reshape(hq, Q_BLOCK, ncp)
        t0 = it * Q_BLOCK
        neg = jnp.where(cq[...] <= t0, 0.0, NEG_INF)
        tf = t0.astype(F32)
        ps = []
        for h in range(hq):
            sb = s[h] + (neg + (bias0[h, 0:1] - slopes_ref[hq * g + h] * tf))
            m = jnp.max(sb, axis=-1, keepdims=True)
            m = jnp.where(m < 0.1 * NEG_INF, -NEG_INF, m)
            ps.append(jnp.exp2(sb - m).astype(BF16))
        big = _dot(jnp.concatenate(ps, axis=0), rhs[...])
        inv = 1.0 / jnp.maximum(_row_sums_all_lanes(big[:, 0:LANES]), TINY)
        oc = _unstack_heads(big[:, 0:LANES] * inv)
        yc_ref[0, rows, :] = _expand_gate(ga_ref[0, rows, :], gexp_ref[0, 0]) * oc
        pov = big[:, LANES:2 * LANES] * inv
        imp = pov[0:Q_BLOCK]
        for h in range(1, hq):
            imp = imp + pov[Q_BLOCK * h:Q_BLOCK * (h + 1)]

        qrow = _iota((Q_BLOCK, LANES), 0)
        j = _iota((Q_BLOCK, LANES), 1)
        jt = 2 * it + (qrow >> 6)
        valid = j <= jt
        forced = (j == 0) | (j == jt) | (j == jt - 1)
        score = jnp.where(forced, FORCE_SCORE, jnp.where(valid, imp, -1.0))
        st = score.T
        jr = _iota((LANES, Q_BLOCK), 0)
        cur = st
        taken = -3e38
        for _ in range(n_top):
            mx = jnp.max(cur, axis=0, keepdims=True)
            first = jnp.min(jnp.where(cur == mx, jr, LANES), axis=0, keepdims=True)
            cur = jnp.where(jr == first, taken, cur)
        keep = jnp.where((cur == taken) & (st >= 0.0), 1.0, 0.0).T
        sel_ref[0, 0, rows, :] = keep.astype(BF16)

        fl_ref[k] = (jnp.max(keep, axis=0, keepdims=True) > 0.0).astype(jnp.int32)


def _cmp(slopes, qa, kvc, ga, ov, gexp, n_top):
    b, s, _ = qa.shape
    g = NSA_KV_GROUPS
    nq = s // Q_BLOCK
    ncp = kvc.shape[2]
    tq = CMP_TILES * Q_BLOCK
    nt = s // tq
    return pl.pallas_call(
        functools.partial(_cmp_kernel, n_top=n_top),
        grid=(b, g, nt),
        in_specs=[pl.BlockSpec(memory_space=pltpu.SMEM),
                  pl.BlockSpec((1, tq, QP_W), lambda bi, gi, i: (bi, i, gi)),
                  pl.BlockSpec((1, 1, ncp, LANES), lambda bi, gi, i: (bi, gi, 0, 0)),
                  pl.BlockSpec((1, tq, LANES), lambda bi, gi, i: (bi, i, 0)),
                  pl.BlockSpec((ncp, LANES), lambda bi, gi, i: (0, 0)),
                  pl.BlockSpec((1, 3, LANES, GROUP_W), lambda bi, gi, i: (gi, 0, 0, 0))],
        out_specs=[pl.BlockSpec((1, tq, GROUP_W), lambda bi, gi, i: (bi, i, gi)),
                   pl.BlockSpec((1, 1, tq, LANES), lambda bi, gi, i: (bi, gi, i, 0)),
                   pl.BlockSpec((CMP_TILES, 1, LANES), lambda bi, gi, i: ((bi * g + gi) * nt + i, 0, 0))],
        out_shape=[jax.ShapeDtypeStruct((b, s, g * GROUP_W), F32),
                   jax.ShapeDtypeStruct((b, g, s, LANES), BF16),
                   jax.ShapeDtypeStruct((b * g * nq, 1, LANES), jnp.int32)],
        scratch_shapes=[pltpu.VMEM((NSA_HPG, 8, ncp), F32),
                        pltpu.VMEM((Q_BLOCK, ncp), jnp.int32),
                        pltpu.VMEM((ncp, 2 * LANES), BF16)],
        compiler_params=_params(("parallel", "parallel", "arbitrary")),
        name="cmp",
    )(slopes, qa, kvc, ga, ov, gexp)


WIN_UNITS = NSA_WINDOW // Q_BLOCK + 1
LIST_UNROLL = 4
SLOT_CHUNK = 8


def _selwin_kernel(pf_ref, slopes_ref, q_ref, kvs_ref, kvw_ref, sel_ref, ga_ref, yc_ref,
                   gexp_ref, ya_ref, sbuf, slot_unit, mrun, acc, accw, *, nq):
    b = pl.program_id(0)
    g = pl.program_id(1)
    i = pl.program_id(2)
    hq = NSA_HPG
    qmk = _iota((Q_BLOCK, LANES), 0) - _iota((Q_BLOCK, LANES), 1)
    koff = _iota((1, LANES), 1).astype(F32)
    q4 = _stack_heads(q_ref[0])
    sel = sel_ref[0, 0]
    dmat = _iota((LANES, LANES), 0) - (_iota((LANES, LANES), 1) >> 6)
    dead = -(1 << 20)

    for d in range(WIN_UNITS):
        u = i - (WIN_UNITS - 1) + d
        slot_unit[d] = jnp.where(u >= 0, u, -1)

    base = ((b * NSA_KV_GROUPS + g) * nq + i) * LANES

    def list_body(j4, k):
        for t in range(LIST_UNROLL):
            jj = j4 * LIST_UNROLL + t
            used = (pf_ref[base + 2 * jj] | pf_ref[base + 2 * jj + 1]) != 0

            @pl.when(used)
            def _(jj=jj, k=k):
                slot_unit[WIN_UNITS + k] = jj
            k = k + used.astype(jnp.int32)
        return k

    nsel = lax.fori_loop(0, i // LIST_UNROLL + 1, list_body, jnp.int32(0))
    for t in range(SLOT_CHUNK - 1):
        slot_unit[WIN_UNITS + nsel + t] = -1
    nchunk = (nsel + (SLOT_CHUNK - 1)) // SLOT_CHUNK

    def kv_tile(kv_ref, slot):
        u = jnp.maximum(slot_unit[slot], 0)
        return kv_ref[0, pl.ds(pl.multiple_of(u * Q_BLOCK, Q_BLOCK), Q_BLOCK), :]

    def score_slots(kv_ref, slots, window):
        units = [slot_unit[slot] for slot in slots]
        s_all = _nt_dot(q4, jnp.concatenate([kv_tile(kv_ref, slot) for slot in slots], axis=0))
        if not window:
            spread = jnp.concatenate(
                [jnp.where(dmat == 2 * u, 1.0, 0.0).astype(BF16) for u in units], axis=1)
            chosen = _dot(sel, spread)
        for n, (slot, u) in enumerate(zip(slots, units)):
            s = s_all[:, n * LANES:(n + 1) * LANES].reshape(hq, Q_BLOCK, LANES)
            off = jnp.where(u >= 0, (i - u) * Q_BLOCK, dead)
            dist = qmk + off
            if window:
                allow = (dist >= 0) & (dist < NSA_WINDOW)
            else:
                allow = (dist >= 0) & (chosen[:, n * LANES:(n + 1) * LANES] > 0.5)
            neg = jnp.where(allow, 0.0, NEG_INF)
            kd = koff - off.astype(F32)
            for h in range(hq):
                sb = s[h] + (neg + slopes_ref[hq * g + h] * kd)
                sbuf[slot, h] = sb
                mrun[h] = jnp.maximum(mrun[h], sb)

    def spread_max():
        m = jnp.max(mrun[...], axis=-1, keepdims=True)
        mrun[...] = jnp.broadcast_to(m, mrun.shape)

    def weighted_values(kv_ref, slots):
        tot = None
        for n in range(0, len(slots), 2):
            pair = slots[n:n + 2]
            p = jnp.concatenate(
                [jnp.exp2(sbuf[slot] - mrun[...]).astype(BF16).reshape(hq * Q_BLOCK, LANES)
                 for slot in pair], axis=1)
            v = jnp.concatenate([_ones_in_k_lanes(kv_tile(kv_ref, slot)) for slot in pair], axis=0)
            d = _dot(p, v)
            tot = d if tot is None else tot + d
        return tot

    mrun[...] = jnp.full_like(mrun, NEG_INF)
    for d in range(0, WIN_UNITS, 2):
        score_slots(kvw_ref, list(range(d, min(d + 2, WIN_UNITS))), True)
    spread_max()
    accw[...] = weighted_values(kvw_ref, list(range(WIN_UNITS)))

    mrun[...] = jnp.full_like(mrun, NEG_INF)

    def score_chunk(ci, carry):
        for t in range(0, SLOT_CHUNK, 2):
            first = WIN_UNITS + ci * SLOT_CHUNK + t
            score_slots(kvs_ref, [first, first + 1], False)
        return carry

    lax.fori_loop(0, nchunk, score_chunk, 0)
    spread_max()
    acc[...] = jnp.zeros_like(acc)

    def value_chunk(ci, carry):
        acc[...] += weighted_values(kvs_ref, [WIN_UNITS + ci * SLOT_CHUNK + t for t in range(SLOT_CHUNK)])
        return carry

    lax.fori_loop(0, nchunk, value_chunk, 0)
    o_s = _normalise_unstack(acc[...])
    o_w = _normalise_unstack(accw[...])

    gates = ga_ref[0]
    ya = (yc_ref[0] + _expand_gate(gates, gexp_ref[0, 1]) * o_s
          + _expand_gate(gates, gexp_ref[0, 2]) * o_w)
    ya_ref[0] = ya.astype(BF16)


def _selwin(pflags, slopes, qa, kva, sel, ga, yc, gexp):
    b, s, _ = qa.shape
    g = NSA_KV_GROUPS
    nq = s // Q_BLOCK
    nbuf = WIN_UNITS + nq + SLOT_CHUNK - 1
    grid_spec = pltpu.PrefetchScalarGridSpec(
        num_scalar_prefetch=1,
        grid=(b, g, nq),
        in_specs=[pl.BlockSpec(memory_space=pltpu.SMEM),
                  pl.BlockSpec((1, Q_BLOCK, QP_W), lambda bi, gi, i, pf: (bi, i, gi)),
                  pl.BlockSpec((1, s, LANES), lambda bi, gi, i, pf: (bi, 0, gi)),
                  pl.BlockSpec((1, s, LANES), lambda bi, gi, i, pf: (bi, 0, NSA_KV_GROUPS + gi)),
                  pl.BlockSpec((1, 1, Q_BLOCK, LANES), lambda bi, gi, i, pf: (bi, gi, i, 0)),
                  pl.BlockSpec((1, Q_BLOCK, LANES), lambda bi, gi, i, pf: (bi, i, 0)),
                  pl.BlockSpec((1, Q_BLOCK, GROUP_W), lambda bi, gi, i, pf: (bi, i, gi)),
                  pl.BlockSpec((1, 3, LANES, GROUP_W), lambda bi, gi, i, pf: (gi, 0, 0, 0))],
        out_specs=pl.BlockSpec((1, Q_BLOCK, GROUP_W), lambda bi, gi, i, pf: (bi, i, gi)),
        scratch_shapes=[pltpu.VMEM((nbuf, NSA_HPG, Q_BLOCK, LANES), F32),
                        pltpu.SMEM((nbuf,), jnp.int32),
                        pltpu.VMEM((NSA_HPG, Q_BLOCK, LANES), F32),
                        pltpu.VMEM((NSA_HPG * Q_BLOCK, LANES), F32),
                        pltpu.VMEM((NSA_HPG * Q_BLOCK, LANES), F32)],
    )
    return pl.pallas_call(
        functools.partial(_selwin_kernel, nq=nq),
        grid_spec=grid_spec,
        out_shape=jax.ShapeDtypeStruct((b, s, g * GROUP_W), BF16),
        compiler_params=_params(("parallel", "parallel", "arbitrary")),
        name="selwin",
    )(pflags, slopes, qa, kva, kva, sel, ga, yc, gexp)


def _dil_bias(slopes):
    c = DIL_BLOCK
    qi = np.arange(c)[:, None]
    ki = np.arange(2 * c)[None, :]
    delta = qi + c - ki
    out = []
    for k, (window, dil) in enumerate(DIL_GROUPS):
        in_band = jnp.asarray((delta >= 0) & (delta <= window // dil))
        dist = jnp.asarray((delta * dil).astype(np.float32))
        for h in range(DIL_HPG):
            out.append(jnp.where(in_band, -(slopes[DIL_HPG * k + h] * dist) * LOG2E, NEG_INF))
    return jnp.stack(out).reshape(len(out) // 2, 2 * c, 2 * c)


def _dil_group(k, dil, sup, q_ref, kvc_ref, kvp_ref, bias_ref, mx_ref, w_ref, y_ref):
    c = DIL_BLOCK
    nblk = sup // (c * dil)
    has_halo = pl.program_id(1) > 0
    nch = DIL_HPG * HEAD_DIM // LANES
    upper = _iota((c, LANES), 1) >= HEAD_DIM

    def block(idx):
        r = idx // nblk
        n = idx - r * nblk
        start = r + dil * c * n
        rows = pl.ds(start, c, stride=dil)
        rows_prev = pl.ds(jnp.maximum(start - dil * c, r), c, stride=dil)
        rows_halo = pl.ds(r + dil * c * (nblk - 1), c, stride=dil)
        pen = jnp.where(jnp.logical_or(n >= 1, has_halo), 0.0, NEG_INF)

        def prev(ch):
            return jnp.where(n >= 1, kvc_ref[ch, 0, rows_prev, :], kvp_ref[ch, 0, rows_halo, :]).astype(BF16)

        for ch in range(nch):
            qb = q_ref[ch, 0, rows, :].astype(BF16)
            zero = jnp.zeros_like(qb)
            qs = jnp.concatenate([jnp.where(upper, zero, qb), jnp.where(upper, qb, zero)], axis=0)
            kband = jnp.concatenate([prev(ch), kvc_ref[ch, 0, rows, :].astype(BF16)], axis=0)
            vband = jnp.concatenate([prev(nch + ch), kvc_ref[nch + ch, 0, rows, :].astype(BF16)], axis=0)
            s = _nt_dot(qs, kband) + bias_ref[ch]
            s = jnp.concatenate([s[:, :c] + pen, s[:, c:]], axis=1)
            m = jnp.max(s, axis=-1, keepdims=True)
            e = jnp.exp2(s - m).astype(BF16)
            ones = jnp.ones((2 * c, LANES), BF16)
            ol = _dot(e, jnp.concatenate([vband, ones], axis=1))
            lt = jnp.maximum(ol[:, LANES:], TINY)
            o2 = ol[:, :LANES] * (1.0 / lt)
            l2 = m * LN2 + jnp.log(lt)
            o = jnp.where(upper, o2[c:], o2[:c])
            lse = jnp.where(upper, l2[c:], l2[:c])
            if k == 0:
                mx_ref[ch, rows, :] = lse
                w_ref[ch, rows, :] = jnp.ones_like(lse)
                y_ref[ch, rows, :] = o
            else:
                mx_old = mx_ref[ch, rows, :]
                mx_new = jnp.maximum(mx_old, lse)
                a = jnp.exp(mx_old - mx_new)
                bnew = jnp.exp(lse - mx_new)
                mx_ref[ch, rows, :] = mx_new
                w_ref[ch, rows, :] = w_ref[ch, rows, :] * a + bnew
                y_ref[ch, rows, :] = y_ref[ch, rows, :] * a + bnew * o

    def body(t, carry):
        for j in range(DIL_UNROLL):
            block(t * DIL_UNROLL + j)
        return carry

    lax.fori_loop(0, sup // (c * DIL_UNROLL), body, 0)


def _dil_kernel(q_ref, kvc_ref, kvp_ref, bias_ref, yb_ref, mx_ref, w_ref, y_ref, *, sup):
    gi = pl.program_id(2)
    for k, (_, dil) in enumerate(DIL_GROUPS):
        @pl.when(gi == k)
        def _(k=k, dil=dil):
            _dil_group(k, dil, sup, q_ref, kvc_ref, kvp_ref, bias_ref, mx_ref, w_ref, y_ref)

    @pl.when(gi == len(DIL_GROUPS) - 1)
    def _():
        for ch in range(DIL_HPG * HEAD_DIM // LANES):
            yb_ref[0, :, ch * LANES:(ch + 1) * LANES] = (y_ref[ch] * (1.0 / w_ref[ch])).astype(BF16)


def _dil(qb, kvb, bias, sup):
    _, b, s, _ = qb.shape
    hw = DIL_HPG * HEAD_DIM
    nch = hw // LANES
    ng = len(DIL_GROUPS)
    return pl.pallas_call(
        functools.partial(_dil_kernel, sup=sup),
        grid=(b, s // sup, ng),
        in_specs=[pl.BlockSpec((nch, 1, sup, LANES), lambda bi, si, gi: (gi, bi, si, 0)),
                  pl.BlockSpec((2 * nch, 1, sup, LANES), lambda bi, si, gi: (gi, bi, si, 0)),
                  pl.BlockSpec((2 * nch, 1, sup, LANES),
                               lambda bi, si, gi: (gi, bi, jnp.maximum(si - 1, 0), 0)),
                  pl.BlockSpec((nch, 2 * DIL_BLOCK, 2 * DIL_BLOCK), lambda bi, si, gi: (gi, 0, 0))],
        out_specs=pl.BlockSpec((1, sup, hw), lambda bi, si, gi: (bi, si, 0)),
        out_shape=jax.ShapeDtypeStruct((b, s, hw), BF16),
        scratch_shapes=[pltpu.VMEM((nch, sup, LANES), F32), pltpu.VMEM((nch, sup, LANES), F32),
                        pltpu.VMEM((nch, sup, LANES), F32)],
        compiler_params=_params(("parallel", "parallel", "arbitrary")),
        name="dil",
    )(qb, kvb, kvb, bias)


def _merge_kernel(x_ref, g_ref, ya_ref, yb_ref, wg_ref, ua_ref, ub_ref, wo_ref, o_ref):
    x = x_ref[...]
    d = x.shape[1]
    hn = _rms(x, g_ref[...]).astype(BF16)
    gate_a = jax.nn.sigmoid(_dot(hn, wg_ref[:, 0:d]))
    gate_b = jax.nn.sigmoid(_dot(hn, wg_ref[:, d:2 * d]))
    merged = gate_a * _dot(ya_ref[...], ua_ref[...]) + gate_b * _dot(yb_ref[...], ub_ref[...])
    o_ref[...] = x + _dot(merged.astype(BF16), wo_ref[...])


def _merge(x2, g, ya, yb, wg, ua, ub, wo, tm):
    n, d = x2.shape
    row = lambda i: (i, 0)
    fixed = lambda i: (0, 0)
    return pl.pallas_call(
        _merge_kernel,
        grid=(n // tm,),
        in_specs=[pl.BlockSpec((tm, d), row), pl.BlockSpec((1, d), fixed),
                  pl.BlockSpec((tm, ya.shape[1]), row), pl.BlockSpec((tm, yb.shape[1]), row),
                  pl.BlockSpec(wg.shape, fixed), pl.BlockSpec(ua.shape, fixed),
                  pl.BlockSpec(ub.shape, fixed), pl.BlockSpec(wo.shape, fixed)],
        out_specs=pl.BlockSpec((tm, d), row),
        out_shape=jax.ShapeDtypeStruct((n, d), F32),
        compiler_params=_params(("parallel",)),
        name="merge",
    )(x2, g, ya, yb, wg, ua, ub, wo)


def _mixffn_kernel(x_ref, g_ref, ya_ref, yb_ref, wg_ref, ua_ref, ub_ref, wo_ref,
                   g2_ref, wi_ref, wf_ref, gf_ref, o_ref, *, tf, final_norm):
    x = x_ref[...]
    d = x.shape[1]
    hn = _rms(x, g_ref[...]).astype(BF16)
    gate_a = jax.nn.sigmoid(_dot(hn, wg_ref[:, 0:d]))
    gate_b = jax.nn.sigmoid(_dot(hn, wg_ref[:, d:2 * d]))
    merged = gate_a * _dot(ya_ref[...], ua_ref[...]) + gate_b * _dot(yb_ref[...], ub_ref[...])
    x = x + _dot(merged.astype(BF16), wo_ref[...])
    dff = wf_ref.shape[0]
    hn = _rms(x, g2_ref[...]).astype(BF16)
    y = x
    for c in range(0, dff, tf):
        gt = _dot(hn, wi_ref[:, c:c + tf])
        up = _dot(hn, wi_ref[:, dff + c:dff + c + tf])
        act = ((gt * jax.nn.sigmoid(gt)) * up).astype(BF16)
        y = y + _dot(act, wf_ref[c:c + tf, :])
    if final_norm:
        y = _rms(y, gf_ref[...])
    o_ref[...] = y


def _mixffn(x2, g, ya, yb, wg, ua, ub, wo, g2, w_in, w_out, gf, tm, tf, final_norm):
    n, d = x2.shape
    row = lambda i: (i, 0)
    fixed = lambda i: (0, 0)
    once = pl.Buffered(1)
    weights = [wg, ua, ub, wo]
    return pl.pallas_call(
        functools.partial(_mixffn_kernel, tf=tf, final_norm=final_norm),
        grid=(n // tm,),
        in_specs=([pl.BlockSpec((tm, d), row), pl.BlockSpec((1, d), fixed),
                   pl.BlockSpec((tm, ya.shape[1]), row), pl.BlockSpec((tm, yb.shape[1]), row)]
                  + [pl.BlockSpec(w.shape, fixed, pipeline_mode=once) for w in weights]
                  + [pl.BlockSpec((1, d), fixed),
                     pl.BlockSpec(w_in.shape, fixed, pipeline_mode=once),
                     pl.BlockSpec(w_out.shape, fixed, pipeline_mode=once),
                     pl.BlockSpec((1, d), fixed)]),
        out_specs=pl.BlockSpec((tm, d), row),
        out_shape=jax.ShapeDtypeStruct((n, d), F32),
        compiler_params=pltpu.CompilerParams(dimension_semantics=("parallel",),
                                             vmem_limit_bytes=MIXFFN_VMEM_LIMIT),
        name="mixffn",
    )(x2, g, ya, yb, wg, ua, ub, wo, g2, w_in, w_out, gf)


def _ffn_kernel(x_ref, g_ref, wi_ref, wo_ref, gf_ref, o_ref, *, tf, final_norm):
    x = x_ref[...]
    dff = wo_ref.shape[0]
    hn = _rms(x, g_ref[...]).astype(BF16)
    y = x
    for c in range(0, dff, tf):
        gt = _dot(hn, wi_ref[:, c:c + tf])
        up = _dot(hn, wi_ref[:, dff + c:dff + c + tf])
        act = ((gt * jax.nn.sigmoid(gt)) * up).astype(BF16)
        y = y + _dot(act, wo_ref[c:c + tf, :])
    if final_norm:
        y = _rms(y, gf_ref[...])
    o_ref[...] = y


def _ffn(x2, g, w_in, w_out, gf, tm, tf, final_norm):
    n, d = x2.shape
    fixed = lambda i: (0, 0)
    once = pl.Buffered(1)
    return pl.pallas_call(
        functools.partial(_ffn_kernel, tf=tf, final_norm=final_norm),
        grid=(n // tm,),
        in_specs=[pl.BlockSpec((tm, d), lambda i: (i, 0)),
                  pl.BlockSpec((1, d), fixed),
                  pl.BlockSpec(w_in.shape, fixed, pipeline_mode=once),
                  pl.BlockSpec(w_out.shape, fixed, pipeline_mode=once),
                  pl.BlockSpec((1, d), fixed)],
        out_specs=pl.BlockSpec((tm, d), lambda i: (i, 0)),
        out_shape=jax.ShapeDtypeStruct((n, d), F32),
        compiler_params=_params(("parallel",)),
        name="ffn",
    )(x2, g, w_in, w_out, gf)


def _alibi_slopes():
    k = jnp.arange(1, N_ALIBI + 1, dtype=F32)
    s = jnp.exp2(-8.0 * k / N_ALIBI)
    nsa = s[2 * DIL_HPG:2 * DIL_HPG + NSA_HEADS]
    dil = jnp.concatenate([s[:2 * DIL_HPG], s[2 * DIL_HPG + NSA_HEADS:]])
    return nsa, dil


def _overlap(seq, ncp):
    n_c = (seq - CMP_BLOCK) // CMP_STRIDE + 1
    n_s = seq // SEL_BLOCK
    c_start = np.arange(n_c) * CMP_STRIDE
    s_start = np.arange(n_s) * SEL_BLOCK
    ov = np.clip(np.minimum(c_start[:, None] + CMP_BLOCK, s_start[None, :] + SEL_BLOCK)
                 - np.maximum(c_start[:, None], s_start[None, :]), 0, None).astype(np.float32) / CMP_BLOCK
    out = np.zeros((ncp, LANES), np.float32)
    out[:n_c, :n_s] = ov
    return jnp.asarray(out, BF16)


def _proj_weights(w_in):
    dep, d, _ = w_in.shape
    scale = HEAD_DIM ** -0.5 * LOG2E
    c0 = NSA_HEADS * HEAD_DIM
    c1 = c0 + 3 * 2 * NSA_KV_GROUPS * HEAD_DIM
    c2 = c1 + NSA_HEADS * 3
    c3 = c2 + 3 * DIL_HEADS * HEAD_DIM
    qa = (w_in[..., :c0] * scale).astype(BF16)
    wb = w_in.astype(BF16)
    kva = wb[..., c0:c1].reshape(dep, d, 3, 2, NSA_KV_GROUPS, HEAD_DIM)
    kva = kva.transpose(0, 1, 2, 4, 3, 5).reshape(dep, d, CMP_W + KVA_W)
    ga = jnp.pad(wb[..., c1:c2], ((0, 0), (0, 0), (0, GA_W - (c2 - c1))))
    qkvb = w_in[..., c2:c3].reshape(dep, d, 3, len(DIL_GROUPS), DIL_HPG * HEAD_DIM)
    qb = (qkvb[:, :, 0] * scale).astype(BF16).reshape(dep, d, QB_W)
    kvb = qkvb[:, :, 1:3].astype(BF16).transpose(0, 1, 3, 2, 4).reshape(dep, d, KVB_W)
    w_a = jnp.concatenate([qa, kva, qb, kvb, ga], axis=-1)
    return w_a, wb[..., c3:]


def _compress_weights(pe_k, pe_v, w_ck1, w_ck2, w_cv1, w_cv2):
    dep = pe_k.shape[0]
    half = CMP_BLOCK // 2
    ck = w_ck1.reshape(dep, CMP_BLOCK, HEAD_DIM, CMP_HIDDEN)
    cv = w_cv1.reshape(dep, CMP_BLOCK, HEAD_DIM, CMP_HIDDEN)
    z = jnp.zeros_like(ck)
    w = jnp.concatenate([jnp.concatenate([ck, z], axis=-1),
                         jnp.concatenate([z, cv], axis=-1)], axis=2)
    feat = half * 2 * HEAD_DIM
    wa = w[:, :half].reshape(dep, feat, 2 * CMP_HIDDEN).astype(BF16)
    wb = w[:, half:].reshape(dep, feat, 2 * CMP_HIDDEN).astype(BF16)
    pe = jnp.concatenate([pe_k, pe_v], axis=-1)
    pe2 = jnp.stack([pe[:, :half].reshape(dep, feat), pe[:, half:].reshape(dep, feat)], axis=1)
    pe2 = jnp.pad(pe2, ((0, 0), (0, 6), (0, 0)))
    z2 = jnp.zeros_like(w_ck2)
    w2 = jnp.concatenate([jnp.concatenate([w_ck2, z2], axis=-1),
                          jnp.concatenate([z2, w_cv2], axis=-1)], axis=1).astype(BF16)
    return pe2, wa, wb, w2


def kernel(x, norm_mix, w_in, pe_k, pe_v, w_ck1, w_ck2, w_cv1, w_cv2, w_up_nsa, w_up_dil,
           w_out, norm_ffn, w_ffn_in, w_ffn_out, norm_final):
    b, s, d = x.shape
    depth = w_in.shape[0]
    n = b * s
    nq = s // Q_BLOCK
    nch = s // CMP_STRIDE
    n_s = s // SEL_BLOCK
    assert s % DIL_SUPER == 0 and n_s <= LANES
    n_top = min(SEL_TOPK, n_s)
    tm = 512
    tm_ffn = 512
    tf = w_ffn_out.shape[1] // 2

    slopes_nsa, slopes_dil = _alibi_slopes()
    slopes_nsa = slopes_nsa * LOG2E
    ov = _overlap(s, nch)
    gexp = _gate_spread()
    dbias = _dil_bias(slopes_dil)
    w_a, w_gm = _proj_weights(w_in)
    pe2, wa, wb, w2 = _compress_weights(pe_k, pe_v, w_ck1, w_ck2, w_cv1, w_cv2)
    ua = w_up_nsa.astype(BF16)
    ub = w_up_dil.astype(BF16)
    wo = w_out.astype(BF16)
    wfi = w_ffn_in.astype(BF16)
    wfo = w_ffn_out.astype(BF16)
    gf = norm_final.reshape(1, d)

    x2 = x.reshape(n, d)
    for l in range(depth):
        g_mix = norm_mix[l].reshape(1, d)
        qa, xc, kva, qb, kvb, ga = _proj(x2, g_mix, w_a[l], tm)
        qa = qa.reshape(b, s, QA_W)
        kva = kva.reshape(b, s, KVA_W)
        ga = ga.reshape(b, s, GA_W)
        kvc = _compress(xc, pe2[l], wa[l], wb[l], w2[l], b)
        yc, sel, flags = _cmp(slopes_nsa, qa, kvc, ga, ov, gexp, n_top)
        pflags = flags.reshape(-1)
        ya = _selwin(pflags, slopes_nsa, qa, kva, sel, ga, yc, gexp)
        yb = _dil(qb.reshape(-1, b, s, LANES), kvb.reshape(-1, b, s, LANES), dbias, DIL_SUPER)
        x2 = _mixffn(x2, g_mix, ya.reshape(n, NSA_HEADS * HEAD_DIM), yb.reshape(n, DIL_HPG * HEAD_DIM),
                     w_gm[l], ua[l], ub[l], wo[l], norm_ffn[l].reshape(1, d), wfi[l], wfo[l], gf,
                     tm_ffn, tf, final_norm=(l == depth - 1))
    return x2.reshape(b, s, d)
```

```python
import functools

import numpy as np
import jax
import jax.numpy as jnp
from jax import lax
from jax.experimental import pallas as pl
from jax.experimental.pallas import tpu as pltpu

F32 = jnp.float32
BF16 = jnp.bfloat16

HEAD_DIM = 64
NSA_HEADS = 8
NSA_KV_GROUPS = 2
NSA_HPG = NSA_HEADS // NSA_KV_GROUPS
CMP_BLOCK = 32
CMP_STRIDE = 16
CMP_HIDDEN = 256
SEL_BLOCK = 64
SEL_TOPK = 16
NSA_WINDOW = 512
Q_BLOCK = 128
FORCE_SCORE = 1e6
DIL_GROUPS = ((128, 1), (512, 4), (2048, 16))
DIL_HPG = 4
DIL_HEADS = DIL_HPG * len(DIL_GROUPS)
DIL_BLOCK = 128
N_ALIBI = NSA_HEADS + DIL_HEADS
EPS = 1e-6
NEG_INF = -1e30
TINY = 1e-30
LOG2E = 1.4426950408889634
LN2 = 0.6931471805599453

LANES = 128
GROUP_W = NSA_HPG * HEAD_DIM
DIL_SUPER = 2048
DIL_UNROLL = 4
CMP_TILES = 8
VMEM_LIMIT = 48 * 1024 * 1024
MIXFFN_VMEM_LIMIT = 56 * 1024 * 1024


def _dot(a, b):
    return jnp.dot(a, b, preferred_element_type=F32)


def _nt_dot(a, b):
    return lax.dot_general(a, b, (((1,), (1,)), ((), ())), preferred_element_type=F32)


def _iota(shape, dim):
    return lax.broadcasted_iota(jnp.int32, shape, dim)


def _rms(x, g):
    ms = jnp.mean(x * x, axis=-1, keepdims=True)
    return (x * lax.rsqrt(ms + EPS)) * g


def _params(sem):
    return pltpu.CompilerParams(dimension_semantics=sem, vmem_limit_bytes=VMEM_LIMIT)


QA_W = NSA_HEADS * LANES
QP_W = NSA_HPG * LANES
CMP_W = NSA_KV_GROUPS * LANES
KVA_W, QB_W, KVB_W, GA_W = 512, 768, 1536, 128
PROJ_COLS = NSA_HEADS * HEAD_DIM + CMP_W + KVA_W + QB_W + KVB_W + GA_W


def _proj_kernel(x_ref, g_ref, w_ref, qa_ref, xc_ref, kva_ref, qb_ref, kvb_ref, ga_ref, cmp_scr):
    y = _rms(x_ref[...], g_ref[...]).astype(BF16)
    c = 0
    low = _iota((x_ref.shape[0], LANES), 1) < HEAD_DIM
    for p in range(0, NSA_HEADS // 2, 2):
        r2 = _dot(y, w_ref[:, c:c + 2 * LANES])
        c += 2 * LANES
        for k in range(2):
            r = r2[:, k * LANES:(k + 1) * LANES]
            h0 = 2 * (p + k)
            qa_ref[:, h0 * LANES:(h0 + 1) * LANES] = jnp.where(low, r, 0.0).astype(BF16)
            qa_ref[:, (h0 + 1) * LANES:(h0 + 2) * LANES] = (
                jnp.where(low, pltpu.roll(r, HEAD_DIM, 1), 0.0).astype(BF16))
    rc = _dot(y, w_ref[:, c:c + CMP_W])
    c += CMP_W
    nrow = xc_ref.shape[1]
    for gi in range(NSA_KV_GROUPS):
        cmp_scr[gi] = rc[:, gi * LANES:(gi + 1) * LANES]
        for l in range(CMP_STRIDE):
            xc_ref[gi, :, l * LANES:(l + 1) * LANES] = (
                cmp_scr[gi, pl.ds(l, nrow, stride=CMP_STRIDE), :].astype(BF16))
    kva_ref[...] = _dot(y, w_ref[:, c:c + KVA_W]).astype(BF16)
    c += KVA_W
    for j in range(0, QB_W // LANES, 2):
        r = _dot(y, w_ref[:, c:c + 2 * LANES])
        qb_ref[j] = r[:, :LANES]
        qb_ref[j + 1] = r[:, LANES:]
        c += 2 * LANES
    for j in range(0, KVB_W // LANES, 2):
        r = _dot(y, w_ref[:, c:c + 2 * LANES])
        kvb_ref[j] = r[:, :LANES]
        kvb_ref[j + 1] = r[:, LANES:]
        c += 2 * LANES
    ga_ref[...] = jax.nn.sigmoid(_dot(y, w_ref[:, c:c + GA_W]))


def _proj(x2, g, w, tm):
    n, d = x2.shape
    row = lambda i: (i, 0)
    fixed = lambda i: (0, 0)
    return pl.pallas_call(
        _proj_kernel,
        grid=(n // tm,),
        in_specs=[pl.BlockSpec((tm, d), row), pl.BlockSpec((1, d), fixed),
                  pl.BlockSpec((d, PROJ_COLS), fixed, pipeline_mode=pl.Buffered(1))],
        out_specs=[pl.BlockSpec((tm, QA_W), row),
                   pl.BlockSpec((NSA_KV_GROUPS, tm // CMP_STRIDE, CMP_STRIDE * LANES), lambda i: (0, i, 0)),
                   pl.BlockSpec((tm, KVA_W), row),
                   pl.BlockSpec((QB_W // LANES, tm, LANES), lambda i: (0, i, 0)),
                   pl.BlockSpec((KVB_W // LANES, tm, LANES), lambda i: (0, i, 0)),
                   pl.BlockSpec((tm, GA_W), row)],
        out_shape=[jax.ShapeDtypeStruct((n, QA_W), BF16),
                   jax.ShapeDtypeStruct((NSA_KV_GROUPS, n // CMP_STRIDE, CMP_STRIDE * LANES), BF16),
                   jax.ShapeDtypeStruct((n, KVA_W), BF16),
                   jax.ShapeDtypeStruct((QB_W // LANES, n, LANES), F32),
                   jax.ShapeDtypeStruct((KVB_W // LANES, n, LANES), F32),
                   jax.ShapeDtypeStruct((n, GA_W), F32)],
        scratch_shapes=[pltpu.VMEM((NSA_KV_GROUPS, tm, LANES), F32)],
        compiler_params=_params(("parallel",)),
        name="proj",
    )(x2, g, w)


def _compress_kernel(x_ref, pe_ref, wa_ref, wb_ref, w2_ref, o_ref):
    xf = x_ref[0].astype(F32)
    nch = xf.shape[0]
    xa = (xf + pe_ref[0:1, :]).astype(BF16)
    xb = (xf + pe_ref[1:2, :]).astype(BF16)
    ya = _dot(xa, wa_ref[...])
    yb = _dot(xb, wb_ref[...])
    h = ya + pltpu.roll(yb, nch - 1, 0)
    a = (h * jax.nn.sigmoid(h)).astype(BF16)
    o_ref[0, 0] = _dot(a, w2_ref[...]).astype(BF16)


def _compress(xc, pe, wa, wb, w2, b):
    g, rows, f = xc.shape
    nch = rows // b
    hid = wa.shape[1]
    return pl.pallas_call(
        _compress_kernel,
        grid=(b, g),
        in_specs=[pl.BlockSpec((1, nch, f), lambda i, j: (j, i, 0)),
                  pl.BlockSpec((8, f), lambda i, j: (0, 0)),
                  pl.BlockSpec((f, hid), lambda i, j: (0, 0)),
                  pl.BlockSpec((f, hid), lambda i, j: (0, 0)),
                  pl.BlockSpec((hid, LANES), lambda i, j: (0, 0))],
        out_specs=pl.BlockSpec((1, 1, nch, LANES), lambda i, j: (i, j, 0, 0)),
        out_shape=jax.ShapeDtypeStruct((b, g, nch, LANES), BF16),
        compiler_params=_params(("parallel", "parallel")),
        name="compress",
    )(xc, pe, wa, wb, w2)


def _gate_spread():
    g = np.arange(NSA_KV_GROUPS)[:, None, None, None]
    br = np.arange(3)[None, :, None, None]
    r = np.arange(LANES)[None, None, :, None]
    c = np.arange(GROUP_W)[None, None, None, :]
    return jnp.asarray(r == 3 * (NSA_HPG * g + c // HEAD_DIM) + br, BF16)


def _stack_heads(q):
    return jnp.concatenate([q[:, LANES * h:LANES * (h + 1)] for h in range(NSA_HPG)], axis=0)


def _unstack_heads(o4):
    lane = _iota((Q_BLOCK, LANES), 1)
    o = [o4[Q_BLOCK * h:Q_BLOCK * (h + 1)] for h in range(NSA_HPG)]
    c0 = jnp.where(lane < HEAD_DIM, pltpu.roll(o[0], HEAD_DIM, 1), o[1])
    c1 = jnp.where(lane < HEAD_DIM, pltpu.roll(o[2], HEAD_DIM, 1), o[3])
    return jnp.concatenate([c0, c1], axis=1)


def _row_sums_all_lanes(a):
    lane = _iota(a.shape, 1)
    return jnp.where(lane < HEAD_DIM, a, pltpu.roll(a, HEAD_DIM, 1))


def _normalise_unstack(a):
    return _unstack_heads(a * (1.0 / jnp.maximum(_row_sums_all_lanes(a), TINY)))


def _expand_gate(gates, spread):
    hi = gates.astype(BF16)
    lo = (gates - hi.astype(F32)).astype(BF16)
    return _dot(hi, spread) + _dot(lo, spread)


def _ones_in_k_lanes(kv):
    lane = _iota(kv.shape, 1)
    return jnp.where(lane < HEAD_DIM, jnp.ones_like(kv), kv)


def _cmp_kernel(slopes_ref, q_ref, kvc_ref, ga_ref, ov_ref, gexp_ref,
                yc_ref, sel_ref, fl_ref, bias0, cq, rhs, *, n_top):
    g = pl.program_id(1)
    i = pl.program_id(2)
    hq = NSA_HPG
    ncp = cq.shape[1]

    @pl.when(i == 0)
    def _():
        kpos = CMP_STRIDE * _iota((Q_BLOCK, ncp), 1) + (CMP_BLOCK - 1)
        cq[...] = kpos - _iota((Q_BLOCK, ncp), 0)
        kf = kpos[0:8].astype(F32)
        for h in range(hq):
            bias0[h] = slopes_ref[hq * g + h] * kf
        rhs[:, 0:LANES] = _ones_in_k_lanes(kvc_ref[0, 0])
        rhs[:, LANES:2 * LANES] = ov_ref[...]

    for k in range(CMP_TILES):
        it = CMP_TILES * i + k
        rows = slice(k * Q_BLOCK, (k + 1) * Q_BLOCK)
        q4 = _stack_heads(q_ref[0, rows, :])
        s = _nt_dot(q4, kvc_ref[0, 0]).reshape(hq, Q_BLOCK, ncp)
        t0 = it * Q_BLOCK
        neg = jnp.where(cq[...] <= t0, 0.0, NEG_INF)
        tf = t0.astype(F32)
        ps = []
        for h in range(hq):
            sb = s[h] + (neg + (bias0[h, 0:1] - slopes_ref[hq * g + h] * tf))
            m = jnp.max(sb, axis=-1, keepdims=True)
            m = jnp.where(m < 0.1 * NEG_INF, -NEG_INF, m)
            ps.append(jnp.exp2(sb - m).astype(BF16))
        big = _dot(jnp.concatenate(ps, axis=0), rhs[...])
        inv = 1.0 / jnp.maximum(_row_sums_all_lanes(big[:, 0:LANES]), TINY)
        oc = _unstack_heads(big[:, 0:LANES] * inv)
        yc_ref[0, rows, :] = _expand_gate(ga_ref[0, rows, :], gexp_ref[0, 0]) * oc
        pov = big[:, LANES:2 * LANES] * inv
        imp = pov[0:Q_BLOCK]
        for h in range(1, hq):
            imp = imp + pov[Q_BLOCK * h:Q_BLOCK * (h + 1)]

        qrow = _iota((Q_BLOCK, LANES), 0)
        j = _iota((Q_BLOCK, LANES), 1)
        jt = 2 * it + (qrow >> 6)
        valid = j <= jt
        forced = (j == 0) | (j == jt) | (j == jt - 1)
        score = jnp.where(forced, FORCE_SCORE, jnp.where(valid, imp, -1.0))
        st = score.T
        jr = _iota((LANES, Q_BLOCK), 0)
        cur = st
        taken = -3e38
        for _ in range(n_top):
            mx = jnp.max(cur, axis=0, keepdims=True)
            first = jnp.min(jnp.where(cur == mx, jr, LANES), axis=0, keepdims=True)
            cur = jnp.where(jr == first, taken, cur)
        keep = jnp.where((cur == taken) & (st >= 0.0), 1.0, 0.0).T
        sel_ref[0, 0, rows, :] = keep.astype(BF16)

        fl_ref[k] = (jnp.max(keep, axis=0, keepdims=True) > 0.0).astype(jnp.int32)


def _cmp(slopes, qa, kvc, ga, ov, gexp, n_top):
    b, s, _ = qa.shape
    g = NSA_KV_GROUPS
    nq = s // Q_BLOCK
    ncp = kvc.shape[2]
    tq = CMP_TILES * Q_BLOCK
    nt = s // tq
    return pl.pallas_call(
        functools.partial(_cmp_kernel, n_top=n_top),
        grid=(b, g, nt),
        in_specs=[pl.BlockSpec(memory_space=pltpu.SMEM),
                  pl.BlockSpec((1, tq, QP_W), lambda bi, gi, i: (bi, i, gi)),
                  pl.BlockSpec((1, 1, ncp, LANES), lambda bi, gi, i: (bi, gi, 0, 0)),
                  pl.BlockSpec((1, tq, LANES), lambda bi, gi, i: (bi, i, 0)),
                  pl.BlockSpec((ncp, LANES), lambda bi, gi, i: (0, 0)),
                  pl.BlockSpec((1, 3, LANES, GROUP_W), lambda bi, gi, i: (gi, 0, 0, 0))],
        out_specs=[pl.BlockSpec((1, tq, GROUP_W), lambda bi, gi, i: (bi, i, gi)),
                   pl.BlockSpec((1, 1, tq, LANES), lambda bi, gi, i: (bi, gi, i, 0)),
                   pl.BlockSpec((CMP_TILES, 1, LANES), lambda bi, gi, i: ((bi * g + gi) * nt + i, 0, 0))],
        out_shape=[jax.ShapeDtypeStruct((b, s, g * GROUP_W), F32),
                   jax.ShapeDtypeStruct((b, g, s, LANES), BF16),
                   jax.ShapeDtypeStruct((b * g * nq, 1, LANES), jnp.int32)],
        scratch_shapes=[pltpu.VMEM((NSA_HPG, 8, ncp), F32),
                        pltpu.VMEM((Q_BLOCK, ncp), jnp.int32),
                        pltpu.VMEM((ncp, 2 * LANES), BF16)],
        compiler_params=_params(("parallel", "parallel", "arbitrary")),
        name="cmp",
    )(slopes, qa, kvc, ga, ov, gexp)


WIN_UNITS = NSA_WINDOW // Q_BLOCK + 1
LIST_UNROLL = 4
SLOT_CHUNK = 10


def _selwin_kernel(pf_ref, slopes_ref, q_ref, kvs_ref, kvw_ref, sel_ref, ga_ref, yc_ref,
                   gexp_ref, ya_ref, sbuf, slot_unit, mrun, mrunw, acc, *, nq):
    b = pl.program_id(0)
    g = pl.program_id(1)
    i = pl.program_id(2)
    hq = NSA_HPG
    qmk = _iota((Q_BLOCK, LANES), 0) - _iota((Q_BLOCK, LANES), 1)
    koff = _iota((1, LANES), 1).astype(F32)
    q4 = _stack_heads(q_ref[0])
    sel = sel_ref[0, 0]
    dmat = _iota((LANES, LANES), 0) - (_iota((LANES, LANES), 1) >> 6)
    dead = -(1 << 20)

    for d in range(WIN_UNITS):
        u = i - (WIN_UNITS - 1) + d
        slot_unit[d] = jnp.where(u >= 0, u, -1)

    base = ((b * NSA_KV_GROUPS + g) * nq + i) * LANES

    def list_body(j4, k):
        for t in range(LIST_UNROLL):
            jj = j4 * LIST_UNROLL + t
            used = (pf_ref[base + 2 * jj] | pf_ref[base + 2 * jj + 1]) != 0

            @pl.when(used)
            def _(jj=jj, k=k):
                slot_unit[WIN_UNITS + k] = jj
            k = k + used.astype(jnp.int32)
        return k

    nsel = lax.fori_loop(0, i // LIST_UNROLL + 1, list_body, jnp.int32(0))
    for t in range(SLOT_CHUNK - 1):
        slot_unit[WIN_UNITS + nsel + t] = -1
    nchunk = (nsel + (SLOT_CHUNK - 1)) // SLOT_CHUNK

    def kv_tile(kv_ref, slot):
        u = jnp.maximum(slot_unit[slot], 0)
        return kv_ref[0, pl.ds(pl.multiple_of(u * Q_BLOCK, Q_BLOCK), Q_BLOCK), :]

    def score_slots(kv_ref, slots, window):
        units = [slot_unit[slot] for slot in slots]
        s_all = _nt_dot(q4, jnp.concatenate([kv_tile(kv_ref, slot) for slot in slots], axis=0))
        if not window:
            spread = jnp.concatenate(
                [jnp.where(dmat == 2 * u, 1.0, 0.0).astype(BF16) for u in units], axis=1)
            chosen = _dot(sel, spread)
        for n, (slot, u) in enumerate(zip(slots, units)):
            s = s_all[:, n * LANES:(n + 1) * LANES].reshape(hq, Q_BLOCK, LANES)
            off = jnp.where(u >= 0, (i - u) * Q_BLOCK, dead)
            dist = qmk + off
            if window:
                allow = (dist >= 0) & (dist < NSA_WINDOW)
            else:
                allow = (dist >= 0) & (chosen[:, n * LANES:(n + 1) * LANES] > 0.5)
            neg = jnp.where(allow, 0.0, NEG_INF)
            kd = koff - off.astype(F32)
            for h in range(hq):
                sb = s[h] + (neg + slopes_ref[hq * g + h] * kd)
                sbuf[slot, h] = sb
                mx = mrunw if window else mrun
                mx[h] = jnp.maximum(mx[h], sb)

    def spread_max(mx):
        m = jnp.max(mx[...], axis=-1, keepdims=True)
        mx[...] = jnp.broadcast_to(m, mx.shape)

    def weighted_values(kv_ref, slots, mx):
        tot = None
        for n in range(0, len(slots), 2):
            pair = slots[n:n + 2]
            p = jnp.concatenate(
                [jnp.exp2(sbuf[slot] - mx[...]).astype(BF16).reshape(hq * Q_BLOCK, LANES)
                 for slot in pair], axis=1)
            v = jnp.concatenate([_ones_in_k_lanes(kv_tile(kv_ref, slot)) for slot in pair], axis=0)
            d = _dot(p, v)
            tot = d if tot is None else tot + d
        return tot

    def score_chunk(ci, carry):
        for t in range(0, SLOT_CHUNK, 2):
            first = WIN_UNITS + ci * SLOT_CHUNK + t
            score_slots(kvs_ref, [first, first + 1], False)
        return carry

    def value_chunk(ci):
        return weighted_values(kvs_ref, [WIN_UNITS + ci * SLOT_CHUNK + t for t in range(SLOT_CHUNK)], mrun)

    def value_loop(ci, carry):
        acc[...] += value_chunk(ci)
        return carry

    mrunw[...] = jnp.full_like(mrunw, NEG_INF)
    mrun[...] = jnp.full_like(mrun, NEG_INF)
    for d in range(0, WIN_UNITS, 2):
        score_slots(kvw_ref, list(range(d, min(d + 2, WIN_UNITS))), True)
    score_chunk(0, 0)
    lax.fori_loop(1, nchunk, score_chunk, 0)
    spread_max(mrunw)
    spread_max(mrun)
    accw = weighted_values(kvw_ref, list(range(WIN_UNITS)), mrunw)
    acc[...] = value_chunk(0)
    lax.fori_loop(1, nchunk, value_loop, 0)
    o_s = _normalise_unstack(acc[...])
    o_w = _normalise_unstack(accw)

    gates = ga_ref[0]
    ya = (yc_ref[0] + _expand_gate(gates, gexp_ref[0, 1]) * o_s
          + _expand_gate(gates, gexp_ref[0, 2]) * o_w)
    ya_ref[0] = ya.astype(BF16)


def _selwin(pflags, slopes, qa, kva, sel, ga, yc, gexp):
    b, s, _ = qa.shape
    g = NSA_KV_GROUPS
    nq = s // Q_BLOCK
    nbuf = WIN_UNITS + nq + SLOT_CHUNK - 1
    grid_spec = pltpu.PrefetchScalarGridSpec(
        num_scalar_prefetch=1,
        grid=(b, g, nq),
        in_specs=[pl.BlockSpec(memory_space=pltpu.SMEM),
                  pl.BlockSpec((1, Q_BLOCK, QP_W), lambda bi, gi, i, pf: (bi, i, gi)),
                  pl.BlockSpec((1, s, LANES), lambda bi, gi, i, pf: (bi, 0, gi)),
                  pl.BlockSpec((1, s, LANES), lambda bi, gi, i, pf: (bi, 0, NSA_KV_GROUPS + gi)),
                  pl.BlockSpec((1, 1, Q_BLOCK, LANES), lambda bi, gi, i, pf: (bi, gi, i, 0)),
                  pl.BlockSpec((1, Q_BLOCK, LANES), lambda bi, gi, i, pf: (bi, i, 0)),
                  pl.BlockSpec((1, Q_BLOCK, GROUP_W), lambda bi, gi, i, pf: (bi, i, gi)),
                  pl.BlockSpec((1, 3, LANES, GROUP_W), lambda bi, gi, i, pf: (gi, 0, 0, 0))],
        out_specs=pl.BlockSpec((1, Q_BLOCK, GROUP_W), lambda bi, gi, i, pf: (bi, i, gi)),
        scratch_shapes=[pltpu.VMEM((nbuf, NSA_HPG, Q_BLOCK, LANES), F32),
                        pltpu.SMEM((nbuf,), jnp.int32),
                        pltpu.VMEM((NSA_HPG, Q_BLOCK, LANES), F32),
                        pltpu.VMEM((NSA_HPG, Q_BLOCK, LANES), F32),
                        pltpu.VMEM((NSA_HPG * Q_BLOCK, LANES), F32)],
    )
    return pl.pallas_call(
        functools.partial(_selwin_kernel, nq=nq),
        grid_spec=grid_spec,
        out_shape=jax.ShapeDtypeStruct((b, s, g * GROUP_W), BF16),
        compiler_params=_params(("parallel", "parallel", "arbitrary")),
        name="selwin",
    )(pflags, slopes, qa, kva, kva, sel, ga, yc, gexp)


def _dil_bias(slopes):
    c = DIL_BLOCK
    qi = np.arange(c)[:, None]
    ki = np.arange(2 * c)[None, :]
    delta = qi + c - ki
    out = []
    for k, (window, dil) in enumerate(DIL_GROUPS):
        in_band = jnp.asarray((delta >= 0) & (delta <= window // dil))
        dist = jnp.asarray((delta * dil).astype(np.float32))
        for h in range(DIL_HPG):
            out.append(jnp.where(in_band, -(slopes[DIL_HPG * k + h] * dist) * LOG2E, NEG_INF))
    return jnp.stack(out).reshape(len(out) // 2, 2 * c, 2 * c)


def _dil_group(k, dil, sup, q_ref, kvc_ref, kvp_ref, bias_ref, mx_ref, w_ref, y_ref):
    c = DIL_BLOCK
    nblk = sup // (c * dil)
    has_halo = pl.program_id(1) > 0
    nch = DIL_HPG * HEAD_DIM // LANES
    upper = _iota((c, LANES), 1) >= HEAD_DIM

    def block(idx):
        r = idx // nblk
        n = idx - r * nblk
        start = r + dil * c * n
        rows = pl.ds(start, c, stride=dil)
        rows_prev = pl.ds(jnp.maximum(start - dil * c, r), c, stride=dil)
        rows_halo = pl.ds(r + dil * c * (nblk - 1), c, stride=dil)
        pen = jnp.where(jnp.logical_or(n >= 1, has_halo), 0.0, NEG_INF)

        def prev(ch):
            return jnp.where(n >= 1, kvc_ref[ch, 0, rows_prev, :], kvp_ref[ch, 0, rows_halo, :]).astype(BF16)

        for ch in range(nch):
            qb = q_ref[ch, 0, rows, :].astype(BF16)
            zero = jnp.zeros_like(qb)
            qs = jnp.concatenate([jnp.where(upper, zero, qb), jnp.where(upper, qb, zero)], axis=0)
            kband = jnp.concatenate([prev(ch), kvc_ref[ch, 0, rows, :].astype(BF16)], axis=0)
            vband = jnp.concatenate([prev(nch + ch), kvc_ref[nch + ch, 0, rows, :].astype(BF16)], axis=0)
            s = _nt_dot(qs, kband) + bias_ref[ch]
            s = jnp.concatenate([s[:, :c] + pen, s[:, c:]], axis=1)
            m = jnp.max(s, axis=-1, keepdims=True)
            e = jnp.exp2(s - m).astype(BF16)
            ones = jnp.ones((2 * c, LANES), BF16)
            ol = _dot(e, jnp.concatenate([vband, ones], axis=1))
            lt = jnp.maximum(ol[:, LANES:], TINY)
            o2 = ol[:, :LANES] * (1.0 / lt)
            l2 = m * LN2 + jnp.log(lt)
            o = jnp.where(upper, o2[c:], o2[:c])
            lse = jnp.where(upper, l2[c:], l2[:c])
            if k == 0:
                mx_ref[ch, rows, :] = lse
                w_ref[ch, rows, :] = jnp.ones_like(lse)
                y_ref[ch, rows, :] = o
            else:
                mx_old = mx_ref[ch, rows, :]
                mx_new = jnp.maximum(mx_old, lse)
                a = jnp.exp(mx_old - mx_new)
                bnew = jnp.exp(lse - mx_new)
                mx_ref[ch, rows, :] = mx_new
                w_ref[ch, rows, :] = w_ref[ch, rows, :] * a + bnew
                y_ref[ch, rows, :] = y_ref[ch, rows, :] * a + bnew * o

    def body(t, carry):
        for j in range(DIL_UNROLL):
            block(t * DIL_UNROLL + j)
        return carry

    lax.fori_loop(0, sup // (c * DIL_UNROLL), body, 0)


def _dil_kernel(q_ref, kvc_ref, kvp_ref, bias_ref, yb_ref, mx_ref, w_ref, y_ref, *, sup):
    gi = pl.program_id(2)
    for k, (_, dil) in enumerate(DIL_GROUPS):
        @pl.when(gi == k)
        def _(k=k, dil=dil):
            _dil_group(k, dil, sup, q_ref, kvc_ref, kvp_ref, bias_ref, mx_ref, w_ref, y_ref)

    @pl.when(gi == len(DIL_GROUPS) - 1)
    def _():
        for ch in range(DIL_HPG * HEAD_DIM // LANES):
            yb_ref[0, :, ch * LANES:(ch + 1) * LANES] = (y_ref[ch] * (1.0 / w_ref[ch])).astype(BF16)


def _dil(qb, kvb, bias, sup):
    _, b, s, _ = qb.shape
    hw = DIL_HPG * HEAD_DIM
    nch = hw // LANES
    ng = len(DIL_GROUPS)
    return pl.pallas_call(
        functools.partial(_dil_kernel, sup=sup),
        grid=(b, s // sup, ng),
        in_specs=[pl.BlockSpec((nch, 1, sup, LANES), lambda bi, si, gi: (gi, bi, si, 0)),
                  pl.BlockSpec((2 * nch, 1, sup, LANES), lambda bi, si, gi: (gi, bi, si, 0)),
                  pl.BlockSpec((2 * nch, 1, sup, LANES),
                               lambda bi, si, gi: (gi, bi, jnp.maximum(si - 1, 0), 0)),
                  pl.BlockSpec((nch, 2 * DIL_BLOCK, 2 * DIL_BLOCK), lambda bi, si, gi: (gi, 0, 0))],
        out_specs=pl.BlockSpec((1, sup, hw), lambda bi, si, gi: (bi, si, 0)),
        out_shape=jax.ShapeDtypeStruct((b, s, hw), BF16),
        scratch_shapes=[pltpu.VMEM((nch, sup, LANES), F32), pltpu.VMEM((nch, sup, LANES), F32),
                        pltpu.VMEM((nch, sup, LANES), F32)],
        compiler_params=_params(("parallel", "parallel", "arbitrary")),
        name="dil",
    )(qb, kvb, kvb, bias)


def _merge_kernel(x_ref, g_ref, ya_ref, yb_ref, wg_ref, ua_ref, ub_ref, wo_ref, o_ref):
    x = x_ref[...]
    d = x.shape[1]
    hn = _rms(x, g_ref[...]).astype(BF16)
    gate_a = jax.nn.sigmoid(_dot(hn, wg_ref[:, 0:d]))
    gate_b = jax.nn.sigmoid(_dot(hn, wg_ref[:, d:2 * d]))
    merged = gate_a * _dot(ya_ref[...], ua_ref[...]) + gate_b * _dot(yb_ref[...], ub_ref[...])
    o_ref[...] = x + _dot(merged.astype(BF16), wo_ref[...])


def _merge(x2, g, ya, yb, wg, ua, ub, wo, tm):
    n, d = x2.shape
    row = lambda i: (i, 0)
    fixed = lambda i: (0, 0)
    return pl.pallas_call(
        _merge_kernel,
        grid=(n // tm,),
        in_specs=[pl.BlockSpec((tm, d), row), pl.BlockSpec((1, d), fixed),
                  pl.BlockSpec((tm, ya.shape[1]), row), pl.BlockSpec((tm, yb.shape[1]), row),
                  pl.BlockSpec(wg.shape, fixed), pl.BlockSpec(ua.shape, fixed),
                  pl.BlockSpec(ub.shape, fixed), pl.BlockSpec(wo.shape, fixed)],
        out_specs=pl.BlockSpec((tm, d), row),
        out_shape=jax.ShapeDtypeStruct((n, d), F32),
        compiler_params=_params(("parallel",)),
        name="merge",
    )(x2, g, ya, yb, wg, ua, ub, wo)


def _mixffn_kernel(x_ref, g_ref, ya_ref, yb_ref, wg_ref, ua_ref, ub_ref, wo_ref,
                   g2_ref, wi_ref, wf_ref, gf_ref, o_ref, *, tf, final_norm):
    x = x_ref[...]
    d = x.shape[1]
    hn = _rms(x, g_ref[...]).astype(BF16)
    gate_a = jax.nn.sigmoid(_dot(hn, wg_ref[:, 0:d]))
    gate_b = jax.nn.sigmoid(_dot(hn, wg_ref[:, d:2 * d]))
    merged = gate_a * _dot(ya_ref[...], ua_ref[...]) + gate_b * _dot(yb_ref[...], ub_ref[...])
    x = x + _dot(merged.astype(BF16), wo_ref[...])
    dff = wf_ref.shape[0]
    hn = _rms(x, g2_ref[...]).astype(BF16)
    y = x
    for c in range(0, dff, tf):
        gt = _dot(hn, wi_ref[:, c:c + tf])
        up = _dot(hn, wi_ref[:, dff + c:dff + c + tf])
        act = ((gt * jax.nn.sigmoid(gt)) * up).astype(BF16)
        y = y + _dot(act, wf_ref[c:c + tf, :])
    if final_norm:
        y = _rms(y, gf_ref[...])
    o_ref[...] = y


def _mixffn(x2, g, ya, yb, wg, ua, ub, wo, g2, w_in, w_out, gf, tm, tf, final_norm):
    n, d = x2.shape
    row = lambda i: (i, 0)
    fixed = lambda i: (0, 0)
    once = pl.Buffered(1)
    weights = [wg, ua, ub, wo]
    return pl.pallas_call(
        functools.partial(_mixffn_kernel, tf=tf, final_norm=final_norm),
        grid=(n // tm,),
        in_specs=([pl.BlockSpec((tm, d), row), pl.BlockSpec((1, d), fixed),
                   pl.BlockSpec((tm, ya.shape[1]), row), pl.BlockSpec((tm, yb.shape[1]), row)]
                  + [pl.BlockSpec(w.shape, fixed, pipeline_mode=once) for w in weights]
                  + [pl.BlockSpec((1, d), fixed),
                     pl.BlockSpec(w_in.shape, fixed, pipeline_mode=once),
                     pl.BlockSpec(w_out.shape, fixed, pipeline_mode=once),
                     pl.BlockSpec((1, d), fixed)]),
        out_specs=pl.BlockSpec((tm, d), row),
        out_shape=jax.ShapeDtypeStruct((n, d), F32),
        compiler_params=pltpu.CompilerParams(dimension_semantics=("parallel",),
                                             vmem_limit_bytes=MIXFFN_VMEM_LIMIT),
        name="mixffn",
    )(x2, g, ya, yb, wg, ua, ub, wo, g2, w_in, w_out, gf)


def _ffn_kernel(x_ref, g_ref, wi_ref, wo_ref, gf_ref, o_ref, *, tf, final_norm):
    x = x_ref[...]
    dff = wo_ref.shape[0]
    hn = _rms(x, g_ref[...]).astype(BF16)
    y = x
    for c in range(0, dff, tf):
        gt = _dot(hn, wi_ref[:, c:c + tf])
        up = _dot(hn, wi_ref[:, dff + c:dff + c + tf])
        act = ((gt * jax.nn.sigmoid(gt)) * up).astype(BF16)
        y = y + _dot(act, wo_ref[c:c + tf, :])
    if final_norm:
        y = _rms(y, gf_ref[...])
    o_ref[...] = y


def _ffn(x2, g, w_in, w_out, gf, tm, tf, final_norm):
    n, d = x2.shape
    fixed = lambda i: (0, 0)
    once = pl.Buffered(1)
    return pl.pallas_call(
        functools.partial(_ffn_kernel, tf=tf, final_norm=final_norm),
        grid=(n // tm,),
        in_specs=[pl.BlockSpec((tm, d), lambda i: (i, 0)),
                  pl.BlockSpec((1, d), fixed),
                  pl.BlockSpec(w_in.shape, fixed, pipeline_mode=once),
                  pl.BlockSpec(w_out.shape, fixed, pipeline_mode=once),
                  pl.BlockSpec((1, d), fixed)],
        out_specs=pl.BlockSpec((tm, d), lambda i: (i, 0)),
        out_shape=jax.ShapeDtypeStruct((n, d), F32),
        compiler_params=_params(("parallel",)),
        name="ffn",
    )(x2, g, w_in, w_out, gf)


def _alibi_slopes():
    k = jnp.arange(1, N_ALIBI + 1, dtype=F32)
    s = jnp.exp2(-8.0 * k / N_ALIBI)
    nsa = s[2 * DIL_HPG:2 * DIL_HPG + NSA_HEADS]
    dil = jnp.concatenate([s[:2 * DIL_HPG], s[2 * DIL_HPG + NSA_HEADS:]])
    return nsa, dil


def _overlap(seq, ncp):
    n_c = (seq - CMP_BLOCK) // CMP_STRIDE + 1
    n_s = seq // SEL_BLOCK
    c_start = np.arange(n_c) * CMP_STRIDE
    s_start = np.arange(n_s) * SEL_BLOCK
    ov = np.clip(np.minimum(c_start[:, None] + CMP_BLOCK, s_start[None, :] + SEL_BLOCK)
                 - np.maximum(c_start[:, None], s_start[None, :]), 0, None).astype(np.float32) / CMP_BLOCK
    out = np.zeros((ncp, LANES), np.float32)
    out[:n_c, :n_s] = ov
    return jnp.asarray(out, BF16)


def _proj_weights(w_in):
    dep, d, _ = w_in.shape
    scale = HEAD_DIM ** -0.5 * LOG2E
    c0 = NSA_HEADS * HEAD_DIM
    c1 = c0 + 3 * 2 * NSA_KV_GROUPS * HEAD_DIM
    c2 = c1 + NSA_HEADS * 3
    c3 = c2 + 3 * DIL_HEADS * HEAD_DIM
    qa = (w_in[..., :c0] * scale).astype(BF16)
    wb = w_in.astype(BF16)
    kva = wb[..., c0:c1].reshape(dep, d, 3, 2, NSA_KV_GROUPS, HEAD_DIM)
    kva = kva.transpose(0, 1, 2, 4, 3, 5).reshape(dep, d, CMP_W + KVA_W)
    ga = jnp.pad(wb[..., c1:c2], ((0, 0), (0, 0), (0, GA_W - (c2 - c1))))
    qkvb = w_in[..., c2:c3].reshape(dep, d, 3, len(DIL_GROUPS), DIL_HPG * HEAD_DIM)
    qb = (qkvb[:, :, 0] * scale).astype(BF16).reshape(dep, d, QB_W)
    kvb = qkvb[:, :, 1:3].astype(BF16).transpose(0, 1, 3, 2, 4).reshape(dep, d, KVB_W)
    w_a = jnp.concatenate([qa, kva, qb, kvb, ga], axis=-1)
    return w_a, wb[..., c3:]


def _compress_weights(pe_k, pe_v, w_ck1, w_ck2, w_cv1, w_cv2):
    dep = pe_k.shape[0]
    half = CMP_BLOCK // 2
    ck = w_ck1.reshape(dep, CMP_BLOCK, HEAD_DIM, CMP_HIDDEN)
    cv = w_cv1.reshape(dep, CMP_BLOCK, HEAD_DIM, CMP_HIDDEN)
    z = jnp.zeros_like(ck)
    w = jnp.concatenate([jnp.concatenate([ck, z], axis=-1),
                         jnp.concatenate([z, cv], axis=-1)], axis=2)
    feat = half * 2 * HEAD_DIM
    wa = w[:, :half].reshape(dep, feat, 2 * CMP_HIDDEN).astype(BF16)
    wb = w[:, half:].reshape(dep, feat, 2 * CMP_HIDDEN).astype(BF16)
    pe = jnp.concatenate([pe_k, pe_v], axis=-1)
    pe2 = jnp.stack([pe[:, :half].reshape(dep, feat), pe[:, half:].reshape(dep, feat)], axis=1)
    pe2 = jnp.pad(pe2, ((0, 0), (0, 6), (0, 0)))
    z2 = jnp.zeros_like(w_ck2)
    w2 = jnp.concatenate([jnp.concatenate([w_ck2, z2], axis=-1),
                          jnp.concatenate([z2, w_cv2], axis=-1)], axis=1).astype(BF16)
    return pe2, wa, wb, w2


def kernel(x, norm_mix, w_in, pe_k, pe_v, w_ck1, w_ck2, w_cv1, w_cv2, w_up_nsa, w_up_dil,
           w_out, norm_ffn, w_ffn_in, w_ffn_out, norm_final):
    b, s, d = x.shape
    depth = w_in.shape[0]
    n = b * s
    nq = s // Q_BLOCK
    nch = s // CMP_STRIDE
    n_s = s // SEL_BLOCK
    assert s % DIL_SUPER == 0 and n_s <= LANES
    n_top = min(SEL_TOPK, n_s)
    tm = 512
    tm_ffn = 512
    tf = w_ffn_out.shape[1] // 2

    slopes_nsa, slopes_dil = _alibi_slopes()
    slopes_nsa = slopes_nsa * LOG2E
    ov = _overlap(s, nch)
    gexp = _gate_spread()
    dbias = _dil_bias(slopes_dil)
    w_a, w_gm = _proj_weights(w_in)
    pe2, wa, wb, w2 = _compress_weights(pe_k, pe_v, w_ck1, w_ck2, w_cv1, w_cv2)
    ua = w_up_nsa.astype(BF16)
    ub = w_up_dil.astype(BF16)
    wo = w_out.astype(BF16)
    wfi = w_ffn_in.astype(BF16)
    wfo = w_ffn_out.astype(BF16)
    gf = norm_final.reshape(1, d)

    x2 = x.reshape(n, d)
    for l in range(depth):
        g_mix = norm_mix[l].reshape(1, d)
        qa, xc, kva, qb, kvb, ga = _proj(x2, g_mix, w_a[l], tm)
        qa = qa.reshape(b, s, QA_W)
        kva = kva.reshape(b, s, KVA_W)
        ga = ga.reshape(b, s, GA_W)
        kvc = _compress(xc, pe2[l], wa[l], wb[l], w2[l], b)
        yc, sel, flags = _cmp(slopes_nsa, qa, kvc, ga, ov, gexp, n_top)
        pflags = flags.reshape(-1)
        ya = _selwin(pflags, slopes_nsa, qa, kva, sel, ga, yc, gexp)
        yb = _dil(qb.reshape(-1, b, s, LANES), kvb.reshape(-1, b, s, LANES), dbias, DIL_SUPER)
        x2 = _mixffn(x2, g_mix, ya.reshape(n, NSA_HEADS * HEAD_DIM), yb.reshape(n, DIL_HPG * HEAD_DIM),
                     w_gm[l], ua[l], ub[l], wo[l], norm_ffn[l].reshape(1, d), wfi[l], wfo[l], gf,
                     tm_ffn, tf, final_norm=(l == depth - 1))
    return x2.reshape(b, s, d)
```

```python
import functools

import numpy as np
import jax
import jax.numpy as jnp
from jax import lax
from jax.experimental import pallas as pl
from jax.experimental.pallas import tpu as pltpu

F32 = jnp.float32
BF16 = jnp.bfloat16

HEAD_DIM = 64
NSA_HEADS = 8
NSA_KV_GROUPS = 2
NSA_HPG = NSA_HEADS // NSA_KV_GROUPS
CMP_BLOCK = 32
CMP_STRIDE = 16
CMP_HIDDEN = 256
SEL_BLOCK = 64
SEL_TOPK = 16
NSA_WINDOW = 512
Q_BLOCK = 128
FORCE_SCORE = 1e6
DIL_GROUPS = ((128, 1), (512, 4), (2048, 16))
DIL_HPG = 4
DIL_HEADS = DIL_HPG * len(DIL_GROUPS)
DIL_BLOCK = 128
N_ALIBI = NSA_HEADS + DIL_HEADS
EPS = 1e-6
NEG_INF = -1e30
TINY = 1e-30
LOG2E = 1.4426950408889634
LN2 = 0.6931471805599453

LANES = 128
GROUP_W = NSA_HPG * HEAD_DIM
DIL_SUPER = 2048
DIL_UNROLL = 4
CMP_TILES = 8
VMEM_LIMIT = 48 * 1024 * 1024
MIXFFN_VMEM_LIMIT = 56 * 1024 * 1024


def _dot(a, b):
    return jnp.dot(a, b, preferred_element_type=F32)


def _nt_dot(a, b):
    return lax.dot_general(a, b, (((1,), (1,)), ((), ())), preferred_element_type=F32)


def _iota(shape, dim):
    return lax.broadcasted_iota(jnp.int32, shape, dim)


def _rms(x, g):
    ms = jnp.mean(x * x, axis=-1, keepdims=True)
    return (x * lax.rsqrt(ms + EPS)) * g


def _params(sem):
    return pltpu.CompilerParams(dimension_semantics=sem, vmem_limit_bytes=VMEM_LIMIT)


QA_W = NSA_HEADS * LANES
QP_W = NSA_HPG * LANES
CMP_W = NSA_KV_GROUPS * LANES
KVA_W, QB_W, KVB_W, GA_W = 512, 768, 1536, 128
PROJ_COLS = NSA_HEADS * HEAD_DIM + CMP_W + KVA_W + QB_W + KVB_W + GA_W


def _proj_kernel(x_ref, g_ref, w_ref, qa_ref, xc_ref, kva_ref, qb_ref, kvb_ref, ga_ref, cmp_scr):
    y = _rms(x_ref[...], g_ref[...]).astype(BF16)
    c = 0
    low = _iota((x_ref.shape[0], LANES), 1) < HEAD_DIM
    for p in range(0, NSA_HEADS // 2, 2):
        r2 = _dot(y, w_ref[:, c:c + 2 * LANES])
        c += 2 * LANES
        for k in range(2):
            r = r2[:, k * LANES:(k + 1) * LANES]
            h0 = 2 * (p + k)
            qa_ref[:, h0 * LANES:(h0 + 1) * LANES] = jnp.where(low, r, 0.0).astype(BF16)
            qa_ref[:, (h0 + 1) * LANES:(h0 + 2) * LANES] = (
                jnp.where(low, pltpu.roll(r, HEAD_DIM, 1), 0.0).astype(BF16))
    rc = _dot(y, w_ref[:, c:c + CMP_W])
    c += CMP_W
    nrow = xc_ref.shape[1]
    for gi in range(NSA_KV_GROUPS):
        cmp_scr[gi] = rc[:, gi * LANES:(gi + 1) * LANES]
        for l in range(CMP_STRIDE):
            xc_ref[gi, :, l * LANES:(l + 1) * LANES] = (
                cmp_scr[gi, pl.ds(l, nrow, stride=CMP_STRIDE), :].astype(BF16))
    kva_ref[...] = _dot(y, w_ref[:, c:c + KVA_W]).astype(BF16)
    c += KVA_W
    for j in range(0, QB_W // LANES, 2):
        r = _dot(y, w_ref[:, c:c + 2 * LANES])
        qb_ref[j] = r[:, :LANES]
        qb_ref[j + 1] = r[:, LANES:]
        c += 2 * LANES
    for j in range(0, KVB_W // LANES, 2):
        r = _dot(y, w_ref[:, c:c + 2 * LANES])
        kvb_ref[j] = r[:, :LANES]
        kvb_ref[j + 1] = r[:, LANES:]
        c += 2 * LANES
    ga_ref[...] = jax.nn.sigmoid(_dot(y, w_ref[:, c:c + GA_W]))


def _proj(x2, g, w, tm):
    n, d = x2.shape
    row = lambda i: (i, 0)
    fixed = lambda i: (0, 0)
    return pl.pallas_call(
        _proj_kernel,
        grid=(n // tm,),
        in_specs=[pl.BlockSpec((tm, d), row), pl.BlockSpec((1, d), fixed),
                  pl.BlockSpec((d, PROJ_COLS), fixed, pipeline_mode=pl.Buffered(1))],
        out_specs=[pl.BlockSpec((tm, QA_W), row),
                   pl.BlockSpec((NSA_KV_GROUPS, tm // CMP_STRIDE, CMP_STRIDE * LANES), lambda i: (0, i, 0)),
                   pl.BlockSpec((tm, KVA_W), row),
                   pl.BlockSpec((QB_W // LANES, tm, LANES), lambda i: (0, i, 0)),
                   pl.BlockSpec((KVB_W // LANES, tm, LANES), lambda i: (0, i, 0)),
                   pl.BlockSpec((tm, GA_W), row)],
        out_shape=[jax.ShapeDtypeStruct((n, QA_W), BF16),
                   jax.ShapeDtypeStruct((NSA_KV_GROUPS, n // CMP_STRIDE, CMP_STRIDE * LANES), BF16),
                   jax.ShapeDtypeStruct((n, KVA_W), BF16),
                   jax.ShapeDtypeStruct((QB_W // LANES, n, LANES), F32),
                   jax.ShapeDtypeStruct((KVB_W // LANES, n, LANES), F32),
                   jax.ShapeDtypeStruct((n, GA_W), F32)],
        scratch_shapes=[pltpu.VMEM((NSA_KV_GROUPS, tm, LANES), F32)],
        compiler_params=_params(("parallel",)),
        name="proj",
    )(x2, g, w)


def _compress_kernel(x_ref, pe_ref, wa_ref, wb_ref, w2_ref, o_ref):
    xf = x_ref[0].astype(F32)
    nch = xf.shape[0]
    xa = (xf + pe_ref[0:1, :]).astype(BF16)
    xb = (xf + pe_ref[1:2, :]).astype(BF16)
    ya = _dot(xa, wa_ref[...])
    yb = _dot(xb, wb_ref[...])
    h = ya + pltpu.roll(yb, nch - 1, 0)
    a = (h * jax.nn.sigmoid(h)).astype(BF16)
    o_ref[0, 0] = _dot(a, w2_ref[...]).astype(BF16)


def _compress(xc, pe, wa, wb, w2, b):
    g, rows, f = xc.shape
    nch = rows // b
    hid = wa.shape[1]
    return pl.pallas_call(
        _compress_kernel,
        grid=(b, g),
        in_specs=[pl.BlockSpec((1, nch, f), lambda i, j: (j, i, 0)),
                  pl.BlockSpec((8, f), lambda i, j: (0, 0)),
                  pl.BlockSpec((f, hid), lambda i, j: (0, 0)),
                  pl.BlockSpec((f, hid), lambda i, j: (0, 0)),
                  pl.BlockSpec((hid, LANES), lambda i, j: (0, 0))],
        out_specs=pl.BlockSpec((1, 1, nch, LANES), lambda i, j: (i, j, 0, 0)),
        out_shape=jax.ShapeDtypeStruct((b, g, nch, LANES), BF16),
        compiler_params=_params(("parallel", "parallel")),
        name="compress",
    )(xc, pe, wa, wb, w2)


def _gate_spread():
    g = np.arange(NSA_KV_GROUPS)[:, None, None, None]
    br = np.arange(3)[None, :, None, None]
    r = np.arange(LANES)[None, None, :, None]
    c = np.arange(GROUP_W)[None, None, None, :]
    return jnp.asarray(r == 3 * (NSA_HPG * g + c // HEAD_DIM) + br, BF16)


def _stack_heads(q):
    return jnp.concatenate([q[:, LANES * h:LANES * (h + 1)] for h in range(NSA_HPG)], axis=0)


def _unstack_heads(o4):
    lane = _iota((Q_BLOCK, LANES), 1)
    o = [o4[Q_BLOCK * h:Q_BLOCK * (h + 1)] for h in range(NSA_HPG)]
    c0 = jnp.where(lane < HEAD_DIM, pltpu.roll(o[0], HEAD_DIM, 1), o[1])
    c1 = jnp.where(lane < HEAD_DIM, pltpu.roll(o[2], HEAD_DIM, 1), o[3])
    return jnp.concatenate([c0, c1], axis=1)


def _row_sums_all_lanes(a):
    lane = _iota(a.shape, 1)
    return jnp.where(lane < HEAD_DIM, a, pltpu.roll(a, HEAD_DIM, 1))


def _normalise_unstack(a):
    return _unstack_heads(a * (1.0 / jnp.maximum(_row_sums_all_lanes(a), TINY)))


def _expand_gate(gates, spread):
    hi = gates.astype(BF16)
    lo = (gates - hi.astype(F32)).astype(BF16)
    return _dot(hi, spread) + _dot(lo, spread)


def _ones_in_k_lanes(kv):
    lane = _iota(kv.shape, 1)
    return jnp.where(lane < HEAD_DIM, jnp.ones_like(kv), kv)


def _cmp_kernel(slopes_ref, q_ref, kvc_ref, ga_ref, ov_ref, gexp_ref,
                yc_ref, sel_ref, fl_ref, bias0, cq, rhs, *, n_top):
    g = pl.program_id(1)
    i = pl.program_id(2)
    hq = NSA_HPG
    ncp = cq.shape[1]

    @pl.when(i == 0)
    def _():
        kpos = CMP_STRIDE * _iota((Q_BLOCK, ncp), 1) + (CMP_BLOCK - 1)
        cq[...] = kpos - _iota((Q_BLOCK, ncp), 0)
        kf = kpos[0:8].astype(F32)
        for h in range(hq):
            bias0[h] = slopes_ref[hq * g + h] * kf
        rhs[:, 0:LANES] = _ones_in_k_lanes(kvc_ref[0, 0])
        rhs[:, LANES:2 * LANES] = ov_ref[...]

    for k in range(CMP_TILES):
        it = CMP_TILES * i + k
        rows = slice(k * Q_BLOCK, (k + 1) * Q_BLOCK)
        q4 = _stack_heads(q_ref[0, rows, :])
        s = _nt_dot(q4, kvc_ref[0, 0]).reshape(hq, Q_BLOCK, ncp)
        t0 = it * Q_BLOCK
        neg = jnp.where(cq[...] <= t0, 0.0, NEG_INF)
        tf = t0.astype(F32)
        ps = []
        for h in range(hq):
            sb = s[h] + (neg + (bias0[h, 0:1] - slopes_ref[hq * g + h] * tf))
            m = jnp.max(sb, axis=-1, keepdims=True)
            m = jnp.where(m < 0.1 * NEG_INF, -NEG_INF, m)
            ps.append(jnp.exp2(sb - m).astype(BF16))
        big = _dot(jnp.concatenate(ps, axis=0), rhs[...])
        inv = 1.0 / jnp.maximum(_row_sums_all_lanes(big[:, 0:LANES]), TINY)
        oc = _unstack_heads(big[:, 0:LANES] * inv)
        yc_ref[0, rows, :] = _expand_gate(ga_ref[0, rows, :], gexp_ref[0, 0]) * oc
        pov = big[:, LANES:2 * LANES] * inv
        imp = pov[0:Q_BLOCK]
        for h in range(1, hq):
            imp = imp + pov[Q_BLOCK * h:Q_BLOCK * (h + 1)]

        qrow = _iota((Q_BLOCK, LANES), 0)
        j = _iota((Q_BLOCK, LANES), 1)
        jt = 2 * it + (qrow >> 6)
        valid = j <= jt
        forced = (j == 0) | (j == jt) | (j == jt - 1)
        score = jnp.where(forced, FORCE_SCORE, jnp.where(valid, imp, -1.0))
        st = score.T
        jr = _iota((LANES, Q_BLOCK), 0)
        cur = st
        taken = -3e38
        for _ in range(n_top):
            mx = jnp.max(cur, axis=0, keepdims=True)
            first = jnp.min(jnp.where(cur == mx, jr, LANES), axis=0, keepdims=True)
            cur = jnp.where(jr == first, taken, cur)
        keep = jnp.where((cur == taken) & (st >= 0.0), 1.0, 0.0).T
        sel_ref[0, 0, rows, :] = keep.astype(BF16)

        fl_ref[k] = (jnp.max(keep, axis=0, keepdims=True) > 0.0).astype(jnp.int32)


def _cmp(slopes, qa, kvc, ga, ov, gexp, n_top):
    b, s, _ = qa.shape
    g = NSA_KV_GROUPS
    nq = s // Q_BLOCK
    ncp = kvc.shape[2]
    tq = CMP_TILES * Q_BLOCK
    nt = s // tq
    return pl.pallas_call(
        functools.partial(_cmp_kernel, n_top=n_top),
        grid=(b, g, nt),
        in_specs=[pl.BlockSpec(memory_space=pltpu.SMEM),
                  pl.BlockSpec((1, tq, QP_W), lambda bi, gi, i: (bi, i, gi)),
                  pl.BlockSpec((1, 1, ncp, LANES), lambda bi, gi, i: (bi, gi, 0, 0)),
                  pl.BlockSpec((1, tq, LANES), lambda bi, gi, i: (bi, i, 0)),
                  pl.BlockSpec((ncp, LANES), lambda bi, gi, i: (0, 0)),
                  pl.BlockSpec((1, 3, LANES, GROUP_W), lambda bi, gi, i: (gi, 0, 0, 0))],
        out_specs=[pl.BlockSpec((1, tq, GROUP_W), lambda bi, gi, i: (bi, i, gi)),
                   pl.BlockSpec((1, 1, tq, LANES), lambda bi, gi, i: (bi, gi, i, 0)),
                   pl.BlockSpec((CMP_TILES, 1, LANES), lambda bi, gi, i: ((bi * g + gi) * nt + i, 0, 0))],
        out_shape=[jax.ShapeDtypeStruct((b, s, g * GROUP_W), F32),
                   jax.ShapeDtypeStruct((b, g, s, LANES), BF16),
                   jax.ShapeDtypeStruct((b * g * nq, 1, LANES), jnp.int32)],
        scratch_shapes=[pltpu.VMEM((NSA_HPG, 8, ncp), F32),
                        pltpu.VMEM((Q_BLOCK, ncp), jnp.int32),
                        pltpu.VMEM((ncp, 2 * LANES), BF16)],
        compiler_params=_params(("parallel", "parallel", "arbitrary")),
        name="cmp",
    )(slopes, qa, kvc, ga, ov, gexp)


WIN_UNITS = NSA_WINDOW // Q_BLOCK + 1
LIST_UNROLL = 4
SLOT_CHUNK = 10


SELWIN_TILES = 4


def _selwin_kernel(pf_ref, slopes_ref, q_ref, kvs_ref, kvw_ref, sel_ref, ga_ref, yc_ref,
                    gexp_ref, ya_ref, sbuf, slot_unit, mrun, mrunw, acc, accw, *, nq):
    b = pl.program_id(0)
    g = pl.program_id(1)
    i = pl.program_id(2)
    hq = NSA_HPG
    qmk = _iota((Q_BLOCK, LANES), 0) - _iota((Q_BLOCK, LANES), 1)
    koff = _iota((1, LANES), 1).astype(F32)
    dmat = _iota((LANES, LANES), 0) - (_iota((LANES, LANES), 1) >> 6)
    dead = -(1 << 20)
    win_pairs = [(d, d) for d in range(WIN_UNITS)]

    def chunk_pairs(ci):
        return [(WIN_UNITS + t, WIN_UNITS + ci * SLOT_CHUNK + t) for t in range(SLOT_CHUNK)]

    def setup(k):
        it = SELWIN_TILES * i + k
        for d in range(WIN_UNITS):
            u = it - (WIN_UNITS - 1) + d
            slot_unit[k, d] = jnp.where(u >= 0, u, -1)
        base = ((b * NSA_KV_GROUPS + g) * nq + it) * LANES

        def list_body(j4, cnt):
            for t in range(LIST_UNROLL):
                jj = j4 * LIST_UNROLL + t
                used = (pf_ref[base + 2 * jj] | pf_ref[base + 2 * jj + 1]) != 0

                @pl.when(used)
                def _(jj=jj, cnt=cnt):
                    slot_unit[k, WIN_UNITS + cnt] = jj
                cnt = cnt + used.astype(jnp.int32)
            return cnt

        nsel = lax.fori_loop(0, it // LIST_UNROLL + 1, list_body, jnp.int32(0))
        for t in range(SLOT_CHUNK - 1):
            slot_unit[k, WIN_UNITS + nsel + t] = -1
        rows = slice(k * Q_BLOCK, (k + 1) * Q_BLOCK)
        return dict(k=k, it=it, rows=rows, nchunk=(nsel + (SLOT_CHUNK - 1)) // SLOT_CHUNK,
                    q4=_stack_heads(q_ref[0, rows, :]), sel=sel_ref[0, 0, rows, :])

    def kv_tile(kv_ref, k, li):
        u = jnp.maximum(slot_unit[k, li], 0)
        return kv_ref[0, pl.ds(pl.multiple_of(u * Q_BLOCK, Q_BLOCK), Q_BLOCK), :]

    def score_slots(t, kv_ref, pairs, window, mx):
        k = t["k"]
        units = [slot_unit[k, li] for _, li in pairs]
        s_all = _nt_dot(t["q4"], jnp.concatenate([kv_tile(kv_ref, k, li) for _, li in pairs], axis=0))
        if not window:
            spread = jnp.concatenate(
                [jnp.where(dmat == 2 * u, 1.0, 0.0).astype(BF16) for u in units], axis=1)
            chosen = _dot(t["sel"], spread)
        for n, ((bs, _), u) in enumerate(zip(pairs, units)):
            s = s_all[:, n * LANES:(n + 1) * LANES].reshape(hq, Q_BLOCK, LANES)
            off = jnp.where(u >= 0, (t["it"] - u) * Q_BLOCK, dead)
            dist = qmk + off
            if window:
                allow = (dist >= 0) & (dist < NSA_WINDOW)
            else:
                allow = (dist >= 0) & (chosen[:, n * LANES:(n + 1) * LANES] > 0.5)
            neg = jnp.where(allow, 0.0, NEG_INF)
            kd = koff - off.astype(F32)
            for h in range(hq):
                sb = s[h] + (neg + slopes_ref[hq * g + h] * kd)
                sbuf[k, bs, h] = sb
                mx[k, h] = jnp.maximum(mx[k, h], sb)

    def score_pairs(t, kv_ref, pairs, window, mx):
        for n in range(0, len(pairs), 2):
            score_slots(t, kv_ref, pairs[n:n + 2], window, mx)

    def row_max(mx, k):
        return jnp.broadcast_to(jnp.max(mx[k], axis=-1, keepdims=True), mx.shape[1:])

    def weighted_values(t, kv_ref, pairs, mx):
        k = t["k"]
        tot = None
        for n in range(0, len(pairs), 2):
            two = pairs[n:n + 2]
            p = jnp.concatenate(
                [jnp.exp2(sbuf[k, bs] - mx[k]).astype(BF16).reshape(hq * Q_BLOCK, LANES)
                 for bs, _ in two], axis=1)
            v = jnp.concatenate([_ones_in_k_lanes(kv_tile(kv_ref, k, li)) for _, li in two], axis=0)
            d = _dot(p, v)
            tot = d if tot is None else tot + d
        return tot

    tiles = [setup(k) for k in range(SELWIN_TILES)]

    for t in tiles:
        k = t["k"]
        mrunw[k] = jnp.full(mrunw.shape[1:], NEG_INF, F32)
        mrun[k] = jnp.full(mrun.shape[1:], NEG_INF, F32)
        score_pairs(t, kvw_ref, win_pairs, True, mrunw)
        score_pairs(t, kvs_ref, chunk_pairs(0), False, mrun)
    for t in tiles:
        k = t["k"]
        mrunw[k] = row_max(mrunw, k)
        mrun[k] = row_max(mrun, k)
        accw[k] = weighted_values(t, kvw_ref, win_pairs, mrunw)
        acc[k] = weighted_values(t, kvs_ref, chunk_pairs(0), mrun)

    for t in tiles:
        k = t["k"]

        def more(ci, carry, t=t, k=k):
            mrunw[k] = jnp.full(mrunw.shape[1:], NEG_INF, F32)
            score_pairs(t, kvs_ref, chunk_pairs(ci), False, mrunw)
            m_old = mrun[k]
            m_new = jnp.maximum(m_old, row_max(mrunw, k))
            mrun[k] = m_new
            alpha = jnp.exp2(m_old - m_new).reshape(hq * Q_BLOCK, LANES)
            acc[k] = acc[k] * alpha + weighted_values(t, kvs_ref, chunk_pairs(ci), mrun)
            return carry

        lax.fori_loop(1, t["nchunk"], more, 0)

    for t in tiles:
        k = t["k"]
        rows = t["rows"]
        o_s = _normalise_unstack(acc[k])
        o_w = _normalise_unstack(accw[k])
        gates = ga_ref[0, rows, :]
        ya = (yc_ref[0, rows, :] + _expand_gate(gates, gexp_ref[0, 1]) * o_s
              + _expand_gate(gates, gexp_ref[0, 2]) * o_w)
        ya_ref[0, rows, :] = ya.astype(BF16)


def _selwin(pflags, slopes, qa, kva, sel, ga, yc, gexp):
    b, s, _ = qa.shape
    g = NSA_KV_GROUPS
    nq = s // Q_BLOCK
    tq = SELWIN_TILES * Q_BLOCK
    nlist = WIN_UNITS + nq + SLOT_CHUNK - 1
    rows = NSA_HPG * Q_BLOCK
    stat = (SELWIN_TILES, NSA_HPG, Q_BLOCK, LANES)
    grid_spec = pltpu.PrefetchScalarGridSpec(
        num_scalar_prefetch=1,
        grid=(b, g, s // tq),
        in_specs=[pl.BlockSpec(memory_space=pltpu.SMEM),
                  pl.BlockSpec((1, tq, QP_W), lambda bi, gi, i, pf: (bi, i, gi)),
                  pl.BlockSpec((1, s, LANES), lambda bi, gi, i, pf: (bi, 0, gi)),
                  pl.BlockSpec((1, s, LANES), lambda bi, gi, i, pf: (bi, 0, NSA_KV_GROUPS + gi)),
                  pl.BlockSpec((1, 1, tq, LANES), lambda bi, gi, i, pf: (bi, gi, i, 0)),
                  pl.BlockSpec((1, tq, LANES), lambda bi, gi, i, pf: (bi, i, 0)),
                  pl.BlockSpec((1, tq, GROUP_W), lambda bi, gi, i, pf: (bi, i, gi)),
                  pl.BlockSpec((1, 3, LANES, GROUP_W), lambda bi, gi, i, pf: (gi, 0, 0, 0))],
        out_specs=pl.BlockSpec((1, tq, GROUP_W), lambda bi, gi, i, pf: (bi, i, gi)),
        scratch_shapes=[pltpu.VMEM((SELWIN_TILES, WIN_UNITS + SLOT_CHUNK) + stat[1:], F32),
                        pltpu.SMEM((SELWIN_TILES, nlist), jnp.int32),
                        pltpu.VMEM(stat, F32),
                        pltpu.VMEM(stat, F32),
                        pltpu.VMEM((SELWIN_TILES, rows, LANES), F32),
                        pltpu.VMEM((SELWIN_TILES, rows, LANES), F32)],
    )
    return pl.pallas_call(
        functools.partial(_selwin_kernel, nq=nq),
        grid_spec=grid_spec,
        out_shape=jax.ShapeDtypeStruct((b, s, g * GROUP_W), BF16),
        compiler_params=_params(("parallel", "parallel", "arbitrary")),
        name="selwin",
    )(pflags, slopes, qa, kva, kva, sel, ga, yc, gexp)


def _dil_bias(slopes):
    c = DIL_BLOCK
    qi = np.arange(c)[:, None]
    ki = np.arange(2 * c)[None, :]
    delta = qi + c - ki
    out = []
    for k, (window, dil) in enumerate(DIL_GROUPS):
        in_band = jnp.asarray((delta >= 0) & (delta <= window // dil))
        dist = jnp.asarray((delta * dil).astype(np.float32))
        for h in range(DIL_HPG):
            out.append(jnp.where(in_band, -(slopes[DIL_HPG * k + h] * dist) * LOG2E, NEG_INF))
    return jnp.stack(out).reshape(len(out) // 2, 2 * c, 2 * c)


def _dil_group(k, dil, sup, q_ref, kvc_ref, kvp_ref, bias_ref, mx_ref, w_ref, y_ref):
    c = DIL_BLOCK
    nblk = sup // (c * dil)
    has_halo = pl.program_id(1) > 0
    nch = DIL_HPG * HEAD_DIM // LANES
    upper = _iota((c, LANES), 1) >= HEAD_DIM

    def block(idx):
        r = idx // nblk
        n = idx - r * nblk
        start = r + dil * c * n
        rows = pl.ds(start, c, stride=dil)
        rows_prev = pl.ds(jnp.maximum(start - dil * c, r), c, stride=dil)
        rows_halo = pl.ds(r + dil * c * (nblk - 1), c, stride=dil)
        pen = jnp.where(jnp.logical_or(n >= 1, has_halo), 0.0, NEG_INF)

        def prev(ch):
            return jnp.where(n >= 1, kvc_ref[ch, 0, rows_prev, :], kvp_ref[ch, 0, rows_halo, :]).astype(BF16)

        for ch in range(nch):
            qb = q_ref[ch, 0, rows, :].astype(BF16)
            zero = jnp.zeros_like(qb)
            qs = jnp.concatenate([jnp.where(upper, zero, qb), jnp.where(upper, qb, zero)], axis=0)
            kband = jnp.concatenate([prev(ch), kvc_ref[ch, 0, rows, :].astype(BF16)], axis=0)
            vband = jnp.concatenate([prev(nch + ch), kvc_ref[nch + ch, 0, rows, :].astype(BF16)], axis=0)
            s = _nt_dot(qs, kband) + bias_ref[ch]
            s = jnp.concatenate([s[:, :c] + pen, s[:, c:]], axis=1)
            m = jnp.max(s, axis=-1, keepdims=True)
            e = jnp.exp2(s - m).astype(BF16)
            ones = jnp.ones((2 * c, LANES), BF16)
            ol = _dot(e, jnp.concatenate([vband, ones], axis=1))
            lt = jnp.maximum(ol[:, LANES:], TINY)
            o2 = ol[:, :LANES] * (1.0 / lt)
            l2 = m * LN2 + jnp.log(lt)
            o = jnp.where(upper, o2[c:], o2[:c])
            lse = jnp.where(upper, l2[c:], l2[:c])
            if k == 0:
                mx_ref[ch, rows, :] = lse
                w_ref[ch, rows, :] = jnp.ones_like(lse)
                y_ref[ch, rows, :] = o
            else:
                mx_old = mx_ref[ch, rows, :]
                mx_new = jnp.maximum(mx_old, lse)
                a = jnp.exp(mx_old - mx_new)
                bnew = jnp.exp(lse - mx_new)
                mx_ref[ch, rows, :] = mx_new
                w_ref[ch, rows, :] = w_ref[ch, rows, :] * a + bnew
                y_ref[ch, rows, :] = y_ref[ch, rows, :] * a + bnew * o

    def body(t, carry):
        for j in range(DIL_UNROLL):
            block(t * DIL_UNROLL + j)
        return carry

    lax.fori_loop(0, sup // (c * DIL_UNROLL), body, 0)


def _dil_kernel(q_ref, kvc_ref, kvp_ref, bias_ref, yb_ref, mx_ref, w_ref, y_ref, *, sup):
    gi = pl.program_id(2)
    for k, (_, dil) in enumerate(DIL_GROUPS):
        @pl.when(gi == k)
        def _(k=k, dil=dil):
            _dil_group(k, dil, sup, q_ref, kvc_ref, kvp_ref, bias_ref, mx_ref, w_ref, y_ref)

    @pl.when(gi == len(DIL_GROUPS) - 1)
    def _():
        for ch in range(DIL_HPG * HEAD_DIM // LANES):
            yb_ref[0, :, ch * LANES:(ch + 1) * LANES] = (y_ref[ch] * (1.0 / w_ref[ch])).astype(BF16)


def _dil(qb, kvb, bias, sup):
    _, b, s, _ = qb.shape
    hw = DIL_HPG * HEAD_DIM
    nch = hw // LANES
    ng = len(DIL_GROUPS)
    return pl.pallas_call(
        functools.partial(_dil_kernel, sup=sup),
        grid=(b, s // sup, ng),
        in_specs=[pl.BlockSpec((nch, 1, sup, LANES), lambda bi, si, gi: (gi, bi, si, 0)),
                  pl.BlockSpec((2 * nch, 1, sup, LANES), lambda bi, si, gi: (gi, bi, si, 0)),
                  pl.BlockSpec((2 * nch, 1, sup, LANES),
                               lambda bi, si, gi: (gi, bi, jnp.maximum(si - 1, 0), 0)),
                  pl.BlockSpec((nch, 2 * DIL_BLOCK, 2 * DIL_BLOCK), lambda bi, si, gi: (gi, 0, 0))],
        out_specs=pl.BlockSpec((1, sup, hw), lambda bi, si, gi: (bi, si, 0)),
        out_shape=jax.ShapeDtypeStruct((b, s, hw), BF16),
        scratch_shapes=[pltpu.VMEM((nch, sup, LANES), F32), pltpu.VMEM((nch, sup, LANES), F32),
                        pltpu.VMEM((nch, sup, LANES), F32)],
        compiler_params=_params(("parallel", "parallel", "arbitrary")),
        name="dil",
    )(qb, kvb, kvb, bias)


def _merge_kernel(x_ref, g_ref, ya_ref, yb_ref, wg_ref, ua_ref, ub_ref, wo_ref, o_ref):
    x = x_ref[...]
    d = x.shape[1]
    hn = _rms(x, g_ref[...]).astype(BF16)
    gate_a = jax.nn.sigmoid(_dot(hn, wg_ref[:, 0:d]))
    gate_b = jax.nn.sigmoid(_dot(hn, wg_ref[:, d:2 * d]))
    merged = gate_a * _dot(ya_ref[...], ua_ref[...]) + gate_b * _dot(yb_ref[...], ub_ref[...])
    o_ref[...] = x + _dot(merged.astype(BF16), wo_ref[...])


def _merge(x2, g, ya, yb, wg, ua, ub, wo, tm):
    n, d = x2.shape
    row = lambda i: (i, 0)
    fixed = lambda i: (0, 0)
    return pl.pallas_call(
        _merge_kernel,
        grid=(n // tm,),
        in_specs=[pl.BlockSpec((tm, d), row), pl.BlockSpec((1, d), fixed),
                  pl.BlockSpec((tm, ya.shape[1]), row), pl.BlockSpec((tm, yb.shape[1]), row),
                  pl.BlockSpec(wg.shape, fixed), pl.BlockSpec(ua.shape, fixed),
                  pl.BlockSpec(ub.shape, fixed), pl.BlockSpec(wo.shape, fixed)],
        out_specs=pl.BlockSpec((tm, d), row),
        out_shape=jax.ShapeDtypeStruct((n, d), F32),
        compiler_params=_params(("parallel",)),
        name="merge",
    )(x2, g, ya, yb, wg, ua, ub, wo)


def _mixffn_kernel(x_ref, g_ref, ya_ref, yb_ref, wg_ref, ua_ref, ub_ref, wo_ref,
                   g2_ref, wi_ref, wf_ref, gf_ref, o_ref, *, tf, final_norm):
    x = x_ref[...]
    d = x.shape[1]
    hn = _rms(x, g_ref[...]).astype(BF16)
    gate_a = jax.nn.sigmoid(_dot(hn, wg_ref[:, 0:d]))
    gate_b = jax.nn.sigmoid(_dot(hn, wg_ref[:, d:2 * d]))
    merged = gate_a * _dot(ya_ref[...], ua_ref[...]) + gate_b * _dot(yb_ref[...], ub_ref[...])
    x = x + _dot(merged.astype(BF16), wo_ref[...])
    dff = wf_ref.shape[0]
    hn = _rms(x, g2_ref[...]).astype(BF16)
    y = x
    for c in range(0, dff, tf):
        gt = _dot(hn, wi_ref[:, c:c + tf])
        up = _dot(hn, wi_ref[:, dff + c:dff + c + tf])
        act = ((gt * jax.nn.sigmoid(gt)) * up).astype(BF16)
        y = y + _dot(act, wf_ref[c:c + tf, :])
    if final_norm:
        y = _rms(y, gf_ref[...])
    o_ref[...] = y


def _mixffn(x2, g, ya, yb, wg, ua, ub, wo, g2, w_in, w_out, gf, tm, tf, final_norm):
    n, d = x2.shape
    row = lambda i: (i, 0)
    fixed = lambda i: (0, 0)
    once = pl.Buffered(1)
    weights = [wg, ua, ub, wo]
    return pl.pallas_call(
        functools.partial(_mixffn_kernel, tf=tf, final_norm=final_norm),
        grid=(n // tm,),
        in_specs=([pl.BlockSpec((tm, d), row), pl.BlockSpec((1, d), fixed),
                   pl.BlockSpec((tm, ya.shape[1]), row), pl.BlockSpec((tm, yb.shape[1]), row)]
                  + [pl.BlockSpec(w.shape, fixed, pipeline_mode=once) for w in weights]
                  + [pl.BlockSpec((1, d), fixed),
                     pl.BlockSpec(w_in.shape, fixed, pipeline_mode=once),
                     pl.BlockSpec(w_out.shape, fixed, pipeline_mode=once),
                     pl.BlockSpec((1, d), fixed)]),
        out_specs=pl.BlockSpec((tm, d), row),
        out_shape=jax.ShapeDtypeStruct((n, d), F32),
        compiler_params=pltpu.CompilerParams(dimension_semantics=("parallel",),
                                             vmem_limit_bytes=MIXFFN_VMEM_LIMIT),
        name="mixffn",
    )(x2, g, ya, yb, wg, ua, ub, wo, g2, w_in, w_out, gf)


def _ffn_kernel(x_ref, g_ref, wi_ref, wo_ref, gf_ref, o_ref, *, tf, final_norm):
    x = x_ref[...]
    dff = wo_ref.shape[0]
    hn = _rms(x, g_ref[...]).astype(BF16)
    y = x
    for c in range(0, dff, tf):
        gt = _dot(hn, wi_ref[:, c:c + tf])
        up = _dot(hn, wi_ref[:, dff + c:dff + c + tf])
        act = ((gt * jax.nn.sigmoid(gt)) * up).astype(BF16)
        y = y + _dot(act, wo_ref[c:c + tf, :])
    if final_norm:
        y = _rms(y, gf_ref[...])
    o_ref[...] = y


def _ffn(x2, g, w_in, w_out, gf, tm, tf, final_norm):
    n, d = x2.shape
    fixed = lambda i: (0, 0)
    once = pl.Buffered(1)
    return pl.pallas_call(
        functools.partial(_ffn_kernel, tf=tf, final_norm=final_norm),
        grid=(n // tm,),
        in_specs=[pl.BlockSpec((tm, d), lambda i: (i, 0)),
                  pl.BlockSpec((1, d), fixed),
                  pl.BlockSpec(w_in.shape, fixed, pipeline_mode=once),
                  pl.BlockSpec(w_out.shape, fixed, pipeline_mode=once),
                  pl.BlockSpec((1, d), fixed)],
        out_specs=pl.BlockSpec((tm, d), lambda i: (i, 0)),
        out_shape=jax.ShapeDtypeStruct((n, d), F32),
        compiler_params=_params(("parallel",)),
        name="ffn",
    )(x2, g, w_in, w_out, gf)


def _alibi_slopes():
    k = jnp.arange(1, N_ALIBI + 1, dtype=F32)
    s = jnp.exp2(-8.0 * k / N_ALIBI)
    nsa = s[2 * DIL_HPG:2 * DIL_HPG + NSA_HEADS]
    dil = jnp.concatenate([s[:2 * DIL_HPG], s[2 * DIL_HPG + NSA_HEADS:]])
    return nsa, dil


def _overlap(seq, ncp):
    n_c = (seq - CMP_BLOCK) // CMP_STRIDE + 1
    n_s = seq // SEL_BLOCK
    c_start = np.arange(n_c) * CMP_STRIDE
    s_start = np.arange(n_s) * SEL_BLOCK
    ov = np.clip(np.minimum(c_start[:, None] + CMP_BLOCK, s_start[None, :] + SEL_BLOCK)
                 - np.maximum(c_start[:, None], s_start[None, :]), 0, None).astype(np.float32) / CMP_BLOCK
    out = np.zeros((ncp, LANES), np.float32)
    out[:n_c, :n_s] = ov
    return jnp.asarray(out, BF16)


def _proj_weights(w_in):
    dep, d, _ = w_in.shape
    scale = HEAD_DIM ** -0.5 * LOG2E
    c0 = NSA_HEADS * HEAD_DIM
    c1 = c0 + 3 * 2 * NSA_KV_GROUPS * HEAD_DIM
    c2 = c1 + NSA_HEADS * 3
    c3 = c2 + 3 * DIL_HEADS * HEAD_DIM
    qa = (w_in[..., :c0] * scale).astype(BF16)
    wb = w_in.astype(BF16)
    kva = wb[..., c0:c1].reshape(dep, d, 3, 2, NSA_KV_GROUPS, HEAD_DIM)
    kva = kva.transpose(0, 1, 2, 4, 3, 5).reshape(dep, d, CMP_W + KVA_W)
    ga = jnp.pad(wb[..., c1:c2], ((0, 0), (0, 0), (0, GA_W - (c2 - c1))))
    qkvb = w_in[..., c2:c3].reshape(dep, d, 3, len(DIL_GROUPS), DIL_HPG * HEAD_DIM)
    qb = (qkvb[:, :, 0] * scale).astype(BF16).reshape(dep, d, QB_W)
    kvb = qkvb[:, :, 1:3].astype(BF16).transpose(0, 1, 3, 2, 4).reshape(dep, d, KVB_W)
    w_a = jnp.concatenate([qa, kva, qb, kvb, ga], axis=-1)
    return w_a, wb[..., c3:]


def _compress_weights(pe_k, pe_v, w_ck1, w_ck2, w_cv1, w_cv2):
    dep = pe_k.shape[0]
    half = CMP_BLOCK // 2
    ck = w_ck1.reshape(dep, CMP_BLOCK, HEAD_DIM, CMP_HIDDEN)
    cv = w_cv1.reshape(dep, CMP_BLOCK, HEAD_DIM, CMP_HIDDEN)
    z = jnp.zeros_like(ck)
    w = jnp.concatenate([jnp.concatenate([ck, z], axis=-1),
                         jnp.concatenate([z, cv], axis=-1)], axis=2)
    feat = half * 2 * HEAD_DIM
    wa = w[:, :half].reshape(dep, feat, 2 * CMP_HIDDEN).astype(BF16)
    wb = w[:, half:].reshape(dep, feat, 2 * CMP_HIDDEN).astype(BF16)
    pe = jnp.concatenate([pe_k, pe_v], axis=-1)
    pe2 = jnp.stack([pe[:, :half].reshape(dep, feat), pe[:, half:].reshape(dep, feat)], axis=1)
    pe2 = jnp.pad(pe2, ((0, 0), (0, 6), (0, 0)))
    z2 = jnp.zeros_like(w_ck2)
    w2 = jnp.concatenate([jnp.concatenate([w_ck2, z2], axis=-1),
                          jnp.concatenate([z2, w_cv2], axis=-1)], axis=1).astype(BF16)
    return pe2, wa, wb, w2


def kernel(x, norm_mix, w_in, pe_k, pe_v, w_ck1, w_ck2, w_cv1, w_cv2, w_up_nsa, w_up_dil,
           w_out, norm_ffn, w_ffn_in, w_ffn_out, norm_final):
    b, s, d = x.shape
    depth = w_in.shape[0]
    n = b * s
    nq = s // Q_BLOCK
    nch = s // CMP_STRIDE
    n_s = s // SEL_BLOCK
    assert s % DIL_SUPER == 0 and n_s <= LANES
    n_top = min(SEL_TOPK, n_s)
    tm = 512
    tm_ffn = 512
    tf = w_ffn_out.shape[1] // 2

    slopes_nsa, slopes_dil = _alibi_slopes()
    slopes_nsa = slopes_nsa * LOG2E
    ov = _overlap(s, nch)
    gexp = _gate_spread()
    dbias = _dil_bias(slopes_dil)
    w_a, w_gm = _proj_weights(w_in)
    pe2, wa, wb, w2 = _compress_weights(pe_k, pe_v, w_ck1, w_ck2, w_cv1, w_cv2)
    ua = w_up_nsa.astype(BF16)
    ub = w_up_dil.astype(BF16)
    wo = w_out.astype(BF16)
    wfi = w_ffn_in.astype(BF16)
    wfo = w_ffn_out.astype(BF16)
    gf = norm_final.reshape(1, d)

    x2 = x.reshape(n, d)
    for l in range(depth):
        g_mix = norm_mix[l].reshape(1, d)
        qa, xc, kva, qb, kvb, ga = _proj(x2, g_mix, w_a[l], tm)
        qa = qa.reshape(b, s, QA_W)
        kva = kva.reshape(b, s, KVA_W)
        ga = ga.reshape(b, s, GA_W)
        kvc = _compress(xc, pe2[l], wa[l], wb[l], w2[l], b)
        yc, sel, flags = _cmp(slopes_nsa, qa, kvc, ga, ov, gexp, n_top)
        pflags = flags.reshape(-1)
        ya = _selwin(pflags, slopes_nsa, qa, kva, sel, ga, yc, gexp)
        yb = _dil(qb.reshape(-1, b, s, LANES), kvb.reshape(-1, b, s, LANES), dbias, DIL_SUPER)
        x2 = _mixffn(x2, g_mix, ya.reshape(n, NSA_HEADS * HEAD_DIM), yb.reshape(n, DIL_HPG * HEAD_DIM),
                     w_gm[l], ua[l], ub[l], wo[l], norm_ffn[l].reshape(1, d), wfi[l], wfo[l], gf,
                     tm_ffn, tf, final_norm=(l == depth - 1))
    return x2.reshape(b, s, d)
```

```python
import functools

import numpy as np
import jax
import jax.numpy as jnp
from jax import lax
from jax.experimental import pallas as pl
from jax.experimental.pallas import tpu as pltpu

F32 = jnp.float32
BF16 = jnp.bfloat16

HEAD_DIM = 64
NSA_HEADS = 8
NSA_KV_GROUPS = 2
NSA_HPG = NSA_HEADS // NSA_KV_GROUPS
CMP_BLOCK = 32
CMP_STRIDE = 16
CMP_HIDDEN = 256
SEL_BLOCK = 64
SEL_TOPK = 16
NSA_WINDOW = 512
Q_BLOCK = 128
FORCE_SCORE = 1e6
DIL_GROUPS = ((128, 1), (512, 4), (2048, 16))
DIL_HPG = 4
DIL_HEADS = DIL_HPG * len(DIL_GROUPS)
DIL_BLOCK = 128
N_ALIBI = NSA_HEADS + DIL_HEADS
EPS = 1e-6
NEG_INF = -1e30
TINY = 1e-30
LOG2E = 1.4426950408889634
LN2 = 0.6931471805599453

LANES = 128
GROUP_W = NSA_HPG * HEAD_DIM
DIL_SUPER = 2048
DIL_UNROLL = 4
CMP_TILES = 8
VMEM_LIMIT = 48 * 1024 * 1024
MIXFFN_VMEM_LIMIT = 56 * 1024 * 1024


def _dot(a, b):
    return jnp.dot(a, b, preferred_element_type=F32)


def _nt_dot(a, b):
    return lax.dot_general(a, b, (((1,), (1,)), ((), ())), preferred_element_type=F32)


def _iota(shape, dim):
    return lax.broadcasted_iota(jnp.int32, shape, dim)


def _rms(x, g):
    ms = jnp.mean(x * x, axis=-1, keepdims=True)
    return (x * lax.rsqrt(ms + EPS)) * g


def _params(sem):
    return pltpu.CompilerParams(dimension_semantics=sem, vmem_limit_bytes=VMEM_LIMIT)


QA_W = NSA_HEADS * LANES
QP_W = NSA_HPG * LANES
CMP_W = NSA_KV_GROUPS * LANES
KVA_W, QB_W, KVB_W, GA_W = 512, 768, 1536, 128
PROJ_COLS = NSA_HEADS * HEAD_DIM + CMP_W + KVA_W + QB_W + KVB_W + GA_W


def _proj_kernel(x_ref, g_ref, w_ref, qa_ref, xc_ref, kva_ref, qb_ref, kvb_ref, ga_ref, cmp_scr):
    y = _rms(x_ref[...], g_ref[...]).astype(BF16)
    c = 0
    low = _iota((x_ref.shape[0], LANES), 1) < HEAD_DIM
    for p in range(0, NSA_HEADS // 2, 2):
        r2 = _dot(y, w_ref[:, c:c + 2 * LANES])
        c += 2 * LANES
        for k in range(2):
            r = r2[:, k * LANES:(k + 1) * LANES]
            h0 = 2 * (p + k)
            qa_ref[:, h0 * LANES:(h0 + 1) * LANES] = jnp.where(low, r, 0.0).astype(BF16)
            qa_ref[:, (h0 + 1) * LANES:(h0 + 2) * LANES] = (
                jnp.where(low, pltpu.roll(r, HEAD_DIM, 1), 0.0).astype(BF16))
    rc = _dot(y, w_ref[:, c:c + CMP_W])
    c += CMP_W
    nrow = xc_ref.shape[1]
    for gi in range(NSA_KV_GROUPS):
        cmp_scr[gi] = rc[:, gi * LANES:(gi + 1) * LANES]
        for l in range(CMP_STRIDE):
            xc_ref[gi, :, l * LANES:(l + 1) * LANES] = (
                cmp_scr[gi, pl.ds(l, nrow, stride=CMP_STRIDE), :].astype(BF16))
    kva_ref[...] = _dot(y, w_ref[:, c:c + KVA_W]).astype(BF16)
    c += KVA_W
    for j in range(0, QB_W // LANES, 2):
        r = _dot(y, w_ref[:, c:c + 2 * LANES])
        qb_ref[j] = r[:, :LANES]
        qb_ref[j + 1] = r[:, LANES:]
        c += 2 * LANES
    for j in range(0, KVB_W // LANES, 2):
        r = _dot(y, w_ref[:, c:c + 2 * LANES])
        kvb_ref[j] = r[:, :LANES]
        kvb_ref[j + 1] = r[:, LANES:]
        c += 2 * LANES
    ga_ref[...] = jax.nn.sigmoid(_dot(y, w_ref[:, c:c + GA_W]))


def _layer_spec(w, layer):
    return pl.BlockSpec((None,) + w.shape[1:], lambda i: (layer, 0, 0), pipeline_mode=pl.Buffered(1))


def _proj(x2, g, w, layer, tm):
    n, d = x2.shape
    row = lambda i: (i, 0)
    fixed = lambda i: (0, 0)
    return pl.pallas_call(
        _proj_kernel,
        grid=(n // tm,),
        in_specs=[pl.BlockSpec((tm, d), row), pl.BlockSpec((1, d), fixed), _layer_spec(w, layer)],
        out_specs=[pl.BlockSpec((tm, QA_W), row),
                   pl.BlockSpec((NSA_KV_GROUPS, tm // CMP_STRIDE, CMP_STRIDE * LANES), lambda i: (0, i, 0)),
                   pl.BlockSpec((tm, KVA_W), row),
                   pl.BlockSpec((QB_W // LANES, tm, LANES), lambda i: (0, i, 0)),
                   pl.BlockSpec((KVB_W // LANES, tm, LANES), lambda i: (0, i, 0)),
                   pl.BlockSpec((tm, GA_W), row)],
        out_shape=[jax.ShapeDtypeStruct((n, QA_W), BF16),
                   jax.ShapeDtypeStruct((NSA_KV_GROUPS, n // CMP_STRIDE, CMP_STRIDE * LANES), BF16),
                   jax.ShapeDtypeStruct((n, KVA_W), BF16),
                   jax.ShapeDtypeStruct((QB_W // LANES, n, LANES), F32),
                   jax.ShapeDtypeStruct((KVB_W // LANES, n, LANES), F32),
                   jax.ShapeDtypeStruct((n, GA_W), F32)],
        scratch_shapes=[pltpu.VMEM((NSA_KV_GROUPS, tm, LANES), F32)],
        compiler_params=_params(("parallel",)),
        name="proj",
    )(x2, g, w)


def _compress_kernel(x_ref, pe_ref, wa_ref, wb_ref, w2_ref, o_ref):
    xf = x_ref[0].astype(F32)
    nch = xf.shape[0]
    xa = (xf + pe_ref[0:1, :]).astype(BF16)
    xb = (xf + pe_ref[1:2, :]).astype(BF16)
    ya = _dot(xa, wa_ref[...])
    yb = _dot(xb, wb_ref[...])
    h = ya + pltpu.roll(yb, nch - 1, 0)
    a = (h * jax.nn.sigmoid(h)).astype(BF16)
    o_ref[0, 0] = _dot(a, w2_ref[...]).astype(BF16)


def _compress(xc, pe, wa, wb, w2, b):
    g, rows, f = xc.shape
    nch = rows // b
    hid = wa.shape[1]
    return pl.pallas_call(
        _compress_kernel,
        grid=(b, g),
        in_specs=[pl.BlockSpec((1, nch, f), lambda i, j: (j, i, 0)),
                  pl.BlockSpec((8, f), lambda i, j: (0, 0)),
                  pl.BlockSpec((f, hid), lambda i, j: (0, 0)),
                  pl.BlockSpec((f, hid), lambda i, j: (0, 0)),
                  pl.BlockSpec((hid, LANES), lambda i, j: (0, 0))],
        out_specs=pl.BlockSpec((1, 1, nch, LANES), lambda i, j: (i, j, 0, 0)),
        out_shape=jax.ShapeDtypeStruct((b, g, nch, LANES), BF16),
        compiler_params=_params(("parallel", "parallel")),
        name="compress",
    )(xc, pe, wa, wb, w2)


def _gate_spread():
    g = np.arange(NSA_KV_GROUPS)[:, None, None, None]
    br = np.arange(3)[None, :, None, None]
    r = np.arange(LANES)[None, None, :, None]
    c = np.arange(GROUP_W)[None, None, None, :]
    return jnp.asarray(r == 3 * (NSA_HPG * g + c // HEAD_DIM) + br, BF16)


def _stack_heads(q):
    return jnp.concatenate([q[:, LANES * h:LANES * (h + 1)] for h in range(NSA_HPG)], axis=0)


def _unstack_heads(o4):
    lane = _iota((Q_BLOCK, LANES), 1)
    o = [o4[Q_BLOCK * h:Q_BLOCK * (h + 1)] for h in range(NSA_HPG)]
    c0 = jnp.where(lane < HEAD_DIM, pltpu.roll(o[0], HEAD_DIM, 1), o[1])
    c1 = jnp.where(lane < HEAD_DIM, pltpu.roll(o[2], HEAD_DIM, 1), o[3])
    return jnp.concatenate([c0, c1], axis=1)


def _row_sums_all_lanes(a):
    lane = _iota(a.shape, 1)
    return jnp.where(lane < HEAD_DIM, a, pltpu.roll(a, HEAD_DIM, 1))


def _normalise_unstack(a):
    return _unstack_heads(a * (1.0 / jnp.maximum(_row_sums_all_lanes(a), TINY)))


def _expand_gate(gates, spread):
    hi = gates.astype(BF16)
    lo = (gates - hi.astype(F32)).astype(BF16)
    return _dot(hi, spread) + _dot(lo, spread)


def _ones_in_k_lanes(kv):
    lane = _iota(kv.shape, 1)
    return jnp.where(lane < HEAD_DIM, jnp.ones_like(kv), kv)


def _cmp_kernel(slopes_ref, q_ref, kvc_ref, ga_ref, ov_ref, gexp_ref,
                yc_ref, sel_ref, fl_ref, bias0, cq, rhs, *, n_top):
    g = pl.program_id(1)
    i = pl.program_id(2)
    hq = NSA_HPG
    ncp = cq.shape[1]

    @pl.when(i == 0)
    def _():
        kpos = CMP_STRIDE * _iota((Q_BLOCK, ncp), 1) + (CMP_BLOCK - 1)
        cq[...] = kpos - _iota((Q_BLOCK, ncp), 0)
        kf = kpos[0:8].astype(F32)
        for h in range(hq):
            bias0[h] = slopes_ref[hq * g + h] * kf
        rhs[:, 0:LANES] = _ones_in_k_lanes(kvc_ref[0, 0])
        rhs[:, LANES:2 * LANES] = ov_ref[...]

    for k in range(CMP_TILES):
        it = CMP_TILES * i + k
        rows = slice(k * Q_BLOCK, (k + 1) * Q_BLOCK)
        q4 = _stack_heads(q_ref[0, rows, :])
        s = _nt_dot(q4, kvc_ref[0, 0]).reshape(hq, Q_BLOCK, ncp)
        t0 = it * Q_BLOCK
        neg = jnp.where(cq[...] <= t0, 0.0, NEG_INF)
        tf = t0.astype(F32)
        ps = []
        for h in range(hq):
            sb = s[h] + (neg + (bias0[h, 0:1] - slopes_ref[hq * g + h] * tf))
            m = jnp.max(sb, axis=-1, keepdims=True)
            m = jnp.where(m < 0.1 * NEG_INF, -NEG_INF, m)
            ps.append(jnp.exp2(sb - m).astype(BF16))
        big = _dot(jnp.concatenate(ps, axis=0), rhs[...])
        inv = 1.0 / jnp.maximum(_row_sums_all_lanes(big[:, 0:LANES]), TINY)
        oc = _unstack_heads(big[:, 0:LANES] * inv)
        yc_ref[0, rows, :] = _expand_gate(ga_ref[0, rows, :], gexp_ref[0, 0]) * oc
        pov = big[:, LANES:2 * LANES] * inv
        imp = pov[0:Q_BLOCK]
        for h in range(1, hq):
            imp = imp + pov[Q_BLOCK * h:Q_BLOCK * (h + 1)]

        qrow = _iota((Q_BLOCK, LANES), 0)
        j = _iota((Q_BLOCK, LANES), 1)
        jt = 2 * it + (qrow >> 6)
        valid = j <= jt
        forced = (j == 0) | (j == jt) | (j == jt - 1)
        score = jnp.where(forced, FORCE_SCORE, jnp.where(valid, imp, -1.0))
        st = score.T
        jr = _iota((LANES, Q_BLOCK), 0)
        cur = st
        taken = -3e38
        for _ in range(n_top):
            mx = jnp.max(cur, axis=0, keepdims=True)
            first = jnp.min(jnp.where(cur == mx, jr, LANES), axis=0, keepdims=True)
            cur = jnp.where(jr == first, taken, cur)
        keep = jnp.where((cur == taken) & (st >= 0.0), 1.0, 0.0).T
        sel_ref[0, 0, rows, :] = keep.astype(BF16)

        fl_ref[k] = (jnp.max(keep, axis=0, keepdims=True) > 0.0).astype(jnp.int32)


def _cmp(slopes, qa, kvc, ga, ov, gexp, n_top):
    b, s, _ = qa.shape
    g = NSA_KV_GROUPS
    nq = s // Q_BLOCK
    ncp = kvc.shape[2]
    tq = CMP_TILES * Q_BLOCK
    nt = s // tq
    return pl.pallas_call(
        functools.partial(_cmp_kernel, n_top=n_top),
        grid=(b, g, nt),
        in_specs=[pl.BlockSpec(memory_space=pltpu.SMEM),
                  pl.BlockSpec((1, tq, QP_W), lambda bi, gi, i: (bi, i, gi)),
                  pl.BlockSpec((1, 1, ncp, LANES), lambda bi, gi, i: (bi, gi, 0, 0)),
                  pl.BlockSpec((1, tq, LANES), lambda bi, gi, i: (bi, i, 0)),
                  pl.BlockSpec((ncp, LANES), lambda bi, gi, i: (0, 0)),
                  pl.BlockSpec((1, 3, LANES, GROUP_W), lambda bi, gi, i: (gi, 0, 0, 0))],
        out_specs=[pl.BlockSpec((1, tq, GROUP_W), lambda bi, gi, i: (bi, i, gi)),
                   pl.BlockSpec((1, 1, tq, LANES), lambda bi, gi, i: (bi, gi, i, 0)),
                   pl.BlockSpec((CMP_TILES, 1, LANES), lambda bi, gi, i: ((bi * g + gi) * nt + i, 0, 0))],
        out_shape=[jax.ShapeDtypeStruct((b, s, g * GROUP_W), F32),
                   jax.ShapeDtypeStruct((b, g, s, LANES), BF16),
                   jax.ShapeDtypeStruct((b * g * nq, 1, LANES), jnp.int32)],
        scratch_shapes=[pltpu.VMEM((NSA_HPG, 8, ncp), F32),
                        pltpu.VMEM((Q_BLOCK, ncp), jnp.int32),
                        pltpu.VMEM((ncp, 2 * LANES), BF16)],
        compiler_params=_params(("parallel", "parallel", "arbitrary")),
        name="cmp",
    )(slopes, qa, kvc, ga, ov, gexp)


WIN_UNITS = NSA_WINDOW // Q_BLOCK + 1
LIST_UNROLL = 4
SLOT_CHUNK = 10


SELWIN_TILES = 4


def _selwin_kernel(pf_ref, slopes_ref, q_ref, kvs_ref, kvw_ref, sel_ref, ga_ref, yc_ref,
                    gexp_ref, ya_ref, sbuf, slot_unit, mrun, mrunw, acc, accw, *, nq):
    b = pl.program_id(0)
    g = pl.program_id(1)
    i = pl.program_id(2)
    hq = NSA_HPG
    qmk = _iota((Q_BLOCK, LANES), 0) - _iota((Q_BLOCK, LANES), 1)
    koff = _iota((1, LANES), 1).astype(F32)
    dmat = _iota((LANES, LANES), 0) - (_iota((LANES, LANES), 1) >> 6)
    dead = -(1 << 20)
    win_pairs = [(d, d) for d in range(WIN_UNITS)]

    def chunk_pairs(ci):
        return [(WIN_UNITS + t, WIN_UNITS + ci * SLOT_CHUNK + t) for t in range(SLOT_CHUNK)]

    def setup(k):
        it = SELWIN_TILES * i + k
        for d in range(WIN_UNITS):
            u = it - (WIN_UNITS - 1) + d
            slot_unit[k, d] = jnp.where(u >= 0, u, -1)
        base = ((b * NSA_KV_GROUPS + g) * nq + it) * LANES

        def list_body(j4, cnt):
            for t in range(LIST_UNROLL):
                jj = j4 * LIST_UNROLL + t
                used = (pf_ref[base + 2 * jj] | pf_ref[base + 2 * jj + 1]) != 0

                @pl.when(used)
                def _(jj=jj, cnt=cnt):
                    slot_unit[k, WIN_UNITS + cnt] = jj
                cnt = cnt + used.astype(jnp.int32)
            return cnt

        nsel = lax.fori_loop(0, it // LIST_UNROLL + 1, list_body, jnp.int32(0))
        for t in range(SLOT_CHUNK - 1):
            slot_unit[k, WIN_UNITS + nsel + t] = -1
        rows = slice(k * Q_BLOCK, (k + 1) * Q_BLOCK)
        return dict(k=k, it=it, rows=rows, nchunk=(nsel + (SLOT_CHUNK - 1)) // SLOT_CHUNK,
                    q4=_stack_heads(q_ref[0, rows, :]), sel=sel_ref[0, 0, rows, :])

    def kv_tile(kv_ref, k, li):
        u = jnp.maximum(slot_unit[k, li], 0)
        return kv_ref[0, pl.ds(pl.multiple_of(u * Q_BLOCK, Q_BLOCK), Q_BLOCK), :]

    def score_slots(t, kv_ref, pairs, window, mx):
        k = t["k"]
        units = [slot_unit[k, li] for _, li in pairs]
        s_all = _nt_dot(t["q4"], jnp.concatenate([kv_tile(kv_ref, k, li) for _, li in pairs], axis=0))
        if not window:
            spread = jnp.concatenate(
                [jnp.where(dmat == 2 * u, 1.0, 0.0).astype(BF16) for u in units], axis=1)
            chosen = _dot(t["sel"], spread)
        for n, ((bs, _), u) in enumerate(zip(pairs, units)):
            s = s_all[:, n * LANES:(n + 1) * LANES].reshape(hq, Q_BLOCK, LANES)
            off = jnp.where(u >= 0, (t["it"] - u) * Q_BLOCK, dead)
            dist = qmk + off
            if window:
                allow = (dist >= 0) & (dist < NSA_WINDOW)
            else:
                allow = (dist >= 0) & (chosen[:, n * LANES:(n + 1) * LANES] > 0.5)
            neg = jnp.where(allow, 0.0, NEG_INF)
            kd = koff - off.astype(F32)
            for h in range(hq):
                sb = s[h] + (neg + slopes_ref[hq * g + h] * kd)
                sbuf[k, bs, h] = sb
                mx[k, h] = jnp.maximum(mx[k, h], sb)

    def score_pairs(t, kv_ref, pairs, window, mx):
        for n in range(0, len(pairs), 2):
            score_slots(t, kv_ref, pairs[n:n + 2], window, mx)

    def row_max(mx, k):
        return jnp.broadcast_to(jnp.max(mx[k], axis=-1, keepdims=True), mx.shape[1:])

    def weighted_values(t, kv_ref, pairs, mx):
        k = t["k"]
        tot = None
        for n in range(0, len(pairs), 2):
            two = pairs[n:n + 2]
            p = jnp.concatenate(
                [jnp.exp2(sbuf[k, bs] - mx[k]).astype(BF16).reshape(hq * Q_BLOCK, LANES)
                 for bs, _ in two], axis=1)
            v = jnp.concatenate([_ones_in_k_lanes(kv_tile(kv_ref, k, li)) for _, li in two], axis=0)
            d = _dot(p, v)
            tot = d if tot is None else tot + d
        return tot

    tiles = [setup(k) for k in range(SELWIN_TILES)]

    for t in tiles:
        k = t["k"]
        mrunw[k] = jnp.full(mrunw.shape[1:], NEG_INF, F32)
        mrun[k] = jnp.full(mrun.shape[1:], NEG_INF, F32)
        score_pairs(t, kvw_ref, win_pairs, True, mrunw)
        score_pairs(t, kvs_ref, chunk_pairs(0), False, mrun)
    for t in tiles:
        k = t["k"]
        mrunw[k] = row_max(mrunw, k)
        mrun[k] = row_max(mrun, k)
        accw[k] = weighted_values(t, kvw_ref, win_pairs, mrunw)
        acc[k] = weighted_values(t, kvs_ref, chunk_pairs(0), mrun)

    for t in tiles:
        k = t["k"]

        def more(ci, carry, t=t, k=k):
            mrunw[k] = jnp.full(mrunw.shape[1:], NEG_INF, F32)
            score_pairs(t, kvs_ref, chunk_pairs(ci), False, mrunw)
            m_old = mrun[k]
            m_new = jnp.maximum(m_old, row_max(mrunw, k))
            mrun[k] = m_new
            alpha = jnp.exp2(m_old - m_new).reshape(hq * Q_BLOCK, LANES)
            acc[k] = acc[k] * alpha + weighted_values(t, kvs_ref, chunk_pairs(ci), mrun)
            return carry

        lax.fori_loop(1, t["nchunk"], more, 0)

    for t in tiles:
        k = t["k"]
        rows = t["rows"]
        o_s = _normalise_unstack(acc[k])
        o_w = _normalise_unstack(accw[k])
        gates = ga_ref[0, rows, :]
        ya = (yc_ref[0, rows, :] + _expand_gate(gates, gexp_ref[0, 1]) * o_s
              + _expand_gate(gates, gexp_ref[0, 2]) * o_w)
        ya_ref[0, rows, :] = ya.astype(BF16)


def _selwin(pflags, slopes, qa, kva, sel, ga, yc, gexp):
    b, s, _ = qa.shape
    g = NSA_KV_GROUPS
    nq = s // Q_BLOCK
    tq = SELWIN_TILES * Q_BLOCK
    nlist = WIN_UNITS + nq + SLOT_CHUNK - 1
    rows = NSA_HPG * Q_BLOCK
    stat = (SELWIN_TILES, NSA_HPG, Q_BLOCK, LANES)
    grid_spec = pltpu.PrefetchScalarGridSpec(
        num_scalar_prefetch=1,
        grid=(b, g, s // tq),
        in_specs=[pl.BlockSpec(memory_space=pltpu.SMEM),
                  pl.BlockSpec((1, tq, QP_W), lambda bi, gi, i, pf: (bi, i, gi)),
                  pl.BlockSpec((1, s, LANES), lambda bi, gi, i, pf: (bi, 0, gi)),
                  pl.BlockSpec((1, s, LANES), lambda bi, gi, i, pf: (bi, 0, NSA_KV_GROUPS + gi)),
                  pl.BlockSpec((1, 1, tq, LANES), lambda bi, gi, i, pf: (bi, gi, i, 0)),
                  pl.BlockSpec((1, tq, LANES), lambda bi, gi, i, pf: (bi, i, 0)),
                  pl.BlockSpec((1, tq, GROUP_W), lambda bi, gi, i, pf: (bi, i, gi)),
                  pl.BlockSpec((1, 3, LANES, GROUP_W), lambda bi, gi, i, pf: (gi, 0, 0, 0))],
        out_specs=pl.BlockSpec((1, tq, GROUP_W), lambda bi, gi, i, pf: (bi, i, gi)),
        scratch_shapes=[pltpu.VMEM((SELWIN_TILES, WIN_UNITS + SLOT_CHUNK) + stat[1:], F32),
                        pltpu.SMEM((SELWIN_TILES, nlist), jnp.int32),
                        pltpu.VMEM(stat, F32),
                        pltpu.VMEM(stat, F32),
                        pltpu.VMEM((SELWIN_TILES, rows, LANES), F32),
                        pltpu.VMEM((SELWIN_TILES, rows, LANES), F32)],
    )
    return pl.pallas_call(
        functools.partial(_selwin_kernel, nq=nq),
        grid_spec=grid_spec,
        out_shape=jax.ShapeDtypeStruct((b, s, g * GROUP_W), BF16),
        compiler_params=_params(("parallel", "parallel", "arbitrary")),
        name="selwin",
    )(pflags, slopes, qa, kva, kva, sel, ga, yc, gexp)


def _dil_bias(slopes):
    c = DIL_BLOCK
    qi = np.arange(c)[:, None]
    ki = np.arange(2 * c)[None, :]
    delta = qi + c - ki
    groups = []
    for k, (window, dil) in enumerate(DIL_GROUPS):
        dist = jnp.asarray((delta * dil).astype(np.float32))
        variants = []
        for has_prev in (False, True):
            in_band = jnp.asarray((delta >= 0) & (delta <= window // dil) & ((ki >= c) | has_prev))
            heads = [jnp.where(in_band, -(slopes[DIL_HPG * k + h] * dist) * LOG2E, NEG_INF)
                     for h in range(DIL_HPG)]
            variants.append(jnp.stack(heads).reshape(DIL_HPG // 2, 2 * c, 2 * c))
        groups.append(jnp.stack(variants))
    return jnp.stack(groups)


def _dil_group(k, dil, sup, q_ref, kvc_ref, kvp_ref, bias_ref, mx_ref, w_ref, y_ref):
    c = DIL_BLOCK
    nblk = sup // (c * dil)
    has_halo = pl.program_id(1) > 0
    nch = DIL_HPG * HEAD_DIM // LANES
    upper = _iota((c, LANES), 1) >= HEAD_DIM

    def block(idx):
        r = idx // nblk
        n = idx - r * nblk
        start = r + dil * c * n
        rows = pl.ds(start, c, stride=dil)
        rows_prev = pl.ds(jnp.maximum(start - dil * c, r), c, stride=dil)
        rows_halo = pl.ds(r + dil * c * (nblk - 1), c, stride=dil)
        variant = jnp.where(jnp.logical_or(n >= 1, has_halo), 1, 0)

        def prev(ch):
            return jnp.where(n >= 1, kvc_ref[ch, 0, rows_prev, :], kvp_ref[ch, 0, rows_halo, :]).astype(BF16)

        for ch in range(nch):
            qb = q_ref[ch, 0, rows, :].astype(BF16)
            zero = jnp.zeros_like(qb)
            qs = jnp.concatenate([jnp.where(upper, zero, qb), jnp.where(upper, qb, zero)], axis=0)
            kband = jnp.concatenate([prev(ch), kvc_ref[ch, 0, rows, :].astype(BF16)], axis=0)
            vband = jnp.concatenate([prev(nch + ch), kvc_ref[nch + ch, 0, rows, :].astype(BF16)], axis=0)
            s = _nt_dot(qs, kband) + bias_ref[0, variant, ch]
            m = jnp.max(s, axis=-1, keepdims=True)
            e = jnp.exp2(s - m).astype(BF16)
            ones = jnp.ones((2 * c, LANES), BF16)
            ol = _dot(e, jnp.concatenate([vband, ones], axis=1))
            lt = jnp.maximum(ol[:, LANES:], TINY)
            o2 = ol[:, :LANES] * (1.0 / lt)
            l2 = m * LN2 + jnp.log(lt)
            o = jnp.where(upper, o2[c:], o2[:c])
            lse = jnp.where(upper, l2[c:], l2[:c])
            if k == 0:
                mx_ref[ch, rows, :] = lse
                w_ref[ch, rows, :] = jnp.ones_like(lse)
                y_ref[ch, rows, :] = o
            else:
                mx_old = mx_ref[ch, rows, :]
                mx_new = jnp.maximum(mx_old, lse)
                a = jnp.exp(mx_old - mx_new)
                bnew = jnp.exp(lse - mx_new)
                mx_ref[ch, rows, :] = mx_new
                w_ref[ch, rows, :] = w_ref[ch, rows, :] * a + bnew
                y_ref[ch, rows, :] = y_ref[ch, rows, :] * a + bnew * o

    def body(t, carry):
        for j in range(DIL_UNROLL):
            block(t * DIL_UNROLL + j)
        return carry

    lax.fori_loop(0, sup // (c * DIL_UNROLL), body, 0)


def _dil_kernel(q_ref, kvc_ref, kvp_ref, bias_ref, yb_ref, mx_ref, w_ref, y_ref, *, sup):
    gi = pl.program_id(2)
    for k, (_, dil) in enumerate(DIL_GROUPS):
        @pl.when(gi == k)
        def _(k=k, dil=dil):
            _dil_group(k, dil, sup, q_ref, kvc_ref, kvp_ref, bias_ref, mx_ref, w_ref, y_ref)

    @pl.when(gi == len(DIL_GROUPS) - 1)
    def _():
        for ch in range(DIL_HPG * HEAD_DIM // LANES):
            yb_ref[0, :, ch * LANES:(ch + 1) * LANES] = (y_ref[ch] * (1.0 / w_ref[ch])).astype(BF16)


def _dil(qb, kvb, bias, sup):
    _, b, s, _ = qb.shape
    hw = DIL_HPG * HEAD_DIM
    nch = hw // LANES
    ng = len(DIL_GROUPS)
    return pl.pallas_call(
        functools.partial(_dil_kernel, sup=sup),
        grid=(b, s // sup, ng),
        in_specs=[pl.BlockSpec((nch, 1, sup, LANES), lambda bi, si, gi: (gi, bi, si, 0)),
                  pl.BlockSpec((2 * nch, 1, sup, LANES), lambda bi, si, gi: (gi, bi, si, 0)),
                  pl.BlockSpec((2 * nch, 1, sup, LANES),
                               lambda bi, si, gi: (gi, bi, jnp.maximum(si - 1, 0), 0)),
                  pl.BlockSpec((1, 2, nch, 2 * DIL_BLOCK, 2 * DIL_BLOCK),
                               lambda bi, si, gi: (gi, 0, 0, 0, 0))],
        out_specs=pl.BlockSpec((1, sup, hw), lambda bi, si, gi: (bi, si, 0)),
        out_shape=jax.ShapeDtypeStruct((b, s, hw), BF16),
        scratch_shapes=[pltpu.VMEM((nch, sup, LANES), F32), pltpu.VMEM((nch, sup, LANES), F32),
                        pltpu.VMEM((nch, sup, LANES), F32)],
        compiler_params=_params(("parallel", "parallel", "arbitrary")),
        name="dil",
    )(qb, kvb, kvb, bias)


def _mixffn_kernel(x_ref, g_ref, ya_ref, yb_ref, wg_ref, ua_ref, ub_ref, wo_ref,
                   g2_ref, wi_ref, wf_ref, gf_ref, o_ref, *, tf, final_norm):
    x = x_ref[...]
    d = x.shape[1]
    hn = _rms(x, g_ref[...]).astype(BF16)
    gate_a = jax.nn.sigmoid(_dot(hn, wg_ref[:, 0:d]))
    gate_b = jax.nn.sigmoid(_dot(hn, wg_ref[:, d:2 * d]))
    merged = gate_a * _dot(ya_ref[...], ua_ref[...]) + gate_b * _dot(yb_ref[...], ub_ref[...])
    x = x + _dot(merged.astype(BF16), wo_ref[...])
    dff = wf_ref.shape[0]
    hn = _rms(x, g2_ref[...]).astype(BF16)
    y = x
    for c in range(0, dff, tf):
        gt = _dot(hn, wi_ref[:, c:c + tf])
        up = _dot(hn, wi_ref[:, dff + c:dff + c + tf])
        act = ((gt * jax.nn.sigmoid(gt)) * up).astype(BF16)
        y = y + _dot(act, wf_ref[c:c + tf, :])
    if final_norm:
        y = _rms(y, gf_ref[...])
    o_ref[...] = y


def _mixffn(x2, g, ya, yb, wg, ua, ub, wo, g2, w_in, w_out, gf, layer, tm, tf, final_norm):
    n, d = x2.shape
    row = lambda i: (i, 0)
    fixed = lambda i: (0, 0)
    return pl.pallas_call(
        functools.partial(_mixffn_kernel, tf=tf, final_norm=final_norm),
        grid=(n // tm,),
        in_specs=([pl.BlockSpec((tm, d), row), pl.BlockSpec((1, d), fixed),
                   pl.BlockSpec((tm, ya.shape[1]), row), pl.BlockSpec((tm, yb.shape[1]), row)]
                  + [_layer_spec(w, layer) for w in (wg, ua, ub, wo)]
                  + [pl.BlockSpec((1, d), fixed), _layer_spec(w_in, layer), _layer_spec(w_out, layer),
                     pl.BlockSpec((1, d), fixed)]),
        out_specs=pl.BlockSpec((tm, d), row),
        out_shape=jax.ShapeDtypeStruct((n, d), F32),
        compiler_params=pltpu.CompilerParams(dimension_semantics=("parallel",),
                                             vmem_limit_bytes=MIXFFN_VMEM_LIMIT),
        name="mixffn",
    )(x2, g, ya, yb, wg, ua, ub, wo, g2, w_in, w_out, gf)


def _alibi_slopes():
    k = jnp.arange(1, N_ALIBI + 1, dtype=F32)
    s = jnp.exp2(-8.0 * k / N_ALIBI)
    nsa = s[2 * DIL_HPG:2 * DIL_HPG + NSA_HEADS]
    dil = jnp.concatenate([s[:2 * DIL_HPG], s[2 * DIL_HPG + NSA_HEADS:]])
    return nsa, dil


def _overlap(seq, ncp):
    n_c = (seq - CMP_BLOCK) // CMP_STRIDE + 1
    n_s = seq // SEL_BLOCK
    c_start = np.arange(n_c) * CMP_STRIDE
    s_start = np.arange(n_s) * SEL_BLOCK
    ov = np.clip(np.minimum(c_start[:, None] + CMP_BLOCK, s_start[None, :] + SEL_BLOCK)
                 - np.maximum(c_start[:, None], s_start[None, :]), 0, None).astype(np.float32) / CMP_BLOCK
    out = np.zeros((ncp, LANES), np.float32)
    out[:n_c, :n_s] = ov
    return jnp.asarray(out, BF16)


def _proj_weights(w_in):
    dep, d, _ = w_in.shape
    scale = HEAD_DIM ** -0.5 * LOG2E
    c0 = NSA_HEADS * HEAD_DIM
    c1 = c0 + 3 * 2 * NSA_KV_GROUPS * HEAD_DIM
    c2 = c1 + NSA_HEADS * 3
    c3 = c2 + 3 * DIL_HEADS * HEAD_DIM
    qa = (w_in[..., :c0] * scale).astype(BF16)
    wb = w_in.astype(BF16)
    kva = wb[..., c0:c1].reshape(dep, d, 3, 2, NSA_KV_GROUPS, HEAD_DIM)
    kva = kva.transpose(0, 1, 2, 4, 3, 5).reshape(dep, d, CMP_W + KVA_W)
    ga = jnp.pad(wb[..., c1:c2], ((0, 0), (0, 0), (0, GA_W - (c2 - c1))))
    qkvb = w_in[..., c2:c3].reshape(dep, d, 3, len(DIL_GROUPS), DIL_HPG * HEAD_DIM)
    qb = (qkvb[:, :, 0] * scale).astype(BF16).reshape(dep, d, QB_W)
    kvb = qkvb[:, :, 1:3].astype(BF16).transpose(0, 1, 3, 2, 4).reshape(dep, d, KVB_W)
    w_a = jnp.concatenate([qa, kva, qb, kvb, ga], axis=-1)
    return w_a, wb[..., c3:]


def _compress_weights(pe_k, pe_v, w_ck1, w_ck2, w_cv1, w_cv2):
    dep = pe_k.shape[0]
    half = CMP_BLOCK // 2
    ck = w_ck1.astype(BF16).reshape(dep, CMP_BLOCK, HEAD_DIM, CMP_HIDDEN)
    cv = w_cv1.astype(BF16).reshape(dep, CMP_BLOCK, HEAD_DIM, CMP_HIDDEN)
    z = jnp.zeros_like(ck)
    w = jnp.concatenate([jnp.concatenate([ck, z], axis=-1),
                         jnp.concatenate([z, cv], axis=-1)], axis=2)
    feat = half * 2 * HEAD_DIM
    wa = w[:, :half].reshape(dep, feat, 2 * CMP_HIDDEN)
    wb = w[:, half:].reshape(dep, feat, 2 * CMP_HIDDEN)
    pe = jnp.concatenate([pe_k, pe_v], axis=-1)
    pe2 = jnp.stack([pe[:, :half].reshape(dep, feat), pe[:, half:].reshape(dep, feat)], axis=1)
    pe2 = jnp.pad(pe2, ((0, 0), (0, 6), (0, 0)))
    z2 = jnp.zeros_like(w_ck2)
    w2 = jnp.concatenate([jnp.concatenate([w_ck2, z2], axis=-1),
                          jnp.concatenate([z2, w_cv2], axis=-1)], axis=1).astype(BF16)
    return pe2, wa, wb, w2


def kernel(x, norm_mix, w_in, pe_k, pe_v, w_ck1, w_ck2, w_cv1, w_cv2, w_up_nsa, w_up_dil,
           w_out, norm_ffn, w_ffn_in, w_ffn_out, norm_final):
    b, s, d = x.shape
    depth = w_in.shape[0]
    n = b * s
    nq = s // Q_BLOCK
    nch = s // CMP_STRIDE
    n_s = s // SEL_BLOCK
    assert s % DIL_SUPER == 0 and n_s <= LANES
    n_top = min(SEL_TOPK, n_s)
    tm = 512
    tf = w_ffn_out.shape[1] // 2

    slopes_nsa, slopes_dil = _alibi_slopes()
    slopes_nsa = slopes_nsa * LOG2E
    ov = _overlap(s, nch)
    gexp = _gate_spread()
    dbias = _dil_bias(slopes_dil)
    w_a, w_gm = _proj_weights(w_in)
    pe2, wa, wb, w2 = _compress_weights(pe_k, pe_v, w_ck1, w_ck2, w_cv1, w_cv2)
    ua = w_up_nsa.astype(BF16)
    ub = w_up_dil.astype(BF16)
    wo = w_out.astype(BF16)
    wfi = w_ffn_in.astype(BF16)
    wfo = w_ffn_out.astype(BF16)
    gf = norm_final.reshape(1, d)

    x2 = x.reshape(n, d)
    for l in range(depth):
        g_mix = norm_mix[l].reshape(1, d)
        qa, xc, kva, qb, kvb, ga = _proj(x2, g_mix, w_a, l, tm)
        qa = qa.reshape(b, s, QA_W)
        kva = kva.reshape(b, s, KVA_W)
        ga = ga.reshape(b, s, GA_W)
        kvc = _compress(xc, pe2[l], wa[l], wb[l], w2[l], b)
        yc, sel, flags = _cmp(slopes_nsa, qa, kvc, ga, ov, gexp, n_top)
        pflags = flags.reshape(-1)
        ya = _selwin(pflags, slopes_nsa, qa, kva, sel, ga, yc, gexp)
        yb = _dil(qb.reshape(-1, b, s, LANES), kvb.reshape(-1, b, s, LANES), dbias, DIL_SUPER)
        x2 = _mixffn(x2, g_mix, ya.reshape(n, NSA_HEADS * HEAD_DIM), yb.reshape(n, DIL_HPG * HEAD_DIM),
                     w_gm, ua, ub, wo, norm_ffn[l].reshape(1, d), wfi, wfo, gf,
                     l, tm, tf, final_norm=(l == depth - 1))
    return x2.reshape(b, s, d)
```

```python
import functools

import numpy as np
import jax
import jax.numpy as jnp
from jax import lax
from jax.experimental import pallas as pl
from jax.experimental.pallas import tpu as pltpu

F32 = jnp.float32
BF16 = jnp.bfloat16

HEAD_DIM = 64
NSA_HEADS = 8
NSA_KV_GROUPS = 2
NSA_HPG = NSA_HEADS // NSA_KV_GROUPS
CMP_BLOCK = 32
CMP_STRIDE = 16
CMP_HIDDEN = 256
SEL_BLOCK = 64
SEL_TOPK = 16
NSA_WINDOW = 512
Q_BLOCK = 128
FORCE_SCORE = 1e6
DIL_GROUPS = ((128, 1), (512, 4), (2048, 16))
DIL_HPG = 4
DIL_HEADS = DIL_HPG * len(DIL_GROUPS)
DIL_BLOCK = 128
N_ALIBI = NSA_HEADS + DIL_HEADS
EPS = 1e-6
NEG_INF = -1e30
TINY = 1e-30
LOG2E = 1.4426950408889634
LN2 = 0.6931471805599453

LANES = 128
GROUP_W = NSA_HPG * HEAD_DIM
DIL_SUPER = 2048
DIL_UNROLL = 4
CMP_TILES = 8
VMEM_LIMIT = 48 * 1024 * 1024
MIXFFN_VMEM_LIMIT = 56 * 1024 * 1024


def _dot(a, b):
    return jnp.dot(a, b, preferred_element_type=F32)


def _nt_dot(a, b):
    return lax.dot_general(a, b, (((1,), (1,)), ((), ())), preferred_element_type=F32)


def _iota(shape, dim):
    return lax.broadcasted_iota(jnp.int32, shape, dim)


def _rms(x, g):
    ms = jnp.mean(x * x, axis=-1, keepdims=True)
    return (x * lax.rsqrt(ms + EPS)) * g


def _params(sem):
    return pltpu.CompilerParams(dimension_semantics=sem, vmem_limit_bytes=VMEM_LIMIT)


QA_W = NSA_HEADS * LANES
QP_W = NSA_HPG * LANES
CMP_W = NSA_KV_GROUPS * LANES
KVA_W, QB_W, KVB_W, GA_W = 512, 768, 1536, 128
PROJ_COLS = NSA_HEADS * HEAD_DIM + CMP_W + KVA_W + QB_W + KVB_W + GA_W


def _proj_kernel(x_ref, g_ref, w_ref, qa_ref, xc_ref, kva_ref, qb_ref, kvb_ref, ga_ref, cmp_scr):
    y = _rms(x_ref[...], g_ref[...]).astype(BF16)
    c = 0
    low = _iota((x_ref.shape[0], LANES), 1) < HEAD_DIM
    for p in range(0, NSA_HEADS // 2, 2):
        r2 = _dot(y, w_ref[:, c:c + 2 * LANES])
        c += 2 * LANES
        for k in range(2):
            r = r2[:, k * LANES:(k + 1) * LANES]
            h0 = 2 * (p + k)
            qa_ref[:, h0 * LANES:(h0 + 1) * LANES] = jnp.where(low, r, 0.0).astype(BF16)
            qa_ref[:, (h0 + 1) * LANES:(h0 + 2) * LANES] = (
                jnp.where(low, pltpu.roll(r, HEAD_DIM, 1), 0.0).astype(BF16))
    rc = _dot(y, w_ref[:, c:c + CMP_W])
    c += CMP_W
    nrow = xc_ref.shape[1]
    for gi in range(NSA_KV_GROUPS):
        cmp_scr[gi] = rc[:, gi * LANES:(gi + 1) * LANES]
        for l in range(CMP_STRIDE):
            xc_ref[gi, :, l * LANES:(l + 1) * LANES] = (
                cmp_scr[gi, pl.ds(l, nrow, stride=CMP_STRIDE), :].astype(BF16))
    kva_ref[...] = _dot(y, w_ref[:, c:c + KVA_W]).astype(BF16)
    c += KVA_W
    for j in range(0, QB_W // LANES, 2):
        r = _dot(y, w_ref[:, c:c + 2 * LANES])
        qb_ref[j] = r[:, :LANES]
        qb_ref[j + 1] = r[:, LANES:]
        c += 2 * LANES
    for j in range(0, KVB_W // LANES, 2):
        r = _dot(y, w_ref[:, c:c + 2 * LANES])
        kvb_ref[j] = r[:, :LANES]
        kvb_ref[j + 1] = r[:, LANES:]
        c += 2 * LANES
    ga_ref[...] = jax.nn.sigmoid(_dot(y, w_ref[:, c:c + GA_W]))


def _layer_spec(w, layer):
    return pl.BlockSpec((None,) + w.shape[1:], lambda i: (layer, 0, 0), pipeline_mode=pl.Buffered(1))


def _proj(x2, g, w, layer, tm):
    n, d = x2.shape
    row = lambda i: (i, 0)
    fixed = lambda i: (0, 0)
    return pl.pallas_call(
        _proj_kernel,
        grid=(n // tm,),
        in_specs=[pl.BlockSpec((tm, d), row), pl.BlockSpec((1, d), fixed), _layer_spec(w, layer)],
        out_specs=[pl.BlockSpec((tm, QA_W), row),
                   pl.BlockSpec((NSA_KV_GROUPS, tm // CMP_STRIDE, CMP_STRIDE * LANES), lambda i: (0, i, 0)),
                   pl.BlockSpec((tm, KVA_W), row),
                   pl.BlockSpec((QB_W // LANES, tm, LANES), lambda i: (0, i, 0)),
                   pl.BlockSpec((KVB_W // LANES, tm, LANES), lambda i: (0, i, 0)),
                   pl.BlockSpec((tm, GA_W), row)],
        out_shape=[jax.ShapeDtypeStruct((n, QA_W), BF16),
                   jax.ShapeDtypeStruct((NSA_KV_GROUPS, n // CMP_STRIDE, CMP_STRIDE * LANES), BF16),
                   jax.ShapeDtypeStruct((n, KVA_W), BF16),
                   jax.ShapeDtypeStruct((QB_W // LANES, n, LANES), F32),
                   jax.ShapeDtypeStruct((KVB_W // LANES, n, LANES), F32),
                   jax.ShapeDtypeStruct((n, GA_W), F32)],
        scratch_shapes=[pltpu.VMEM((NSA_KV_GROUPS, tm, LANES), F32)],
        compiler_params=_params(("parallel",)),
        name="proj",
    )(x2, g, w)


def _compress_kernel(x_ref, pe_ref, wa_ref, wb_ref, w2_ref, o_ref):
    xf = x_ref[0].astype(F32)
    nch = xf.shape[0]
    xa = (xf + pe_ref[0:1, :]).astype(BF16)
    xb = (xf + pe_ref[1:2, :]).astype(BF16)
    ya = _dot(xa, wa_ref[...])
    yb = _dot(xb, wb_ref[...])
    h = ya + pltpu.roll(yb, nch - 1, 0)
    a = (h * jax.nn.sigmoid(h)).astype(BF16)
    o_ref[0, 0] = _dot(a, w2_ref[...]).astype(BF16)


def _compress(xc, pe, wa, wb, w2, b):
    g, rows, f = xc.shape
    nch = rows // b
    hid = wa.shape[1]
    return pl.pallas_call(
        _compress_kernel,
        grid=(b, g),
        in_specs=[pl.BlockSpec((1, nch, f), lambda i, j: (j, i, 0)),
                  pl.BlockSpec((8, f), lambda i, j: (0, 0)),
                  pl.BlockSpec((f, hid), lambda i, j: (0, 0)),
                  pl.BlockSpec((f, hid), lambda i, j: (0, 0)),
                  pl.BlockSpec((hid, LANES), lambda i, j: (0, 0))],
        out_specs=pl.BlockSpec((1, 1, nch, LANES), lambda i, j: (i, j, 0, 0)),
        out_shape=jax.ShapeDtypeStruct((b, g, nch, LANES), BF16),
        compiler_params=_params(("parallel", "parallel")),
        name="compress",
    )(xc, pe, wa, wb, w2)


def _gate_spread():
    g = np.arange(NSA_KV_GROUPS)[:, None, None, None]
    br = np.arange(3)[None, :, None, None]
    r = np.arange(LANES)[None, None, :, None]
    c = np.arange(GROUP_W)[None, None, None, :]
    return jnp.asarray(r == 3 * (NSA_HPG * g + c // HEAD_DIM) + br, BF16)


def _stack_heads(q):
    return jnp.concatenate([q[:, LANES * h:LANES * (h + 1)] for h in range(NSA_HPG)], axis=0)


def _unstack_heads(o4):
    lane = _iota((Q_BLOCK, LANES), 1)
    o = [o4[Q_BLOCK * h:Q_BLOCK * (h + 1)] for h in range(NSA_HPG)]
    c0 = jnp.where(lane < HEAD_DIM, pltpu.roll(o[0], HEAD_DIM, 1), o[1])
    c1 = jnp.where(lane < HEAD_DIM, pltpu.roll(o[2], HEAD_DIM, 1), o[3])
    return jnp.concatenate([c0, c1], axis=1)


def _row_sums_all_lanes(a):
    lane = _iota(a.shape, 1)
    return jnp.where(lane < HEAD_DIM, a, pltpu.roll(a, HEAD_DIM, 1))


def _normalise_unstack(a):
    return _unstack_heads(a * (1.0 / jnp.maximum(_row_sums_all_lanes(a), TINY)))


def _expand_gate(gates, spread):
    hi = gates.astype(BF16)
    lo = (gates - hi.astype(F32)).astype(BF16)
    return _dot(hi, spread) + _dot(lo, spread)


def _ones_in_k_lanes(kv):
    lane = _iota(kv.shape, 1)
    return jnp.where(lane < HEAD_DIM, jnp.ones_like(kv), kv)


def _cmp_kernel(slopes_ref, q_ref, kvc_ref, ga_ref, ov_ref, gexp_ref,
                yc_ref, sel_ref, fl_ref, bias0, cq, rhs, big_ref, *, n_top):
    g = pl.program_id(1)
    i = pl.program_id(2)
    hq = NSA_HPG
    ncp = cq.shape[1]

    @pl.when(i == 0)
    def _():
        kpos = CMP_STRIDE * _iota((Q_BLOCK, ncp), 1) + (CMP_BLOCK - 1)
        cq[...] = kpos - _iota((Q_BLOCK, ncp), 0)
        kf = kpos[0:8].astype(F32)
        for h in range(hq):
            bias0[h] = slopes_ref[hq * g + h] * kf
        rhs[:, 0:LANES] = _ones_in_k_lanes(kvc_ref[0, 0])
        rhs[:, LANES:2 * LANES] = ov_ref[...]

    def attend(width):
        def run():
            kv = kvc_ref[0, 0, 0:width, :]
            for k in range(CMP_TILES):
                q4 = _stack_heads(q_ref[0, k * Q_BLOCK:(k + 1) * Q_BLOCK, :])
                s = _nt_dot(q4, kv).reshape(hq, Q_BLOCK, width)
                t0 = (CMP_TILES * i + k) * Q_BLOCK
                neg = jnp.where(cq[:, 0:width] <= t0, 0.0, NEG_INF)
                tf = t0.astype(F32)
                ps = []
                for h in range(hq):
                    sb = s[h] + (neg + (bias0[h, 0:1, 0:width] - slopes_ref[hq * g + h] * tf))
                    m = jnp.max(sb, axis=-1, keepdims=True)
                    m = jnp.where(m < 0.1 * NEG_INF, -NEG_INF, m)
                    ps.append(jnp.exp2(sb - m).astype(BF16))
                big_ref[k] = _dot(jnp.concatenate(ps, axis=0), rhs[0:width, :])
        return run

    units = ncp // LANES
    per_tile = Q_BLOCK // CMP_STRIDE
    last_visible = (per_tile * (CMP_TILES * i + CMP_TILES - 1)
                    + (Q_BLOCK - CMP_BLOCK) // CMP_STRIDE)
    if units == 1:
        attend(ncp)()
    else:
        lax.switch(jnp.minimum(last_visible // LANES, units - 1),
                   [attend(LANES * (u + 1)) for u in range(units)])

    for k in range(CMP_TILES):
        it = CMP_TILES * i + k
        rows = slice(k * Q_BLOCK, (k + 1) * Q_BLOCK)
        big = big_ref[k]
        inv = 1.0 / jnp.maximum(_row_sums_all_lanes(big[:, 0:LANES]), TINY)
        oc = _unstack_heads(big[:, 0:LANES] * inv)
        yc_ref[0, rows, :] = _expand_gate(ga_ref[0, rows, :], gexp_ref[0, 0]) * oc
        pov = big[:, LANES:2 * LANES] * inv
        imp = pov[0:Q_BLOCK]
        for h in range(1, hq):
            imp = imp + pov[Q_BLOCK * h:Q_BLOCK * (h + 1)]

        qrow = _iota((Q_BLOCK, LANES), 0)
        j = _iota((Q_BLOCK, LANES), 1)
        jt = 2 * it + (qrow >> 6)
        valid = j <= jt
        forced = (j == 0) | (j == jt) | (j == jt - 1)
        score = jnp.where(forced, FORCE_SCORE, jnp.where(valid, imp, -1.0))
        st = score.T
        jr = _iota((LANES, Q_BLOCK), 0)
        jtt = 2 * it + (_iota((LANES, Q_BLOCK), 1) >> 6)
        taken = -3e38
        cur = jnp.where((jr == 0) | (jr == jtt) | (jr == jtt - 1), taken, st)
        for _ in range(n_top - 3):
            mx = jnp.max(cur, axis=0, keepdims=True)
            first = jnp.min(jnp.where(cur == mx, jr, LANES), axis=0, keepdims=True)
            cur = jnp.where(jr == first, taken, cur)
        keep = jnp.where((cur == taken) & (st >= 0.0), 1.0, 0.0).T
        sel_ref[0, 0, rows, :] = keep.astype(BF16)

        fl_ref[k] = (jnp.max(keep, axis=0, keepdims=True) > 0.0).astype(jnp.int32)


def _cmp(slopes, qa, kvc, ga, ov, gexp, n_top):
    b, s, _ = qa.shape
    g = NSA_KV_GROUPS
    nq = s // Q_BLOCK
    ncp = kvc.shape[2]
    tq = CMP_TILES * Q_BLOCK
    nt = s // tq
    return pl.pallas_call(
        functools.partial(_cmp_kernel, n_top=n_top),
        grid=(b, g, nt),
        in_specs=[pl.BlockSpec(memory_space=pltpu.SMEM),
                  pl.BlockSpec((1, tq, QP_W), lambda bi, gi, i: (bi, i, gi)),
                  pl.BlockSpec((1, 1, ncp, LANES), lambda bi, gi, i: (bi, gi, 0, 0)),
                  pl.BlockSpec((1, tq, LANES), lambda bi, gi, i: (bi, i, 0)),
                  pl.BlockSpec((ncp, LANES), lambda bi, gi, i: (0, 0)),
                  pl.BlockSpec((1, 3, LANES, GROUP_W), lambda bi, gi, i: (gi, 0, 0, 0))],
        out_specs=[pl.BlockSpec((1, tq, GROUP_W), lambda bi, gi, i: (bi, i, gi)),
                   pl.BlockSpec((1, 1, tq, LANES), lambda bi, gi, i: (bi, gi, i, 0)),
                   pl.BlockSpec((CMP_TILES, 1, LANES), lambda bi, gi, i: ((bi * g + gi) * nt + i, 0, 0))],
        out_shape=[jax.ShapeDtypeStruct((b, s, g * GROUP_W), F32),
                   jax.ShapeDtypeStruct((b, g, s, LANES), BF16),
                   jax.ShapeDtypeStruct((b * g * nq, 1, LANES), jnp.int32)],
        scratch_shapes=[pltpu.VMEM((NSA_HPG, 8, ncp), F32),
                        pltpu.VMEM((Q_BLOCK, ncp), jnp.int32),
                        pltpu.VMEM((ncp, 2 * LANES), BF16),
                        pltpu.VMEM((CMP_TILES, NSA_HPG * Q_BLOCK, 2 * LANES), F32)],
        compiler_params=_params(("parallel", "parallel", "arbitrary")),
        name="cmp",
    )(slopes, qa, kvc, ga, ov, gexp)


WIN_UNITS = NSA_WINDOW // Q_BLOCK + 1
LIST_UNROLL = 4
SLOT_CHUNK = 10


SELWIN_TILES = 4


def _selwin_kernel(pf_ref, slopes_ref, q_ref, kvs_ref, kvw_ref, sel_ref, ga_ref, yc_ref,
                    gexp_ref, ya_ref, sbuf, slot_unit, mrun, mrunw, acc, accw, *, nq):
    b = pl.program_id(0)
    g = pl.program_id(1)
    i = pl.program_id(2)
    hq = NSA_HPG
    qmk = _iota((Q_BLOCK, LANES), 0) - _iota((Q_BLOCK, LANES), 1)
    koff = _iota((1, LANES), 1).astype(F32)
    dmat = _iota((LANES, LANES), 0) - (_iota((LANES, LANES), 1) >> 6)
    dead = -(1 << 20)
    win_pairs = [(d, d) for d in range(WIN_UNITS)]

    def chunk_pairs(ci):
        return [(WIN_UNITS + t, WIN_UNITS + ci * SLOT_CHUNK + t) for t in range(SLOT_CHUNK)]

    def setup(k):
        it = SELWIN_TILES * i + k
        for d in range(WIN_UNITS):
            u = it - (WIN_UNITS - 1) + d
            slot_unit[k, d] = jnp.where(u >= 0, u, -1)
        base = ((b * NSA_KV_GROUPS + g) * nq + it) * LANES

        def list_body(j4, cnt):
            for t in range(LIST_UNROLL):
                jj = j4 * LIST_UNROLL + t
                used = (pf_ref[base + 2 * jj] | pf_ref[base + 2 * jj + 1]) != 0

                @pl.when(used)
                def _(jj=jj, cnt=cnt):
                    slot_unit[k, WIN_UNITS + cnt] = jj
                cnt = cnt + used.astype(jnp.int32)
            return cnt

        nsel = lax.fori_loop(0, it // LIST_UNROLL + 1, list_body, jnp.int32(0))
        for t in range(SLOT_CHUNK - 1):
            slot_unit[k, WIN_UNITS + nsel + t] = -1
        rows = slice(k * Q_BLOCK, (k + 1) * Q_BLOCK)
        return dict(k=k, it=it, rows=rows, nchunk=(nsel + (SLOT_CHUNK - 1)) // SLOT_CHUNK,
                    q4=_stack_heads(q_ref[0, rows, :]), sel=sel_ref[0, 0, rows, :])

    def kv_tile(kv_ref, k, li):
        u = jnp.maximum(slot_unit[k, li], 0)
        return kv_ref[0, pl.ds(pl.multiple_of(u * Q_BLOCK, Q_BLOCK), Q_BLOCK), :]

    def score_slots(t, kv_ref, pairs, window, mx):
        k = t["k"]
        units = [slot_unit[k, li] for _, li in pairs]
        s_all = _nt_dot(t["q4"], jnp.concatenate([kv_tile(kv_ref, k, li) for _, li in pairs], axis=0))
        if not window:
            spread = jnp.concatenate(
                [jnp.where(dmat == 2 * u, 1.0, 0.0).astype(BF16) for u in units], axis=1)
            chosen = _dot(t["sel"], spread)
        for n, ((bs, _), u) in enumerate(zip(pairs, units)):
            s = s_all[:, n * LANES:(n + 1) * LANES].reshape(hq, Q_BLOCK, LANES)
            off = jnp.where(u >= 0, (t["it"] - u) * Q_BLOCK, dead)
            dist = qmk + off
            if window:
                allow = (dist >= 0) & (dist < NSA_WINDOW)
            else:
                allow = (dist >= 0) & (chosen[:, n * LANES:(n + 1) * LANES] > 0.5)
            neg = jnp.where(allow, 0.0, NEG_INF)
            kd = koff - off.astype(F32)
            for h in range(hq):
                sb = s[h] + (neg + slopes_ref[hq * g + h] * kd)
                sbuf[k, bs, h] = sb
                mx[k, h] = jnp.maximum(mx[k, h], sb)

    def score_pairs(t, kv_ref, pairs, window, mx):
        for n in range(0, len(pairs), 2):
            score_slots(t, kv_ref, pairs[n:n + 2], window, mx)

    def row_max(mx, k):
        return jnp.broadcast_to(jnp.max(mx[k], axis=-1, keepdims=True), mx.shape[1:])

    def weighted_values(t, kv_ref, pairs, mx):
        k = t["k"]
        tot = None
        for n in range(0, len(pairs), 2):
            two = pairs[n:n + 2]
            p = jnp.concatenate(
                [jnp.exp2(sbuf[k, bs] - mx[k]).astype(BF16).reshape(hq * Q_BLOCK, LANES)
                 for bs, _ in two], axis=1)
            v = jnp.concatenate([_ones_in_k_lanes(kv_tile(kv_ref, k, li)) for _, li in two], axis=0)
            d = _dot(p, v)
            tot = d if tot is None else tot + d
        return tot

    tiles = [setup(k) for k in range(SELWIN_TILES)]

    for t in tiles:
        k = t["k"]
        mrunw[k] = jnp.full(mrunw.shape[1:], NEG_INF, F32)
        mrun[k] = jnp.full(mrun.shape[1:], NEG_INF, F32)
        score_pairs(t, kvw_ref, win_pairs, True, mrunw)
        score_pairs(t, kvs_ref, chunk_pairs(0), False, mrun)
    for t in tiles:
        k = t["k"]
        mrunw[k] = row_max(mrunw, k)
        mrun[k] = row_max(mrun, k)
        accw[k] = weighted_values(t, kvw_ref, win_pairs, mrunw)
        acc[k] = weighted_values(t, kvs_ref, chunk_pairs(0), mrun)

    for t in tiles:
        k = t["k"]

        def more(ci, carry, t=t, k=k):
            mrunw[k] = jnp.full(mrunw.shape[1:], NEG_INF, F32)
            score_pairs(t, kvs_ref, chunk_pairs(ci), False, mrunw)
            m_old = mrun[k]
            m_new = jnp.maximum(m_old, row_max(mrunw, k))
            mrun[k] = m_new
            alpha = jnp.exp2(m_old - m_new).reshape(hq * Q_BLOCK, LANES)
            acc[k] = acc[k] * alpha + weighted_values(t, kvs_ref, chunk_pairs(ci), mrun)
            return carry

        lax.fori_loop(1, t["nchunk"], more, 0)

    for t in tiles:
        k = t["k"]
        rows = t["rows"]
        o_s = _normalise_unstack(acc[k])
        o_w = _normalise_unstack(accw[k])
        gates = ga_ref[0, rows, :]
        ya = (yc_ref[0, rows, :] + _expand_gate(gates, gexp_ref[0, 1]) * o_s
              + _expand_gate(gates, gexp_ref[0, 2]) * o_w)
        ya_ref[0, rows, :] = ya.astype(BF16)


def _selwin(pflags, slopes, qa, kva, sel, ga, yc, gexp):
    b, s, _ = qa.shape
    g = NSA_KV_GROUPS
    nq = s // Q_BLOCK
    tq = SELWIN_TILES * Q_BLOCK
    nlist = WIN_UNITS + nq + SLOT_CHUNK - 1
    rows = NSA_HPG * Q_BLOCK
    stat = (SELWIN_TILES, NSA_HPG, Q_BLOCK, LANES)
    grid_spec = pltpu.PrefetchScalarGridSpec(
        num_scalar_prefetch=1,
        grid=(b, g, s // tq),
        in_specs=[pl.BlockSpec(memory_space=pltpu.SMEM),
                  pl.BlockSpec((1, tq, QP_W), lambda bi, gi, i, pf: (bi, i, gi)),
                  pl.BlockSpec((1, s, LANES), lambda bi, gi, i, pf: (bi, 0, gi)),
                  pl.BlockSpec((1, s, LANES), lambda bi, gi, i, pf: (bi, 0, NSA_KV_GROUPS + gi)),
                  pl.BlockSpec((1, 1, tq, LANES), lambda bi, gi, i, pf: (bi, gi, i, 0)),
                  pl.BlockSpec((1, tq, LANES), lambda bi, gi, i, pf: (bi, i, 0)),
                  pl.BlockSpec((1, tq, GROUP_W), lambda bi, gi, i, pf: (bi, i, gi)),
                  pl.BlockSpec((1, 3, LANES, GROUP_W), lambda bi, gi, i, pf: (gi, 0, 0, 0))],
        out_specs=pl.BlockSpec((1, tq, GROUP_W), lambda bi, gi, i, pf: (bi, i, gi)),
        scratch_shapes=[pltpu.VMEM((SELWIN_TILES, WIN_UNITS + SLOT_CHUNK) + stat[1:], F32),
                        pltpu.SMEM((SELWIN_TILES, nlist), jnp.int32),
                        pltpu.VMEM(stat, F32),
                        pltpu.VMEM(stat, F32),
                        pltpu.VMEM((SELWIN_TILES, rows, LANES), F32),
                        pltpu.VMEM((SELWIN_TILES, rows, LANES), F32)],
    )
    return pl.pallas_call(
        functools.partial(_selwin_kernel, nq=nq),
        grid_spec=grid_spec,
        out_shape=jax.ShapeDtypeStruct((b, s, g * GROUP_W), BF16),
        compiler_params=_params(("parallel", "parallel", "arbitrary")),
        name="selwin",
    )(pflags, slopes, qa, kva, kva, sel, ga, yc, gexp)


def _dil_bias(slopes):
    c = DIL_BLOCK
    qi = np.arange(c)[:, None]
    ki = np.arange(2 * c)[None, :]
    delta = qi + c - ki
    groups = []
    for k, (window, dil) in enumerate(DIL_GROUPS):
        dist = jnp.asarray((delta * dil).astype(np.float32))
        variants = []
        for has_prev in (False, True):
            in_band = jnp.asarray((delta >= 0) & (delta <= window // dil) & ((ki >= c) | has_prev))
            heads = [jnp.where(in_band, -(slopes[DIL_HPG * k + h] * dist) * LOG2E, NEG_INF)
                     for h in range(DIL_HPG)]
            variants.append(jnp.stack(heads).reshape(DIL_HPG // 2, 2 * c, 2 * c))
        groups.append(jnp.stack(variants))
    return jnp.stack(groups)


def _dil_group(k, dil, sup, q_ref, kvc_ref, kvp_ref, bias_ref, mx_ref, w_ref, y_ref):
    c = DIL_BLOCK
    nblk = sup // (c * dil)
    has_halo = pl.program_id(1) > 0
    nch = DIL_HPG * HEAD_DIM // LANES
    upper = _iota((c, LANES), 1) >= HEAD_DIM

    def block(idx):
        r = idx // nblk
        n = idx - r * nblk
        start = r + dil * c * n
        rows = pl.ds(start, c, stride=dil)
        rows_prev = pl.ds(jnp.maximum(start - dil * c, r), c, stride=dil)
        rows_halo = pl.ds(r + dil * c * (nblk - 1), c, stride=dil)
        variant = jnp.where(jnp.logical_or(n >= 1, has_halo), 1, 0)

        def prev(ch):
            return jnp.where(n >= 1, kvc_ref[ch, 0, rows_prev, :], kvp_ref[ch, 0, rows_halo, :]).astype(BF16)

        for ch in range(nch):
            qb = q_ref[ch, 0, rows, :].astype(BF16)
            zero = jnp.zeros_like(qb)
            qs = jnp.concatenate([jnp.where(upper, zero, qb), jnp.where(upper, qb, zero)], axis=0)
            kband = jnp.concatenate([prev(ch), kvc_ref[ch, 0, rows, :].astype(BF16)], axis=0)
            vband = jnp.concatenate([prev(nch + ch), kvc_ref[nch + ch, 0, rows, :].astype(BF16)], axis=0)
            s = _nt_dot(qs, kband) + bias_ref[0, variant, ch]
            m = jnp.max(s, axis=-1, keepdims=True)
            e = jnp.exp2(s - m).astype(BF16)
            ones = jnp.ones((2 * c, LANES), BF16)
            ol = _dot(e, jnp.concatenate([vband, ones], axis=1))
            lt = jnp.maximum(ol[:, LANES:], TINY)
            o2 = ol[:, :LANES] * (1.0 / lt)
            l2 = m * LN2 + jnp.log(lt)
            o = jnp.where(upper, o2[c:], o2[:c])
            lse = jnp.where(upper, l2[c:], l2[:c])
            if k == 0:
                mx_ref[ch, rows, :] = lse
                w_ref[ch, rows, :] = jnp.ones_like(lse)
                y_ref[ch, rows, :] = o
            else:
                mx_old = mx_ref[ch, rows, :]
                mx_new = jnp.maximum(mx_old, lse)
                a = jnp.exp(mx_old - mx_new)
                bnew = jnp.exp(lse - mx_new)
                mx_ref[ch, rows, :] = mx_new
                w_ref[ch, rows, :] = w_ref[ch, rows, :] * a + bnew
                y_ref[ch, rows, :] = y_ref[ch, rows, :] * a + bnew * o

    def body(t, carry):
        for j in range(DIL_UNROLL):
            block(t * DIL_UNROLL + j)
        return carry

    lax.fori_loop(0, sup // (c * DIL_UNROLL), body, 0)


def _dil_kernel(q_ref, kvc_ref, kvp_ref, bias_ref, yb_ref, mx_ref, w_ref, y_ref, *, sup):
    gi = pl.program_id(2)
    for k, (_, dil) in enumerate(DIL_GROUPS):
        @pl.when(gi == k)
        def _(k=k, dil=dil):
            _dil_group(k, dil, sup, q_ref, kvc_ref, kvp_ref, bias_ref, mx_ref, w_ref, y_ref)

    @pl.when(gi == len(DIL_GROUPS) - 1)
    def _():
        for ch in range(DIL_HPG * HEAD_DIM // LANES):
            yb_ref[0, :, ch * LANES:(ch + 1) * LANES] = (y_ref[ch] * (1.0 / w_ref[ch])).astype(BF16)


def _dil(qb, kvb, bias, sup):
    _, b, s, _ = qb.shape
    hw = DIL_HPG * HEAD_DIM
    nch = hw // LANES
    ng = len(DIL_GROUPS)
    return pl.pallas_call(
        functools.partial(_dil_kernel, sup=sup),
        grid=(b, s // sup, ng),
        in_specs=[pl.BlockSpec((nch, 1, sup, LANES), lambda bi, si, gi: (gi, bi, si, 0)),
                  pl.BlockSpec((2 * nch, 1, sup, LANES), lambda bi, si, gi: (gi, bi, si, 0)),
                  pl.BlockSpec((2 * nch, 1, sup, LANES),
                               lambda bi, si, gi: (gi, bi, jnp.maximum(si - 1, 0), 0)),
                  pl.BlockSpec((1, 2, nch, 2 * DIL_BLOCK, 2 * DIL_BLOCK),
                               lambda bi, si, gi: (gi, 0, 0, 0, 0))],
        out_specs=pl.BlockSpec((1, sup, hw), lambda bi, si, gi: (bi, si, 0)),
        out_shape=jax.ShapeDtypeStruct((b, s, hw), BF16),
        scratch_shapes=[pltpu.VMEM((nch, sup, LANES), F32), pltpu.VMEM((nch, sup, LANES), F32),
                        pltpu.VMEM((nch, sup, LANES), F32)],
        compiler_params=_params(("parallel", "parallel", "arbitrary")),
        name="dil",
    )(qb, kvb, kvb, bias)


def _mixffn_kernel(x_ref, g_ref, ya_ref, yb_ref, wg_ref, ua_ref, ub_ref, wo_ref,
                   g2_ref, wi_ref, wf_ref, gf_ref, o_ref, *, tf, final_norm):
    x = x_ref[...]
    d = x.shape[1]
    hn = _rms(x, g_ref[...]).astype(BF16)
    gate_a = jax.nn.sigmoid(_dot(hn, wg_ref[:, 0:d]))
    gate_b = jax.nn.sigmoid(_dot(hn, wg_ref[:, d:2 * d]))
    merged = gate_a * _dot(ya_ref[...], ua_ref[...]) + gate_b * _dot(yb_ref[...], ub_ref[...])
    x = x + _dot(merged.astype(BF16), wo_ref[...])
    dff = wf_ref.shape[0]
    hn = _rms(x, g2_ref[...]).astype(BF16)
    y = x
    for c in range(0, dff, tf):
        gt = _dot(hn, wi_ref[:, c:c + tf])
        up = _dot(hn, wi_ref[:, dff + c:dff + c + tf])
        act = ((gt * jax.nn.sigmoid(gt)) * up).astype(BF16)
        y = y + _dot(act, wf_ref[c:c + tf, :])
    if final_norm:
        y = _rms(y, gf_ref[...])
    o_ref[...] = y


def _mixffn(x2, g, ya, yb, wg, ua, ub, wo, g2, w_in, w_out, gf, layer, tm, tf, final_norm):
    n, d = x2.shape
    row = lambda i: (i, 0)
    fixed = lambda i: (0, 0)
    return pl.pallas_call(
        functools.partial(_mixffn_kernel, tf=tf, final_norm=final_norm),
        grid=(n // tm,),
        in_specs=([pl.BlockSpec((tm, d), row), pl.BlockSpec((1, d), fixed),
                   pl.BlockSpec((tm, ya.shape[1]), row), pl.BlockSpec((tm, yb.shape[1]), row)]
                  + [_layer_spec(w, layer) for w in (wg, ua, ub, wo)]
                  + [pl.BlockSpec((1, d), fixed), _layer_spec(w_in, layer), _layer_spec(w_out, layer),
                     pl.BlockSpec((1, d), fixed)]),
        out_specs=pl.BlockSpec((tm, d), row),
        out_shape=jax.ShapeDtypeStruct((n, d), F32),
        compiler_params=pltpu.CompilerParams(dimension_semantics=("parallel",),
                                             vmem_limit_bytes=MIXFFN_VMEM_LIMIT),
        name="mixffn",
    )(x2, g, ya, yb, wg, ua, ub, wo, g2, w_in, w_out, gf)


def _alibi_slopes():
    k = jnp.arange(1, N_ALIBI + 1, dtype=F32)
    s = jnp.exp2(-8.0 * k / N_ALIBI)
    nsa = s[2 * DIL_HPG:2 * DIL_HPG + NSA_HEADS]
    dil = jnp.concatenate([s[:2 * DIL_HPG], s[2 * DIL_HPG + NSA_HEADS:]])
    return nsa, dil


def _overlap(seq, ncp):
    n_c = (seq - CMP_BLOCK) // CMP_STRIDE + 1
    n_s = seq // SEL_BLOCK
    c_start = np.arange(n_c) * CMP_STRIDE
    s_start = np.arange(n_s) * SEL_BLOCK
    ov = np.clip(np.minimum(c_start[:, None] + CMP_BLOCK, s_start[None, :] + SEL_BLOCK)
                 - np.maximum(c_start[:, None], s_start[None, :]), 0, None).astype(np.float32) / CMP_BLOCK
    out = np.zeros((ncp, LANES), np.float32)
    out[:n_c, :n_s] = ov
    return jnp.asarray(out, BF16)


def _proj_weights(w_in):
    dep, d, _ = w_in.shape
    scale = HEAD_DIM ** -0.5 * LOG2E
    c0 = NSA_HEADS * HEAD_DIM
    c1 = c0 + 3 * 2 * NSA_KV_GROUPS * HEAD_DIM
    c2 = c1 + NSA_HEADS * 3
    c3 = c2 + 3 * DIL_HEADS * HEAD_DIM
    qa = (w_in[..., :c0] * scale).astype(BF16)
    wb = w_in.astype(BF16)
    kva = wb[..., c0:c1].reshape(dep, d, 3, 2, NSA_KV_GROUPS, HEAD_DIM)
    kva = kva.transpose(0, 1, 2, 4, 3, 5).reshape(dep, d, CMP_W + KVA_W)
    ga = jnp.pad(wb[..., c1:c2], ((0, 0), (0, 0), (0, GA_W - (c2 - c1))))
    qkvb = w_in[..., c2:c3].reshape(dep, d, 3, len(DIL_GROUPS), DIL_HPG * HEAD_DIM)
    qb = (qkvb[:, :, 0] * scale).astype(BF16).reshape(dep, d, QB_W)
    kvb = qkvb[:, :, 1:3].astype(BF16).transpose(0, 1, 3, 2, 4).reshape(dep, d, KVB_W)
    w_a = jnp.concatenate([qa, kva, qb, kvb, ga], axis=-1)
    return w_a, wb[..., c3:]


def _compress_weights(pe_k, pe_v, w_ck1, w_ck2, w_cv1, w_cv2):
    dep = pe_k.shape[0]
    half = CMP_BLOCK // 2
    ck = w_ck1.astype(BF16).reshape(dep, CMP_BLOCK, HEAD_DIM, CMP_HIDDEN)
    cv = w_cv1.astype(BF16).reshape(dep, CMP_BLOCK, HEAD_DIM, CMP_HIDDEN)
    z = jnp.zeros_like(ck)
    w = jnp.concatenate([jnp.concatenate([ck, z], axis=-1),
                         jnp.concatenate([z, cv], axis=-1)], axis=2)
    feat = half * 2 * HEAD_DIM
    wa = w[:, :half].reshape(dep, feat, 2 * CMP_HIDDEN)
    wb = w[:, half:].reshape(dep, feat, 2 * CMP_HIDDEN)
    pe = jnp.concatenate([pe_k, pe_v], axis=-1)
    pe2 = jnp.stack([pe[:, :half].reshape(dep, feat), pe[:, half:].reshape(dep, feat)], axis=1)
    pe2 = jnp.pad(pe2, ((0, 0), (0, 6), (0, 0)))
    z2 = jnp.zeros_like(w_ck2)
    w2 = jnp.concatenate([jnp.concatenate([w_ck2, z2], axis=-1),
                          jnp.concatenate([z2, w_cv2], axis=-1)], axis=1).astype(BF16)
    return pe2, wa, wb, w2


def kernel(x, norm_mix, w_in, pe_k, pe_v, w_ck1, w_ck2, w_cv1, w_cv2, w_up_nsa, w_up_dil,
           w_out, norm_ffn, w_ffn_in, w_ffn_out, norm_final):
    b, s, d = x.shape
    depth = w_in.shape[0]
    n = b * s
    nq = s // Q_BLOCK
    nch = s // CMP_STRIDE
    n_s = s // SEL_BLOCK
    assert s % DIL_SUPER == 0 and n_s <= LANES
    n_top = min(SEL_TOPK, n_s)
    tm = 512
    tf = w_ffn_out.shape[1] // 2

    slopes_nsa, slopes_dil = _alibi_slopes()
    slopes_nsa = slopes_nsa * LOG2E
    ov = _overlap(s, nch)
    gexp = _gate_spread()
    dbias = _dil_bias(slopes_dil)
    w_a, w_gm = _proj_weights(w_in)
    pe2, wa, wb, w2 = _compress_weights(pe_k, pe_v, w_ck1, w_ck2, w_cv1, w_cv2)
    ua = w_up_nsa.astype(BF16)
    ub = w_up_dil.astype(BF16)
    wo = w_out.astype(BF16)
    wfi = w_ffn_in.astype(BF16)
    wfo = w_ffn_out.astype(BF16)
    gf = norm_final.reshape(1, d)

    x2 = x.reshape(n, d)
    for l in range(depth):
        g_mix = norm_mix[l].reshape(1, d)
        qa, xc, kva, qb, kvb, ga = _proj(x2, g_mix, w_a, l, tm)
        qa = qa.reshape(b, s, QA_W)
        kva = kva.reshape(b, s, KVA_W)
        ga = ga.reshape(b, s, GA_W)
        kvc = _compress(xc, pe2[l], wa[l], wb[l], w2[l], b)
        yc, sel, flags = _cmp(slopes_nsa, qa, kvc, ga, ov, gexp, n_top)
        pflags = flags.reshape(-1)
        ya = _selwin(pflags, slopes_nsa, qa, kva, sel, ga, yc, gexp)
        yb = _dil(qb.reshape(-1, b, s, LANES), kvb.reshape(-1, b, s, LANES), dbias, DIL_SUPER)
        x2 = _mixffn(x2, g_mix, ya.reshape(n, NSA_HEADS * HEAD_DIM), yb.reshape(n, DIL_HPG * HEAD_DIM),
                     w_gm, ua, ub, wo, norm_ffn[l].reshape(1, d), wfi, wfo, gf,
                     l, tm, tf, final_norm=(l == depth - 1))
    return x2.reshape(b, s, d)
```

```python
import functools

import numpy as np
import jax
import jax.numpy as jnp
from jax import lax
from jax.experimental import pallas as pl
from jax.experimental.pallas import tpu as pltpu

F32 = jnp.float32
BF16 = jnp.bfloat16

HEAD_DIM = 64
NSA_HEADS = 8
NSA_KV_GROUPS = 2
NSA_HPG = NSA_HEADS // NSA_KV_GROUPS
CMP_BLOCK = 32
CMP_STRIDE = 16
CMP_HIDDEN = 256
SEL_BLOCK = 64
SEL_TOPK = 16
NSA_WINDOW = 512
Q_BLOCK = 128
FORCE_SCORE = 1e6
DIL_GROUPS = ((128, 1), (512, 4), (2048, 16))
DIL_HPG = 4
DIL_HEADS = DIL_HPG * len(DIL_GROUPS)
DIL_BLOCK = 128
N_ALIBI = NSA_HEADS + DIL_HEADS
EPS = 1e-6
NEG_INF = -1e30
TINY = 1e-30
LOG2E = 1.4426950408889634
LN2 = 0.6931471805599453

LANES = 128
MXU_TILE = 256
GROUP_W = NSA_HPG * HEAD_DIM
DIL_SUPER = 2048
DIL_UNROLL = 4
CMP_TILES = 8
VMEM_LIMIT = 48 * 1024 * 1024
MIXFFN_VMEM_LIMIT = 56 * 1024 * 1024


def _dot(a, b):
    return jnp.dot(a, b, preferred_element_type=F32)


def _nt_dot(a, b):
    return lax.dot_general(a, b, (((1,), (1,)), ((), ())), preferred_element_type=F32)


def _iota(shape, dim):
    return lax.broadcasted_iota(jnp.int32, shape, dim)


def _rms(x, g):
    ms = jnp.mean(x * x, axis=-1, keepdims=True)
    return (x * lax.rsqrt(ms + EPS)) * g


def _params(sem):
    return pltpu.CompilerParams(dimension_semantics=sem, vmem_limit_bytes=VMEM_LIMIT)


QA_W = NSA_HEADS * LANES
QP_W = NSA_HPG * LANES
CMP_W = NSA_KV_GROUPS * LANES
KVA_W, QB_W, KVB_W, GA_W = 512, 768, 1536, 128
PROJ_COLS = NSA_HEADS * HEAD_DIM + CMP_W + KVA_W + QB_W + KVB_W + GA_W


def _proj_kernel(x_ref, g_ref, w_ref, qa_ref, xc_ref, kva_ref, qb_ref, kvb_ref, ga_ref, cmp_scr):
    y = _rms(x_ref[...], g_ref[...]).astype(BF16)
    c = 0
    low = _iota((x_ref.shape[0], LANES), 1) < HEAD_DIM
    for p in range(0, NSA_HEADS // 2, 2):
        r2 = _dot(y, w_ref[:, c:c + 2 * LANES])
        c += 2 * LANES
        for k in range(2):
            r = r2[:, k * LANES:(k + 1) * LANES]
            h0 = 2 * (p + k)
            qa_ref[:, h0 * LANES:(h0 + 1) * LANES] = jnp.where(low, r, 0.0).astype(BF16)
            qa_ref[:, (h0 + 1) * LANES:(h0 + 2) * LANES] = (
                jnp.where(low, pltpu.roll(r, HEAD_DIM, 1), 0.0).astype(BF16))
    rc = _dot(y, w_ref[:, c:c + CMP_W])
    c += CMP_W
    nrow = xc_ref.shape[1]
    for gi in range(NSA_KV_GROUPS):
        cmp_scr[gi] = rc[:, gi * LANES:(gi + 1) * LANES]
        for l in range(CMP_STRIDE):
            xc_ref[gi, :, l * LANES:(l + 1) * LANES] = (
                cmp_scr[gi, pl.ds(l, nrow, stride=CMP_STRIDE), :].astype(BF16))
    kva_ref[...] = _dot(y, w_ref[:, c:c + KVA_W]).astype(BF16)
    c += KVA_W
    for j in range(0, QB_W // LANES, 2):
        r = _dot(y, w_ref[:, c:c + 2 * LANES])
        qb_ref[j] = r[:, :LANES]
        qb_ref[j + 1] = r[:, LANES:]
        c += 2 * LANES
    ngroups = len(DIL_GROUPS)
    for kv in range(2):
        for gi in range(ngroups):
            r = _dot(y, w_ref[:, c:c + 2 * LANES])
            kvb_ref[4 * gi + 2 * kv] = r[:, :LANES]
            kvb_ref[4 * gi + 2 * kv + 1] = r[:, LANES:]
            c += 2 * LANES
    ga_ref[...] = jax.nn.sigmoid(_dot(y, w_ref[:, c:c + GA_W]))


def _layer_spec(w, layer):
    return pl.BlockSpec((None,) + w.shape[1:], lambda i: (layer, 0, 0), pipeline_mode=pl.Buffered(1))


def _proj(x2, g, w, layer, tm):
    n, d = x2.shape
    row = lambda i: (i, 0)
    fixed = lambda i: (0, 0)
    return pl.pallas_call(
        _proj_kernel,
        grid=(n // tm,),
        in_specs=[pl.BlockSpec((tm, d), row), pl.BlockSpec((1, d), fixed), _layer_spec(w, layer)],
        out_specs=[pl.BlockSpec((tm, QA_W), row),
                   pl.BlockSpec((NSA_KV_GROUPS, tm // CMP_STRIDE, CMP_STRIDE * LANES), lambda i: (0, i, 0)),
                   pl.BlockSpec((tm, KVA_W), row),
                   pl.BlockSpec((QB_W // LANES, tm, LANES), lambda i: (0, i, 0)),
                   pl.BlockSpec((KVB_W // LANES, tm, LANES), lambda i: (0, i, 0)),
                   pl.BlockSpec((tm, GA_W), row)],
        out_shape=[jax.ShapeDtypeStruct((n, QA_W), BF16),
                   jax.ShapeDtypeStruct((NSA_KV_GROUPS, n // CMP_STRIDE, CMP_STRIDE * LANES), BF16),
                   jax.ShapeDtypeStruct((n, KVA_W), BF16),
                   jax.ShapeDtypeStruct((QB_W // LANES, n, LANES), F32),
                   jax.ShapeDtypeStruct((KVB_W // LANES, n, LANES), F32),
                   jax.ShapeDtypeStruct((n, GA_W), F32)],
        scratch_shapes=[pltpu.VMEM((NSA_KV_GROUPS, tm, LANES), F32)],
        compiler_params=_params(("parallel",)),
        name="proj",
    )(x2, g, w)


def _compress_kernel(x_ref, pe_ref, wa_ref, wb_ref, w2_ref, o_ref):
    xf = x_ref[0].astype(F32)
    nch = xf.shape[0]
    xa = (xf + pe_ref[0:1, :]).astype(BF16)
    xb = (xf + pe_ref[1:2, :]).astype(BF16)
    ya = _dot(xa, wa_ref[...])
    yb = _dot(xb, wb_ref[...])
    h = ya + pltpu.roll(yb, nch - 1, 0)
    a = (h * jax.nn.sigmoid(h)).astype(BF16)
    o_ref[0, 0] = _dot(a, w2_ref[...]).astype(BF16)


def _compress(xc, pe, wa, wb, w2, b):
    g, rows, f = xc.shape
    nch = rows // b
    hid = wa.shape[1]
    return pl.pallas_call(
        _compress_kernel,
        grid=(b, g),
        in_specs=[pl.BlockSpec((1, nch, f), lambda i, j: (j, i, 0)),
                  pl.BlockSpec((8, f), lambda i, j: (0, 0)),
                  pl.BlockSpec((f, hid), lambda i, j: (0, 0)),
                  pl.BlockSpec((f, hid), lambda i, j: (0, 0)),
                  pl.BlockSpec((hid, LANES), lambda i, j: (0, 0))],
        out_specs=pl.BlockSpec((1, 1, nch, LANES), lambda i, j: (i, j, 0, 0)),
        out_shape=jax.ShapeDtypeStruct((b, g, nch, LANES), BF16),
        compiler_params=_params(("parallel", "parallel")),
        name="compress",
    )(xc, pe, wa, wb, w2)


def _gate_spread():
    g = np.arange(NSA_KV_GROUPS)[:, None, None, None]
    br = np.arange(3)[None, :, None, None]
    r = np.arange(LANES)[None, None, :, None]
    c = np.arange(GROUP_W)[None, None, None, :]
    return jnp.asarray(r == 3 * (NSA_HPG * g + c // HEAD_DIM) + br, BF16)


def _stack_heads(q):
    return jnp.concatenate([q[:, LANES * h:LANES * (h + 1)] for h in range(NSA_HPG)], axis=0)


def _unstack_heads(o4):
    lane = _iota((Q_BLOCK, LANES), 1)
    o = [o4[Q_BLOCK * h:Q_BLOCK * (h + 1)] for h in range(NSA_HPG)]
    c0 = jnp.where(lane < HEAD_DIM, pltpu.roll(o[0], HEAD_DIM, 1), o[1])
    c1 = jnp.where(lane < HEAD_DIM, pltpu.roll(o[2], HEAD_DIM, 1), o[3])
    return jnp.concatenate([c0, c1], axis=1)


def _row_sums_all_lanes(a):
    lane = _iota(a.shape, 1)
    return jnp.where(lane < HEAD_DIM, a, pltpu.roll(a, HEAD_DIM, 1))


def _normalise_unstack(a):
    return _unstack_heads(a * (1.0 / jnp.maximum(_row_sums_all_lanes(a), TINY)))


def _expand_gate(gates, spread):
    hi = gates.astype(BF16)
    lo = (gates - hi.astype(F32)).astype(BF16)
    return _dot(hi, spread) + _dot(lo, spread)


def _ones_in_k_lanes(kv):
    lane = _iota(kv.shape, 1)
    return jnp.where(lane < HEAD_DIM, jnp.ones_like(kv), kv)


def _cmp_kernel(slopes_ref, q_ref, kvc_ref, ga_ref, ov_ref, gexp_ref,
                yc_ref, sel_ref, fl_ref, bias0, cq, rhs, big_ref, *, n_top):
    g = pl.program_id(1)
    i = pl.program_id(2)
    hq = NSA_HPG
    ncp = cq.shape[1]

    @pl.when(i == 0)
    def _():
        kpos = CMP_STRIDE * _iota((Q_BLOCK, ncp), 1) + (CMP_BLOCK - 1)
        cq[...] = kpos - _iota((Q_BLOCK, ncp), 0)
        kf = kpos[0:8].astype(F32)
        for h in range(hq):
            bias0[h] = slopes_ref[hq * g + h] * kf
        rhs[:, 0:LANES] = _ones_in_k_lanes(kvc_ref[0, 0])
        rhs[:, LANES:2 * LANES] = ov_ref[...]

    def attend(width):
        def run():
            kv = kvc_ref[0, 0, 0:width, :]
            for k in range(CMP_TILES):
                q4 = _stack_heads(q_ref[0, k * Q_BLOCK:(k + 1) * Q_BLOCK, :])
                s = _nt_dot(q4, kv).reshape(hq, Q_BLOCK, width)
                t0 = (CMP_TILES * i + k) * Q_BLOCK
                neg = jnp.where(cq[:, 0:width] <= t0, 0.0, NEG_INF)
                tf = t0.astype(F32)
                ps = []
                for h in range(hq):
                    sb = s[h] + (neg + (bias0[h, 0:1, 0:width] - slopes_ref[hq * g + h] * tf))
                    m = jnp.max(sb, axis=-1, keepdims=True)
                    m = jnp.where(m < 0.1 * NEG_INF, -NEG_INF, m)
                    ps.append(jnp.exp2(sb - m).astype(BF16))
                big_ref[k] = _dot(jnp.concatenate(ps, axis=0), rhs[0:width, :])
        return run

    units = ncp // LANES
    per_tile = Q_BLOCK // CMP_STRIDE
    last_visible = (per_tile * (CMP_TILES * i + CMP_TILES - 1)
                    + (Q_BLOCK - CMP_BLOCK) // CMP_STRIDE)
    if units == 1:
        attend(ncp)()
    else:
        lax.switch(jnp.minimum(last_visible // LANES, units - 1),
                   [attend(LANES * (u + 1)) for u in range(units)])

    for k in range(CMP_TILES):
        it = CMP_TILES * i + k
        rows = slice(k * Q_BLOCK, (k + 1) * Q_BLOCK)
        big = big_ref[k]
        inv = 1.0 / jnp.maximum(_row_sums_all_lanes(big[:, 0:LANES]), TINY)
        oc = _unstack_heads(big[:, 0:LANES] * inv)
        yc_ref[0, rows, :] = _expand_gate(ga_ref[0, rows, :], gexp_ref[0, 0]) * oc
        pov = big[:, LANES:2 * LANES] * inv
        imp = pov[0:Q_BLOCK]
        for h in range(1, hq):
            imp = imp + pov[Q_BLOCK * h:Q_BLOCK * (h + 1)]

        qrow = _iota((Q_BLOCK, LANES), 0)
        j = _iota((Q_BLOCK, LANES), 1)
        jt = 2 * it + (qrow >> 6)
        valid = j <= jt
        forced = (j == 0) | (j == jt) | (j == jt - 1)
        score = jnp.where(forced, FORCE_SCORE, jnp.where(valid, imp, -1.0))
        st = score.T
        jr = _iota((LANES, Q_BLOCK), 0)
        jtt = 2 * it + (_iota((LANES, Q_BLOCK), 1) >> 6)
        taken = -3e38
        cur = jnp.where((jr == 0) | (jr == jtt) | (jr == jtt - 1), taken, st)
        for _ in range(n_top - 3):
            mx = jnp.max(cur, axis=0, keepdims=True)
            first = jnp.min(jnp.where(cur == mx, jr, LANES), axis=0, keepdims=True)
            cur = jnp.where(jr == first, taken, cur)
        keep = jnp.where((cur == taken) & (st >= 0.0), 1.0, 0.0).T
        sel_ref[0, 0, rows, :] = keep.astype(BF16)

        fl_ref[k] = (jnp.max(keep, axis=0, keepdims=True) > 0.0).astype(jnp.int32)


def _cmp(slopes, qa, kvc, ga, ov, gexp, n_top):
    b, s, _ = qa.shape
    g = NSA_KV_GROUPS
    nq = s // Q_BLOCK
    ncp = kvc.shape[2]
    tq = CMP_TILES * Q_BLOCK
    nt = s // tq
    return pl.pallas_call(
        functools.partial(_cmp_kernel, n_top=n_top),
        grid=(b, g, nt),
        in_specs=[pl.BlockSpec(memory_space=pltpu.SMEM),
                  pl.BlockSpec((1, tq, QP_W), lambda bi, gi, i: (bi, i, gi)),
                  pl.BlockSpec((1, 1, ncp, LANES), lambda bi, gi, i: (bi, gi, 0, 0)),
                  pl.BlockSpec((1, tq, LANES), lambda bi, gi, i: (bi, i, 0)),
                  pl.BlockSpec((ncp, LANES), lambda bi, gi, i: (0, 0)),
                  pl.BlockSpec((1, 3, LANES, GROUP_W), lambda bi, gi, i: (gi, 0, 0, 0))],
        out_specs=[pl.BlockSpec((1, tq, GROUP_W), lambda bi, gi, i: (bi, i, gi)),
                   pl.BlockSpec((1, 1, tq, LANES), lambda bi, gi, i: (bi, gi, i, 0)),
                   pl.BlockSpec((CMP_TILES, 1, LANES), lambda bi, gi, i: ((bi * g + gi) * nt + i, 0, 0))],
        out_shape=[jax.ShapeDtypeStruct((b, s, g * GROUP_W), F32),
                   jax.ShapeDtypeStruct((b, g, s, LANES), BF16),
                   jax.ShapeDtypeStruct((b * g * nq, 1, LANES), jnp.int32)],
        scratch_shapes=[pltpu.VMEM((NSA_HPG, 8, ncp), F32),
                        pltpu.VMEM((Q_BLOCK, ncp), jnp.int32),
                        pltpu.VMEM((ncp, 2 * LANES), BF16),
                        pltpu.VMEM((CMP_TILES, NSA_HPG * Q_BLOCK, 2 * LANES), F32)],
        compiler_params=_params(("parallel", "parallel", "arbitrary")),
        name="cmp",
    )(slopes, qa, kvc, ga, ov, gexp)


WIN_UNITS = NSA_WINDOW // Q_BLOCK + 1
LIST_UNROLL = 4
SLOT_CHUNK = 10


SELWIN_TILES = 4


def _selwin_kernel(pf_ref, slopes_ref, q_ref, kvs_ref, kvw_ref, sel_ref, ga_ref, yc_ref,
                    gexp_ref, ya_ref, sbuf, slot_unit, mrun, mrunw, acc, accw, *, nq):
    b = pl.program_id(0)
    g = pl.program_id(1)
    i = pl.program_id(2)
    hq = NSA_HPG
    qmk = _iota((Q_BLOCK, LANES), 0) - _iota((Q_BLOCK, LANES), 1)
    koff = _iota((1, LANES), 1).astype(F32)
    dmat = _iota((LANES, LANES), 0) - (_iota((LANES, LANES), 1) >> 6)
    dead = -(1 << 20)
    win_pairs = [(d, d) for d in range(WIN_UNITS)]

    def chunk_pairs(ci):
        return [(WIN_UNITS + t, WIN_UNITS + ci * SLOT_CHUNK + t) for t in range(SLOT_CHUNK)]

    def setup(k):
        it = SELWIN_TILES * i + k
        for d in range(WIN_UNITS):
            u = it - (WIN_UNITS - 1) + d
            slot_unit[k, d] = jnp.where(u >= 0, u, -1)
        base = ((b * NSA_KV_GROUPS + g) * nq + it) * LANES

        def list_body(j4, cnt):
            for t in range(LIST_UNROLL):
                jj = j4 * LIST_UNROLL + t
                used = (pf_ref[base + 2 * jj] | pf_ref[base + 2 * jj + 1]) != 0

                @pl.when(used)
                def _(jj=jj, cnt=cnt):
                    slot_unit[k, WIN_UNITS + cnt] = jj
                cnt = cnt + used.astype(jnp.int32)
            return cnt

        nsel = lax.fori_loop(0, it // LIST_UNROLL + 1, list_body, jnp.int32(0))
        for t in range(SLOT_CHUNK - 1):
            slot_unit[k, WIN_UNITS + nsel + t] = -1
        rows = slice(k * Q_BLOCK, (k + 1) * Q_BLOCK)
        return dict(k=k, it=it, rows=rows, nchunk=(nsel + (SLOT_CHUNK - 1)) // SLOT_CHUNK,
                    q4=_stack_heads(q_ref[0, rows, :]), sel=sel_ref[0, 0, rows, :])

    def kv_tile(kv_ref, k, li):
        u = jnp.maximum(slot_unit[k, li], 0)
        return kv_ref[0, pl.ds(pl.multiple_of(u * Q_BLOCK, Q_BLOCK), Q_BLOCK), :]

    def score_slots(t, kv_ref, pairs, window, mx):
        k = t["k"]
        units = [slot_unit[k, li] for _, li in pairs]
        s_all = _nt_dot(t["q4"], jnp.concatenate([kv_tile(kv_ref, k, li) for _, li in pairs], axis=0))
        if not window:
            spread = jnp.concatenate(
                [jnp.where(dmat == 2 * u, 1.0, 0.0).astype(BF16) for u in units], axis=1)
            chosen = _dot(t["sel"], spread)
        for n, ((bs, _), u) in enumerate(zip(pairs, units)):
            s = s_all[:, n * LANES:(n + 1) * LANES].reshape(hq, Q_BLOCK, LANES)
            off = jnp.where(u >= 0, (t["it"] - u) * Q_BLOCK, dead)
            dist = qmk + off
            if window:
                allow = (dist >= 0) & (dist < NSA_WINDOW)
            else:
                allow = (dist >= 0) & (chosen[:, n * LANES:(n + 1) * LANES] > 0.5)
            neg = jnp.where(allow, 0.0, NEG_INF)
            kd = koff - off.astype(F32)
            for h in range(hq):
                sb = s[h] + (neg + slopes_ref[hq * g + h] * kd)
                sbuf[k, bs, h] = sb
                mx[k, h] = jnp.maximum(mx[k, h], sb)

    def score_pairs(t, kv_ref, pairs, window, mx):
        for n in range(0, len(pairs), 2):
            score_slots(t, kv_ref, pairs[n:n + 2], window, mx)

    def row_max(mx, k):
        return jnp.broadcast_to(jnp.max(mx[k], axis=-1, keepdims=True), mx.shape[1:])

    def weighted_values(t, kv_ref, pairs, mx):
        k = t["k"]
        tot = None
        for n in range(0, len(pairs), 2):
            two = pairs[n:n + 2]
            p = jnp.concatenate(
                [jnp.exp2(sbuf[k, bs] - mx[k]).astype(BF16).reshape(hq * Q_BLOCK, LANES)
                 for bs, _ in two], axis=1)
            v = jnp.concatenate([_ones_in_k_lanes(kv_tile(kv_ref, k, li)) for _, li in two], axis=0)
            d = _dot(p, v)
            tot = d if tot is None else tot + d
        return tot

    tiles = [setup(k) for k in range(SELWIN_TILES)]

    for t in tiles:
        k = t["k"]
        mrunw[k] = jnp.full(mrunw.shape[1:], NEG_INF, F32)
        mrun[k] = jnp.full(mrun.shape[1:], NEG_INF, F32)
        score_pairs(t, kvw_ref, win_pairs, True, mrunw)
        score_pairs(t, kvs_ref, chunk_pairs(0), False, mrun)
    for t in tiles:
        k = t["k"]
        mrunw[k] = row_max(mrunw, k)
        mrun[k] = row_max(mrun, k)
        accw[k] = weighted_values(t, kvw_ref, win_pairs, mrunw)
        acc[k] = weighted_values(t, kvs_ref, chunk_pairs(0), mrun)

    for t in tiles:
        k = t["k"]

        def more(ci, carry, t=t, k=k):
            mrunw[k] = jnp.full(mrunw.shape[1:], NEG_INF, F32)
            score_pairs(t, kvs_ref, chunk_pairs(ci), False, mrunw)
            m_old = mrun[k]
            m_new = jnp.maximum(m_old, row_max(mrunw, k))
            mrun[k] = m_new
            alpha = jnp.exp2(m_old - m_new).reshape(hq * Q_BLOCK, LANES)
            acc[k] = acc[k] * alpha + weighted_values(t, kvs_ref, chunk_pairs(ci), mrun)
            return carry

        lax.fori_loop(1, t["nchunk"], more, 0)

    for t in tiles:
        k = t["k"]
        rows = t["rows"]
        o_s = _normalise_unstack(acc[k])
        o_w = _normalise_unstack(accw[k])
        gates = ga_ref[0, rows, :]
        ya = (yc_ref[0, rows, :] + _expand_gate(gates, gexp_ref[0, 1]) * o_s
              + _expand_gate(gates, gexp_ref[0, 2]) * o_w)
        ya_ref[0, rows, :] = ya.astype(BF16)


def _selwin(pflags, slopes, qa, kva, sel, ga, yc, gexp):
    b, s, _ = qa.shape
    g = NSA_KV_GROUPS
    nq = s // Q_BLOCK
    tq = SELWIN_TILES * Q_BLOCK
    nlist = WIN_UNITS + nq + SLOT_CHUNK - 1
    rows = NSA_HPG * Q_BLOCK
    stat = (SELWIN_TILES, NSA_HPG, Q_BLOCK, LANES)
    grid_spec = pltpu.PrefetchScalarGridSpec(
        num_scalar_prefetch=1,
        grid=(b, g, s // tq),
        in_specs=[pl.BlockSpec(memory_space=pltpu.SMEM),
                  pl.BlockSpec((1, tq, QP_W), lambda bi, gi, i, pf: (bi, i, gi)),
                  pl.BlockSpec((1, s, LANES), lambda bi, gi, i, pf: (bi, 0, gi)),
                  pl.BlockSpec((1, s, LANES), lambda bi, gi, i, pf: (bi, 0, NSA_KV_GROUPS + gi)),
                  pl.BlockSpec((1, 1, tq, LANES), lambda bi, gi, i, pf: (bi, gi, i, 0)),
                  pl.BlockSpec((1, tq, LANES), lambda bi, gi, i, pf: (bi, i, 0)),
                  pl.BlockSpec((1, tq, GROUP_W), lambda bi, gi, i, pf: (bi, i, gi)),
                  pl.BlockSpec((1, 3, LANES, GROUP_W), lambda bi, gi, i, pf: (gi, 0, 0, 0))],
        out_specs=pl.BlockSpec((1, tq, GROUP_W), lambda bi, gi, i, pf: (bi, i, gi)),
        scratch_shapes=[pltpu.VMEM((SELWIN_TILES, WIN_UNITS + SLOT_CHUNK) + stat[1:], F32),
                        pltpu.SMEM((SELWIN_TILES, nlist), jnp.int32),
                        pltpu.VMEM(stat, F32),
                        pltpu.VMEM(stat, F32),
                        pltpu.VMEM((SELWIN_TILES, rows, LANES), F32),
                        pltpu.VMEM((SELWIN_TILES, rows, LANES), F32)],
    )
    return pl.pallas_call(
        functools.partial(_selwin_kernel, nq=nq),
        grid_spec=grid_spec,
        out_shape=jax.ShapeDtypeStruct((b, s, g * GROUP_W), BF16),
        compiler_params=_params(("parallel", "parallel", "arbitrary")),
        name="selwin",
    )(pflags, slopes, qa, kva, kva, sel, ga, yc, gexp)


def _dil_bias(slopes):
    c = DIL_BLOCK
    qi = np.arange(c)[:, None]
    ki = np.arange(2 * c)[None, :]
    delta = qi + c - ki
    groups = []
    for k, (window, dil) in enumerate(DIL_GROUPS):
        dist = jnp.asarray((delta * dil).astype(np.float32))
        variants = []
        for has_prev in (False, True):
            in_band = jnp.asarray((delta >= 0) & (delta <= window // dil) & ((ki >= c) | has_prev))
            heads = [jnp.where(in_band, -(slopes[DIL_HPG * k + h] * dist) * LOG2E, NEG_INF)
                     for h in range(DIL_HPG)]
            variants.append(jnp.stack(heads).reshape(DIL_HPG // 2, 2 * c, 2 * c))
        groups.append(jnp.stack(variants))
    return jnp.stack(groups)


def _dil_group(k, dil, sup, q_ref, kvc_ref, kvp_ref, bias_ref, mx_ref, w_ref, y_ref):
    c = DIL_BLOCK
    nblk = sup // (c * dil)
    has_halo = pl.program_id(1) > 0
    nch = DIL_HPG * HEAD_DIM // LANES
    upper = _iota((c, LANES), 1) >= HEAD_DIM

    def block(idx):
        r = idx // nblk
        n = idx - r * nblk
        start = r + dil * c * n
        rows = pl.ds(start, c, stride=dil)
        rows_prev = pl.ds(jnp.maximum(start - dil * c, r), c, stride=dil)
        rows_halo = pl.ds(r + dil * c * (nblk - 1), c, stride=dil)
        variant = jnp.where(jnp.logical_or(n >= 1, has_halo), 1, 0)

        def prev(ch):
            return jnp.where(n >= 1, kvc_ref[ch, 0, rows_prev, :], kvp_ref[ch, 0, rows_halo, :]).astype(BF16)

        for ch in range(nch):
            qb = q_ref[ch, 0, rows, :].astype(BF16)
            zero = jnp.zeros_like(qb)
            qs = jnp.concatenate([jnp.where(upper, zero, qb), jnp.where(upper, qb, zero)], axis=0)
            kband = jnp.concatenate([prev(ch), kvc_ref[ch, 0, rows, :].astype(BF16)], axis=0)
            vband = jnp.concatenate([prev(nch + ch), kvc_ref[nch + ch, 0, rows, :].astype(BF16)], axis=0)
            s = _nt_dot(qs, kband) + bias_ref[0, variant, ch]
            m = jnp.max(s, axis=-1, keepdims=True)
            e = jnp.exp2(s - m).astype(BF16)
            ones = jnp.ones((2 * c, LANES), BF16)
            ol = _dot(e, jnp.concatenate([vband, ones], axis=1))
            lt = jnp.maximum(ol[:, LANES:], TINY)
            o2 = ol[:, :LANES] * (1.0 / lt)
            l2 = m * LN2 + jnp.log(lt)
            o = jnp.where(upper, o2[c:], o2[:c])
            lse = jnp.where(upper, l2[c:], l2[:c])
            if k == 0:
                mx_ref[ch, rows, :] = lse
                w_ref[ch, rows, :] = jnp.ones_like(lse)
                y_ref[ch, rows, :] = o
            else:
                mx_old = mx_ref[ch, rows, :]
                mx_new = jnp.maximum(mx_old, lse)
                a = jnp.exp(mx_old - mx_new)
                bnew = jnp.exp(lse - mx_new)
                mx_ref[ch, rows, :] = mx_new
                w_ref[ch, rows, :] = w_ref[ch, rows, :] * a + bnew
                y_ref[ch, rows, :] = y_ref[ch, rows, :] * a + bnew * o

    def body(t, carry):
        for j in range(DIL_UNROLL):
            block(t * DIL_UNROLL + j)
        return carry

    lax.fori_loop(0, sup // (c * DIL_UNROLL), body, 0)


def _dil_kernel(q_ref, kvc_ref, kvp_ref, bias_ref, yb_ref, mx_ref, w_ref, y_ref, *, sup):
    gi = pl.program_id(2)
    for k, (_, dil) in enumerate(DIL_GROUPS):
        @pl.when(gi == k)
        def _(k=k, dil=dil):
            _dil_group(k, dil, sup, q_ref, kvc_ref, kvp_ref, bias_ref, mx_ref, w_ref, y_ref)

    @pl.when(gi == len(DIL_GROUPS) - 1)
    def _():
        for ch in range(DIL_HPG * HEAD_DIM // LANES):
            yb_ref[0, :, ch * LANES:(ch + 1) * LANES] = (y_ref[ch] * (1.0 / w_ref[ch])).astype(BF16)


def _dil(qb, kvb, bias, sup):
    _, b, s, _ = qb.shape
    hw = DIL_HPG * HEAD_DIM
    nch = hw // LANES
    ng = len(DIL_GROUPS)
    return pl.pallas_call(
        functools.partial(_dil_kernel, sup=sup),
        grid=(b, s // sup, ng),
        in_specs=[pl.BlockSpec((nch, 1, sup, LANES), lambda bi, si, gi: (gi, bi, si, 0)),
                  pl.BlockSpec((2 * nch, 1, sup, LANES), lambda bi, si, gi: (gi, bi, si, 0)),
                  pl.BlockSpec((2 * nch, 1, sup, LANES),
                               lambda bi, si, gi: (gi, bi, jnp.maximum(si - 1, 0), 0)),
                  pl.BlockSpec((1, 2, nch, 2 * DIL_BLOCK, 2 * DIL_BLOCK),
                               lambda bi, si, gi: (gi, 0, 0, 0, 0))],
        out_specs=pl.BlockSpec((1, sup, hw), lambda bi, si, gi: (bi, si, 0)),
        out_shape=jax.ShapeDtypeStruct((b, s, hw), BF16),
        scratch_shapes=[pltpu.VMEM((nch, sup, LANES), F32), pltpu.VMEM((nch, sup, LANES), F32),
                        pltpu.VMEM((nch, sup, LANES), F32)],
        compiler_params=_params(("parallel", "parallel", "arbitrary")),
        name="dil",
    )(qb, kvb, kvb, bias)


def _mixffn_kernel(x_ref, g_ref, ya_ref, yb_ref, wg_ref, ua_ref, ub_ref, wo_ref,
                   g2_ref, wi_ref, wf_ref, gf_ref, o_ref, *, tf, final_norm):
    x = x_ref[...]
    d = x.shape[1]
    hn = _rms(x, g_ref[...]).astype(BF16)
    gate_a = jax.nn.sigmoid(_dot(hn, wg_ref[:, 0:d]))
    gate_b = jax.nn.sigmoid(_dot(hn, wg_ref[:, d:2 * d]))
    merged = gate_a * _dot(ya_ref[...], ua_ref[...]) + gate_b * _dot(yb_ref[...], ub_ref[...])
    x = x + _dot(merged.astype(BF16), wo_ref[...])
    dff = wf_ref.shape[0]
    hn = _rms(x, g2_ref[...]).astype(BF16)
    y = x
    for lo, hi in tf:
        gt = _dot(hn, wi_ref[:, lo:hi])
        up = _dot(hn, wi_ref[:, dff + lo:dff + hi])
        act = ((gt * jax.nn.sigmoid(gt)) * up).astype(BF16)
        y = y + _dot(act, wf_ref[lo:hi, :])
    if final_norm:
        y = _rms(y, gf_ref[...])
    o_ref[...] = y


def _mixffn(x2, g, ya, yb, wg, ua, ub, wo, g2, w_in, w_out, gf, layer, tm, tf, final_norm):
    n, d = x2.shape
    row = lambda i: (i, 0)
    fixed = lambda i: (0, 0)
    return pl.pallas_call(
        functools.partial(_mixffn_kernel, tf=tf, final_norm=final_norm),
        grid=(n // tm,),
        in_specs=([pl.BlockSpec((tm, d), row), pl.BlockSpec((1, d), fixed),
                   pl.BlockSpec((tm, ya.shape[1]), row), pl.BlockSpec((tm, yb.shape[1]), row)]
                  + [_layer_spec(w, layer) for w in (wg, ua, ub, wo)]
                  + [pl.BlockSpec((1, d), fixed), _layer_spec(w_in, layer), _layer_spec(w_out, layer),
                     pl.BlockSpec((1, d), fixed)]),
        out_specs=pl.BlockSpec((tm, d), row),
        out_shape=jax.ShapeDtypeStruct((n, d), F32),
        compiler_params=pltpu.CompilerParams(dimension_semantics=("parallel",),
                                             vmem_limit_bytes=MIXFFN_VMEM_LIMIT),
        name="mixffn",
    )(x2, g, ya, yb, wg, ua, ub, wo, g2, w_in, w_out, gf)


def _alibi_slopes():
    k = jnp.arange(1, N_ALIBI + 1, dtype=F32)
    s = jnp.exp2(-8.0 * k / N_ALIBI)
    nsa = s[2 * DIL_HPG:2 * DIL_HPG + NSA_HEADS]
    dil = jnp.concatenate([s[:2 * DIL_HPG], s[2 * DIL_HPG + NSA_HEADS:]])
    return nsa, dil


def _overlap(seq, ncp):
    n_c = (seq - CMP_BLOCK) // CMP_STRIDE + 1
    n_s = seq // SEL_BLOCK
    c_start = np.arange(n_c) * CMP_STRIDE
    s_start = np.arange(n_s) * SEL_BLOCK
    ov = np.clip(np.minimum(c_start[:, None] + CMP_BLOCK, s_start[None, :] + SEL_BLOCK)
                 - np.maximum(c_start[:, None], s_start[None, :]), 0, None).astype(np.float32) / CMP_BLOCK
    out = np.zeros((ncp, LANES), np.float32)
    out[:n_c, :n_s] = ov
    return jnp.asarray(out, BF16)


def _proj_weights(w_in):
    dep, d, _ = w_in.shape
    scale = HEAD_DIM ** -0.5 * LOG2E
    c0 = NSA_HEADS * HEAD_DIM
    c1 = c0 + 3 * 2 * NSA_KV_GROUPS * HEAD_DIM
    c2 = c1 + NSA_HEADS * 3
    c3 = c2 + 3 * DIL_HEADS * HEAD_DIM
    qa = (w_in[..., :c0] * scale).astype(BF16)
    wb = w_in.astype(BF16)
    kva = wb[..., c0:c1].reshape(dep, d, 3, 2, NSA_KV_GROUPS, HEAD_DIM)
    kva = kva.transpose(0, 1, 2, 4, 3, 5).reshape(dep, d, CMP_W + KVA_W)
    ga = jnp.pad(wb[..., c1:c2], ((0, 0), (0, 0), (0, GA_W - (c2 - c1))))
    qkvb = w_in[..., c2:c3].reshape(dep, d, 3, len(DIL_GROUPS), DIL_HPG * HEAD_DIM)
    qb = (qkvb[:, :, 0] * scale).astype(BF16).reshape(dep, d, QB_W)
    kvb = qkvb[:, :, 1:3].astype(BF16).reshape(dep, d, KVB_W)
    w_a = jnp.concatenate([qa, kva, qb, kvb, ga], axis=-1)
    return w_a, wb[..., c3:]


def _compress_weights(pe_k, pe_v, w_ck1, w_ck2, w_cv1, w_cv2):
    dep = pe_k.shape[0]
    half = CMP_BLOCK // 2
    ck = w_ck1.astype(BF16).reshape(dep, CMP_BLOCK, HEAD_DIM, CMP_HIDDEN)
    cv = w_cv1.astype(BF16).reshape(dep, CMP_BLOCK, HEAD_DIM, CMP_HIDDEN)
    z = jnp.zeros_like(ck)
    w = jnp.concatenate([jnp.concatenate([ck, z], axis=-1),
                         jnp.concatenate([z, cv], axis=-1)], axis=2)
    feat = half * 2 * HEAD_DIM
    wa = w[:, :half].reshape(dep, feat, 2 * CMP_HIDDEN)
    wb = w[:, half:].reshape(dep, feat, 2 * CMP_HIDDEN)
    pe = jnp.concatenate([pe_k, pe_v], axis=-1)
    pe2 = jnp.stack([pe[:, :half].reshape(dep, feat), pe[:, half:].reshape(dep, feat)], axis=1)
    pe2 = jnp.pad(pe2, ((0, 0), (0, 6), (0, 0)))
    z2 = jnp.zeros_like(w_ck2)
    w2 = jnp.concatenate([jnp.concatenate([w_ck2, z2], axis=-1),
                          jnp.concatenate([z2, w_cv2], axis=-1)], axis=1).astype(BF16)
    return pe2, wa, wb, w2


def kernel(x, norm_mix, w_in, pe_k, pe_v, w_ck1, w_ck2, w_cv1, w_cv2, w_up_nsa, w_up_dil,
           w_out, norm_ffn, w_ffn_in, w_ffn_out, norm_final):
    b, s, d = x.shape
    depth = w_in.shape[0]
    n = b * s
    nq = s // Q_BLOCK
    nch = s // CMP_STRIDE
    n_s = s // SEL_BLOCK
    assert s % DIL_SUPER == 0 and n_s <= LANES
    n_top = min(SEL_TOPK, n_s)
    tm = 512
    dff = w_ffn_out.shape[1]
    half = (dff // MXU_TILE + 1) // 2 * MXU_TILE
    tf = ((0, half), (half, dff))

    slopes_nsa, slopes_dil = _alibi_slopes()
    slopes_nsa = slopes_nsa * LOG2E
    ov = _overlap(s, nch)
    gexp = _gate_spread()
    dbias = _dil_bias(slopes_dil)
    w_a, w_gm = _proj_weights(w_in)
    pe2, wa, wb, w2 = _compress_weights(pe_k, pe_v, w_ck1, w_ck2, w_cv1, w_cv2)
    ua = w_up_nsa.astype(BF16)
    ub = w_up_dil.astype(BF16)
    wo = w_out.astype(BF16)
    wfi = w_ffn_in.astype(BF16)
    wfo = w_ffn_out.astype(BF16)
    gf = norm_final.reshape(1, d)

    x2 = x.reshape(n, d)
    for l in range(depth):
        g_mix = norm_mix[l].reshape(1, d)
        qa, xc, kva, qb, kvb, ga = _proj(x2, g_mix, w_a, l, tm)
        qa = qa.reshape(b, s, QA_W)
        kva = kva.reshape(b, s, KVA_W)
        ga = ga.reshape(b, s, GA_W)
        kvc = _compress(xc, pe2[l], wa[l], wb[l], w2[l], b)
        yc, sel, flags = _cmp(slopes_nsa, qa, kvc, ga, ov, gexp, n_top)
        pflags = flags.reshape(-1)
        ya = _selwin(pflags, slopes_nsa, qa, kva, sel, ga, yc, gexp)
        yb = _dil(qb.reshape(-1, b, s, LANES), kvb.reshape(-1, b, s, LANES), dbias, DIL_SUPER)
        x2 = _mixffn(x2, g_mix, ya.reshape(n, NSA_HEADS * HEAD_DIM), yb.reshape(n, DIL_HPG * HEAD_DIM),
                     w_gm, ua, ub, wo, norm_ffn[l].reshape(1, d), wfi, wfo, gf,
                     l, tm, tf, final_norm=(l == depth - 1))
    return x2.reshape(b, s, d)
```

```python
import functools

import numpy as np
import jax
import jax.numpy as jnp
from jax import lax
from jax.experimental import pallas as pl
from jax.experimental.pallas import tpu as pltpu

F32 = jnp.float32
BF16 = jnp.bfloat16

HEAD_DIM = 64
NSA_HEADS = 8
NSA_KV_GROUPS = 2
NSA_HPG = NSA_HEADS // NSA_KV_GROUPS
CMP_BLOCK = 32
CMP_STRIDE = 16
CMP_HIDDEN = 256
SEL_BLOCK = 64
SEL_TOPK = 16
NSA_WINDOW = 512
Q_BLOCK = 128
FORCE_SCORE = 1e6
DIL_GROUPS = ((128, 1), (512, 4), (2048, 16))
DIL_HPG = 4
DIL_HEADS = DIL_HPG * len(DIL_GROUPS)
DIL_BLOCK = 128
N_ALIBI = NSA_HEADS + DIL_HEADS
EPS = 1e-6
NEG_INF = -1e30
TINY = 1e-30
LOG2E = 1.4426950408889634
LN2 = 0.6931471805599453

LANES = 128
MXU_TILE = 256
GROUP_W = NSA_HPG * HEAD_DIM
DIL_SUPER = 2048
DIL_UNROLL = 4
CMP_TILES = 8
VMEM_LIMIT = 48 * 1024 * 1024
MIXFFN_VMEM_LIMIT = 56 * 1024 * 1024


def _dot(a, b):
    return jnp.dot(a, b, preferred_element_type=F32)


def _nt_dot(a, b):
    return lax.dot_general(a, b, (((1,), (1,)), ((), ())), preferred_element_type=F32)


def _iota(shape, dim):
    return lax.broadcasted_iota(jnp.int32, shape, dim)


def _rms(x, g):
    ms = jnp.mean(x * x, axis=-1, keepdims=True)
    return (x * lax.rsqrt(ms + EPS)) * g


def _params(sem):
    return pltpu.CompilerParams(dimension_semantics=sem, vmem_limit_bytes=VMEM_LIMIT)


QA_W = NSA_HEADS * LANES
QP_W = NSA_HPG * LANES
CMP_W = NSA_KV_GROUPS * LANES
KVA_W, QB_W, KVB_W, GA_W = 512, 768, 1536, 128


def _proj_kernel(x_ref, g_ref, wq_ref, wkva_ref, wqb_ref, wkvb_ref, wga_ref,
                 qa_ref, xc_ref, kva_ref, qb_ref, kvb_ref, ga_ref, cmp_scr):
    y = _rms(x_ref[...], g_ref[...]).astype(BF16)
    low = _iota((x_ref.shape[0], LANES), 1) < HEAD_DIM
    for p in range(0, NSA_HEADS // 2, 2):
        r2 = _dot(y, wq_ref[:, p * LANES:(p + 2) * LANES])
        for k in range(2):
            r = r2[:, k * LANES:(k + 1) * LANES]
            h0 = 2 * (p + k)
            qa_ref[:, h0 * LANES:(h0 + 1) * LANES] = jnp.where(low, r, 0.0).astype(BF16)
            qa_ref[:, (h0 + 1) * LANES:(h0 + 2) * LANES] = (
                jnp.where(low, pltpu.roll(r, HEAD_DIM, 1), 0.0).astype(BF16))
    rc = _dot(y, wkva_ref[:, 0:CMP_W])
    nrow = xc_ref.shape[1]
    for gi in range(NSA_KV_GROUPS):
        cmp_scr[gi] = rc[:, gi * LANES:(gi + 1) * LANES]
        for l in range(CMP_STRIDE):
            xc_ref[gi, :, l * LANES:(l + 1) * LANES] = (
                cmp_scr[gi, pl.ds(l, nrow, stride=CMP_STRIDE), :].astype(BF16))
    kva_ref[...] = _dot(y, wkva_ref[:, CMP_W:CMP_W + KVA_W]).astype(BF16)
    for j in range(0, QB_W // LANES, 2):
        r = _dot(y, wqb_ref[:, j * LANES:(j + 2) * LANES])
        qb_ref[j] = r[:, :LANES]
        qb_ref[j + 1] = r[:, LANES:]
    ngroups = len(DIL_GROUPS)
    for kv in range(2):
        for gi in range(ngroups):
            c = (kv * ngroups + gi) * 2 * LANES
            r = _dot(y, wkvb_ref[:, c:c + 2 * LANES])
            kvb_ref[4 * gi + 2 * kv] = r[:, :LANES]
            kvb_ref[4 * gi + 2 * kv + 1] = r[:, LANES:]
    ga_ref[...] = jax.nn.sigmoid(_dot(y, wga_ref[...]))


def _layer_spec(w, layer):
    return pl.BlockSpec((None,) + w.shape[1:], lambda i: (layer, 0, 0), pipeline_mode=pl.Buffered(1))


def _proj(x2, g, weights, layer, tm):
    n, d = x2.shape
    row = lambda i: (i, 0)
    fixed = lambda i: (0, 0)
    return pl.pallas_call(
        _proj_kernel,
        grid=(n // tm,),
        in_specs=([pl.BlockSpec((tm, d), row), pl.BlockSpec((1, d), fixed)]
                  + [_layer_spec(w, layer) for w in weights]),
        out_specs=[pl.BlockSpec((tm, QA_W), row),
                   pl.BlockSpec((NSA_KV_GROUPS, tm // CMP_STRIDE, CMP_STRIDE * LANES), lambda i: (0, i, 0)),
                   pl.BlockSpec((tm, KVA_W), row),
                   pl.BlockSpec((QB_W // LANES, tm, LANES), lambda i: (0, i, 0)),
                   pl.BlockSpec((KVB_W // LANES, tm, LANES), lambda i: (0, i, 0)),
                   pl.BlockSpec((tm, GA_W), row)],
        out_shape=[jax.ShapeDtypeStruct((n, QA_W), BF16),
                   jax.ShapeDtypeStruct((NSA_KV_GROUPS, n // CMP_STRIDE, CMP_STRIDE * LANES), BF16),
                   jax.ShapeDtypeStruct((n, KVA_W), BF16),
                   jax.ShapeDtypeStruct((QB_W // LANES, n, LANES), F32),
                   jax.ShapeDtypeStruct((KVB_W // LANES, n, LANES), F32),
                   jax.ShapeDtypeStruct((n, GA_W), F32)],
        scratch_shapes=[pltpu.VMEM((NSA_KV_GROUPS, tm, LANES), F32)],
        compiler_params=_params(("parallel",)),
        name="proj",
    )(x2, g, *weights)


def _compress_kernel(x_ref, pe_ref, wa_ref, wb_ref, w2_ref, o_ref):
    xf = x_ref[0].astype(F32)
    nch = xf.shape[0]
    xa = (xf + pe_ref[0:1, :]).astype(BF16)
    xb = (xf + pe_ref[1:2, :]).astype(BF16)
    ya = _dot(xa, wa_ref[...])
    yb = _dot(xb, wb_ref[...])
    h = ya + pltpu.roll(yb, nch - 1, 0)
    a = (h * jax.nn.sigmoid(h)).astype(BF16)
    o_ref[0, 0] = _dot(a, w2_ref[...]).astype(BF16)


def _compress(xc, pe, wa, wb, w2, b):
    g, rows, f = xc.shape
    nch = rows // b
    hid = wa.shape[1]
    return pl.pallas_call(
        _compress_kernel,
        grid=(b, g),
        in_specs=[pl.BlockSpec((1, nch, f), lambda i, j: (j, i, 0)),
                  pl.BlockSpec((8, f), lambda i, j: (0, 0)),
                  pl.BlockSpec((f, hid), lambda i, j: (0, 0)),
                  pl.BlockSpec((f, hid), lambda i, j: (0, 0)),
                  pl.BlockSpec((hid, LANES), lambda i, j: (0, 0))],
        out_specs=pl.BlockSpec((1, 1, nch, LANES), lambda i, j: (i, j, 0, 0)),
        out_shape=jax.ShapeDtypeStruct((b, g, nch, LANES), BF16),
        compiler_params=_params(("parallel", "parallel")),
        name="compress",
    )(xc, pe, wa, wb, w2)


def _gate_spread():
    g = np.arange(NSA_KV_GROUPS)[:, None, None, None]
    br = np.arange(3)[None, :, None, None]
    r = np.arange(LANES)[None, None, :, None]
    c = np.arange(GROUP_W)[None, None, None, :]
    return jnp.asarray(r == 3 * (NSA_HPG * g + c // HEAD_DIM) + br, BF16)


def _stack_heads(q):
    return jnp.concatenate([q[:, LANES * h:LANES * (h + 1)] for h in range(NSA_HPG)], axis=0)


def _unstack_heads(o4):
    lane = _iota((Q_BLOCK, LANES), 1)
    o = [o4[Q_BLOCK * h:Q_BLOCK * (h + 1)] for h in range(NSA_HPG)]
    c0 = jnp.where(lane < HEAD_DIM, pltpu.roll(o[0], HEAD_DIM, 1), o[1])
    c1 = jnp.where(lane < HEAD_DIM, pltpu.roll(o[2], HEAD_DIM, 1), o[3])
    return jnp.concatenate([c0, c1], axis=1)


def _row_sums_all_lanes(a):
    lane = _iota(a.shape, 1)
    return jnp.where(lane < HEAD_DIM, a, pltpu.roll(a, HEAD_DIM, 1))


def _normalise_unstack(a):
    return _unstack_heads(a * (1.0 / jnp.maximum(_row_sums_all_lanes(a), TINY)))


def _expand_gate(gates, spread):
    hi = gates.astype(BF16)
    lo = (gates - hi.astype(F32)).astype(BF16)
    return _dot(hi, spread) + _dot(lo, spread)


def _ones_in_k_lanes(kv):
    lane = _iota(kv.shape, 1)
    return jnp.where(lane < HEAD_DIM, jnp.ones_like(kv), kv)


def _cmp_kernel(slopes_ref, q_ref, kvc_ref, ga_ref, ov_ref, gexp_ref,
                yc_ref, sel_ref, fl_ref, bias0, cq, rhs, big_ref, *, n_top):
    g = pl.program_id(1)
    i = pl.program_id(2)
    hq = NSA_HPG
    ncp = cq.shape[1]

    @pl.when(i == 0)
    def _():
        kpos = CMP_STRIDE * _iota((Q_BLOCK, ncp), 1) + (CMP_BLOCK - 1)
        cq[...] = kpos - _iota((Q_BLOCK, ncp), 0)
        kf = kpos[0:8].astype(F32)
        for h in range(hq):
            bias0[h] = slopes_ref[hq * g + h] * kf
        rhs[:, 0:LANES] = _ones_in_k_lanes(kvc_ref[0, 0])
        rhs[:, LANES:2 * LANES] = ov_ref[...]

    def attend(width):
        def run():
            kv = kvc_ref[0, 0, 0:width, :]
            for k in range(CMP_TILES):
                q4 = _stack_heads(q_ref[0, k * Q_BLOCK:(k + 1) * Q_BLOCK, :])
                s = _nt_dot(q4, kv).reshape(hq, Q_BLOCK, width)
                t0 = (CMP_TILES * i + k) * Q_BLOCK
                neg = jnp.where(cq[:, 0:width] <= t0, 0.0, NEG_INF)
                tf = t0.astype(F32)
                ps = []
                for h in range(hq):
                    sb = s[h] + (neg + (bias0[h, 0:1, 0:width] - slopes_ref[hq * g + h] * tf))
                    m = jnp.max(sb, axis=-1, keepdims=True)
                    m = jnp.where(m < 0.1 * NEG_INF, -NEG_INF, m)
                    ps.append(jnp.exp2(sb - m).astype(BF16))
                big_ref[k] = _dot(jnp.concatenate(ps, axis=0), rhs[0:width, :])
        return run

    units = ncp // LANES
    per_tile = Q_BLOCK // CMP_STRIDE
    last_visible = (per_tile * (CMP_TILES * i + CMP_TILES - 1)
                    + (Q_BLOCK - CMP_BLOCK) // CMP_STRIDE)
    if units == 1:
        attend(ncp)()
    else:
        lax.switch(jnp.minimum(last_visible // LANES, units - 1),
                   [attend(LANES * (u + 1)) for u in range(units)])

    for k in range(CMP_TILES):
        it = CMP_TILES * i + k
        rows = slice(k * Q_BLOCK, (k + 1) * Q_BLOCK)
        big = big_ref[k]
        inv = 1.0 / jnp.maximum(_row_sums_all_lanes(big[:, 0:LANES]), TINY)
        oc = _unstack_heads(big[:, 0:LANES] * inv)
        yc_ref[0, rows, :] = _expand_gate(ga_ref[0, rows, :], gexp_ref[0, 0]) * oc
        pov = big[:, LANES:2 * LANES] * inv
        imp = pov[0:Q_BLOCK]
        for h in range(1, hq):
            imp = imp + pov[Q_BLOCK * h:Q_BLOCK * (h + 1)]

        qrow = _iota((Q_BLOCK, LANES), 0)
        j = _iota((Q_BLOCK, LANES), 1)
        jt = 2 * it + (qrow >> 6)
        valid = j <= jt
        forced = (j == 0) | (j == jt) | (j == jt - 1)
        score = jnp.where(forced, FORCE_SCORE, jnp.where(valid, imp, -1.0))
        st = score.T
        jr = _iota((LANES, Q_BLOCK), 0)
        jtt = 2 * it + (_iota((LANES, Q_BLOCK), 1) >> 6)
        taken = -3e38
        cur = jnp.where((jr == 0) | (jr == jtt) | (jr == jtt - 1), taken, st)
        for _ in range(n_top - 3):
            mx = jnp.max(cur, axis=0, keepdims=True)
            first = jnp.min(jnp.where(cur == mx, jr, LANES), axis=0, keepdims=True)
            cur = jnp.where(jr == first, taken, cur)
        keep_t = jnp.where((cur == taken) & (st >= 0.0), 1.0, 0.0)
        sel_ref[0, 0, rows, :] = keep_t.T.astype(BF16)

        lanes_sq = (LANES, LANES)
        r_i = _iota(lanes_sq, 0)
        c_i = _iota(lanes_sq, 1)
        any_q = jnp.broadcast_to(jnp.max(keep_t, axis=-1, keepdims=True), lanes_sq).astype(BF16)
        pair = jnp.where((c_i >> 1) == r_i, 1.0, 0.0).astype(BF16)
        used = jnp.where(_dot(pair, any_q) > 0.0, 1.0, 0.0)
        before = jnp.where(c_i < r_i, 1.0, 0.0).astype(BF16)
        place = _dot(before, used.astype(BF16))
        onehot = jnp.where((place == c_i.astype(F32)) & (used > 0.0), 1.0, 0.0).astype(BF16)
        unit_ids = _iota((8, LANES), 1).astype(F32).astype(BF16)
        listed = _dot(unit_ids, onehot)[0:1]
        count = _dot(jnp.ones((8, LANES), BF16), used.astype(BF16))[0:1]
        pos = _iota((1, LANES), 1)
        row = jnp.where(pos.astype(F32) < count, listed, -1.0)
        fl_ref[k] = jnp.where(pos == LANES - 1, count, row).astype(jnp.int32)


def _cmp(slopes, qa, kvc, ga, ov, gexp, n_top):
    b, s, _ = qa.shape
    g = NSA_KV_GROUPS
    nq = s // Q_BLOCK
    ncp = kvc.shape[2]
    tq = CMP_TILES * Q_BLOCK
    nt = s // tq
    return pl.pallas_call(
        functools.partial(_cmp_kernel, n_top=n_top),
        grid=(b, g, nt),
        in_specs=[pl.BlockSpec(memory_space=pltpu.SMEM),
                  pl.BlockSpec((1, tq, QP_W), lambda bi, gi, i: (bi, i, gi)),
                  pl.BlockSpec((1, 1, ncp, LANES), lambda bi, gi, i: (bi, gi, 0, 0)),
                  pl.BlockSpec((1, tq, LANES), lambda bi, gi, i: (bi, i, 0)),
                  pl.BlockSpec((ncp, LANES), lambda bi, gi, i: (0, 0)),
                  pl.BlockSpec((1, 3, LANES, GROUP_W), lambda bi, gi, i: (gi, 0, 0, 0))],
        out_specs=[pl.BlockSpec((1, tq, GROUP_W), lambda bi, gi, i: (bi, i, gi)),
                   pl.BlockSpec((1, 1, tq, LANES), lambda bi, gi, i: (bi, gi, i, 0)),
                   pl.BlockSpec((CMP_TILES, 1, LANES), lambda bi, gi, i: ((bi * g + gi) * nt + i, 0, 0))],
        out_shape=[jax.ShapeDtypeStruct((b, s, g * GROUP_W), F32),
                   jax.ShapeDtypeStruct((b, g, s, LANES), BF16),
                   jax.ShapeDtypeStruct((b * g * nq, 1, LANES), jnp.int32)],
        scratch_shapes=[pltpu.VMEM((NSA_HPG, 8, ncp), F32),
                        pltpu.VMEM((Q_BLOCK, ncp), jnp.int32),
                        pltpu.VMEM((ncp, 2 * LANES), BF16),
                        pltpu.VMEM((CMP_TILES, NSA_HPG * Q_BLOCK, 2 * LANES), F32)],
        compiler_params=_params(("parallel", "parallel", "arbitrary")),
        name="cmp",
    )(slopes, qa, kvc, ga, ov, gexp)


WIN_UNITS = NSA_WINDOW // Q_BLOCK + 1
SLOT_CHUNK = 10


SELWIN_TILES = 4


def _selwin_kernel(pf_ref, slopes_ref, q_ref, kvs_ref, kvw_ref, sel_ref, ga_ref, yc_ref,
                    gexp_ref, ya_ref, sbuf, mrun, mrunw, acc, accw, *, nq):
    b = pl.program_id(0)
    g = pl.program_id(1)
    i = pl.program_id(2)
    hq = NSA_HPG
    qmk = _iota((Q_BLOCK, LANES), 0) - _iota((Q_BLOCK, LANES), 1)
    koff = _iota((1, LANES), 1).astype(F32)
    dmat = _iota((LANES, LANES), 0) - (_iota((LANES, LANES), 1) >> 6)
    dead = -(1 << 20)
    win_pairs = [(d, d) for d in range(WIN_UNITS)]

    def chunk_pairs(ci):
        return [(WIN_UNITS + t, WIN_UNITS + ci * SLOT_CHUNK + t) for t in range(SLOT_CHUNK)]

    def setup(k):
        it = SELWIN_TILES * i + k
        base = ((b * NSA_KV_GROUPS + g) * nq + it) * LANES
        nsel = pf_ref[base + LANES - 1]
        rows = slice(k * Q_BLOCK, (k + 1) * Q_BLOCK)
        return dict(k=k, it=it, base=base, rows=rows, nchunk=(nsel + (SLOT_CHUNK - 1)) // SLOT_CHUNK,
                    q4=_stack_heads(q_ref[0, rows, :]), sel=sel_ref[0, 0, rows, :])

    def unit_of(t, li):
        if isinstance(li, int) and li < WIN_UNITS:
            u = t["it"] - (WIN_UNITS - 1) + li
            return jnp.where(u >= 0, u, -1)
        return pf_ref[t["base"] + li - WIN_UNITS]

    def kv_tile(kv_ref, u):
        return kv_ref[0, pl.ds(pl.multiple_of(jnp.maximum(u, 0) * Q_BLOCK, Q_BLOCK), Q_BLOCK), :]

    def score_slots(t, kv_ref, pairs, window, mx):
        k = t["k"]
        units = [unit_of(t, li) for _, li in pairs]
        s_all = _nt_dot(t["q4"], jnp.concatenate([kv_tile(kv_ref, u) for u in units], axis=0))
        if not window:
            spread = jnp.concatenate(
                [jnp.where(dmat == 2 * u, 1.0, 0.0).astype(BF16) for u in units], axis=1)
            chosen = _dot(t["sel"], spread)
        for n, ((bs, _), u) in enumerate(zip(pairs, units)):
            s = s_all[:, n * LANES:(n + 1) * LANES].reshape(hq, Q_BLOCK, LANES)
            off = jnp.where(u >= 0, (t["it"] - u) * Q_BLOCK, dead)
            dist = qmk + off
            if window:
                allow = (dist >= 0) & (dist < NSA_WINDOW)
            else:
                allow = (dist >= 0) & (chosen[:, n * LANES:(n + 1) * LANES] > 0.5)
            neg = jnp.where(allow, 0.0, NEG_INF)
            kd = koff - off.astype(F32)
            for h in range(hq):
                sb = s[h] + (neg + slopes_ref[hq * g + h] * kd)
                sbuf[k, bs, h] = sb
                mx[k, h] = jnp.maximum(mx[k, h], sb)

    def score_pairs(t, kv_ref, pairs, window, mx):
        for n in range(0, len(pairs), 2):
            score_slots(t, kv_ref, pairs[n:n + 2], window, mx)

    def row_max(mx, k):
        return jnp.broadcast_to(jnp.max(mx[k], axis=-1, keepdims=True), mx.shape[1:])

    def weighted_values(t, kv_ref, pairs, mx):
        k = t["k"]
        tot = None
        for n in range(0, len(pairs), 2):
            two = pairs[n:n + 2]
            p = jnp.concatenate(
                [jnp.exp2(sbuf[k, bs] - mx[k]).astype(BF16).reshape(hq * Q_BLOCK, LANES)
                 for bs, _ in two], axis=1)
            v = jnp.concatenate([_ones_in_k_lanes(kv_tile(kv_ref, unit_of(t, li))) for _, li in two], axis=0)
            d = _dot(p, v)
            tot = d if tot is None else tot + d
        return tot

    tiles = [setup(k) for k in range(SELWIN_TILES)]

    for t in tiles:
        k = t["k"]
        mrunw[k] = jnp.full(mrunw.shape[1:], NEG_INF, F32)
        mrun[k] = jnp.full(mrun.shape[1:], NEG_INF, F32)
        score_pairs(t, kvw_ref, win_pairs, True, mrunw)
        score_pairs(t, kvs_ref, chunk_pairs(0), False, mrun)
    for t in tiles:
        k = t["k"]
        mrunw[k] = row_max(mrunw, k)
        mrun[k] = row_max(mrun, k)
        accw[k] = weighted_values(t, kvw_ref, win_pairs, mrunw)
        acc[k] = weighted_values(t, kvs_ref, chunk_pairs(0), mrun)

    for t in tiles:
        k = t["k"]

        def more(ci, carry, t=t, k=k):
            mrunw[k] = jnp.full(mrunw.shape[1:], NEG_INF, F32)
            score_pairs(t, kvs_ref, chunk_pairs(ci), False, mrunw)
            m_old = mrun[k]
            m_new = jnp.maximum(m_old, row_max(mrunw, k))
            mrun[k] = m_new
            alpha = jnp.exp2(m_old - m_new).reshape(hq * Q_BLOCK, LANES)
            acc[k] = acc[k] * alpha + weighted_values(t, kvs_ref, chunk_pairs(ci), mrun)
            return carry

        lax.fori_loop(1, t["nchunk"], more, 0)

    for t in tiles:
        k = t["k"]
        rows = t["rows"]
        o_s = _normalise_unstack(acc[k])
        o_w = _normalise_unstack(accw[k])
        gates = ga_ref[0, rows, :]
        ya = (yc_ref[0, rows, :] + _expand_gate(gates, gexp_ref[0, 1]) * o_s
              + _expand_gate(gates, gexp_ref[0, 2]) * o_w)
        ya_ref[0, rows, :] = ya.astype(BF16)


def _selwin(pflags, slopes, qa, kva, sel, ga, yc, gexp):
    b, s, _ = qa.shape
    g = NSA_KV_GROUPS
    nq = s // Q_BLOCK
    tq = SELWIN_TILES * Q_BLOCK
    rows = NSA_HPG * Q_BLOCK
    stat = (SELWIN_TILES, NSA_HPG, Q_BLOCK, LANES)
    grid_spec = pltpu.PrefetchScalarGridSpec(
        num_scalar_prefetch=1,
        grid=(b, g, s // tq),
        in_specs=[pl.BlockSpec(memory_space=pltpu.SMEM),
                  pl.BlockSpec((1, tq, QP_W), lambda bi, gi, i, pf: (bi, i, gi)),
                  pl.BlockSpec((1, s, LANES), lambda bi, gi, i, pf: (bi, 0, gi)),
                  pl.BlockSpec((1, s, LANES), lambda bi, gi, i, pf: (bi, 0, NSA_KV_GROUPS + gi)),
                  pl.BlockSpec((1, 1, tq, LANES), lambda bi, gi, i, pf: (bi, gi, i, 0)),
                  pl.BlockSpec((1, tq, LANES), lambda bi, gi, i, pf: (bi, i, 0)),
                  pl.BlockSpec((1, tq, GROUP_W), lambda bi, gi, i, pf: (bi, i, gi)),
                  pl.BlockSpec((1, 3, LANES, GROUP_W), lambda bi, gi, i, pf: (gi, 0, 0, 0))],
        out_specs=pl.BlockSpec((1, tq, GROUP_W), lambda bi, gi, i, pf: (bi, i, gi)),
        scratch_shapes=[pltpu.VMEM((SELWIN_TILES, WIN_UNITS + SLOT_CHUNK) + stat[1:], F32),
                        pltpu.VMEM(stat, F32),
                        pltpu.VMEM(stat, F32),
                        pltpu.VMEM((SELWIN_TILES, rows, LANES), F32),
                        pltpu.VMEM((SELWIN_TILES, rows, LANES), F32)],
    )
    return pl.pallas_call(
        functools.partial(_selwin_kernel, nq=nq),
        grid_spec=grid_spec,
        out_shape=jax.ShapeDtypeStruct((b, s, g * GROUP_W), BF16),
        compiler_params=_params(("parallel", "parallel", "arbitrary")),
        name="selwin",
    )(pflags, slopes, qa, kva, kva, sel, ga, yc, gexp)


def _dil_bias(slopes):
    c = DIL_BLOCK
    qi = np.arange(c)[:, None]
    ki = np.arange(2 * c)[None, :]
    delta = qi + c - ki
    groups = []
    for k, (window, dil) in enumerate(DIL_GROUPS):
        dist = jnp.asarray((delta * dil).astype(np.float32))
        variants = []
        for has_prev in (False, True):
            in_band = jnp.asarray((delta >= 0) & (delta <= window // dil) & ((ki >= c) | has_prev))
            heads = [jnp.where(in_band, -(slopes[DIL_HPG * k + h] * dist) * LOG2E, NEG_INF)
                     for h in range(DIL_HPG)]
            variants.append(jnp.stack(heads).reshape(DIL_HPG // 2, 2 * c, 2 * c))
        groups.append(jnp.stack(variants))
    return jnp.stack(groups)


def _dil_group(k, dil, sup, q_ref, kvc_ref, kvp_ref, bias_ref, mx_ref, w_ref, y_ref):
    c = DIL_BLOCK
    nblk = sup // (c * dil)
    has_halo = pl.program_id(1) > 0
    nch = DIL_HPG * HEAD_DIM // LANES
    upper = _iota((c, LANES), 1) >= HEAD_DIM

    def block(idx):
        r = idx // nblk
        n = idx - r * nblk
        start = r + dil * c * n
        rows = pl.ds(start, c, stride=dil)
        rows_prev = pl.ds(jnp.maximum(start - dil * c, r), c, stride=dil)
        rows_halo = pl.ds(r + dil * c * (nblk - 1), c, stride=dil)
        variant = jnp.where(jnp.logical_or(n >= 1, has_halo), 1, 0)

        def prev(ch):
            return jnp.where(n >= 1, kvc_ref[ch, 0, rows_prev, :], kvp_ref[ch, 0, rows_halo, :]).astype(BF16)

        for ch in range(nch):
            qb = q_ref[ch, 0, rows, :].astype(BF16)
            zero = jnp.zeros_like(qb)
            qs = jnp.concatenate([jnp.where(upper, zero, qb), jnp.where(upper, qb, zero)], axis=0)
            kband = jnp.concatenate([prev(ch), kvc_ref[ch, 0, rows, :].astype(BF16)], axis=0)
            vband = jnp.concatenate([prev(nch + ch), kvc_ref[nch + ch, 0, rows, :].astype(BF16)], axis=0)
            s = _nt_dot(qs, kband) + bias_ref[0, variant, ch]
            m = jnp.max(s, axis=-1, keepdims=True)
            e = jnp.exp2(s - m).astype(BF16)
            ones = jnp.ones((2 * c, LANES), BF16)
            ol = _dot(e, jnp.concatenate([vband, ones], axis=1))
            lt = jnp.maximum(ol[:, LANES:], TINY)
            o2 = ol[:, :LANES] * (1.0 / lt)
            l2 = m * LN2 + jnp.log(lt)
            o = jnp.where(upper, o2[c:], o2[:c])
            lse = jnp.where(upper, l2[c:], l2[:c])
            if k == 0:
                mx_ref[ch, rows, :] = lse
                w_ref[ch, rows, :] = jnp.ones_like(lse)
                y_ref[ch, rows, :] = o
            else:
                mx_old = mx_ref[ch, rows, :]
                mx_new = jnp.maximum(mx_old, lse)
                a = jnp.exp(mx_old - mx_new)
                bnew = jnp.exp(lse - mx_new)
                mx_ref[ch, rows, :] = mx_new
                w_ref[ch, rows, :] = w_ref[ch, rows, :] * a + bnew
                y_ref[ch, rows, :] = y_ref[ch, rows, :] * a + bnew * o

    def body(t, carry):
        for j in range(DIL_UNROLL):
            block(t * DIL_UNROLL + j)
        return carry

    lax.fori_loop(0, sup // (c * DIL_UNROLL), body, 0)


def _dil_kernel(q_ref, kvc_ref, kvp_ref, bias_ref, yb_ref, mx_ref, w_ref, y_ref, *, sup):
    gi = pl.program_id(2)
    for k, (_, dil) in enumerate(DIL_GROUPS):
        @pl.when(gi == k)
        def _(k=k, dil=dil):
            _dil_group(k, dil, sup, q_ref, kvc_ref, kvp_ref, bias_ref, mx_ref, w_ref, y_ref)

    @pl.when(gi == len(DIL_GROUPS) - 1)
    def _():
        for ch in range(DIL_HPG * HEAD_DIM // LANES):
            yb_ref[0, :, ch * LANES:(ch + 1) * LANES] = (y_ref[ch] * (1.0 / w_ref[ch])).astype(BF16)


def _dil(qb, kvb, bias, sup):
    _, b, s, _ = qb.shape
    hw = DIL_HPG * HEAD_DIM
    nch = hw // LANES
    ng = len(DIL_GROUPS)
    return pl.pallas_call(
        functools.partial(_dil_kernel, sup=sup),
        grid=(b, s // sup, ng),
        in_specs=[pl.BlockSpec((nch, 1, sup, LANES), lambda bi, si, gi: (gi, bi, si, 0)),
                  pl.BlockSpec((2 * nch, 1, sup, LANES), lambda bi, si, gi: (gi, bi, si, 0)),
                  pl.BlockSpec((2 * nch, 1, sup, LANES),
                               lambda bi, si, gi: (gi, bi, jnp.maximum(si - 1, 0), 0)),
                  pl.BlockSpec((1, 2, nch, 2 * DIL_BLOCK, 2 * DIL_BLOCK),
                               lambda bi, si, gi: (gi, 0, 0, 0, 0))],
        out_specs=pl.BlockSpec((1, sup, hw), lambda bi, si, gi: (bi, si, 0)),
        out_shape=jax.ShapeDtypeStruct((b, s, hw), BF16),
        scratch_shapes=[pltpu.VMEM((nch, sup, LANES), F32), pltpu.VMEM((nch, sup, LANES), F32),
                        pltpu.VMEM((nch, sup, LANES), F32)],
        compiler_params=_params(("parallel", "parallel", "arbitrary")),
        name="dil",
    )(qb, kvb, kvb, bias)


def _mixffn_kernel(x_ref, g_ref, ya_ref, yb_ref, wg_ref, ua_ref, ub_ref, wo_ref,
                   g2_ref, wi_ref, wf_ref, gf_ref, o_ref, *, tf, final_norm):
    x = x_ref[...]
    d = x.shape[1]
    hn = _rms(x, g_ref[...]).astype(BF16)
    gate_a = jax.nn.sigmoid(_dot(hn, wg_ref[:, 0:d]))
    gate_b = jax.nn.sigmoid(_dot(hn, wg_ref[:, d:2 * d]))
    merged = gate_a * _dot(ya_ref[...], ua_ref[...]) + gate_b * _dot(yb_ref[...], ub_ref[...])
    x = x + _dot(merged.astype(BF16), wo_ref[...])
    dff = wf_ref.shape[0]
    hn = _rms(x, g2_ref[...]).astype(BF16)
    y = x
    for lo, hi in tf:
        gt = _dot(hn, wi_ref[:, lo:hi])
        up = _dot(hn, wi_ref[:, dff + lo:dff + hi])
        act = ((gt * jax.nn.sigmoid(gt)) * up).astype(BF16)
        y = y + _dot(act, wf_ref[lo:hi, :])
    if final_norm:
        y = _rms(y, gf_ref[...])
    o_ref[...] = y


def _mixffn(x2, g, ya, yb, wg, ua, ub, wo, g2, w_in, w_out, gf, layer, tm, tf, final_norm):
    n, d = x2.shape
    row = lambda i: (i, 0)
    fixed = lambda i: (0, 0)
    return pl.pallas_call(
        functools.partial(_mixffn_kernel, tf=tf, final_norm=final_norm),
        grid=(n // tm,),
        in_specs=([pl.BlockSpec((tm, d), row), pl.BlockSpec((1, d), fixed),
                   pl.BlockSpec((tm, ya.shape[1]), row), pl.BlockSpec((tm, yb.shape[1]), row)]
                  + [_layer_spec(w, layer) for w in (wg, ua, ub, wo)]
                  + [pl.BlockSpec((1, d), fixed), _layer_spec(w_in, layer), _layer_spec(w_out, layer),
                     pl.BlockSpec((1, d), fixed)]),
        out_specs=pl.BlockSpec((tm, d), row),
        out_shape=jax.ShapeDtypeStruct((n, d), F32),
        compiler_params=pltpu.CompilerParams(dimension_semantics=("parallel",),
                                             vmem_limit_bytes=MIXFFN_VMEM_LIMIT),
        name="mixffn",
    )(x2, g, ya, yb, wg, ua, ub, wo, g2, w_in, w_out, gf)


def _alibi_slopes():
    k = jnp.arange(1, N_ALIBI + 1, dtype=F32)
    s = jnp.exp2(-8.0 * k / N_ALIBI)
    nsa = s[2 * DIL_HPG:2 * DIL_HPG + NSA_HEADS]
    dil = jnp.concatenate([s[:2 * DIL_HPG], s[2 * DIL_HPG + NSA_HEADS:]])
    return nsa, dil


def _overlap(seq, ncp):
    n_c = (seq - CMP_BLOCK) // CMP_STRIDE + 1
    n_s = seq // SEL_BLOCK
    c_start = np.arange(n_c) * CMP_STRIDE
    s_start = np.arange(n_s) * SEL_BLOCK
    ov = np.clip(np.minimum(c_start[:, None] + CMP_BLOCK, s_start[None, :] + SEL_BLOCK)
                 - np.maximum(c_start[:, None], s_start[None, :]), 0, None).astype(np.float32) / CMP_BLOCK
    out = np.zeros((ncp, LANES), np.float32)
    out[:n_c, :n_s] = ov
    return jnp.asarray(out, BF16)


def _proj_weights(w_in):
    dep, d, _ = w_in.shape
    scale = HEAD_DIM ** -0.5 * LOG2E
    c0 = NSA_HEADS * HEAD_DIM
    c1 = c0 + 3 * 2 * NSA_KV_GROUPS * HEAD_DIM
    c2 = c1 + NSA_HEADS * 3
    c3 = c2 + 3 * DIL_HEADS * HEAD_DIM
    qa = (w_in[..., :c0] * scale).astype(BF16)
    wb = w_in.astype(BF16)
    kva = wb[..., c0:c1].reshape(dep, d, 3, 2, NSA_KV_GROUPS, HEAD_DIM)
    kva = kva.transpose(0, 1, 2, 4, 3, 5).reshape(dep, d, CMP_W + KVA_W)
    ga = jnp.pad(wb[..., c1:c2], ((0, 0), (0, 0), (0, GA_W - (c2 - c1))))
    qkvb = w_in[..., c2:c3].reshape(dep, d, 3, len(DIL_GROUPS), DIL_HPG * HEAD_DIM)
    qb = (qkvb[:, :, 0] * scale).astype(BF16).reshape(dep, d, QB_W)
    kvb = qkvb[:, :, 1:3].astype(BF16).reshape(dep, d, KVB_W)
    return (qa, kva, qb, kvb, ga), wb[..., c3:]


def _compress_weights(pe_k, pe_v, w_ck1, w_ck2, w_cv1, w_cv2):
    dep = pe_k.shape[0]
    half = CMP_BLOCK // 2
    ck = w_ck1.astype(BF16).reshape(dep, CMP_BLOCK, HEAD_DIM, CMP_HIDDEN)
    cv = w_cv1.astype(BF16).reshape(dep, CMP_BLOCK, HEAD_DIM, CMP_HIDDEN)
    z = jnp.zeros_like(ck)
    w = jnp.concatenate([jnp.concatenate([ck, z], axis=-1),
                         jnp.concatenate([z, cv], axis=-1)], axis=2)
    feat = half * 2 * HEAD_DIM
    wa = w[:, :half].reshape(dep, feat, 2 * CMP_HIDDEN)
    wb = w[:, half:].reshape(dep, feat, 2 * CMP_HIDDEN)
    pe = jnp.concatenate([pe_k, pe_v], axis=-1)
    pe2 = jnp.stack([pe[:, :half].reshape(dep, feat), pe[:, half:].reshape(dep, feat)], axis=1)
    pe2 = jnp.pad(pe2, ((0, 0), (0, 6), (0, 0)))
    z2 = jnp.zeros_like(w_ck2)
    w2 = jnp.concatenate([jnp.concatenate([w_ck2, z2], axis=-1),
                          jnp.concatenate([z2, w_cv2], axis=-1)], axis=1).astype(BF16)
    return pe2, wa, wb, w2


def kernel(x, norm_mix, w_in, pe_k, pe_v, w_ck1, w_ck2, w_cv1, w_cv2, w_up_nsa, w_up_dil,
           w_out, norm_ffn, w_ffn_in, w_ffn_out, norm_final):
    b, s, d = x.shape
    depth = w_in.shape[0]
    n = b * s
    nq = s // Q_BLOCK
    nch = s // CMP_STRIDE
    n_s = s // SEL_BLOCK
    assert s % DIL_SUPER == 0 and n_s <= LANES
    n_top = min(SEL_TOPK, n_s)
    tm = 512
    dff = w_ffn_out.shape[1]
    half = (dff // MXU_TILE + 1) // 2 * MXU_TILE
    tf = ((0, half), (half, dff))

    slopes_nsa, slopes_dil = _alibi_slopes()
    slopes_nsa = slopes_nsa * LOG2E
    ov = _overlap(s, nch)
    gexp = _gate_spread()
    dbias = _dil_bias(slopes_dil)
    w_a, w_gm = _proj_weights(w_in)
    pe2, wa, wb, w2 = _compress_weights(pe_k, pe_v, w_ck1, w_ck2, w_cv1, w_cv2)
    ua = w_up_nsa.astype(BF16)
    ub = w_up_dil.astype(BF16)
    wo = w_out.astype(BF16)
    wfi = w_ffn_in.astype(BF16)
    wfo = w_ffn_out.astype(BF16)
    gf = norm_final.reshape(1, d)

    x2 = x.reshape(n, d)
    for l in range(depth):
        g_mix = norm_mix[l].reshape(1, d)
        qa, xc, kva, qb, kvb, ga = _proj(x2, g_mix, w_a, l, tm)
        qa = qa.reshape(b, s, QA_W)
        kva = kva.reshape(b, s, KVA_W)
        ga = ga.reshape(b, s, GA_W)
        kvc = _compress(xc, pe2[l], wa[l], wb[l], w2[l], b)
        yc, sel, flags = _cmp(slopes_nsa, qa, kvc, ga, ov, gexp, n_top)
        pflags = flags.reshape(-1)
        ya = _selwin(pflags, slopes_nsa, qa, kva, sel, ga, yc, gexp)
        yb = _dil(qb.reshape(-1, b, s, LANES), kvb.reshape(-1, b, s, LANES), dbias, DIL_SUPER)
        x2 = _mixffn(x2, g_mix, ya.reshape(n, NSA_HEADS * HEAD_DIM), yb.reshape(n, DIL_HPG * HEAD_DIM),
                     w_gm, ua, ub, wo, norm_ffn[l].reshape(1, d), wfi, wfo, gf,
                     l, tm, tf, final_norm=(l == depth - 1))
    return x2.reshape(b, s, d)
```

```python
import functools

import numpy as np
import jax
import jax.numpy as jnp
from jax import lax
from jax.experimental import pallas as pl
from jax.experimental.pallas import tpu as pltpu

F32 = jnp.float32
BF16 = jnp.bfloat16

HEAD_DIM = 64
NSA_HEADS = 8
NSA_KV_GROUPS = 2
NSA_HPG = NSA_HEADS // NSA_KV_GROUPS
CMP_BLOCK = 32
CMP_STRIDE = 16
CMP_HIDDEN = 256
SEL_BLOCK = 64
SEL_TOPK = 16
NSA_WINDOW = 512
Q_BLOCK = 128
FORCE_SCORE = 1e6
DIL_GROUPS = ((128, 1), (512, 4), (2048, 16))
DIL_HPG = 4
DIL_HEADS = DIL_HPG * len(DIL_GROUPS)
DIL_BLOCK = 128
N_ALIBI = NSA_HEADS + DIL_HEADS
EPS = 1e-6
NEG_INF = -1e30
TINY = 1e-30
LOG2E = 1.4426950408889634
LN2 = 0.6931471805599453
QSCALE = HEAD_DIM ** -0.5 * LOG2E

LANES = 128
MXU_TILE = 256
GROUP_W = NSA_HPG * HEAD_DIM
DIL_SUPER = 2048
DIL_UNROLL = 4
CMP_TILES = 8
VMEM_LIMIT = 48 * 1024 * 1024
MIXFFN_VMEM_LIMIT = 56 * 1024 * 1024


def _dot(a, b):
    return jnp.dot(a, b, preferred_element_type=F32)


def _nt_dot(a, b):
    return lax.dot_general(a, b, (((1,), (1,)), ((), ())), preferred_element_type=F32)


def _iota(shape, dim):
    return lax.broadcasted_iota(jnp.int32, shape, dim)


def _rms(x, g):
    ms = jnp.mean(x * x, axis=-1, keepdims=True)
    return (x * lax.rsqrt(ms + EPS)) * g


def _params(sem):
    return pltpu.CompilerParams(dimension_semantics=sem, vmem_limit_bytes=VMEM_LIMIT)


QA_W = NSA_HEADS * LANES
QP_W = NSA_HPG * LANES
CMP_W = NSA_KV_GROUPS * LANES
KVA_W, QB_W, KVB_W, GA_W = 512, 768, 1536, 128


def _proj_kernel(x_ref, g_ref, wq_ref, wkva_ref, wqb_ref, wkvb_ref, wga_ref,
                 qa_ref, xc_ref, kva_ref, qb_ref, kvb_ref, ga_ref, cmp_scr):
    y = _rms(x_ref[...], g_ref[...]).astype(BF16)
    low = _iota((x_ref.shape[0], LANES), 1) < HEAD_DIM
    for p in range(0, NSA_HEADS // 2, 2):
        r2 = _dot(y, wq_ref[:, p * LANES:(p + 2) * LANES]) * QSCALE
        for k in range(2):
            r = r2[:, k * LANES:(k + 1) * LANES]
            h0 = 2 * (p + k)
            qa_ref[:, h0 * LANES:(h0 + 1) * LANES] = jnp.where(low, r, 0.0).astype(BF16)
            qa_ref[:, (h0 + 1) * LANES:(h0 + 2) * LANES] = (
                jnp.where(low, pltpu.roll(r, HEAD_DIM, 1), 0.0).astype(BF16))
    rc = _dot(y, wkva_ref[:, 0:CMP_W])
    nrow = xc_ref.shape[1]
    for gi in range(NSA_KV_GROUPS):
        cmp_scr[gi] = rc[:, gi * LANES:(gi + 1) * LANES]
        for l in range(CMP_STRIDE):
            xc_ref[gi, :, l * LANES:(l + 1) * LANES] = (
                cmp_scr[gi, pl.ds(l, nrow, stride=CMP_STRIDE), :].astype(BF16))
    kva_ref[...] = _dot(y, wkva_ref[:, CMP_W:CMP_W + KVA_W]).astype(BF16)
    for j in range(0, QB_W // LANES, 2):
        r = _dot(y, wqb_ref[:, j * LANES:(j + 2) * LANES]) * QSCALE
        qb_ref[j] = r[:, :LANES]
        qb_ref[j + 1] = r[:, LANES:]
    ngroups = len(DIL_GROUPS)
    for kv in range(2):
        for gi in range(ngroups):
            c = (kv * ngroups + gi) * 2 * LANES
            r = _dot(y, wkvb_ref[:, c:c + 2 * LANES])
            kvb_ref[4 * gi + 2 * kv] = r[:, :LANES]
            kvb_ref[4 * gi + 2 * kv + 1] = r[:, LANES:]
    ga_ref[...] = jax.nn.sigmoid(_dot(y, wga_ref[...]))


def _layer_spec(w, layer):
    return pl.BlockSpec((None,) + w.shape[1:], lambda i: (layer, 0, 0), pipeline_mode=pl.Buffered(1))


def _proj(x2, g, weights, layer, tm):
    n, d = x2.shape
    row = lambda i: (i, 0)
    fixed = lambda i: (0, 0)
    return pl.pallas_call(
        _proj_kernel,
        grid=(n // tm,),
        in_specs=([pl.BlockSpec((tm, d), row), pl.BlockSpec((1, d), fixed)]
                  + [_layer_spec(w, layer) for w in weights]),
        out_specs=[pl.BlockSpec((tm, QA_W), row),
                   pl.BlockSpec((NSA_KV_GROUPS, tm // CMP_STRIDE, CMP_STRIDE * LANES), lambda i: (0, i, 0)),
                   pl.BlockSpec((tm, KVA_W), row),
                   pl.BlockSpec((QB_W // LANES, tm, LANES), lambda i: (0, i, 0)),
                   pl.BlockSpec((KVB_W // LANES, tm, LANES), lambda i: (0, i, 0)),
                   pl.BlockSpec((tm, GA_W), row)],
        out_shape=[jax.ShapeDtypeStruct((n, QA_W), BF16),
                   jax.ShapeDtypeStruct((NSA_KV_GROUPS, n // CMP_STRIDE, CMP_STRIDE * LANES), BF16),
                   jax.ShapeDtypeStruct((n, KVA_W), BF16),
                   jax.ShapeDtypeStruct((QB_W // LANES, n, LANES), F32),
                   jax.ShapeDtypeStruct((KVB_W // LANES, n, LANES), F32),
                   jax.ShapeDtypeStruct((n, GA_W), F32)],
        scratch_shapes=[pltpu.VMEM((NSA_KV_GROUPS, tm, LANES), F32)],
        compiler_params=_params(("parallel",)),
        name="proj",
    )(x2, g, *weights)


def _compress_kernel(x_ref, pe_ref, wa_ref, wb_ref, w2_ref, o_ref):
    xf = x_ref[0].astype(F32)
    nch = xf.shape[0]
    xa = (xf + pe_ref[0:1, :]).astype(BF16)
    xb = (xf + pe_ref[1:2, :]).astype(BF16)
    ya = _dot(xa, wa_ref[...])
    yb = _dot(xb, wb_ref[...])
    h = ya + pltpu.roll(yb, nch - 1, 0)
    a = (h * jax.nn.sigmoid(h)).astype(BF16)
    o_ref[0, 0] = _dot(a, w2_ref[...]).astype(BF16)


def _compress(xc, pe, wa, wb, w2, b):
    g, rows, f = xc.shape
    nch = rows // b
    hid = wa.shape[1]
    return pl.pallas_call(
        _compress_kernel,
        grid=(b, g),
        in_specs=[pl.BlockSpec((1, nch, f), lambda i, j: (j, i, 0)),
                  pl.BlockSpec((8, f), lambda i, j: (0, 0)),
                  pl.BlockSpec((f, hid), lambda i, j: (0, 0)),
                  pl.BlockSpec((f, hid), lambda i, j: (0, 0)),
                  pl.BlockSpec((hid, LANES), lambda i, j: (0, 0))],
        out_specs=pl.BlockSpec((1, 1, nch, LANES), lambda i, j: (i, j, 0, 0)),
        out_shape=jax.ShapeDtypeStruct((b, g, nch, LANES), BF16),
        compiler_params=_params(("parallel", "parallel")),
        name="compress",
    )(xc, pe, wa, wb, w2)


def _gate_spread():
    g = np.arange(NSA_KV_GROUPS)[:, None, None, None]
    br = np.arange(3)[None, :, None, None]
    r = np.arange(LANES)[None, None, :, None]
    c = np.arange(GROUP_W)[None, None, None, :]
    return jnp.asarray(r == 3 * (NSA_HPG * g + c // HEAD_DIM) + br, BF16)


def _stack_heads(q):
    return jnp.concatenate([q[:, LANES * h:LANES * (h + 1)] for h in range(NSA_HPG)], axis=0)


def _unstack_heads(o4):
    lane = _iota((Q_BLOCK, LANES), 1)
    o = [o4[Q_BLOCK * h:Q_BLOCK * (h + 1)] for h in range(NSA_HPG)]
    c0 = jnp.where(lane < HEAD_DIM, pltpu.roll(o[0], HEAD_DIM, 1), o[1])
    c1 = jnp.where(lane < HEAD_DIM, pltpu.roll(o[2], HEAD_DIM, 1), o[3])
    return jnp.concatenate([c0, c1], axis=1)


def _row_sums_all_lanes(a):
    lane = _iota(a.shape, 1)
    return jnp.where(lane < HEAD_DIM, a, pltpu.roll(a, HEAD_DIM, 1))


def _normalise_unstack(a):
    return _unstack_heads(a * (1.0 / jnp.maximum(_row_sums_all_lanes(a), TINY)))


def _expand_gate(gates, spread):
    hi = gates.astype(BF16)
    lo = (gates - hi.astype(F32)).astype(BF16)
    return _dot(hi, spread) + _dot(lo, spread)


def _ones_in_k_lanes(kv):
    lane = _iota(kv.shape, 1)
    return jnp.where(lane < HEAD_DIM, jnp.ones_like(kv), kv)


def _cmp_kernel(slopes_ref, q_ref, kvc_ref, ga_ref, ov_ref, gexp_ref,
                yc_ref, sel_ref, fl_ref, bias0, cq, rhs, big_ref, *, n_top):
    g = pl.program_id(1)
    i = pl.program_id(2)
    hq = NSA_HPG
    ncp = cq.shape[1]

    @pl.when(i == 0)
    def _():
        kpos = CMP_STRIDE * _iota((Q_BLOCK, ncp), 1) + (CMP_BLOCK - 1)
        cq[...] = kpos - _iota((Q_BLOCK, ncp), 0)
        kf = kpos[0:8].astype(F32)
        for h in range(hq):
            bias0[h] = slopes_ref[hq * g + h] * kf
        rhs[:, 0:LANES] = _ones_in_k_lanes(kvc_ref[0, 0])
        rhs[:, LANES:2 * LANES] = ov_ref[...]

    def attend(width):
        def run():
            kv = kvc_ref[0, 0, 0:width, :]
            for k in range(CMP_TILES):
                q4 = _stack_heads(q_ref[0, k * Q_BLOCK:(k + 1) * Q_BLOCK, :])
                s = _nt_dot(q4, kv).reshape(hq, Q_BLOCK, width)
                t0 = (CMP_TILES * i + k) * Q_BLOCK
                neg = jnp.where(cq[:, 0:width] <= t0, 0.0, NEG_INF)
                tf = t0.astype(F32)
                ps = []
                for h in range(hq):
                    sb = s[h] + (neg + (bias0[h, 0:1, 0:width] - slopes_ref[hq * g + h] * tf))
                    m = jnp.max(sb, axis=-1, keepdims=True)
                    m = jnp.where(m < 0.1 * NEG_INF, -NEG_INF, m)
                    ps.append(jnp.exp2(sb - m).astype(BF16))
                big_ref[k] = _dot(jnp.concatenate(ps, axis=0), rhs[0:width, :])
        return run

    units = ncp // LANES
    per_tile = Q_BLOCK // CMP_STRIDE
    last_visible = (per_tile * (CMP_TILES * i + CMP_TILES - 1)
                    + (Q_BLOCK - CMP_BLOCK) // CMP_STRIDE)
    if units == 1:
        attend(ncp)()
    else:
        lax.switch(jnp.minimum(last_visible // LANES, units - 1),
                   [attend(LANES * (u + 1)) for u in range(units)])

    for k in range(CMP_TILES):
        it = CMP_TILES * i + k
        rows = slice(k * Q_BLOCK, (k + 1) * Q_BLOCK)
        big = big_ref[k]
        inv = 1.0 / jnp.maximum(_row_sums_all_lanes(big[:, 0:LANES]), TINY)
        oc = _unstack_heads(big[:, 0:LANES] * inv)
        yc_ref[0, rows, :] = _expand_gate(ga_ref[0, rows, :], gexp_ref[0, 0]) * oc
        pov = big[:, LANES:2 * LANES] * inv
        imp = pov[0:Q_BLOCK]
        for h in range(1, hq):
            imp = imp + pov[Q_BLOCK * h:Q_BLOCK * (h + 1)]

        qrow = _iota((Q_BLOCK, LANES), 0)
        j = _iota((Q_BLOCK, LANES), 1)
        jt = 2 * it + (qrow >> 6)
        valid = j <= jt
        forced = (j == 0) | (j == jt) | (j == jt - 1)
        score = jnp.where(forced, FORCE_SCORE, jnp.where(valid, imp, -1.0))
        st = score.T
        jr = _iota((LANES, Q_BLOCK), 0)
        jtt = 2 * it + (_iota((LANES, Q_BLOCK), 1) >> 6)
        taken = -3e38
        cur = jnp.where((jr == 0) | (jr == jtt) | (jr == jtt - 1), taken, st)
        for _ in range(n_top - 3):
            mx = jnp.max(cur, axis=0, keepdims=True)
            first = jnp.min(jnp.where(cur == mx, jr, LANES), axis=0, keepdims=True)
            cur = jnp.where(jr == first, taken, cur)
        keep_t = jnp.where((cur == taken) & (st >= 0.0), 1.0, 0.0)
        sel_ref[0, 0, rows, :] = keep_t.T.astype(BF16)

        lanes_sq = (LANES, LANES)
        r_i = _iota(lanes_sq, 0)
        c_i = _iota(lanes_sq, 1)
        any_q = jnp.broadcast_to(jnp.max(keep_t, axis=-1, keepdims=True), lanes_sq).astype(BF16)
        pair = jnp.where((c_i >> 1) == r_i, 1.0, 0.0).astype(BF16)
        used = jnp.where(_dot(pair, any_q) > 0.0, 1.0, 0.0)
        before = jnp.where(c_i < r_i, 1.0, 0.0).astype(BF16)
        place = _dot(before, used.astype(BF16))
        onehot = jnp.where((place == c_i.astype(F32)) & (used > 0.0), 1.0, 0.0).astype(BF16)
        unit_ids = _iota((8, LANES), 1).astype(F32).astype(BF16)
        listed = _dot(unit_ids, onehot)[0:1]
        count = _dot(jnp.ones((8, LANES), BF16), used.astype(BF16))[0:1]
        pos = _iota((1, LANES), 1)
        row = jnp.where(pos.astype(F32) < count, listed, -1.0)
        fl_ref[k] = jnp.where(pos == LANES - 1, count, row).astype(jnp.int32)


def _cmp(slopes, qa, kvc, ga, ov, gexp, n_top):
    b, s, _ = qa.shape
    g = NSA_KV_GROUPS
    nq = s // Q_BLOCK
    ncp = kvc.shape[2]
    tq = CMP_TILES * Q_BLOCK
    nt = s // tq
    return pl.pallas_call(
        functools.partial(_cmp_kernel, n_top=n_top),
        grid=(b, g, nt),
        in_specs=[pl.BlockSpec(memory_space=pltpu.SMEM),
                  pl.BlockSpec((1, tq, QP_W), lambda bi, gi, i: (bi, i, gi)),
                  pl.BlockSpec((1, 1, ncp, LANES), lambda bi, gi, i: (bi, gi, 0, 0)),
                  pl.BlockSpec((1, tq, LANES), lambda bi, gi, i: (bi, i, 0)),
                  pl.BlockSpec((ncp, LANES), lambda bi, gi, i: (0, 0)),
                  pl.BlockSpec((1, 3, LANES, GROUP_W), lambda bi, gi, i: (gi, 0, 0, 0))],
        out_specs=[pl.BlockSpec((1, tq, GROUP_W), lambda bi, gi, i: (bi, i, gi)),
                   pl.BlockSpec((1, 1, tq, LANES), lambda bi, gi, i: (bi, gi, i, 0)),
                   pl.BlockSpec((CMP_TILES, 1, LANES), lambda bi, gi, i: ((bi * g + gi) * nt + i, 0, 0))],
        out_shape=[jax.ShapeDtypeStruct((b, s, g * GROUP_W), F32),
                   jax.ShapeDtypeStruct((b, g, s, LANES), BF16),
                   jax.ShapeDtypeStruct((b * g * nq, 1, LANES), jnp.int32)],
        scratch_shapes=[pltpu.VMEM((NSA_HPG, 8, ncp), F32),
                        pltpu.VMEM((Q_BLOCK, ncp), jnp.int32),
                        pltpu.VMEM((ncp, 2 * LANES), BF16),
                        pltpu.VMEM((CMP_TILES, NSA_HPG * Q_BLOCK, 2 * LANES), F32)],
        compiler_params=_params(("parallel", "parallel", "arbitrary")),
        name="cmp",
    )(slopes, qa, kvc, ga, ov, gexp)


WIN_UNITS = NSA_WINDOW // Q_BLOCK + 1
SLOT_CHUNK = 10


SELWIN_TILES = 4


def _selwin_kernel(pf_ref, slopes_ref, q_ref, kvs_ref, kvw_ref, sel_ref, ga_ref, yc_ref,
                    gexp_ref, ya_ref, sbuf, mrun, mrunw, acc, accw, *, nq):
    b = pl.program_id(0)
    g = pl.program_id(1)
    i = pl.program_id(2)
    hq = NSA_HPG
    qmk = _iota((Q_BLOCK, LANES), 0) - _iota((Q_BLOCK, LANES), 1)
    koff = _iota((1, LANES), 1).astype(F32)
    dmat = _iota((LANES, LANES), 0) - (_iota((LANES, LANES), 1) >> 6)
    dead = -(1 << 20)
    win_pairs = [(d, d) for d in range(WIN_UNITS)]

    def chunk_pairs(ci):
        return [(WIN_UNITS + t, WIN_UNITS + ci * SLOT_CHUNK + t) for t in range(SLOT_CHUNK)]

    def setup(k):
        it = SELWIN_TILES * i + k
        base = ((b * NSA_KV_GROUPS + g) * nq + it) * LANES
        nsel = pf_ref[base + LANES - 1]
        rows = slice(k * Q_BLOCK, (k + 1) * Q_BLOCK)
        return dict(k=k, it=it, base=base, rows=rows, nchunk=(nsel + (SLOT_CHUNK - 1)) // SLOT_CHUNK,
                    q4=_stack_heads(q_ref[0, rows, :]), sel=sel_ref[0, 0, rows, :])

    def unit_of(t, li):
        if isinstance(li, int) and li < WIN_UNITS:
            u = t["it"] - (WIN_UNITS - 1) + li
            return jnp.where(u >= 0, u, -1)
        return pf_ref[t["base"] + li - WIN_UNITS]

    def kv_tile(kv_ref, u):
        return kv_ref[0, pl.ds(pl.multiple_of(jnp.maximum(u, 0) * Q_BLOCK, Q_BLOCK), Q_BLOCK), :]

    def score_slots(t, kv_ref, pairs, window, mx):
        k = t["k"]
        units = [unit_of(t, li) for _, li in pairs]
        s_all = _nt_dot(t["q4"], jnp.concatenate([kv_tile(kv_ref, u) for u in units], axis=0))
        if not window:
            spread = jnp.concatenate(
                [jnp.where(dmat == 2 * u, 1.0, 0.0).astype(BF16) for u in units], axis=1)
            chosen = _dot(t["sel"], spread)
        for n, ((bs, _), u) in enumerate(zip(pairs, units)):
            s = s_all[:, n * LANES:(n + 1) * LANES].reshape(hq, Q_BLOCK, LANES)
            off = jnp.where(u >= 0, (t["it"] - u) * Q_BLOCK, dead)
            dist = qmk + off
            if window:
                allow = (dist >= 0) & (dist < NSA_WINDOW)
            else:
                allow = (dist >= 0) & (chosen[:, n * LANES:(n + 1) * LANES] > 0.5)
            neg = jnp.where(allow, 0.0, NEG_INF)
            kd = koff - off.astype(F32)
            for h in range(hq):
                sb = s[h] + (neg + slopes_ref[hq * g + h] * kd)
                sbuf[k, bs, h] = sb
                mx[k, h] = jnp.maximum(mx[k, h], sb)

    def score_pairs(t, kv_ref, pairs, window, mx):
        for n in range(0, len(pairs), 2):
            score_slots(t, kv_ref, pairs[n:n + 2], window, mx)

    def row_max(mx, k):
        return jnp.broadcast_to(jnp.max(mx[k], axis=-1, keepdims=True), mx.shape[1:])

    def weighted_values(t, kv_ref, pairs, mx):
        k = t["k"]
        tot = None
        for n in range(0, len(pairs), 2):
            two = pairs[n:n + 2]
            p = jnp.concatenate(
                [jnp.exp2(sbuf[k, bs] - mx[k]).astype(BF16).reshape(hq * Q_BLOCK, LANES)
                 for bs, _ in two], axis=1)
            v = jnp.concatenate([_ones_in_k_lanes(kv_tile(kv_ref, unit_of(t, li))) for _, li in two], axis=0)
            d = _dot(p, v)
            tot = d if tot is None else tot + d
        return tot

    tiles = [setup(k) for k in range(SELWIN_TILES)]

    for t in tiles:
        k = t["k"]
        mrunw[k] = jnp.full(mrunw.shape[1:], NEG_INF, F32)
        mrun[k] = jnp.full(mrun.shape[1:], NEG_INF, F32)
        score_pairs(t, kvw_ref, win_pairs, True, mrunw)
        score_pairs(t, kvs_ref, chunk_pairs(0), False, mrun)
    for t in tiles:
        k = t["k"]
        mrunw[k] = row_max(mrunw, k)
        mrun[k] = row_max(mrun, k)
        accw[k] = weighted_values(t, kvw_ref, win_pairs, mrunw)
        acc[k] = weighted_values(t, kvs_ref, chunk_pairs(0), mrun)

    for t in tiles:
        k = t["k"]

        def more(ci, carry, t=t, k=k):
            mrunw[k] = jnp.full(mrunw.shape[1:], NEG_INF, F32)
            score_pairs(t, kvs_ref, chunk_pairs(ci), False, mrunw)
            m_old = mrun[k]
            m_new = jnp.maximum(m_old, row_max(mrunw, k))
            mrun[k] = m_new
            alpha = jnp.exp2(m_old - m_new).reshape(hq * Q_BLOCK, LANES)
            acc[k] = acc[k] * alpha + weighted_values(t, kvs_ref, chunk_pairs(ci), mrun)
            return carry

        lax.fori_loop(1, t["nchunk"], more, 0)

    for t in tiles:
        k = t["k"]
        rows = t["rows"]
        o_s = _normalise_unstack(acc[k])
        o_w = _normalise_unstack(accw[k])
        gates = ga_ref[0, rows, :]
        ya = (yc_ref[0, rows, :] + _expand_gate(gates, gexp_ref[0, 1]) * o_s
              + _expand_gate(gates, gexp_ref[0, 2]) * o_w)
        ya_ref[0, rows, :] = ya.astype(BF16)


def _selwin(pflags, slopes, qa, kva, sel, ga, yc, gexp):
    b, s, _ = qa.shape
    g = NSA_KV_GROUPS
    nq = s // Q_BLOCK
    tq = SELWIN_TILES * Q_BLOCK
    rows = NSA_HPG * Q_BLOCK
    stat = (SELWIN_TILES, NSA_HPG, Q_BLOCK, LANES)
    grid_spec = pltpu.PrefetchScalarGridSpec(
        num_scalar_prefetch=1,
        grid=(b, g, s // tq),
        in_specs=[pl.BlockSpec(memory_space=pltpu.SMEM),
                  pl.BlockSpec((1, tq, QP_W), lambda bi, gi, i, pf: (bi, i, gi)),
                  pl.BlockSpec((1, s, LANES), lambda bi, gi, i, pf: (bi, 0, gi)),
                  pl.BlockSpec((1, s, LANES), lambda bi, gi, i, pf: (bi, 0, NSA_KV_GROUPS + gi)),
                  pl.BlockSpec((1, 1, tq, LANES), lambda bi, gi, i, pf: (bi, gi, i, 0)),
                  pl.BlockSpec((1, tq, LANES), lambda bi, gi, i, pf: (bi, i, 0)),
                  pl.BlockSpec((1, tq, GROUP_W), lambda bi, gi, i, pf: (bi, i, gi)),
                  pl.BlockSpec((1, 3, LANES, GROUP_W), lambda bi, gi, i, pf: (gi, 0, 0, 0))],
        out_specs=pl.BlockSpec((1, tq, GROUP_W), lambda bi, gi, i, pf: (bi, i, gi)),
        scratch_shapes=[pltpu.VMEM((SELWIN_TILES, WIN_UNITS + SLOT_CHUNK) + stat[1:], F32),
                        pltpu.VMEM(stat, F32),
                        pltpu.VMEM(stat, F32),
                        pltpu.VMEM((SELWIN_TILES, rows, LANES), F32),
                        pltpu.VMEM((SELWIN_TILES, rows, LANES), F32)],
    )
    return pl.pallas_call(
        functools.partial(_selwin_kernel, nq=nq),
        grid_spec=grid_spec,
        out_shape=jax.ShapeDtypeStruct((b, s, g * GROUP_W), BF16),
        compiler_params=_params(("parallel", "parallel", "arbitrary")),
        name="selwin",
    )(pflags, slopes, qa, kva, kva, sel, ga, yc, gexp)


def _dil_bias(slopes):
    c = DIL_BLOCK
    qi = np.arange(c)[:, None]
    ki = np.arange(2 * c)[None, :]
    delta = qi + c - ki
    groups = []
    for k, (window, dil) in enumerate(DIL_GROUPS):
        dist = jnp.asarray((delta * dil).astype(np.float32))
        variants = []
        for has_prev in (False, True):
            in_band = jnp.asarray((delta >= 0) & (delta <= window // dil) & ((ki >= c) | has_prev))
            heads = [jnp.where(in_band, -(slopes[DIL_HPG * k + h] * dist) * LOG2E, NEG_INF)
                     for h in range(DIL_HPG)]
            variants.append(jnp.stack(heads).reshape(DIL_HPG // 2, 2 * c, 2 * c))
        groups.append(jnp.stack(variants))
    return jnp.stack(groups)


def _dil_group(k, dil, sup, q_ref, kvc_ref, kvp_ref, bias_ref, mx_ref, w_ref, y_ref):
    c = DIL_BLOCK
    nblk = sup // (c * dil)
    has_halo = pl.program_id(1) > 0
    nch = DIL_HPG * HEAD_DIM // LANES
    upper = _iota((c, LANES), 1) >= HEAD_DIM

    def block(idx):
        r = idx // nblk
        n = idx - r * nblk
        start = r + dil * c * n
        rows = pl.ds(start, c, stride=dil)
        rows_prev = pl.ds(jnp.maximum(start - dil * c, r), c, stride=dil)
        rows_halo = pl.ds(r + dil * c * (nblk - 1), c, stride=dil)
        variant = jnp.where(jnp.logical_or(n >= 1, has_halo), 1, 0)

        def prev(ch):
            return jnp.where(n >= 1, kvc_ref[ch, 0, rows_prev, :], kvp_ref[ch, 0, rows_halo, :]).astype(BF16)

        for ch in range(nch):
            qb = q_ref[ch, 0, rows, :].astype(BF16)
            zero = jnp.zeros_like(qb)
            qs = jnp.concatenate([jnp.where(upper, zero, qb), jnp.where(upper, qb, zero)], axis=0)
            kband = jnp.concatenate([prev(ch), kvc_ref[ch, 0, rows, :].astype(BF16)], axis=0)
            vband = jnp.concatenate([prev(nch + ch), kvc_ref[nch + ch, 0, rows, :].astype(BF16)], axis=0)
            s = _nt_dot(qs, kband) + bias_ref[0, variant, ch]
            m = jnp.max(s, axis=-1, keepdims=True)
            e = jnp.exp2(s - m).astype(BF16)
            ones = jnp.ones((2 * c, LANES), BF16)
            ol = _dot(e, jnp.concatenate([vband, ones], axis=1))
            lt = jnp.maximum(ol[:, LANES:], TINY)
            o2 = ol[:, :LANES] * (1.0 / lt)
            l2 = m * LN2 + jnp.log(lt)
            o = jnp.where(upper, o2[c:], o2[:c])
            lse = jnp.where(upper, l2[c:], l2[:c])
            if k == 0:
                mx_ref[ch, rows, :] = lse
                w_ref[ch, rows, :] = jnp.ones_like(lse)
                y_ref[ch, rows, :] = o
            else:
                mx_old = mx_ref[ch, rows, :]
                mx_new = jnp.maximum(mx_old, lse)
                a = jnp.exp(mx_old - mx_new)
                bnew = jnp.exp(lse - mx_new)
                mx_ref[ch, rows, :] = mx_new
                w_ref[ch, rows, :] = w_ref[ch, rows, :] * a + bnew
                y_ref[ch, rows, :] = y_ref[ch, rows, :] * a + bnew * o

    def body(t, carry):
        for j in range(DIL_UNROLL):
            block(t * DIL_UNROLL + j)
        return carry

    lax.fori_loop(0, sup // (c * DIL_UNROLL), body, 0)


def _dil_kernel(q_ref, kvc_ref, kvp_ref, bias_ref, yb_ref, mx_ref, w_ref, y_ref, *, sup):
    gi = pl.program_id(2)
    for k, (_, dil) in enumerate(DIL_GROUPS):
        @pl.when(gi == k)
        def _(k=k, dil=dil):
            _dil_group(k, dil, sup, q_ref, kvc_ref, kvp_ref, bias_ref, mx_ref, w_ref, y_ref)

    @pl.when(gi == len(DIL_GROUPS) - 1)
    def _():
        for ch in range(DIL_HPG * HEAD_DIM // LANES):
            yb_ref[0, :, ch * LANES:(ch + 1) * LANES] = (y_ref[ch] * (1.0 / w_ref[ch])).astype(BF16)


def _dil(qb, kvb, bias, sup):
    _, b, s, _ = qb.shape
    hw = DIL_HPG * HEAD_DIM
    nch = hw // LANES
    ng = len(DIL_GROUPS)
    return pl.pallas_call(
        functools.partial(_dil_kernel, sup=sup),
        grid=(b, s // sup, ng),
        in_specs=[pl.BlockSpec((nch, 1, sup, LANES), lambda bi, si, gi: (gi, bi, si, 0)),
                  pl.BlockSpec((2 * nch, 1, sup, LANES), lambda bi, si, gi: (gi, bi, si, 0)),
                  pl.BlockSpec((2 * nch, 1, sup, LANES),
                               lambda bi, si, gi: (gi, bi, jnp.maximum(si - 1, 0), 0)),
                  pl.BlockSpec((1, 2, nch, 2 * DIL_BLOCK, 2 * DIL_BLOCK),
                               lambda bi, si, gi: (gi, 0, 0, 0, 0))],
        out_specs=pl.BlockSpec((1, sup, hw), lambda bi, si, gi: (bi, si, 0)),
        out_shape=jax.ShapeDtypeStruct((b, s, hw), BF16),
        scratch_shapes=[pltpu.VMEM((nch, sup, LANES), F32), pltpu.VMEM((nch, sup, LANES), F32),
                        pltpu.VMEM((nch, sup, LANES), F32)],
        compiler_params=_params(("parallel", "parallel", "arbitrary")),
        name="dil",
    )(qb, kvb, kvb, bias)


def _mixffn_kernel(x_ref, g_ref, ya_ref, yb_ref, wg_ref, ua_ref, ub_ref, wo_ref,
                   g2_ref, wi_ref, wf_ref, gf_ref, o_ref, *, tf, final_norm):
    x = x_ref[...]
    d = x.shape[1]
    hn = _rms(x, g_ref[...]).astype(BF16)
    gate_a = jax.nn.sigmoid(_dot(hn, wg_ref[:, 0:d]))
    gate_b = jax.nn.sigmoid(_dot(hn, wg_ref[:, d:2 * d]))
    merged = gate_a * _dot(ya_ref[...], ua_ref[...]) + gate_b * _dot(yb_ref[...], ub_ref[...])
    x = x + _dot(merged.astype(BF16), wo_ref[...])
    dff = wf_ref.shape[0]
    hn = _rms(x, g2_ref[...]).astype(BF16)
    y = x
    for lo, hi in tf:
        gt = _dot(hn, wi_ref[:, lo:hi])
        up = _dot(hn, wi_ref[:, dff + lo:dff + hi])
        act = ((gt * jax.nn.sigmoid(gt)) * up).astype(BF16)
        y = y + _dot(act, wf_ref[lo:hi, :])
    if final_norm:
        y = _rms(y, gf_ref[...])
    o_ref[...] = y


def _mixffn(x2, g, ya, yb, wg, ua, ub, wo, g2, w_in, w_out, gf, layer, tm, tf, final_norm):
    n, d = x2.shape
    row = lambda i: (i, 0)
    fixed = lambda i: (0, 0)
    return pl.pallas_call(
        functools.partial(_mixffn_kernel, tf=tf, final_norm=final_norm),
        grid=(n // tm,),
        in_specs=([pl.BlockSpec((tm, d), row), pl.BlockSpec((1, d), fixed),
                   pl.BlockSpec((tm, ya.shape[1]), row), pl.BlockSpec((tm, yb.shape[1]), row)]
                  + [_layer_spec(w, layer) for w in (wg, ua, ub, wo)]
                  + [pl.BlockSpec((1, d), fixed), _layer_spec(w_in, layer), _layer_spec(w_out, layer),
                     pl.BlockSpec((1, d), fixed)]),
        out_specs=pl.BlockSpec((tm, d), row),
        out_shape=jax.ShapeDtypeStruct((n, d), F32),
        compiler_params=pltpu.CompilerParams(dimension_semantics=("parallel",),
                                             vmem_limit_bytes=MIXFFN_VMEM_LIMIT),
        name="mixffn",
    )(x2, g, ya, yb, wg, ua, ub, wo, g2, w_in, w_out, gf)


def _alibi_slopes():
    k = jnp.arange(1, N_ALIBI + 1, dtype=F32)
    s = jnp.exp2(-8.0 * k / N_ALIBI)
    nsa = s[2 * DIL_HPG:2 * DIL_HPG + NSA_HEADS]
    dil = jnp.concatenate([s[:2 * DIL_HPG], s[2 * DIL_HPG + NSA_HEADS:]])
    return nsa, dil


def _overlap(seq, ncp):
    n_c = (seq - CMP_BLOCK) // CMP_STRIDE + 1
    n_s = seq // SEL_BLOCK
    c_start = np.arange(n_c) * CMP_STRIDE
    s_start = np.arange(n_s) * SEL_BLOCK
    ov = np.clip(np.minimum(c_start[:, None] + CMP_BLOCK, s_start[None, :] + SEL_BLOCK)
                 - np.maximum(c_start[:, None], s_start[None, :]), 0, None).astype(np.float32) / CMP_BLOCK
    out = np.zeros((ncp, LANES), np.float32)
    out[:n_c, :n_s] = ov
    return jnp.asarray(out, BF16)


def _proj_weights(w_in):
    dep, d, _ = w_in.shape
    c0 = NSA_HEADS * HEAD_DIM
    c1 = c0 + 3 * 2 * NSA_KV_GROUPS * HEAD_DIM
    c2 = c1 + NSA_HEADS * 3
    c3 = c2 + 3 * DIL_HEADS * HEAD_DIM
    wb = lax.optimization_barrier(w_in.astype(BF16))
    qa = wb[..., :c0]
    kva = wb[..., c0:c1].reshape(dep, d, 3, 2, NSA_KV_GROUPS, HEAD_DIM)
    kva = kva.transpose(0, 1, 2, 4, 3, 5).reshape(dep, d, CMP_W + KVA_W)
    ga = wb[..., c1:c1 + GA_W]
    qkvb = wb[..., c2:c3].reshape(dep, d, 3, len(DIL_GROUPS), DIL_HPG * HEAD_DIM)
    qb = qkvb[:, :, 0].reshape(dep, d, QB_W)
    kvb = qkvb[:, :, 1:3].reshape(dep, d, KVB_W)
    return (qa, kva, qb, kvb, ga), wb[..., c3:]


def _compress_weights(pe_k, pe_v, w_ck1, w_ck2, w_cv1, w_cv2):
    dep = pe_k.shape[0]
    half = CMP_BLOCK // 2
    ck = w_ck1.astype(BF16).reshape(dep, CMP_BLOCK, HEAD_DIM, CMP_HIDDEN)
    cv = w_cv1.astype(BF16).reshape(dep, CMP_BLOCK, HEAD_DIM, CMP_HIDDEN)
    z = jnp.zeros_like(ck)
    w = jnp.concatenate([jnp.concatenate([ck, z], axis=-1),
                         jnp.concatenate([z, cv], axis=-1)], axis=2)
    feat = half * 2 * HEAD_DIM
    wa = w[:, :half].reshape(dep, feat, 2 * CMP_HIDDEN)
    wb = w[:, half:].reshape(dep, feat, 2 * CMP_HIDDEN)
    pe = jnp.concatenate([pe_k, pe_v], axis=-1)
    pe2 = jnp.stack([pe[:, :half].reshape(dep, feat), pe[:, half:].reshape(dep, feat)], axis=1)
    pe2 = jnp.pad(pe2, ((0, 0), (0, 6), (0, 0)))
    z2 = jnp.zeros_like(w_ck2)
    w2 = jnp.concatenate([jnp.concatenate([w_ck2, z2], axis=-1),
                          jnp.concatenate([z2, w_cv2], axis=-1)], axis=1).astype(BF16)
    return pe2, wa, wb, w2


def kernel(x, norm_mix, w_in, pe_k, pe_v, w_ck1, w_ck2, w_cv1, w_cv2, w_up_nsa, w_up_dil,
           w_out, norm_ffn, w_ffn_in, w_ffn_out, norm_final):
    b, s, d = x.shape
    depth = w_in.shape[0]
    n = b * s
    nq = s // Q_BLOCK
    nch = s // CMP_STRIDE
    n_s = s // SEL_BLOCK
    assert s % DIL_SUPER == 0 and n_s <= LANES
    n_top = min(SEL_TOPK, n_s)
    tm = 512
    dff = w_ffn_out.shape[1]
    half = (dff // MXU_TILE + 1) // 2 * MXU_TILE
    tf = ((0, half), (half, dff))

    slopes_nsa, slopes_dil = _alibi_slopes()
    slopes_nsa = slopes_nsa * LOG2E
    ov = _overlap(s, nch)
    gexp = _gate_spread()
    dbias = _dil_bias(slopes_dil)
    w_a, w_gm = _proj_weights(w_in)
    pe2, wa, wb, w2 = _compress_weights(pe_k, pe_v, w_ck1, w_ck2, w_cv1, w_cv2)
    ua = w_up_nsa.astype(BF16)
    ub = w_up_dil.astype(BF16)
    wo = w_out.astype(BF16)
    wfi = w_ffn_in.astype(BF16)
    wfo = w_ffn_out.astype(BF16)
    gf = norm_final.reshape(1, d)

    x2 = x.reshape(n, d)
    for l in range(depth):
        g_mix = norm_mix[l].reshape(1, d)
        qa, xc, kva, qb, kvb, ga = _proj(x2, g_mix, w_a, l, tm)
        qa = qa.reshape(b, s, QA_W)
        kva = kva.reshape(b, s, KVA_W)
        ga = ga.reshape(b, s, GA_W)
        kvc = _compress(xc, pe2[l], wa[l], wb[l], w2[l], b)
        yc, sel, flags = _cmp(slopes_nsa, qa, kvc, ga, ov, gexp, n_top)
        pflags = flags.reshape(-1)
        ya = _selwin(pflags, slopes_nsa, qa, kva, sel, ga, yc, gexp)
        yb = _dil(qb.reshape(-1, b, s, LANES), kvb.reshape(-1, b, s, LANES), dbias, DIL_SUPER)
        x2 = _mixffn(x2, g_mix, ya.reshape(n, NSA_HEADS * HEAD_DIM), yb.reshape(n, DIL_HPG * HEAD_DIM),
                     w_gm, ua, ub, wo, norm_ffn[l].reshape(1, d), wfi, wfo, gf,
                     l, tm, tf, final_norm=(l == depth - 1))
    return x2.reshape(b, s, d)
```
